```python
import math
import jax, jax.numpy as jnp
from jax import lax
import numpy as np

D_MODEL = 1024
BATCH = 8
SEQ = 2048
DEPTH = 1

D_MIX = D_MODEL
D_MLSTM = D_MIX // 2
N_HEADS_MLSTM = 4
HEAD_DIM_MLSTM = D_MLSTM // N_HEADS_MLSTM
D_MOBA = D_MIX - D_MLSTM
N_HEADS_MOBA = 8
HEAD_DIM_MOBA = D_MOBA // N_HEADS_MOBA
CONV_K = 4
MLSTM_CHUNK = 64
MOBA_BLOCK = 256
MOBA_TOPK = 3
MOBA_Q_CHUNK = 16
EPS = 1e-6
SPLIT_SIZES = [D_MLSTM, D_MLSTM, D_MLSTM, D_MLSTM, N_HEADS_MLSTM, N_HEADS_MLSTM, D_MLSTM,
               D_MOBA, D_MOBA, D_MOBA, D_MOBA]
PROJ_DIM = sum(SPLIT_SIZES)

kernel_name = "hymba_mlstm_moba_adaln_layer"


def rms_norm(x, g):
    x32 = x.astype(jnp.float32)
    return x32 * lax.rsqrt(jnp.mean(x32 * x32, axis=-1, keepdims=True) + EPS) * g.astype(jnp.float32)


def to_heads(u, n_heads):
    b, s, _ = u.shape
    return u.reshape(b, s, n_heads, -1).transpose(0, 2, 1, 3)


def causal_short_conv(u, w, bias):
    s = u.shape[1]
    u_pad = jnp.pad(u, ((0, 0), (CONV_K - 1, 0), (0, 0)))
    out = bias.astype(jnp.float32)
    for tap in range(CONV_K):
        out = out + u_pad[:, tap:tap + s] * w[tap].astype(jnp.float32)
    return out


def alibi_slopes(n_heads):
    return jnp.asarray([2.0 ** (-8.0 * (h + 1) / n_heads) for h in range(n_heads)], dtype=jnp.float32)


def mlstm_chunkwise(q, k, v, i_pre, f_pre):
    b_, h_, s_, dh = q.shape
    L = MLSTM_CHUNK
    nc = s_ // L
    k = k * (dh ** -0.5)
    q_c = q.reshape(b_, h_, nc, L, dh)
    k_c = k.reshape(b_, h_, nc, L, dh)
    v_c = v.reshape(b_, h_, nc, L, dh)
    log_f = jax.nn.log_sigmoid(f_pre).reshape(b_, h_, nc, L)
    log_i = i_pre.reshape(b_, h_, nc, L)
    cum_f = jnp.cumsum(log_f, axis=-1)
    f_tot = cum_f[..., -1]
    a = f_tot[..., None] - cum_f + log_i
    a_max = jnp.max(a, axis=-1)
    w = jnp.exp(a - a_max[..., None])
    C_loc = jnp.einsum('bhcs,bhcsv,bhcsk->bhcvk', w, v_c, k_c)
    n_loc = jnp.einsum('bhcs,bhcsk->bhck', w, k_c)

    def step(carry, inp):
        C, n, m = carry
        C_l, n_l, am, ft = inp
        m_new = jnp.maximum(ft + m, am)
        s_old = jnp.exp(ft + m - m_new)
        s_new = jnp.exp(am - m_new)
        C_n = s_old[..., None, None] * C + s_new[..., None, None] * C_l
        n_n = s_old[..., None] * n + s_new[..., None] * n_l
        return (C_n, n_n, m_new), (C, n, m)

    init = (jnp.zeros((b_, h_, dh, dh), jnp.float32), jnp.zeros((b_, h_, dh), jnp.float32),
            jnp.zeros((b_, h_), jnp.float32))
    xs = (C_loc.transpose(2, 0, 1, 3, 4), n_loc.transpose(2, 0, 1, 3),
          a_max.transpose(2, 0, 1), f_tot.transpose(2, 0, 1))
    _, (C_prev, n_prev, m_prev) = lax.scan(step, init, xs)
    C_prev = C_prev.transpose(1, 2, 0, 3, 4)
    n_prev = n_prev.transpose(1, 2, 0, 3)
    m_prev = m_prev.transpose(1, 2, 0)

    causal = jnp.tril(jnp.ones((L, L), dtype=bool))
    D = cum_f[..., :, None] - cum_f[..., None, :] + log_i[..., None, :]
    D = jnp.where(causal, D, -jnp.inf)
    m_inter = cum_f + m_prev[..., None]
    m_t = jnp.maximum(m_inter, jnp.max(D, axis=-1))
    inter_scale = jnp.exp(m_inter - m_t)
    S_qk = jnp.einsum('bhctd,bhcsd->bhcts', q_c, k_c) * jnp.exp(D - m_t[..., None])
    num = (jnp.einsum('bhcts,bhcsd->bhctd', S_qk, v_c)
           + inter_scale[..., None] * jnp.einsum('bhcvk,bhctk->bhctv', C_prev, q_c))
    den = jnp.sum(S_qk, axis=-1) + inter_scale * jnp.einsum('bhck,bhctk->bhct', n_prev, q_c)
    h = num / jnp.maximum(jnp.abs(den), jnp.exp(-m_t))[..., None]
    return h.reshape(b_, h_, s_, dh)


def moba_attention(q, k, v, slopes):
    b_, h_, s_, dh = q.shape
    blk = MOBA_BLOCK
    nb = -(-s_ // blk)
    s_pad = nb * blk
    k_eff = min(MOBA_TOPK, nb)
    pad = ((0, 0), (0, 0), (0, s_pad - s_), (0, 0))
    k_p = jnp.pad(k, pad)
    v_p = jnp.pad(v, pad)
    k_blk = k_p.reshape(b_, h_, nb, blk, dh)
    v_blk = v_p.reshape(b_, h_, nb, blk, dh)
    k_mean = jnp.mean(k_blk, axis=3)
    scale = dh ** -0.5
    qc_n = MOBA_Q_CHUNK
    n_q = s_ // qc_n
    q_chunks = q.reshape(b_, h_, n_q, qc_n, dh).transpose(2, 0, 1, 3, 4)
    b_idx = jnp.arange(b_)[:, None, None, None]
    h_idx = jnp.arange(h_)[None, :, None, None]
    slope = slopes[None, :, None, None]

    def step(args):
        qc, ci = args
        t0 = ci * qc_n
        pos_q = t0 + jnp.arange(qc_n, dtype=jnp.int32)
        j = t0 // blk
        gate = jnp.einsum('bhqd,bhnd->bhqn', qc, k_mean)
        past = jnp.arange(nb, dtype=jnp.int32) < j
        gate = jnp.where(past, gate, -jnp.inf)
        _, sel = lax.top_k(gate, k_eff)
        sel_valid = jnp.arange(k_eff, dtype=jnp.int32) < j
        k_sel = k_blk[b_idx, h_idx, sel]
        v_sel = v_blk[b_idx, h_idx, sel]
        key_pos = sel[..., None] * blk + jnp.arange(blk, dtype=jnp.int32)
        dist = (pos_q[None, None, :, None, None] - key_pos).astype(jnp.float32)
        sc_sel = jnp.einsum('bhqd,bhqnkd->bhqnk', qc, k_sel) * scale - slope[..., None] * jnp.abs(dist)
        sc_sel = jnp.where(sel_valid[:, None], sc_sel, -jnp.inf)
        k_own = lax.dynamic_slice_in_dim(k_p, j * blk, blk, axis=2)
        v_own = lax.dynamic_slice_in_dim(v_p, j * blk, blk, axis=2)
        own_pos = j * blk + jnp.arange(blk, dtype=jnp.int32)
        dist_own = (pos_q[:, None] - own_pos[None, :])
        sc_own = jnp.einsum('bhqd,bhkd->bhqk', qc, k_own) * scale - slope * jnp.abs(dist_own).astype(jnp.float32)
        sc_own = jnp.where(dist_own >= 0, sc_own, -jnp.inf)
        scores = jnp.concatenate([sc_sel.reshape(b_, h_, qc_n, k_eff * blk), sc_own], axis=-1)
        p = jax.nn.softmax(scores, axis=-1)
        p_sel = p[..., :k_eff * blk].reshape(b_, h_, qc_n, k_eff, blk)
        p_own = p[..., k_eff * blk:]
        return (jnp.einsum('bhqnk,bhqnkd->bhqd', p_sel, v_sel)
                + jnp.einsum('bhqk,bhkd->bhqd', p_own, v_own))

    out = lax.map(step, (q_chunks, jnp.arange(n_q, dtype=jnp.int32)))
    return out.transpose(1, 2, 0, 3, 4).reshape(b_, h_, s_, dh)


def hybrid_layer(x, c, w_ada, b_ada, g_norm, w_in, conv_w, conv_b, b_igate, b_fgate,
                 g_mlstm_head, w_out):
    b_, s_, _ = x.shape
    ada = jax.nn.silu(c.astype(jnp.float32)) @ w_ada.astype(jnp.float32) + b_ada.astype(jnp.float32)
    shift, scale, gate = jnp.split(ada, 3, axis=-1)
    h = rms_norm(x, g_norm) * (1.0 + scale[:, None]) + shift[:, None]
    proj = h @ w_in.astype(jnp.float32)
    (qm, km, vm, om, im, fm, zm, qb, kb, vb, zb) = jnp.split(
        proj, np.cumsum(SPLIT_SIZES)[:-1].tolist(), axis=-1)

    qk = jax.nn.silu(causal_short_conv(jnp.concatenate([qm, km], axis=-1), conv_w, conv_b))
    qm, km = jnp.split(qk, 2, axis=-1)
    i_pre = (im + b_igate.astype(jnp.float32)).transpose(0, 2, 1)
    f_pre = (fm + b_fgate.astype(jnp.float32)).transpose(0, 2, 1)
    hm = mlstm_chunkwise(to_heads(qm, N_HEADS_MLSTM), to_heads(km, N_HEADS_MLSTM),
                         to_heads(vm, N_HEADS_MLSTM), i_pre, f_pre)
    hm = hm * lax.rsqrt(jnp.mean(hm * hm, axis=-1, keepdims=True) + EPS)
    hm = hm.transpose(0, 2, 1, 3).reshape(b_, s_, D_MLSTM) * g_mlstm_head.astype(jnp.float32)
    out_a = hm * jax.nn.sigmoid(om) * jax.nn.silu(zm)

    hb = moba_attention(to_heads(qb, N_HEADS_MOBA), to_heads(kb, N_HEADS_MOBA),
                        to_heads(vb, N_HEADS_MOBA), alibi_slopes(N_HEADS_MOBA))
    out_b = hb.transpose(0, 2, 1, 3).reshape(b_, s_, D_MOBA) * jax.nn.silu(zb)

    y = jnp.concatenate([out_a, out_b], axis=-1) @ w_out.astype(jnp.float32)
    return x.astype(jnp.float32) + gate[:, None] * y


def setup_inputs(seed: int = 0) -> dict:
    key = jax.random.key(seed)
    ks = jax.random.split(key, 13)
    f32 = jnp.float32
    x = jax.random.normal(ks[0], (BATCH, SEQ, D_MODEL), f32)
    c = jax.random.normal(ks[1], (BATCH, D_MODEL), f32)
    w_ada = jax.random.normal(ks[2], (DEPTH, D_MODEL, 3 * D_MODEL), f32) * D_MODEL ** -0.5
    b_ada = 0.01 * jax.random.normal(ks[3], (DEPTH, 3 * D_MODEL), f32)
    g_norm = 1.0 + 0.02 * jax.random.normal(ks[4], (DEPTH, D_MODEL), f32)
    w_in = jax.random.normal(ks[5], (DEPTH, D_MODEL, PROJ_DIM), f32) * D_MODEL ** -0.5
    conv_w = jax.random.normal(ks[6], (DEPTH, CONV_K, 2 * D_MLSTM), f32) * CONV_K ** -0.5
    conv_b = 0.01 * jax.random.normal(ks[7], (DEPTH, 2 * D_MLSTM), f32)
    b_igate = 0.1 * jax.random.normal(ks[8], (DEPTH, N_HEADS_MLSTM), f32)
    b_fgate = (jnp.linspace(3.0, 6.0, N_HEADS_MLSTM, dtype=f32)[None, :]
               + 0.1 * jax.random.normal(ks[9], (DEPTH, N_HEADS_MLSTM), f32))
    g_mlstm_head = 1.0 + 0.02 * jax.random.normal(ks[10], (DEPTH, D_MLSTM), f32)
    w_out = jax.random.normal(ks[11], (DEPTH, D_MIX, D_MODEL), f32) * D_MIX ** -0.5
    g_final = 1.0 + 0.02 * jax.random.normal(ks[12], (D_MODEL,), f32)
    return {"x": x, "c": c, "w_ada": w_ada, "b_ada": b_ada, "g_norm": g_norm, "w_in": w_in,
            "conv_w": conv_w, "conv_b": conv_b, "b_igate": b_igate, "b_fgate": b_fgate,
            "g_mlstm_head": g_mlstm_head, "w_out": w_out, "g_final": g_final}


def reference(x, c, w_ada, b_ada, g_norm, w_in, conv_w, conv_b, b_igate, b_fgate,
              g_mlstm_head, w_out, g_final):
    h = x.astype(jnp.float32)
    for layer in range(DEPTH):
        h = hybrid_layer(h, c, w_ada[layer], b_ada[layer], g_norm[layer], w_in[layer],
                         conv_w[layer], conv_b[layer], b_igate[layer], b_fgate[layer],
                         g_mlstm_head[layer], w_out[layer])
    return rms_norm(h, g_final).astype(x.dtype)
```

```python
import functools

import jax
import jax.numpy as jnp
from jax import lax
from jax.experimental import pallas as pl
from jax.experimental.pallas import tpu as pltpu

F32 = jnp.float32
BF16 = jnp.bfloat16

D_MODEL = 1024
D_MLSTM = 512
N_HEADS_MLSTM = 4
HEAD_DIM_MLSTM = 128
D_MOBA = 512
N_HEADS_MOBA = 8
HEAD_DIM_MOBA = 64
CONV_K = 4
MOBA_BLOCK = 256
MOBA_TOPK = 3
EPS = 1e-6

ROW_TILE = 256
MLSTM_CHUNK = 128
MASK_BIAS = -1e30
VMEM_LIMIT = 56 * 1024 * 1024

_OFF = {}
_o = 0
for _name, _size in (("qm", 512), ("km", 512), ("vm", 512), ("om", 512), ("im", 4), ("fm", 4),
                     ("zm", 512), ("qb", 512), ("kb", 512), ("vb", 512), ("zb", 512)):
    _OFF[_name] = (_o, _o + _size)
    _o += _size


def _silu(v):
    return v * jax.nn.sigmoid(v)


def _dot_nt(a, b):
    return lax.dot_general(a, b, (((1,), (1,)), ((), ())), preferred_element_type=F32)


def _dot_tn(a, b):
    return lax.dot_general(a, b, (((0,), (0,)), ((), ())), preferred_element_type=F32)


def _ada_kernel(c_ref, w_ref, b_ref, o_ref):
    a = _silu(c_ref[...])
    o_ref[...] = jnp.dot(a, w_ref[...], preferred_element_type=F32,
                         precision=lax.Precision.HIGHEST) + b_ref[...]


def _ada(c, w_ada, b_ada):
    b, d = c.shape
    n = w_ada.shape[1]
    tn = 1024
    return pl.pallas_call(
        _ada_kernel,
        grid=(n // tn,),
        in_specs=[pl.BlockSpec((b, d), lambda i: (0, 0)),
                  pl.BlockSpec((d, tn), lambda i: (0, i)),
                  pl.BlockSpec((1, tn), lambda i: (0, i))],
        out_specs=pl.BlockSpec((b, tn), lambda i: (0, i)),
        out_shape=jax.ShapeDtypeStruct((b, n), F32),
        compiler_params=pltpu.CompilerParams(dimension_semantics=("arbitrary",),
                                             vmem_limit_bytes=VMEM_LIMIT),
        name="ada",
    )(c, w_ada, b_ada.reshape(1, n))


def _proj_kernel(x_ref, shift_ref, scale_ref, gn_ref, wnn_ref, wnt_ref, wif_ref, cw_ref, cb_ref, bif_ref,
                 qm_ref, km_ref, vm_ref, ga_ref, ift_ref, qbt_ref, kb_ref, kmean_ref, vbt_ref, gbt_ref,
                 ext_ref):
    j = pl.program_id(1)
    tm = x_ref.shape[1]

    x = x_ref[0]
    ms = jnp.mean(x * x, axis=-1, keepdims=True)
    h = x * lax.rsqrt(ms + EPS) * gn_ref[...] * (1.0 + scale_ref[0]) + shift_ref[0]
    hb = h.astype(BF16)

    pn = jnp.dot(hb, wnn_ref[...], preferred_element_type=F32)

    @pl.when(j == 0)
    def _():
        ext_ref[0:8, :] = jnp.zeros((8, 2 * D_MLSTM), F32)

    qk = pn[:, 0:2 * D_MLSTM]
    ext_ref[8:8 + tm, :] = qk
    conv = cb_ref[...] + qk * cw_ref[CONV_K - 1:CONV_K, :]
    for tap in range(CONV_K - 1):
        conv = conv + ext_ref[pl.ds(8 - (CONV_K - 1) + tap, tm), :] * cw_ref[tap:tap + 1, :]
    ext_ref[0:8, :] = qk[tm - 8:tm, :]
    act = _silu(conv)
    qm_ref[0] = act[:, 0:D_MLSTM].astype(BF16)
    km_ref[0] = (act[:, D_MLSTM:2 * D_MLSTM] * (HEAD_DIM_MLSTM ** -0.5)).astype(BF16)

    vm_ref[0] = pn[:, 1024:1536].astype(BF16)
    ga_ref[0] = (jax.nn.sigmoid(pn[:, 1536:2048]) * _silu(pn[:, 2048:2560])).astype(BF16)
    kb = pn[:, 2560:3072]
    kb_ref[0] = kb.astype(BF16)
    kmean_ref[0, pl.ds(j, 1), :] = jnp.mean(kb, axis=0, keepdims=True)

    pt = _dot_nt(wnt_ref[...], hb)
    qbt_ref[0] = (pt[0:512] * (HEAD_DIM_MOBA ** -0.5)).astype(BF16)
    vbt_ref[0, 0] = pt[512:1024].astype(BF16)
    gbt_ref[0] = _silu(pt[1024:1536]).astype(BF16)

    ift_ref[0] = _dot_nt(wif_ref[...], hb) + bif_ref[...]


def _proj(x, mod, g_norm, wnn, wnt, wif, conv_w, conv_b, bif):
    b, s, d = x.shape
    tm = ROW_TILE
    nb = s // MOBA_BLOCK
    assert tm == MOBA_BLOCK
    row = lambda bi, j: (bi, j, 0)
    col = lambda bi, j: (bi, 0, j)
    full2 = lambda bi, j: (0, 0)
    out_shapes = (
        jax.ShapeDtypeStruct((b, s, D_MLSTM), BF16),
        jax.ShapeDtypeStruct((b, s, D_MLSTM), BF16),
        jax.ShapeDtypeStruct((b, s, D_MLSTM), BF16),
        jax.ShapeDtypeStruct((b, s, D_MLSTM), BF16),
        jax.ShapeDtypeStruct((b, 16, s), F32),
        jax.ShapeDtypeStruct((b, D_MOBA, s), BF16),
        jax.ShapeDtypeStruct((b, s, D_MOBA), BF16),
        jax.ShapeDtypeStruct((b, nb, D_MOBA), F32),
        jax.ShapeDtypeStruct((b, nb, D_MOBA, MOBA_BLOCK), BF16),
        jax.ShapeDtypeStruct((b, D_MOBA, s), BF16),
    )
    out_specs = (
        pl.BlockSpec((1, tm, D_MLSTM), row),
        pl.BlockSpec((1, tm, D_MLSTM), row),
        pl.BlockSpec((1, tm, D_MLSTM), row),
        pl.BlockSpec((1, tm, D_MLSTM), row),
        pl.BlockSpec((1, 16, tm), col),
        pl.BlockSpec((1, D_MOBA, tm), col),
        pl.BlockSpec((1, tm, D_MOBA), row),
        pl.BlockSpec((1, nb, D_MOBA), lambda bi, j: (bi, 0, 0)),
        pl.BlockSpec((1, 1, D_MOBA, tm), lambda bi, j: (bi, j, 0, 0)),
        pl.BlockSpec((1, D_MOBA, tm), col),
    )
    in_specs = [
        pl.BlockSpec((1, tm, d), row),
        pl.BlockSpec((1, 1, d), lambda bi, j: (bi, 0, 0)),
        pl.BlockSpec((1, 1, d), lambda bi, j: (bi, 0, 1)),
        pl.BlockSpec((1, d), full2),
        pl.BlockSpec(wnn.shape, full2),
        pl.BlockSpec(wnt.shape, full2),
        pl.BlockSpec(wif.shape, full2),
        pl.BlockSpec(conv_w.shape, full2),
        pl.BlockSpec((1, 2 * D_MLSTM), full2),
        pl.BlockSpec((16, 1), full2),
    ]
    return pl.pallas_call(
        _proj_kernel,
        grid=(b, s // tm),
        in_specs=in_specs,
        out_specs=out_specs,
        out_shape=out_shapes,
        scratch_shapes=[pltpu.VMEM((tm + 8, 2 * D_MLSTM), F32)],
        compiler_params=pltpu.CompilerParams(dimension_semantics=("arbitrary", "arbitrary"),
                                             vmem_limit_bytes=VMEM_LIMIT),
        name="proj",
    )(x, mod, mod, g_norm.reshape(1, d), wnn, wnt, wif, conv_w, conv_b.reshape(1, -1), bif)


def _lane_scan(v, combine, fill):
    n = v.shape[1]
    lane = lax.broadcasted_iota(jnp.int32, v.shape, 1)
    d = 1
    while d < n:
        shifted = pltpu.roll(v, d, axis=1)
        v = combine(v, jnp.where(lane >= d, shifted, fill))
        d *= 2
    return v


def _col_bcast(row, n):
    return jnp.transpose(jnp.broadcast_to(row, (n, n)))


def _mlstm_kernel(q_ref, k_ref, v_ref, ga_ref, ift_ref, gh_ref, o_ref, c_ref, n_ref, m_ref):
    ci = pl.program_id(1)
    L = q_ref.shape[1]
    dh = HEAD_DIM_MLSTM

    @pl.when(ci == 0)
    def _():
        c_ref[...] = jnp.zeros(c_ref.shape, F32)
        n_ref[...] = jnp.zeros(n_ref.shape, F32)
        m_ref[...] = jnp.zeros(m_ref.shape, F32)

    log_i = ift_ref[0, 0:8, :]
    f_pre = ift_ref[0, 8:16, :]
    log_f = jnp.minimum(f_pre, 0.0) - jnp.log1p(jnp.exp(-jnp.abs(f_pre)))
    cum_f = _lane_scan(log_f, jnp.add, 0.0)
    f_tot = jnp.sum(log_f, axis=1, keepdims=True)
    g = log_i - cum_f
    a = f_tot + g
    a_max = jnp.max(a, axis=1, keepdims=True)
    w = jnp.exp(a - a_max)
    g_cmax = _lane_scan(g, jnp.maximum, -jnp.inf)
    m_prev = m_ref[:, 0:1]
    neg_a = jnp.maximum(m_prev, g_cmax)
    a_row = -neg_a
    x_row = a_row - cum_f
    m_new = jnp.maximum(f_tot + m_prev, a_max)
    s_old = jnp.exp(f_tot + m_prev - m_new)
    s_new = jnp.exp(a_max - m_new)
    m_ref[...] = jnp.broadcast_to(m_new, m_ref.shape)

    t_idx = lax.broadcasted_iota(jnp.int32, (L, L), 0)
    s_idx = lax.broadcasted_iota(jnp.int32, (L, L), 1)
    causal = s_idx <= t_idx

    for hd in range(N_HEADS_MLSTM):
        sl = slice(hd * dh, (hd + 1) * dh)
        q = q_ref[0, :, sl]
        k = k_ref[0, :, sl]
        v = v_ref[0, :, sl]
        c_prev = c_ref[hd]
        n_prev = n_ref[hd:hd + 1, :]

        a_col = _col_bcast(a_row[hd:hd + 1, :], L)
        x_col = _col_bcast(x_row[hd:hd + 1, :], L)
        w_col = _col_bcast(w[hd:hd + 1, :], L)
        inter = jnp.exp(a_col + m_prev[hd:hd + 1, :])
        floor = jnp.exp(x_col)

        decay = jnp.exp(jnp.where(causal, a_col + g[hd:hd + 1, :], -jnp.inf))
        s_qk = _dot_nt(q, k) * decay
        num = (jnp.dot(s_qk.astype(BF16), v, preferred_element_type=F32)
               + inter * _dot_nt(q, c_prev.astype(BF16)))
        qn = jnp.sum(q.astype(F32) * n_prev, axis=1, keepdims=True)
        den = jnp.sum(s_qk, axis=1, keepdims=True) + inter * qn
        hh = num / jnp.maximum(jnp.abs(den), floor)
        hh = hh * lax.rsqrt(jnp.mean(hh * hh, axis=-1, keepdims=True) + EPS)
        o_ref[0, :, sl] = (hh * gh_ref[:, sl] * ga_ref[0, :, sl].astype(F32)).astype(BF16)

        vw = (v.astype(F32) * w_col).astype(BF16)
        c_loc = _dot_tn(vw, k)
        n_loc = jnp.sum(k.astype(F32) * w_col, axis=0, keepdims=True)
        c_ref[hd] = s_old[hd:hd + 1, :] * c_prev + s_new[hd:hd + 1, :] * c_loc
        n_ref[hd:hd + 1, :] = s_old[hd:hd + 1, :] * n_prev + s_new[hd:hd + 1, :] * n_loc


def _mlstm(qm, km, vm, ga, ift, g_head):
    b, s, dm = qm.shape
    L = MLSTM_CHUNK
    row = lambda bi, ci: (bi, ci, 0)
    return pl.pallas_call(
        _mlstm_kernel,
        grid=(b, s // L),
        in_specs=[pl.BlockSpec((1, L, dm), row)] * 4
        + [pl.BlockSpec((1, 16, L), lambda bi, ci: (bi, 0, ci)),
           pl.BlockSpec((1, dm), lambda bi, ci: (0, 0))],
        out_specs=pl.BlockSpec((1, L, dm), row),
        out_shape=jax.ShapeDtypeStruct((b, s, dm), BF16),
        scratch_shapes=[pltpu.VMEM((N_HEADS_MLSTM, HEAD_DIM_MLSTM, HEAD_DIM_MLSTM), F32),
                        pltpu.VMEM((8, HEAD_DIM_MLSTM), F32),
                        pltpu.VMEM((8, 128), F32)],
        compiler_params=pltpu.CompilerParams(dimension_semantics=("arbitrary", "arbitrary"),
                                             vmem_limit_bytes=VMEM_LIMIT),
        name="mlstm",
    )(qm, km, vm, ga, ift, g_head.reshape(1, dm))


def _moba_kernel(qt_ref, k_ref, vt_ref, kmean_ref, gt_ref, o_ref):
    p = pl.program_id(1)
    j = pl.program_id(2)
    blk = MOBA_BLOCK
    dh = HEAD_DIM_MOBA
    nb = kmean_ref.shape[1]

    qt_pair = qt_ref[0]
    lane_k = lax.broadcasted_iota(jnp.int32, (blk, 2 * dh), 1)
    row_k = lax.broadcasted_iota(jnp.int32, (blk, 2 * dh), 0)
    blk_i = lax.broadcasted_iota(jnp.int32, (nb, blk), 0)
    key_c = lax.broadcasted_iota(jnp.int32, (blk, blk), 0)
    qry_r = lax.broadcasted_iota(jnp.int32, (blk, blk), 1)
    lane_m = lax.broadcasted_iota(jnp.int32, (nb, 2 * dh), 1)
    ones_rows = jnp.where(blk_i < 2, 1.0, 0.0)

    for hh in range(2):
        head = 2 * p + hh
        slope = lax.bitcast_convert_type(
            jnp.broadcast_to((126 - head) << 23, (blk, 2 * dh)).astype(jnp.int32), F32)
        k_lo, aug_lo = (0, dh) if hh == 0 else (dh, 0)
        is_k_lane = (lane_k >= k_lo) & (lane_k < k_lo + dh)

        km = jnp.where((lane_m >= k_lo) & (lane_m < k_lo + dh), kmean_ref[0], 0.0).astype(BF16)
        gate = jnp.dot(km, qt_pair, preferred_element_type=F32)
        rank = jnp.zeros((nb, blk), jnp.int32)
        for i2 in range(nb):
            g2 = gate[i2:i2 + 1, :]
            beats = (g2 > gate) | ((g2 == gate) & (i2 < blk_i))
            rank = rank + jnp.where(beats & (i2 < j), 1, 0)
        keep = ((blk_i < j) & (rank < MOBA_TOPK)) | (blk_i == j)
        sel_bias = jnp.where(keep, 0.0, MASK_BIAS)

        extra = jnp.concatenate(
            [sel_bias, ones_rows, jnp.zeros((dh - 16, blk), F32)], axis=0)
        extra = extra.astype(BF16)
        qt_h = qt_pair[k_lo:k_lo + dh]
        qt_aug = jnp.concatenate([qt_h, extra] if hh == 0 else [extra, qt_h], axis=0)
        pos_lo = jnp.where(lane_k == aug_lo + nb + 1, slope * row_k.astype(F32), 0.0)

        def scores(i):
            kblk = k_ref[0, pl.ds(pl.multiple_of(i * blk, blk), blk), :]
            aug = (pos_lo + jnp.where(lane_k == aug_lo + i, 1.0, 0.0)
                   + jnp.where(lane_k == aug_lo + nb, slope * (i * blk).astype(F32), 0.0))
            k_aug = jnp.where(is_k_lane, kblk, aug.astype(BF16))
            return jnp.dot(k_aug, qt_aug, preferred_element_type=F32)

        def values(i):
            return vt_ref[0, i, k_lo:k_lo + dh, :]

        s = jnp.where(key_c <= qry_r, scores(j), MASK_BIAS)
        m0 = jnp.max(s, axis=0, keepdims=True)
        p0 = jnp.exp(s - m0)
        l0 = jnp.sum(p0, axis=0, keepdims=True)
        acc0 = jnp.dot(values(j), p0.astype(BF16), preferred_element_type=F32)

        def body(i, carry):
            m_run, l_run, acc = carry
            s = scores(i)
            m_new = jnp.maximum(m_run, jnp.max(s, axis=0, keepdims=True))
            alpha = jnp.exp(m_run - m_new)
            pr = jnp.exp(s - m_new)
            l_new = alpha * l_run + jnp.sum(pr, axis=0, keepdims=True)
            acc = alpha * acc + jnp.dot(values(i), pr.astype(BF16), preferred_element_type=F32)
            return m_new, l_new, acc

        _, l_fin, acc = lax.fori_loop(0, j, body, (m0, l0, acc0))
        out = acc / l_fin * gt_ref[0, k_lo:k_lo + dh, :].astype(F32)
        o_ref[0, k_lo:k_lo + dh, :] = out.astype(BF16)


def _moba(qbt, kb, vbt, kmean, gbt):
    b, dmb, s = qbt.shape
    blk = MOBA_BLOCK
    nb = s // blk
    npair = N_HEADS_MOBA // 2
    pw = 2 * HEAD_DIM_MOBA
    return pl.pallas_call(
        _moba_kernel,
        grid=(b, npair, nb),
        in_specs=[pl.BlockSpec((1, pw, blk), lambda bi, p, j: (bi, p, j)),
                  pl.BlockSpec((1, s, pw), lambda bi, p, j: (bi, 0, p)),
                  pl.BlockSpec((1, nb, pw, blk), lambda bi, p, j: (bi, 0, p, 0)),
                  pl.BlockSpec((1, nb, pw), lambda bi, p, j: (bi, 0, p)),
                  pl.BlockSpec((1, pw, blk), lambda bi, p, j: (bi, p, j))],
        out_specs=pl.BlockSpec((1, pw, blk), lambda bi, p, j: (bi, p, j)),
        out_shape=jax.ShapeDtypeStruct((b, dmb, s), BF16),
        compiler_params=pltpu.CompilerParams(dimension_semantics=("arbitrary", "arbitrary", "arbitrary"),
                                             vmem_limit_bytes=VMEM_LIMIT),
        name="moba",
    )(qbt, kb, vbt, kmean, gbt)


def _out_kernel(x_ref, gate_ref, oa_ref, obt_ref, wa_ref, wb_ref, gf_ref, o_ref):
    y = (jnp.dot(oa_ref[0], wa_ref[...], preferred_element_type=F32)
         + _dot_tn(obt_ref[0], wb_ref[...]))
    r = x_ref[0] + gate_ref[0] * y
    o_ref[0] = r * lax.rsqrt(jnp.mean(r * r, axis=-1, keepdims=True) + EPS) * gf_ref[...]


def _out(x, mod, out_a, out_bt, w_a, w_b, g_final):
    b, s, d = x.shape
    tm = ROW_TILE
    row = lambda bi, j: (bi, j, 0)
    full2 = lambda bi, j: (0, 0)
    return pl.pallas_call(
        _out_kernel,
        grid=(b, s // tm),
        in_specs=[pl.BlockSpec((1, tm, d), row),
                  pl.BlockSpec((1, 1, d), lambda bi, j: (bi, 0, 2)),
                  pl.BlockSpec((1, tm, D_MLSTM), row),
                  pl.BlockSpec((1, D_MOBA, tm), lambda bi, j: (bi, 0, j)),
                  pl.BlockSpec(w_a.shape, full2),
                  pl.BlockSpec(w_b.shape, full2),
                  pl.BlockSpec((1, d), full2)],
        out_specs=pl.BlockSpec((1, tm, d), row),
        out_shape=jax.ShapeDtypeStruct((b, s, d), x.dtype),
        compiler_params=pltpu.CompilerParams(dimension_semantics=("arbitrary", "arbitrary"),
                                             vmem_limit_bytes=VMEM_LIMIT),
        name="out",
    )(x, mod, out_a, out_bt, w_a, w_b, g_final.reshape(1, d))


def _layer(x, c, w_ada, b_ada, g_norm, w_in, conv_w, conv_b, b_igate, b_fgate, g_mlstm_head, w_out,
           g_final):
    b = x.shape[0]
    mod = _ada(c, w_ada, b_ada).reshape(b, 1, 3 * D_MODEL)

    cols = lambda name: w_in[:, _OFF[name][0]:_OFF[name][1]]
    wnn = jnp.concatenate([cols("qm"), cols("km"), cols("vm"), cols("om"), cols("zm"), cols("kb")],
                          axis=1).astype(BF16)
    wnt = jnp.concatenate([cols("qb"), cols("vb"), cols("zb")], axis=1).T.astype(BF16)
    zpad = jnp.zeros((4, D_MODEL), w_in.dtype)
    wif = jnp.concatenate([cols("im").T, zpad, cols("fm").T, zpad], axis=0).astype(BF16)
    z4 = jnp.zeros((4,), F32)
    bif = jnp.concatenate([b_igate.astype(F32), z4, b_fgate.astype(F32), z4]).reshape(16, 1)

    qm, km, vm, ga, ift, qbt, kb, kmean, vbt, gbt = _proj(
        x, mod, g_norm, wnn, wnt, wif, conv_w, conv_b, bif)
    out_a = _mlstm(qm, km, vm, ga, ift, g_mlstm_head)
    out_bt = _moba(qbt, kb, vbt, kmean, gbt)
    w_a = w_out[:D_MLSTM].astype(BF16)
    w_b = w_out[D_MLSTM:].astype(BF16)
    return _out(x, mod, out_a, out_bt, w_a, w_b, g_final)


def kernel(x, c, w_ada, b_ada, g_norm, w_in, conv_w, conv_b, b_igate, b_fgate, g_mlstm_head, w_out,
           g_final):
    assert w_ada.shape[0] == 1, "single-layer trunk"
    return _layer(x, c, w_ada[0], b_ada[0], g_norm[0], w_in[0], conv_w[0], conv_b[0], b_igate[0],
                  b_fgate[0], g_mlstm_head[0], w_out[0], g_final)
```

```python
import jax
import jax.numpy as jnp
import numpy as np
from jax import lax
from jax.experimental import pallas as pl
from jax.experimental.pallas import tpu as pltpu

F32 = jnp.float32
BF16 = jnp.bfloat16

D_MODEL = 1024
D_MLSTM = 512
N_HEADS_MLSTM = 4
HEAD_DIM_MLSTM = 128
D_MOBA = 512
N_HEADS_MOBA = 8
HEAD_DIM_MOBA = 64
CONV_K = 4
MOBA_BLOCK = 256
MOBA_TOPK = 3
EPS = 1e-6

ROW_TILE = 256
MLSTM_CHUNK = 128
MASK_BIAS = -1e30
LOG2E = 1.4426950408889634
N_POS_TERMS = 3
VMEM_LIMIT = 56 * 1024 * 1024

_OFF = {}
_o = 0
for _name, _size in (("qm", 512), ("km", 512), ("vm", 512), ("om", 512), ("im", 4), ("fm", 4),
                     ("zm", 512), ("qb", 512), ("kb", 512), ("vb", 512), ("zb", 512)):
    _OFF[_name] = (_o, _o + _size)
    _o += _size


def _silu(v):
    return v * jax.nn.sigmoid(v)


def _dot_nt(a, b):
    return lax.dot_general(a, b, (((1,), (1,)), ((), ())), preferred_element_type=F32)


def _dot_tn(a, b):
    return lax.dot_general(a, b, (((0,), (0,)), ((), ())), preferred_element_type=F32)


def _ada_kernel(c_ref, w_ref, b_ref, o_ref):
    a = _silu(c_ref[...])
    o_ref[...] = jnp.dot(a, w_ref[...], preferred_element_type=F32,
                         precision=lax.Precision.HIGHEST) + b_ref[...]


def _ada(c, w_ada, b_ada):
    b, d = c.shape
    n = w_ada.shape[1]
    tn = 1024
    return pl.pallas_call(
        _ada_kernel,
        grid=(n // tn,),
        in_specs=[pl.BlockSpec((b, d), lambda i: (0, 0)),
                  pl.BlockSpec((d, tn), lambda i: (0, i)),
                  pl.BlockSpec((1, tn), lambda i: (0, i))],
        out_specs=pl.BlockSpec((b, tn), lambda i: (0, i)),
        out_shape=jax.ShapeDtypeStruct((b, n), F32),
        compiler_params=pltpu.CompilerParams(dimension_semantics=("arbitrary",),
                                             vmem_limit_bytes=VMEM_LIMIT),
        name="ada",
    )(c, w_ada, b_ada.reshape(1, n))


def _proj_kernel(x_ref, shift_ref, scale_ref, gn_ref, wnn_ref, wnt_ref, wif_ref, cw_ref, cb_ref, bif_ref,
                 kx_ref,
                 qm_ref, km_ref, vm_ref, ga_ref, ift_ref, qbt_ref, kaug_ref, kmean_ref, vbt_ref, gbt_ref,
                 ext_ref):
    j = pl.program_id(1)
    tm = x_ref.shape[1]

    x = x_ref[0]
    ms = jnp.mean(x * x, axis=-1, keepdims=True)
    h = x * lax.rsqrt(ms + EPS) * gn_ref[...] * (1.0 + scale_ref[0]) + shift_ref[0]
    hb = h.astype(BF16)

    pn = jnp.dot(hb, wnn_ref[...], preferred_element_type=F32)

    @pl.when(j == 0)
    def _():
        ext_ref[0:8, :] = jnp.zeros((8, 2 * D_MLSTM), F32)

    qk = pn[:, 0:2 * D_MLSTM]
    ext_ref[8:8 + tm, :] = qk
    conv = cb_ref[...] + qk * cw_ref[CONV_K - 1:CONV_K, :]
    for tap in range(CONV_K - 1):
        conv = conv + ext_ref[pl.ds(8 - (CONV_K - 1) + tap, tm), :] * cw_ref[tap:tap + 1, :]
    ext_ref[0:8, :] = qk[tm - 8:tm, :]
    act = _silu(conv)
    qm_ref[0] = act[:, 0:D_MLSTM].astype(BF16)
    km_ref[0] = (act[:, D_MLSTM:2 * D_MLSTM] * (HEAD_DIM_MLSTM ** -0.5)).astype(BF16)

    vm_ref[0] = pn[:, 1024:1536].astype(BF16)
    ga_ref[0] = (jax.nn.sigmoid(pn[:, 1536:2048]) * _silu(pn[:, 2048:2560])).astype(BF16)
    kb = pn[:, 2560:3072]
    kmean_ref[0, pl.ds(j, 1), :] = jnp.mean(kb, axis=0, keepdims=True)
    kb2 = (kb * LOG2E).astype(BF16)
    lane = lax.broadcasted_iota(jnp.int32, (tm, 128), 1)
    for hd in range(N_HEADS_MOBA):
        pr, hh = divmod(hd, 2)
        is_key = (lane < HEAD_DIM_MOBA) if hh == 0 else (lane >= HEAD_DIM_MOBA)
        grp = slice(128 * hd, 128 * (hd + 1))
        kaug_ref[0, :, grp] = jnp.where(is_key, kb2[:, 128 * pr:128 * (pr + 1)], kx_ref[0, :, grp])

    pt = _dot_nt(wnt_ref[...], hb)
    qbt_ref[0] = (pt[0:512] * (HEAD_DIM_MOBA ** -0.5)).astype(BF16)
    vbt_ref[0, 0] = pt[512:1024].astype(BF16)
    gbt_ref[0] = _silu(pt[1024:1536]).astype(BF16)

    ift_ref[0] = _dot_nt(wif_ref[...], hb) + bif_ref[...]


def _proj(x, mod, g_norm, wnn, wnt, wif, conv_w, conv_b, bif):
    b, s, d = x.shape
    tm = ROW_TILE
    nb = s // MOBA_BLOCK
    assert tm == MOBA_BLOCK
    kx = _moba_key_extras(nb)
    kw = N_HEADS_MOBA * 128
    row = lambda bi, j: (bi, j, 0)
    col = lambda bi, j: (bi, 0, j)
    full2 = lambda bi, j: (0, 0)
    out_shapes = (
        jax.ShapeDtypeStruct((b, s, D_MLSTM), BF16),
        jax.ShapeDtypeStruct((b, s, D_MLSTM), BF16),
        jax.ShapeDtypeStruct((b, s, D_MLSTM), BF16),
        jax.ShapeDtypeStruct((b, s, D_MLSTM), BF16),
        jax.ShapeDtypeStruct((b, 16, s), F32),
        jax.ShapeDtypeStruct((b, D_MOBA, s), BF16),
        jax.ShapeDtypeStruct((b, s, kw), BF16),
        jax.ShapeDtypeStruct((b, nb, D_MOBA), F32),
        jax.ShapeDtypeStruct((b, nb, D_MOBA, MOBA_BLOCK), BF16),
        jax.ShapeDtypeStruct((b, D_MOBA, s), BF16),
    )
    out_specs = (
        pl.BlockSpec((1, tm, D_MLSTM), row),
        pl.BlockSpec((1, tm, D_MLSTM), row),
        pl.BlockSpec((1, tm, D_MLSTM), row),
        pl.BlockSpec((1, tm, D_MLSTM), row),
        pl.BlockSpec((1, 16, tm), col),
        pl.BlockSpec((1, D_MOBA, tm), col),
        pl.BlockSpec((1, tm, kw), row),
        pl.BlockSpec((1, nb, D_MOBA), lambda bi, j: (bi, 0, 0)),
        pl.BlockSpec((1, 1, D_MOBA, tm), lambda bi, j: (bi, j, 0, 0)),
        pl.BlockSpec((1, D_MOBA, tm), col),
    )
    in_specs = [
        pl.BlockSpec((1, tm, d), row),
        pl.BlockSpec((1, 1, d), lambda bi, j: (bi, 0, 0)),
        pl.BlockSpec((1, 1, d), lambda bi, j: (bi, 0, 1)),
        pl.BlockSpec((1, d), full2),
        pl.BlockSpec(wnn.shape, full2),
        pl.BlockSpec(wnt.shape, full2),
        pl.BlockSpec(wif.shape, full2),
        pl.BlockSpec(conv_w.shape, full2),
        pl.BlockSpec((1, 2 * D_MLSTM), full2),
        pl.BlockSpec((16, 1), full2),
        pl.BlockSpec((1, MOBA_BLOCK, kw), lambda bi, j: (j, 0, 0)),
    ]
    return pl.pallas_call(
        _proj_kernel,
        grid=(b, s // tm),
        in_specs=in_specs,
        out_specs=out_specs,
        out_shape=out_shapes,
        scratch_shapes=[pltpu.VMEM((tm + 8, 2 * D_MLSTM), F32)],
        compiler_params=pltpu.CompilerParams(dimension_semantics=("arbitrary", "arbitrary"),
                                             vmem_limit_bytes=VMEM_LIMIT),
        name="proj",
    )(x, mod, mod, g_norm.reshape(1, d), wnn, wnt, wif, conv_w, conv_b.reshape(1, -1), bif, kx)


def _lane_scan(v, combine, fill):
    n = v.shape[1]
    lane = lax.broadcasted_iota(jnp.int32, v.shape, 1)
    d = 1
    while d < n:
        shifted = pltpu.roll(v, d, axis=1)
        v = combine(v, jnp.where(lane >= d, shifted, fill))
        d *= 2
    return v


def _col_bcast(row, n):
    return jnp.transpose(jnp.broadcast_to(row, (n, n)))


def _mlstm_kernel(q_ref, k_ref, v_ref, ga_ref, ift_ref, gh_ref, o_ref, c_ref, n_ref, m_ref):
    ci = pl.program_id(1)
    L = q_ref.shape[1]
    dh = HEAD_DIM_MLSTM

    @pl.when(ci == 0)
    def _():
        c_ref[...] = jnp.zeros(c_ref.shape, F32)
        n_ref[...] = jnp.zeros(n_ref.shape, F32)
        m_ref[...] = jnp.zeros(m_ref.shape, F32)

    log_i = ift_ref[0, 0:8, :]
    f_pre = ift_ref[0, 8:16, :]
    log_f = jnp.minimum(f_pre, 0.0) - jnp.log1p(jnp.exp(-jnp.abs(f_pre)))
    cum_f = _lane_scan(log_f, jnp.add, 0.0)
    f_tot = jnp.sum(log_f, axis=1, keepdims=True)
    g = log_i - cum_f
    a = f_tot + g
    a_max = jnp.max(a, axis=1, keepdims=True)
    w = jnp.exp(a - a_max)
    g_cmax = _lane_scan(g, jnp.maximum, -jnp.inf)
    m_prev = m_ref[:, 0:1]
    neg_a = jnp.maximum(m_prev, g_cmax)
    a_row = -neg_a
    x_row = a_row - cum_f
    m_new = jnp.maximum(f_tot + m_prev, a_max)
    s_old = jnp.exp(f_tot + m_prev - m_new)
    s_new = jnp.exp(a_max - m_new)
    m_ref[...] = jnp.broadcast_to(m_new, m_ref.shape)

    t_idx = lax.broadcasted_iota(jnp.int32, (L, L), 0)
    s_idx = lax.broadcasted_iota(jnp.int32, (L, L), 1)
    causal = s_idx <= t_idx

    for hd in range(N_HEADS_MLSTM):
        sl = slice(hd * dh, (hd + 1) * dh)
        q = q_ref[0, :, sl]
        k = k_ref[0, :, sl]
        v = v_ref[0, :, sl]
        c_prev = c_ref[hd]
        n_prev = n_ref[hd:hd + 1, :]

        a_col = _col_bcast(a_row[hd:hd + 1, :], L)
        x_col = _col_bcast(x_row[hd:hd + 1, :], L)
        w_col = _col_bcast(w[hd:hd + 1, :], L)
        inter = jnp.exp(a_col + m_prev[hd:hd + 1, :])
        floor = jnp.exp(x_col)

        decay = jnp.exp(jnp.where(causal, a_col + g[hd:hd + 1, :], -jnp.inf))
        s_qk = _dot_nt(q, k) * decay
        num = (jnp.dot(s_qk.astype(BF16), v, preferred_element_type=F32)
               + inter * _dot_nt(q, c_prev.astype(BF16)))
        qn = jnp.sum(q.astype(F32) * n_prev, axis=1, keepdims=True)
        den = jnp.sum(s_qk, axis=1, keepdims=True) + inter * qn
        hh = num / jnp.maximum(jnp.abs(den), floor)
        hh = hh * lax.rsqrt(jnp.mean(hh * hh, axis=-1, keepdims=True) + EPS)
        o_ref[0, :, sl] = (hh * gh_ref[:, sl] * ga_ref[0, :, sl].astype(F32)).astype(BF16)

        vw = (v.astype(F32) * w_col).astype(BF16)
        c_loc = _dot_tn(vw, k)
        n_loc = jnp.sum(k.astype(F32) * w_col, axis=0, keepdims=True)
        c_ref[hd] = s_old[hd:hd + 1, :] * c_prev + s_new[hd:hd + 1, :] * c_loc
        n_ref[hd:hd + 1, :] = s_old[hd:hd + 1, :] * n_prev + s_new[hd:hd + 1, :] * n_loc


def _mlstm(qm, km, vm, ga, ift, g_head):
    b, s, dm = qm.shape
    L = MLSTM_CHUNK
    row = lambda bi, ci: (bi, ci, 0)
    return pl.pallas_call(
        _mlstm_kernel,
        grid=(b, s // L),
        in_specs=[pl.BlockSpec((1, L, dm), row)] * 4
        + [pl.BlockSpec((1, 16, L), lambda bi, ci: (bi, 0, ci)),
           pl.BlockSpec((1, dm), lambda bi, ci: (0, 0))],
        out_specs=pl.BlockSpec((1, L, dm), row),
        out_shape=jax.ShapeDtypeStruct((b, s, dm), BF16),
        scratch_shapes=[pltpu.VMEM((N_HEADS_MLSTM, HEAD_DIM_MLSTM, HEAD_DIM_MLSTM), F32),
                        pltpu.VMEM((8, HEAD_DIM_MLSTM), F32),
                        pltpu.VMEM((8, 128), F32)],
        compiler_params=pltpu.CompilerParams(dimension_semantics=("arbitrary", "arbitrary"),
                                             vmem_limit_bytes=VMEM_LIMIT),
        name="mlstm",
    )(qm, km, vm, ga, ift, g_head.reshape(1, dm))


def _moba_steps(nb):
    jt, it = [], []
    for j in range(nb):
        for i in [j] + list(range(j)):
            jt.append(j)
            it.append(i)
    return jt, it


def _moba_key_extras(nb):
    blk = MOBA_BLOCK
    lane = np.arange(N_HEADS_MOBA * 128)
    head = lane // 128
    w = (lane % 128 - np.where(head % 2 == 0, HEAD_DIM_MOBA, 0))[None, None, :]
    slope = (2.0 ** -(head + 1.0))[None, None, :]
    i = np.arange(nb)[:, None, None]
    c = np.arange(blk)[None, :, None]
    tab = np.where(w == i, 1.0, 0.0)
    rest = LOG2E * slope * (i * blk + c)
    for term in range(N_POS_TERMS):
        piece = rest.astype(np.float32).astype(BF16).astype(np.float64)
        tab = tab + np.where(w == nb + term, piece, 0.0)
        rest = rest - piece
    return jnp.asarray(tab.astype(np.float32)).astype(BF16)


def _moba_kernel(jt_ref, it_ref, qt_ref, k_ref, vt_ref, kmean_ref, gt_ref, o_ref,
                 qaug_ref, m_ref, acc_ref, s_ref, cmax_ref):
    t = pl.program_id(1)
    j = jt_ref[t]
    i = it_ref[t]
    blk = MOBA_BLOCK
    dh = HEAD_DIM_MOBA
    nb = kmean_ref.shape[1]
    first = i == j
    last = jnp.logical_or(i == j - 1, j == 0)

    def key_tile(h):
        return k_ref[0, :, 128 * h:128 * (h + 1)]

    def rows(h):
        return slice(dh * h, dh * (h + 1))

    ones_tail = jnp.ones((16, blk), BF16)

    def values_t(h):
        return jnp.concatenate([vt_ref[0, 0, rows(h), :], ones_tail], axis=0)

    @pl.when(first)
    def _():
        blk_i = lax.broadcasted_iota(jnp.int32, (nb, blk), 0)
        lane_m = lax.broadcasted_iota(jnp.int32, (nb, 2 * dh), 1)
        key_c = lax.broadcasted_iota(jnp.int32, (blk, blk), 0)
        qry_r = lax.broadcasted_iota(jnp.int32, (blk, blk), 1)
        causal = key_c <= qry_r
        ones_rows = jnp.where(blk_i < N_POS_TERMS, 1.0, 0.0)
        for h in range(N_HEADS_MOBA):
            pr, hh = divmod(h, 2)
            qt_pair = qt_ref[0, 128 * pr:128 * (pr + 1), :]
            in_head = (lane_m < dh) if hh == 0 else (lane_m >= dh)
            km = jnp.where(in_head, kmean_ref[0, :, 128 * pr:128 * (pr + 1)], 0.0).astype(BF16)
            gate = jnp.dot(km, qt_pair, preferred_element_type=F32)
            rank = jnp.zeros((nb, blk), jnp.int32)
            for i2 in range(nb):
                g2 = gate[i2:i2 + 1, :]
                beats = (g2 > gate) | ((g2 == gate) & (i2 < blk_i))
                rank = rank + jnp.where(beats & (i2 < j), 1, 0)
            keep = ((blk_i < j) & (rank < MOBA_TOPK)) | (blk_i == j)
            sel_bias = jnp.where(keep, 0.0, MASK_BIAS)
            extra = jnp.concatenate([sel_bias, ones_rows, jnp.zeros((dh - 16, blk), F32)], axis=0)
            extra = extra.astype(BF16)
            qt_h = qt_pair[dh * hh:dh * (hh + 1)]
            qaug = jnp.concatenate([qt_h, extra] if hh == 0 else [extra, qt_h], axis=0)
            qaug_ref[h] = qaug

            s = jnp.dot(key_tile(h), qaug, preferred_element_type=F32)
            s = jnp.where(causal, s, MASK_BIAS)
            s_ref[h] = s
            m_ref[h:h + 1, :] = jnp.max(s, axis=0, keepdims=True)

        for h in range(N_HEADS_MOBA):
            p0 = jnp.exp2(s_ref[h] - m_ref[h:h + 1, :])
            acc_ref[h] = jnp.dot(values_t(h), p0.astype(BF16), preferred_element_type=F32)

    @pl.when(jnp.logical_not(first))
    def _():
        for h in range(N_HEADS_MOBA):
            s = jnp.dot(key_tile(h), qaug_ref[h], preferred_element_type=F32)
            s_ref[h] = s
            cmax_ref[h:h + 1, :] = jnp.max(s, axis=0, keepdims=True)
        for h in range(N_HEADS_MOBA):
            m_run = m_ref[h:h + 1, :]
            m_new = jnp.maximum(m_run, cmax_ref[h:h + 1, :])
            alpha = jnp.exp2(m_run - m_new)
            pr_ = jnp.exp2(s_ref[h] - m_new)
            m_ref[h:h + 1, :] = m_new
            acc_ref[h] = alpha * acc_ref[h] + jnp.dot(values_t(h), pr_.astype(BF16),
                                                      preferred_element_type=F32)

    @pl.when(last)
    def _():
        for h in range(N_HEADS_MOBA):
            acc = acc_ref[h]
            out = acc[0:dh] / acc[dh:dh + 1] * gt_ref[0, rows(h), :].astype(F32)
            o_ref[0, rows(h), :] = out.astype(BF16)


def _moba(qbt, kaug, vbt, kmean, gbt):
    b, dmb, s = qbt.shape
    blk = MOBA_BLOCK
    nb = s // blk
    kw = kaug.shape[2]
    assert nb == 8, "selection extras assume 8 key blocks"
    jt, it = _moba_steps(nb)
    qmap = lambda bi, t, jt_r, it_r: (bi, 0, jt_r[t])
    grid_spec = pltpu.PrefetchScalarGridSpec(
        num_scalar_prefetch=2,
        grid=(b, len(jt)),
        in_specs=[pl.BlockSpec((1, dmb, blk), qmap),
                  pl.BlockSpec((1, blk, kw), lambda bi, t, jt_r, it_r: (bi, it_r[t], 0)),
                  pl.BlockSpec((1, 1, dmb, blk), lambda bi, t, jt_r, it_r: (bi, it_r[t], 0, 0)),
                  pl.BlockSpec((1, nb, dmb), lambda bi, t, jt_r, it_r: (bi, 0, 0)),
                  pl.BlockSpec((1, dmb, blk), qmap)],
        out_specs=pl.BlockSpec((1, dmb, blk), qmap),
        scratch_shapes=[pltpu.VMEM((N_HEADS_MOBA, 2 * HEAD_DIM_MOBA, blk), BF16),
                        pltpu.VMEM((N_HEADS_MOBA, blk), F32),
                        pltpu.VMEM((N_HEADS_MOBA, HEAD_DIM_MOBA + 16, blk), F32),
                        pltpu.VMEM((N_HEADS_MOBA, blk, blk), F32),
                        pltpu.VMEM((N_HEADS_MOBA, blk), F32)],
    )
    return pl.pallas_call(
        _moba_kernel,
        grid_spec=grid_spec,
        out_shape=jax.ShapeDtypeStruct((b, dmb, s), BF16),
        compiler_params=pltpu.CompilerParams(dimension_semantics=("arbitrary", "arbitrary"),
                                             vmem_limit_bytes=VMEM_LIMIT),
        name="moba",
    )(jnp.asarray(jt, jnp.int32), jnp.asarray(it, jnp.int32), qbt, kaug, vbt, kmean, gbt)


def _out_kernel(x_ref, gate_ref, oa_ref, obt_ref, wa_ref, wb_ref, gf_ref, o_ref):
    y = (jnp.dot(oa_ref[0], wa_ref[...], preferred_element_type=F32)
         + _dot_tn(obt_ref[0], wb_ref[...]))
    r = x_ref[0] + gate_ref[0] * y
    o_ref[0] = r * lax.rsqrt(jnp.mean(r * r, axis=-1, keepdims=True) + EPS) * gf_ref[...]


def _out(x, mod, out_a, out_bt, w_a, w_b, g_final):
    b, s, d = x.shape
    tm = ROW_TILE
    row = lambda bi, j: (bi, j, 0)
    full2 = lambda bi, j: (0, 0)
    return pl.pallas_call(
        _out_kernel,
        grid=(b, s // tm),
        in_specs=[pl.BlockSpec((1, tm, d), row),
                  pl.BlockSpec((1, 1, d), lambda bi, j: (bi, 0, 2)),
                  pl.BlockSpec((1, tm, D_MLSTM), row),
                  pl.BlockSpec((1, D_MOBA, tm), lambda bi, j: (bi, 0, j)),
                  pl.BlockSpec(w_a.shape, full2),
                  pl.BlockSpec(w_b.shape, full2),
                  pl.BlockSpec((1, d), full2)],
        out_specs=pl.BlockSpec((1, tm, d), row),
        out_shape=jax.ShapeDtypeStruct((b, s, d), x.dtype),
        compiler_params=pltpu.CompilerParams(dimension_semantics=("arbitrary", "arbitrary"),
                                             vmem_limit_bytes=VMEM_LIMIT),
        name="out",
    )(x, mod, out_a, out_bt, w_a, w_b, g_final.reshape(1, d))


def _layer(x, c, w_ada, b_ada, g_norm, w_in, conv_w, conv_b, b_igate, b_fgate, g_mlstm_head, w_out,
           g_final):
    b = x.shape[0]
    mod = _ada(c, w_ada, b_ada).reshape(b, 1, 3 * D_MODEL)

    cols = lambda name: w_in[:, _OFF[name][0]:_OFF[name][1]]
    wnn = jnp.concatenate([cols("qm"), cols("km"), cols("vm"), cols("om"), cols("zm"), cols("kb")],
                          axis=1).astype(BF16)
    wnt = jnp.concatenate([cols("qb"), cols("vb"), cols("zb")], axis=1).T.astype(BF16)
    zpad = jnp.zeros((4, D_MODEL), w_in.dtype)
    wif = jnp.concatenate([cols("im").T, zpad, cols("fm").T, zpad], axis=0).astype(BF16)
    z4 = jnp.zeros((4,), F32)
    bif = jnp.concatenate([b_igate.astype(F32), z4, b_fgate.astype(F32), z4]).reshape(16, 1)

    qm, km, vm, ga, ift, qbt, kaug, kmean, vbt, gbt = _proj(
        x, mod, g_norm, wnn, wnt, wif, conv_w, conv_b, bif)
    out_a = _mlstm(qm, km, vm, ga, ift, g_mlstm_head)
    out_bt = _moba(qbt, kaug, vbt, kmean, gbt)
    w_a = w_out[:D_MLSTM].astype(BF16)
    w_b = w_out[D_MLSTM:].astype(BF16)
    return _out(x, mod, out_a, out_bt, w_a, w_b, g_final)


def kernel(x, c, w_ada, b_ada, g_norm, w_in, conv_w, conv_b, b_igate, b_fgate, g_mlstm_head, w_out,
           g_final):
    assert w_ada.shape[0] == 1, "single-layer trunk"
    return _layer(x, c, w_ada[0], b_ada[0], g_norm[0], w_in[0], conv_w[0], conv_b[0], b_igate[0],
                  b_fgate[0], g_mlstm_head[0], w_out[0], g_final)
```

```python
import jax
import jax.numpy as jnp
import numpy as np
from jax import lax
from jax.experimental import pallas as pl
from jax.experimental.pallas import tpu as pltpu

F32 = jnp.float32
BF16 = jnp.bfloat16

D_MODEL = 1024
D_MLSTM = 512
N_HEADS_MLSTM = 4
HEAD_DIM_MLSTM = 128
D_MOBA = 512
N_HEADS_MOBA = 8
HEAD_DIM_MOBA = 64
CONV_K = 4
MOBA_BLOCK = 256
MOBA_TOPK = 3
EPS = 1e-6

ROW_TILE = 256
PROJ_CHUNK = 256
MLSTM_CHUNK = 128
MLSTM_STEP_CHUNKS = 4
MASK_BIAS = -1e30
LOG2E = 1.4426950408889634
N_POS_TERMS = 3
VMEM_LIMIT = 56 * 1024 * 1024

_OFF = {}
_o = 0
for _name, _size in (("qm", 512), ("km", 512), ("vm", 512), ("om", 512), ("im", 4), ("fm", 4),
                     ("zm", 512), ("qb", 512), ("kb", 512), ("vb", 512), ("zb", 512)):
    _OFF[_name] = (_o, _o + _size)
    _o += _size


def _silu(v):
    return v * jax.nn.sigmoid(v)


def _dot_nt(a, b):
    return lax.dot_general(a, b, (((1,), (1,)), ((), ())), preferred_element_type=F32)


def _dot_tn(a, b):
    return lax.dot_general(a, b, (((0,), (0,)), ((), ())), preferred_element_type=F32)


def _ada_kernel(c_ref, w_ref, b_ref, o_ref):
    a = _silu(c_ref[...])
    o_ref[...] = jnp.dot(a, w_ref[...], preferred_element_type=F32,
                         precision=lax.Precision.HIGHEST) + b_ref[...]


def _ada(c, w_ada, b_ada):
    b, d = c.shape
    n = w_ada.shape[1]
    tn = 1024
    return pl.pallas_call(
        _ada_kernel,
        grid=(n // tn,),
        in_specs=[pl.BlockSpec((b, d), lambda i: (0, 0)),
                  pl.BlockSpec((d, tn), lambda i: (0, i)),
                  pl.BlockSpec((1, tn), lambda i: (0, i))],
        out_specs=pl.BlockSpec((b, tn), lambda i: (0, i)),
        out_shape=jax.ShapeDtypeStruct((b, n), F32),
        compiler_params=pltpu.CompilerParams(dimension_semantics=("arbitrary",),
                                             vmem_limit_bytes=VMEM_LIMIT),
        name="ada",
    )(c, w_ada, b_ada.reshape(1, n))


def _chunk_scan(v, combine, fill):
    pos = lax.broadcasted_iota(jnp.int32, v.shape, 1) & (MLSTM_CHUNK - 1)
    d = 1
    while d < MLSTM_CHUNK:
        shifted = pltpu.roll(v, d, axis=1)
        v = combine(v, jnp.where(pos >= d, shifted, fill))
        d *= 2
    return v


def _proj_kernel(x_ref, shift_ref, scale_ref, gn_ref, wnn_ref, wnt_ref, wif_ref, cw_ref, cb_ref, bif_ref,
                 kx_ref,
                 qm_ref, km_ref, vm_ref, ga_ref, gates_ref, qbt_ref, kaug_ref, kmean_ref, vbt_ref, gbt_ref,
                 ext_ref):
    j = pl.program_id(1)
    tm = x_ref.shape[1]

    @pl.when(j == 0)
    def _():
        ext_ref[0:8, :] = jnp.zeros((8, 2 * D_MLSTM), F32)

    x = x_ref[0]
    ms = jnp.mean(x * x, axis=-1, keepdims=True)
    h = x * lax.rsqrt(ms + EPS) * gn_ref[...] * (1.0 + scale_ref[0]) + shift_ref[0]
    hb = h.astype(BF16)

    ifc = jnp.dot(hb, wif_ref[...], preferred_element_type=F32) + bif_ref[...]
    ift = jnp.transpose(ifc)[0:16, :]
    log_i = ift[0:8]
    f_pre = ift[8:16]
    log_f = jnp.minimum(f_pre, 0.0) - jnp.log1p(jnp.exp(-jnp.abs(f_pre)))
    cum_f = _chunk_scan(log_f, jnp.add, 0.0)
    g = log_i - cum_f
    gates_ref[0, 0:8, :] = cum_f
    gates_ref[0, 8:16, :] = g
    gates_ref[0, 16:24, :] = _chunk_scan(g, jnp.maximum, -jnp.inf)

    def nn(lo, hi):
        return jnp.dot(hb, wnn_ref[:, lo:hi], preferred_element_type=F32)

    def nt(lo, hi):
        return _dot_nt(wnt_ref[lo:hi, :], hb)

    cw = PROJ_CHUNK

    def conv_chunk(lo):
        cs = slice(lo, lo + cw)
        qk = nn(lo, lo + cw)
        ext_ref[8:8 + tm, cs] = qk
        conv = cb_ref[:, cs] + qk * cw_ref[CONV_K - 1:CONV_K, cs]
        for tap in range(CONV_K - 1):
            conv = conv + ext_ref[pl.ds(8 - (CONV_K - 1) + tap, tm), cs] * cw_ref[tap:tap + 1, cs]
        ext_ref[0:8, cs] = qk[tm - 8:tm, :]
        act = _silu(conv)
        if lo < D_MLSTM:
            qm_ref[0, :, lo:lo + cw] = act.astype(BF16)
        else:
            km_ref[0, :, lo - D_MLSTM:lo - D_MLSTM + cw] = (act * (HEAD_DIM_MLSTM ** -0.5)).astype(BF16)

    def value_chunk(lo):
        vm_ref[0, :, lo:lo + cw] = nn(1024 + lo, 1024 + lo + cw).astype(BF16)

    def gate_chunk(lo):
        ga_ref[0, :, lo:lo + cw] = (jax.nn.sigmoid(nn(1536 + lo, 1536 + lo + cw))
                                   * _silu(nn(2048 + lo, 2048 + lo + cw))).astype(BF16)

    lane = lax.broadcasted_iota(jnp.int32, (tm, 128), 1)

    def key_chunk(lo):
        kb = nn(2560 + lo, 2560 + lo + cw)
        kmean_ref[0, pl.ds(j, 1), lo:lo + cw] = jnp.mean(kb, axis=0, keepdims=True)
        kb2 = (kb * LOG2E).astype(BF16)
        for pr in range(lo // 128, (lo + cw) // 128):
            for hh in range(2):
                hd = 2 * pr + hh
                is_key = (lane < HEAD_DIM_MOBA) if hh == 0 else (lane >= HEAD_DIM_MOBA)
                grp = slice(128 * hd, 128 * (hd + 1))
                kaug_ref[0, :, grp] = jnp.where(is_key, kb2[:, 128 * pr - lo:128 * (pr + 1) - lo],
                                                kx_ref[0, :, grp])

    def qt_chunk(lo):
        qbt_ref[0, lo:lo + cw, :] = (nt(lo, lo + cw) * (HEAD_DIM_MOBA ** -0.5)).astype(BF16)

    def vt_chunk(lo):
        vbt_ref[0, 0, lo:lo + cw, :] = nt(512 + lo, 512 + lo + cw).astype(BF16)

    def gt_chunk(lo):
        gbt_ref[0, lo:lo + cw, :] = _silu(nt(1024 + lo, 1024 + lo + cw)).astype(BF16)

    heavy = ([(conv_chunk, lo) for lo in range(0, 2 * D_MLSTM, cw)]
             + [(gate_chunk, lo) for lo in range(0, D_MLSTM, cw)]
             + [(key_chunk, lo) for lo in range(0, D_MOBA, cw)])
    light = ([(qt_chunk, lo) for lo in range(0, D_MOBA, cw)]
             + [(vt_chunk, lo) for lo in range(0, D_MOBA, cw)]
             + [(value_chunk, lo) for lo in range(0, D_MLSTM, cw)]
             + [(gt_chunk, lo) for lo in range(0, D_MOBA, cw)])
    for idx in range(max(len(heavy), len(light))):
        for group in (heavy, light):
            if idx < len(group):
                fn, lo = group[idx]
                fn(lo)


def _proj(x, mod, g_norm, wnn, wnt, wif, conv_w, conv_b, bif):
    b, s, d = x.shape
    tm = ROW_TILE
    nb = s // MOBA_BLOCK
    assert tm == MOBA_BLOCK
    kx = _moba_key_extras(nb)
    kw = N_HEADS_MOBA * 128
    row = lambda bi, j: (bi, j, 0)
    col = lambda bi, j: (bi, 0, j)
    full2 = lambda bi, j: (0, 0)
    out_shapes = (
        jax.ShapeDtypeStruct((b, s, D_MLSTM), BF16),
        jax.ShapeDtypeStruct((b, s, D_MLSTM), BF16),
        jax.ShapeDtypeStruct((b, s, D_MLSTM), BF16),
        jax.ShapeDtypeStruct((b, s, D_MLSTM), BF16),
        jax.ShapeDtypeStruct((b, 24, s), F32),
        jax.ShapeDtypeStruct((b, D_MOBA, s), BF16),
        jax.ShapeDtypeStruct((b, s, kw), BF16),
        jax.ShapeDtypeStruct((b, nb, D_MOBA), F32),
        jax.ShapeDtypeStruct((b, nb, D_MOBA, MOBA_BLOCK), BF16),
        jax.ShapeDtypeStruct((b, D_MOBA, s), BF16),
    )
    out_specs = (
        pl.BlockSpec((1, tm, D_MLSTM), row),
        pl.BlockSpec((1, tm, D_MLSTM), row),
        pl.BlockSpec((1, tm, D_MLSTM), row),
        pl.BlockSpec((1, tm, D_MLSTM), row),
        pl.BlockSpec((1, 24, tm), col),
        pl.BlockSpec((1, D_MOBA, tm), col),
        pl.BlockSpec((1, tm, kw), row),
        pl.BlockSpec((1, nb, D_MOBA), lambda bi, j: (bi, 0, 0)),
        pl.BlockSpec((1, 1, D_MOBA, tm), lambda bi, j: (bi, j, 0, 0)),
        pl.BlockSpec((1, D_MOBA, tm), col),
    )
    in_specs = [
        pl.BlockSpec((1, tm, d), row),
        pl.BlockSpec((1, 1, d), lambda bi, j: (bi, 0, 0)),
        pl.BlockSpec((1, 1, d), lambda bi, j: (bi, 0, 1)),
        pl.BlockSpec((1, d), full2),
        pl.BlockSpec(wnn.shape, full2),
        pl.BlockSpec(wnt.shape, full2),
        pl.BlockSpec(wif.shape, full2),
        pl.BlockSpec(conv_w.shape, full2),
        pl.BlockSpec((1, 2 * D_MLSTM), full2),
        pl.BlockSpec((1, 128), full2),
        pl.BlockSpec((1, MOBA_BLOCK, kw), lambda bi, j: (j, 0, 0)),
    ]
    return pl.pallas_call(
        _proj_kernel,
        grid=(b, s // tm),
        in_specs=in_specs,
        out_specs=out_specs,
        out_shape=out_shapes,
        scratch_shapes=[pltpu.VMEM((tm + 8, 2 * D_MLSTM), F32)],
        compiler_params=pltpu.CompilerParams(dimension_semantics=("arbitrary", "arbitrary"),
                                             vmem_limit_bytes=VMEM_LIMIT),
        name="proj",
    )(x, mod, mod, g_norm.reshape(1, d), wnn, wnt, wif, conv_w, conv_b.reshape(1, -1), bif, kx)


def _col_bcast(row, n):
    return jnp.transpose(jnp.broadcast_to(row, (n, n)))


def _mlstm_kernel(q_ref, k_ref, v_ref, ga_ref, gates_ref, gh_ref, o_ref, st_ref, m_ref):
    ci = pl.program_id(1)
    L = MLSTM_CHUNK
    dh = HEAD_DIM_MLSTM
    assert L == dh == 128

    @pl.when(ci == 0)
    def _():
        st_ref[...] = jnp.zeros(st_ref.shape, F32)
        m_ref[...] = jnp.zeros(m_ref.shape, F32)

    t_idx = lax.broadcasted_iota(jnp.int32, (L, L), 0)
    s_idx = lax.broadcasted_iota(jnp.int32, (L, L), 1)
    causal = s_idx <= t_idx
    ones_blk = jnp.ones((L, 128), BF16)

    m_prev = m_ref[:, 0:1]
    st = [st_ref[hd] for hd in range(N_HEADS_MLSTM)]
    for c in range(MLSTM_STEP_CHUNKS):
        ts = slice(c * L, (c + 1) * L)
        cum_f = gates_ref[0, 0:8, ts]
        g = gates_ref[0, 8:16, ts]
        g_cmax = gates_ref[0, 16:24, ts]
        f_tot = cum_f[:, L - 1:L]
        g_max = g_cmax[:, L - 1:L]
        a_max = f_tot + g_max
        w = jnp.exp(g - g_max)
        a_row = -jnp.maximum(m_prev, g_cmax)
        x_row = a_row - cum_f
        m_new = jnp.maximum(f_tot + m_prev, a_max)
        s_old = jnp.exp(f_tot + m_prev - m_new)
        s_new = jnp.exp(a_max - m_new)

        for hd in range(N_HEADS_MLSTM):
            sl = slice(hd * dh, (hd + 1) * dh)
            q = q_ref[0, ts, sl]
            k = k_ref[0, ts, sl]
            v = v_ref[0, ts, sl]
            st_prev = st[hd]

            a_col = _col_bcast(a_row[hd:hd + 1, :], L)
            x_col = _col_bcast(x_row[hd:hd + 1, :], L)
            w_col = _col_bcast(w[hd:hd + 1, :], L)
            inter = jnp.exp(a_col + m_prev[hd:hd + 1, :])
            floor = jnp.exp(x_col)

            decay = jnp.exp(jnp.where(causal, a_col + g[hd:hd + 1, :], -jnp.inf))
            s_qk = _dot_nt(q, k) * decay
            intra = jnp.dot(s_qk.astype(BF16), jnp.concatenate([v, ones_blk], axis=1),
                            preferred_element_type=F32)
            carried = jnp.dot(q, st_prev.astype(BF16), preferred_element_type=F32)
            num = intra[:, 0:dh] + inter * carried[:, 0:dh]
            den = intra[:, dh:] + inter * carried[:, dh:]
            hh = num / jnp.maximum(jnp.abs(den), floor)
            hh = hh * lax.rsqrt(jnp.mean(hh * hh, axis=-1, keepdims=True) + EPS)
            o_ref[0, ts, sl] = (hh * gh_ref[:, sl] * ga_ref[0, ts, sl].astype(F32)).astype(BF16)

            vw = jnp.concatenate([(v.astype(F32) * w_col).astype(BF16), w_col.astype(BF16)], axis=1)
            st_loc = _dot_tn(k, vw)
            st[hd] = s_old[hd:hd + 1, :] * st_prev + s_new[hd:hd + 1, :] * st_loc
        m_prev = m_new

    for hd in range(N_HEADS_MLSTM):
        st_ref[hd] = st[hd]
    m_ref[...] = jnp.broadcast_to(m_prev, m_ref.shape)


def _mlstm(qm, km, vm, ga, gates, g_head):
    b, s, dm = qm.shape
    L = MLSTM_CHUNK * MLSTM_STEP_CHUNKS
    row = lambda bi, ci: (bi, ci, 0)
    return pl.pallas_call(
        _mlstm_kernel,
        grid=(b, s // L),
        in_specs=[pl.BlockSpec((1, L, dm), row)] * 4
        + [pl.BlockSpec((1, 24, L), lambda bi, ci: (bi, 0, ci)),
           pl.BlockSpec((1, dm), lambda bi, ci: (0, 0))],
        out_specs=pl.BlockSpec((1, L, dm), row),
        out_shape=jax.ShapeDtypeStruct((b, s, dm), BF16),
        scratch_shapes=[pltpu.VMEM((N_HEADS_MLSTM, HEAD_DIM_MLSTM, HEAD_DIM_MLSTM + 128), F32),
                        pltpu.VMEM((8, 128), F32)],
        compiler_params=pltpu.CompilerParams(dimension_semantics=("arbitrary", "arbitrary"),
                                             vmem_limit_bytes=VMEM_LIMIT),
        name="mlstm",
    )(qm, km, vm, ga, gates, g_head.reshape(1, dm))


def _moba_steps(nb):
    jt, it = [], []
    for j in range(nb):
        for i in [j] + list(range(j)):
            jt.append(j)
            it.append(i)
    return jt, it


def _moba_key_extras(nb):
    blk = MOBA_BLOCK
    lane = np.arange(N_HEADS_MOBA * 128)
    head = lane // 128
    w = (lane % 128 - np.where(head % 2 == 0, HEAD_DIM_MOBA, 0))[None, None, :]
    slope = (2.0 ** -(head + 1.0))[None, None, :]
    i = np.arange(nb)[:, None, None]
    c = np.arange(blk)[None, :, None]
    tab = np.where(w == i, 1.0, 0.0)
    rest = LOG2E * slope * (i * blk + c)
    for term in range(N_POS_TERMS):
        piece = rest.astype(np.float32).astype(BF16).astype(np.float64)
        tab = tab + np.where(w == nb + term, piece, 0.0)
        rest = rest - piece
    return jnp.asarray(tab.astype(np.float32)).astype(BF16)


def _moba_kernel(jt_ref, it_ref, qt_ref, k_ref, vt_ref, kmean_ref, gt_ref, o_ref,
                 qaug_ref, m_ref, acc_ref, s_ref, cmax_ref):
    t = pl.program_id(1)
    j = jt_ref[t]
    i = it_ref[t]
    blk = MOBA_BLOCK
    dh = HEAD_DIM_MOBA
    nb = kmean_ref.shape[1]
    first = i == j
    last = jnp.logical_or(i == j - 1, j == 0)

    def key_tile(h):
        return k_ref[0, :, 128 * h:128 * (h + 1)]

    def rows(h):
        return slice(dh * h, dh * (h + 1))

    ones_tail = jnp.ones((16, blk), BF16)

    def values_t(h):
        return jnp.concatenate([vt_ref[0, 0, rows(h), :], ones_tail], axis=0)

    @pl.when(first)
    def _():
        blk_i = lax.broadcasted_iota(jnp.int32, (nb, blk), 0)
        lane_m = lax.broadcasted_iota(jnp.int32, (nb, 2 * dh), 1)
        key_c = lax.broadcasted_iota(jnp.int32, (blk, blk), 0)
        qry_r = lax.broadcasted_iota(jnp.int32, (blk, blk), 1)
        causal = key_c <= qry_r
        ones_rows = jnp.where(blk_i < N_POS_TERMS, 1.0, 0.0)
        for h in range(N_HEADS_MOBA):
            pr, hh = divmod(h, 2)
            qt_pair = qt_ref[0, 128 * pr:128 * (pr + 1), :]
            in_head = (lane_m < dh) if hh == 0 else (lane_m >= dh)
            km = jnp.where(in_head, kmean_ref[0, :, 128 * pr:128 * (pr + 1)], 0.0).astype(BF16)
            gate = jnp.dot(km, qt_pair, preferred_element_type=F32)
            rank = jnp.zeros((nb, blk), jnp.int32)
            for i2 in range(nb):
                g2 = gate[i2:i2 + 1, :]
                beats = (g2 > gate) | ((g2 == gate) & (i2 < blk_i))
                rank = rank + jnp.where(beats & (i2 < j), 1, 0)
            keep = ((blk_i < j) & (rank < MOBA_TOPK)) | (blk_i == j)
            sel_bias = jnp.where(keep, 0.0, MASK_BIAS)
            extra = jnp.concatenate([sel_bias, ones_rows, jnp.zeros((dh - 16, blk), F32)], axis=0)
            extra = extra.astype(BF16)
            qt_h = qt_pair[dh * hh:dh * (hh + 1)]
            qaug = jnp.concatenate([qt_h, extra] if hh == 0 else [extra, qt_h], axis=0)
            qaug_ref[h] = qaug

            s = jnp.dot(key_tile(h), qaug, preferred_element_type=F32)
            s = jnp.where(causal, s, MASK_BIAS)
            s_ref[h] = s
            m_ref[h:h + 1, :] = jnp.max(s, axis=0, keepdims=True)

        for h in range(N_HEADS_MOBA):
            p0 = jnp.exp2(s_ref[h] - m_ref[h:h + 1, :])
            acc_ref[h] = jnp.dot(values_t(h), p0.astype(BF16), preferred_element_type=F32)

    @pl.when(jnp.logical_not(first))
    def _():
        for h in range(N_HEADS_MOBA):
            s = jnp.dot(key_tile(h), qaug_ref[h], preferred_element_type=F32)
            s_ref[h] = s
            cmax_ref[h:h + 1, :] = jnp.max(s, axis=0, keepdims=True)
        for h in range(N_HEADS_MOBA):
            m_run = m_ref[h:h + 1, :]
            m_new = jnp.maximum(m_run, cmax_ref[h:h + 1, :])
            alpha = jnp.exp2(m_run - m_new)
            pr_ = jnp.exp2(s_ref[h] - m_new)
            m_ref[h:h + 1, :] = m_new
            acc_ref[h] = alpha * acc_ref[h] + jnp.dot(values_t(h), pr_.astype(BF16),
                                                      preferred_element_type=F32)

    @pl.when(last)
    def _():
        for h in range(N_HEADS_MOBA):
            acc = acc_ref[h]
            out = acc[0:dh] / acc[dh:dh + 1] * gt_ref[0, rows(h), :].astype(F32)
            o_ref[0, rows(h), :] = out.astype(BF16)


def _moba(qbt, kaug, vbt, kmean, gbt):
    b, dmb, s = qbt.shape
    blk = MOBA_BLOCK
    nb = s // blk
    kw = kaug.shape[2]
    assert nb == 8, "selection extras assume 8 key blocks"
    jt, it = _moba_steps(nb)
    qmap = lambda bi, t, jt_r, it_r: (bi, 0, jt_r[t])
    grid_spec = pltpu.PrefetchScalarGridSpec(
        num_scalar_prefetch=2,
        grid=(b, len(jt)),
        in_specs=[pl.BlockSpec((1, dmb, blk), qmap),
                  pl.BlockSpec((1, blk, kw), lambda bi, t, jt_r, it_r: (bi, it_r[t], 0)),
                  pl.BlockSpec((1, 1, dmb, blk), lambda bi, t, jt_r, it_r: (bi, it_r[t], 0, 0)),
                  pl.BlockSpec((1, nb, dmb), lambda bi, t, jt_r, it_r: (bi, 0, 0)),
                  pl.BlockSpec((1, dmb, blk), qmap)],
        out_specs=pl.BlockSpec((1, dmb, blk), qmap),
        scratch_shapes=[pltpu.VMEM((N_HEADS_MOBA, 2 * HEAD_DIM_MOBA, blk), BF16),
                        pltpu.VMEM((N_HEADS_MOBA, blk), F32),
                        pltpu.VMEM((N_HEADS_MOBA, HEAD_DIM_MOBA + 16, blk), F32),
                        pltpu.VMEM((N_HEADS_MOBA, blk, blk), F32),
                        pltpu.VMEM((N_HEADS_MOBA, blk), F32)],
    )
    return pl.pallas_call(
        _moba_kernel,
        grid_spec=grid_spec,
        out_shape=jax.ShapeDtypeStruct((b, dmb, s), BF16),
        compiler_params=pltpu.CompilerParams(dimension_semantics=("arbitrary", "arbitrary"),
                                             vmem_limit_bytes=VMEM_LIMIT),
        name="moba",
    )(jnp.asarray(jt, jnp.int32), jnp.asarray(it, jnp.int32), qbt, kaug, vbt, kmean, gbt)


def _out_kernel(x_ref, gate_ref, oa_ref, obt_ref, wa_ref, wb_ref, gf_ref, o_ref):
    y = (jnp.dot(oa_ref[0], wa_ref[...], preferred_element_type=F32)
         + _dot_tn(obt_ref[0], wb_ref[...]))
    r = x_ref[0] + gate_ref[0] * y
    o_ref[0] = r * lax.rsqrt(jnp.mean(r * r, axis=-1, keepdims=True) + EPS) * gf_ref[...]


def _out(x, mod, out_a, out_bt, w_a, w_b, g_final):
    b, s, d = x.shape
    tm = ROW_TILE
    row = lambda bi, j: (bi, j, 0)
    full2 = lambda bi, j: (0, 0)
    return pl.pallas_call(
        _out_kernel,
        grid=(b, s // tm),
        in_specs=[pl.BlockSpec((1, tm, d), row),
                  pl.BlockSpec((1, 1, d), lambda bi, j: (bi, 0, 2)),
                  pl.BlockSpec((1, tm, D_MLSTM), row),
                  pl.BlockSpec((1, D_MOBA, tm), lambda bi, j: (bi, 0, j)),
                  pl.BlockSpec(w_a.shape, full2),
                  pl.BlockSpec(w_b.shape, full2),
                  pl.BlockSpec((1, d), full2)],
        out_specs=pl.BlockSpec((1, tm, d), row),
        out_shape=jax.ShapeDtypeStruct((b, s, d), x.dtype),
        compiler_params=pltpu.CompilerParams(dimension_semantics=("arbitrary", "arbitrary"),
                                             vmem_limit_bytes=VMEM_LIMIT),
        name="out",
    )(x, mod, out_a, out_bt, w_a, w_b, g_final.reshape(1, d))


def _layer(x, c, w_ada, b_ada, g_norm, w_in, conv_w, conv_b, b_igate, b_fgate, g_mlstm_head, w_out,
           g_final):
    b = x.shape[0]
    mod = _ada(c, w_ada, b_ada).reshape(b, 1, 3 * D_MODEL)

    cols = lambda name: w_in[:, _OFF[name][0]:_OFF[name][1]]
    wnn = jnp.concatenate([cols("qm"), cols("km"), cols("vm"), cols("om"), cols("zm"), cols("kb")],
                          axis=1).astype(BF16)
    wnt = jnp.concatenate([cols("qb"), cols("vb"), cols("zb")], axis=1).T.astype(BF16)
    zpad = jnp.zeros((D_MODEL, 4), w_in.dtype)
    wif = jnp.concatenate([cols("im"), zpad, cols("fm"), jnp.zeros((D_MODEL, 116), w_in.dtype)],
                          axis=1).astype(BF16)
    bif = jnp.concatenate([b_igate.astype(F32), jnp.zeros((4,), F32), b_fgate.astype(F32),
                           jnp.zeros((116,), F32)]).reshape(1, 128)

    qm, km, vm, ga, gates, qbt, kaug, kmean, vbt, gbt = _proj(
        x, mod, g_norm, wnn, wnt, wif, conv_w, conv_b, bif)
    out_a = _mlstm(qm, km, vm, ga, gates, g_mlstm_head)
    out_bt = _moba(qbt, kaug, vbt, kmean, gbt)
    w_a = w_out[:D_MLSTM].astype(BF16)
    w_b = w_out[D_MLSTM:].astype(BF16)
    return _out(x, mod, out_a, out_bt, w_a, w_b, g_final)


def kernel(x, c, w_ada, b_ada, g_norm, w_in, conv_w, conv_b, b_igate, b_fgate, g_mlstm_head, w_out,
           g_final):
    assert w_ada.shape[0] == 1, "single-layer trunk"
    return _layer(x, c, w_ada[0], b_ada[0], g_norm[0], w_in[0], conv_w[0], conv_b[0], b_igate[0],
                  b_fgate[0], g_mlstm_head[0], w_out[0], g_final)
```

```python
import jax
import jax.numpy as jnp
import numpy as np
from jax import lax
from jax.experimental import pallas as pl
from jax.experimental.pallas import tpu as pltpu

F32 = jnp.float32
BF16 = jnp.bfloat16

D_MODEL = 1024
D_MLSTM = 512
N_HEADS_MLSTM = 4
HEAD_DIM_MLSTM = 128
D_MOBA = 512
N_HEADS_MOBA = 8
HEAD_DIM_MOBA = 64
CONV_K = 4
MOBA_BLOCK = 256
MOBA_TOPK = 3
EPS = 1e-6

ROW_TILE = 256
PROJ_CHUNK = 256
MLSTM_CHUNK = 128
MLSTM_STEP_CHUNKS = 4
MOBA_BATCH = 2
OUT_TILE = 512
MASK_BIAS = -1e30
LOG2E = 1.4426950408889634
N_POS_TERMS = 3
VMEM_LIMIT = 56 * 1024 * 1024

_OFF = {}
_o = 0
for _name, _size in (("qm", 512), ("km", 512), ("vm", 512), ("om", 512), ("im", 4), ("fm", 4),
                     ("zm", 512), ("qb", 512), ("kb", 512), ("vb", 512), ("zb", 512)):
    _OFF[_name] = (_o, _o + _size)
    _o += _size


def _silu(v):
    return v * jax.nn.sigmoid(v)


def _dot_nt(a, b):
    return lax.dot_general(a, b, (((1,), (1,)), ((), ())), preferred_element_type=F32)


def _dot_tn(a, b):
    return lax.dot_general(a, b, (((0,), (0,)), ((), ())), preferred_element_type=F32)


def _ada_kernel(c_ref, w_ref, b_ref, o_ref):
    a = _silu(c_ref[...])
    o_ref[...] = jnp.dot(a, w_ref[...], preferred_element_type=F32,
                         precision=lax.Precision.HIGHEST) + b_ref[...]


def _ada(c, w_ada, b_ada):
    b, d = c.shape
    n = w_ada.shape[1]
    tn = 1024
    return pl.pallas_call(
        _ada_kernel,
        grid=(n // tn,),
        in_specs=[pl.BlockSpec((b, d), lambda i: (0, 0)),
                  pl.BlockSpec((d, tn), lambda i: (0, i)),
                  pl.BlockSpec((1, tn), lambda i: (0, i))],
        out_specs=pl.BlockSpec((b, tn), lambda i: (0, i)),
        out_shape=jax.ShapeDtypeStruct((b, n), F32),
        compiler_params=pltpu.CompilerParams(dimension_semantics=("arbitrary",),
                                             vmem_limit_bytes=VMEM_LIMIT),
        name="ada",
    )(c, w_ada, b_ada.reshape(1, n))


def _chunk_scan(v, combine, fill):
    pos = lax.broadcasted_iota(jnp.int32, v.shape, 1) & (MLSTM_CHUNK - 1)
    d = 1
    while d < MLSTM_CHUNK:
        shifted = pltpu.roll(v, d, axis=1)
        v = combine(v, jnp.where(pos >= d, shifted, fill))
        d *= 2
    return v


def _proj_kernel(x_ref, shift_ref, scale_ref, gn_ref, wnn_ref, wnt_ref, wif_ref, cw_ref, cb_ref, bif_ref,
                 kx_ref,
                 qm_ref, km_ref, vm_ref, ga_ref, gates_ref, qbt_ref, kaug_ref, kmean_ref, vbt_ref, gbt_ref,
                 ext_ref):
    j = pl.program_id(1)
    tm = x_ref.shape[1]

    @pl.when(j == 0)
    def _():
        ext_ref[0:8, :] = jnp.zeros((8, 2 * D_MLSTM), F32)

    x = x_ref[0]
    ms = jnp.mean(x * x, axis=-1, keepdims=True)
    h = x * lax.rsqrt(ms + EPS) * gn_ref[...] * (1.0 + scale_ref[0]) + shift_ref[0]
    hb = h.astype(BF16)

    ifc = jnp.dot(hb, wif_ref[...], preferred_element_type=F32) + bif_ref[...]
    ift = jnp.transpose(ifc)[0:16, :]
    log_i = ift[0:8]
    f_pre = ift[8:16]
    log_f = jnp.minimum(f_pre, 0.0) - jnp.log1p(jnp.exp(-jnp.abs(f_pre)))
    cum_f = _chunk_scan(log_f, jnp.add, 0.0)
    g = log_i - cum_f
    gates_ref[0, 0:8, :] = cum_f
    gates_ref[0, 8:16, :] = g
    gates_ref[0, 16:24, :] = _chunk_scan(g, jnp.maximum, -jnp.inf)

    def nn(lo, hi):
        return jnp.dot(hb, wnn_ref[:, lo:hi], preferred_element_type=F32)

    def nt(lo, hi):
        return _dot_nt(wnt_ref[lo:hi, :], hb)

    cw = PROJ_CHUNK

    def conv_chunk(lo):
        cs = slice(lo, lo + cw)
        qk = nn(lo, lo + cw)
        ext_ref[8:8 + tm, cs] = qk
        conv = cb_ref[:, cs] + qk * cw_ref[CONV_K - 1:CONV_K, cs]
        for tap in range(CONV_K - 1):
            conv = conv + ext_ref[pl.ds(8 - (CONV_K - 1) + tap, tm), cs] * cw_ref[tap:tap + 1, cs]
        ext_ref[0:8, cs] = qk[tm - 8:tm, :]
        act = _silu(conv)
        if lo < D_MLSTM:
            qm_ref[0, :, lo:lo + cw] = act.astype(BF16)
        else:
            km_ref[0, :, lo - D_MLSTM:lo - D_MLSTM + cw] = (act * (HEAD_DIM_MLSTM ** -0.5)).astype(BF16)

    def value_chunk(lo):
        vm_ref[0, :, lo:lo + cw] = nn(1024 + lo, 1024 + lo + cw).astype(BF16)

    def gate_chunk(lo):
        ga_ref[0, :, lo:lo + cw] = (jax.nn.sigmoid(nn(1536 + lo, 1536 + lo + cw))
                                   * _silu(nn(2048 + lo, 2048 + lo + cw))).astype(BF16)

    lane = lax.broadcasted_iota(jnp.int32, (tm, 128), 1)

    def key_chunk(lo):
        kb = nn(2560 + lo, 2560 + lo + cw)
        kmean_ref[0, pl.ds(j, 1), lo:lo + cw] = jnp.mean(kb, axis=0, keepdims=True)
        kb2 = (kb * LOG2E).astype(BF16)
        for pr in range(lo // 128, (lo + cw) // 128):
            for hh in range(2):
                hd = 2 * pr + hh
                is_key = (lane < HEAD_DIM_MOBA) if hh == 0 else (lane >= HEAD_DIM_MOBA)
                grp = slice(128 * hd, 128 * (hd + 1))
                kaug_ref[0, :, grp] = jnp.where(is_key, kb2[:, 128 * pr - lo:128 * (pr + 1) - lo],
                                                kx_ref[0, :, grp])

    def qt_chunk(lo):
        qbt_ref[0, lo:lo + cw, :] = (nt(lo, lo + cw) * (HEAD_DIM_MOBA ** -0.5)).astype(BF16)

    def vt_chunk(lo):
        vbt_ref[0, 0, lo:lo + cw, :] = nt(512 + lo, 512 + lo + cw).astype(BF16)

    def gt_chunk(lo):
        gbt_ref[0, lo:lo + cw, :] = _silu(nt(1024 + lo, 1024 + lo + cw)).astype(BF16)

    heavy = ([(conv_chunk, lo) for lo in range(0, 2 * D_MLSTM, cw)]
             + [(gate_chunk, lo) for lo in range(0, D_MLSTM, cw)]
             + [(key_chunk, lo) for lo in range(0, D_MOBA, cw)])
    light = ([(qt_chunk, lo) for lo in range(0, D_MOBA, cw)]
             + [(vt_chunk, lo) for lo in range(0, D_MOBA, cw)]
             + [(value_chunk, lo) for lo in range(0, D_MLSTM, cw)]
             + [(gt_chunk, lo) for lo in range(0, D_MOBA, cw)])
    for idx in range(max(len(heavy), len(light))):
        for group in (heavy, light):
            if idx < len(group):
                fn, lo = group[idx]
                fn(lo)


def _proj(x, mod, g_norm, wnn, wnt, wif, conv_w, conv_b, bif):
    b, s, d = x.shape
    tm = ROW_TILE
    nb = s // MOBA_BLOCK
    assert tm == MOBA_BLOCK
    kx = _moba_key_extras(nb)
    kw = N_HEADS_MOBA * 128
    row = lambda bi, j: (bi, j, 0)
    col = lambda bi, j: (bi, 0, j)
    full2 = lambda bi, j: (0, 0)
    out_shapes = (
        jax.ShapeDtypeStruct((b, s, D_MLSTM), BF16),
        jax.ShapeDtypeStruct((b, s, D_MLSTM), BF16),
        jax.ShapeDtypeStruct((b, s, D_MLSTM), BF16),
        jax.ShapeDtypeStruct((b, s, D_MLSTM), BF16),
        jax.ShapeDtypeStruct((b, 24, s), F32),
        jax.ShapeDtypeStruct((b, D_MOBA, s), BF16),
        jax.ShapeDtypeStruct((b, s, kw), BF16),
        jax.ShapeDtypeStruct((b, nb, D_MOBA), F32),
        jax.ShapeDtypeStruct((b, nb, D_MOBA, MOBA_BLOCK), BF16),
        jax.ShapeDtypeStruct((b, D_MOBA, s), BF16),
    )
    out_specs = (
        pl.BlockSpec((1, tm, D_MLSTM), row),
        pl.BlockSpec((1, tm, D_MLSTM), row),
        pl.BlockSpec((1, tm, D_MLSTM), row),
        pl.BlockSpec((1, tm, D_MLSTM), row),
        pl.BlockSpec((1, 24, tm), col),
        pl.BlockSpec((1, D_MOBA, tm), col),
        pl.BlockSpec((1, tm, kw), row),
        pl.BlockSpec((1, nb, D_MOBA), lambda bi, j: (bi, 0, 0)),
        pl.BlockSpec((1, 1, D_MOBA, tm), lambda bi, j: (bi, j, 0, 0)),
        pl.BlockSpec((1, D_MOBA, tm), col),
    )
    in_specs = [
        pl.BlockSpec((1, tm, d), row),
        pl.BlockSpec((1, 1, d), lambda bi, j: (bi, 0, 0)),
        pl.BlockSpec((1, 1, d), lambda bi, j: (bi, 0, 1)),
        pl.BlockSpec((1, d), full2),
        pl.BlockSpec(wnn.shape, full2),
        pl.BlockSpec(wnt.shape, full2),
        pl.BlockSpec(wif.shape, full2),
        pl.BlockSpec(conv_w.shape, full2),
        pl.BlockSpec((1, 2 * D_MLSTM), full2),
        pl.BlockSpec((1, 128), full2),
        pl.BlockSpec((1, MOBA_BLOCK, kw), lambda bi, j: (j, 0, 0)),
    ]
    return pl.pallas_call(
        _proj_kernel,
        grid=(b, s // tm),
        in_specs=in_specs,
        out_specs=out_specs,
        out_shape=out_shapes,
        scratch_shapes=[pltpu.VMEM((tm + 8, 2 * D_MLSTM), F32)],
        compiler_params=pltpu.CompilerParams(dimension_semantics=("arbitrary", "arbitrary"),
                                             vmem_limit_bytes=VMEM_LIMIT),
        name="proj",
    )(x, mod, mod, g_norm.reshape(1, d), wnn, wnt, wif, conv_w, conv_b.reshape(1, -1), bif, kx)


def _col_bcast(row, n):
    return jnp.transpose(jnp.broadcast_to(row, (n, n)))


def _mlstm_kernel(q_ref, k_ref, v_ref, ga_ref, gates_ref, gh_ref, o_ref, st_ref, m_ref):
    ci = pl.program_id(1)
    L = MLSTM_CHUNK
    dh = HEAD_DIM_MLSTM
    assert L == dh == 128

    @pl.when(ci == 0)
    def _():
        st_ref[...] = jnp.zeros(st_ref.shape, F32)
        m_ref[...] = jnp.zeros(m_ref.shape, F32)

    t_idx = lax.broadcasted_iota(jnp.int32, (L, L), 0)
    s_idx = lax.broadcasted_iota(jnp.int32, (L, L), 1)
    causal = s_idx <= t_idx
    ones_blk = jnp.ones((L, 128), BF16)

    m_prev = m_ref[:, 0:1]
    st = [st_ref[hd] for hd in range(N_HEADS_MLSTM)]
    for c in range(MLSTM_STEP_CHUNKS):
        ts = slice(c * L, (c + 1) * L)
        cum_f = gates_ref[0, 0:8, ts]
        g = gates_ref[0, 8:16, ts]
        g_cmax = gates_ref[0, 16:24, ts]
        f_tot = cum_f[:, L - 1:L]
        g_max = g_cmax[:, L - 1:L]
        a_max = f_tot + g_max
        w = jnp.exp(g - g_max)
        a_row = -jnp.maximum(m_prev, g_cmax)
        x_row = a_row - cum_f
        m_new = jnp.maximum(f_tot + m_prev, a_max)
        s_old = jnp.exp(f_tot + m_prev - m_new)
        s_new = jnp.exp(a_max - m_new)

        for hd in range(N_HEADS_MLSTM):
            sl = slice(hd * dh, (hd + 1) * dh)
            q = q_ref[0, ts, sl]
            k = k_ref[0, ts, sl]
            v = v_ref[0, ts, sl]
            st_prev = st[hd]

            a_col = _col_bcast(a_row[hd:hd + 1, :], L)
            x_col = _col_bcast(x_row[hd:hd + 1, :], L)
            w_col = _col_bcast(w[hd:hd + 1, :], L)
            inter = jnp.exp(a_col + m_prev[hd:hd + 1, :])
            floor = jnp.exp(x_col)

            decay = jnp.exp(jnp.where(causal, a_col + g[hd:hd + 1, :], -jnp.inf))
            s_qk = _dot_nt(q, k) * decay
            intra = jnp.dot(s_qk.astype(BF16), jnp.concatenate([v, ones_blk], axis=1),
                            preferred_element_type=F32)
            carried = jnp.dot(q, st_prev.astype(BF16), preferred_element_type=F32)
            num = intra[:, 0:dh] + inter * carried[:, 0:dh]
            den = intra[:, dh:] + inter * carried[:, dh:]
            hh = num / jnp.maximum(jnp.abs(den), floor)
            hh = hh * lax.rsqrt(jnp.mean(hh * hh, axis=-1, keepdims=True) + EPS)
            o_ref[0, ts, sl] = (hh * gh_ref[:, sl] * ga_ref[0, ts, sl].astype(F32)).astype(BF16)

            vw = jnp.concatenate([(v.astype(F32) * w_col).astype(BF16), w_col.astype(BF16)], axis=1)
            st_loc = _dot_tn(k, vw)
            st[hd] = s_old[hd:hd + 1, :] * st_prev + s_new[hd:hd + 1, :] * st_loc
        m_prev = m_new

    for hd in range(N_HEADS_MLSTM):
        st_ref[hd] = st[hd]
    m_ref[...] = jnp.broadcast_to(m_prev, m_ref.shape)


def _mlstm(qm, km, vm, ga, gates, g_head):
    b, s, dm = qm.shape
    L = MLSTM_CHUNK * MLSTM_STEP_CHUNKS
    row = lambda bi, ci: (bi, ci, 0)
    return pl.pallas_call(
        _mlstm_kernel,
        grid=(b, s // L),
        in_specs=[pl.BlockSpec((1, L, dm), row)] * 4
        + [pl.BlockSpec((1, 24, L), lambda bi, ci: (bi, 0, ci)),
           pl.BlockSpec((1, dm), lambda bi, ci: (0, 0))],
        out_specs=pl.BlockSpec((1, L, dm), row),
        out_shape=jax.ShapeDtypeStruct((b, s, dm), BF16),
        scratch_shapes=[pltpu.VMEM((N_HEADS_MLSTM, HEAD_DIM_MLSTM, HEAD_DIM_MLSTM + 128), F32),
                        pltpu.VMEM((8, 128), F32)],
        compiler_params=pltpu.CompilerParams(dimension_semantics=("arbitrary", "arbitrary"),
                                             vmem_limit_bytes=VMEM_LIMIT),
        name="mlstm",
    )(qm, km, vm, ga, gates, g_head.reshape(1, dm))


def _moba_steps(nb):
    jt, it = [], []
    for j in range(nb):
        for i in [j] + list(range(j)):
            jt.append(j)
            it.append(i)
    return jt, it


def _moba_key_extras(nb):
    blk = MOBA_BLOCK
    lane = np.arange(N_HEADS_MOBA * 128)
    head = lane // 128
    w = (lane % 128 - np.where(head % 2 == 0, HEAD_DIM_MOBA, 0))[None, None, :]
    slope = (2.0 ** -(head + 1.0))[None, None, :]
    i = np.arange(nb)[:, None, None]
    c = np.arange(blk)[None, :, None]
    tab = np.where(w == i, 1.0, 0.0)
    rest = LOG2E * slope * (i * blk + c)
    for term in range(N_POS_TERMS):
        piece = rest.astype(np.float32).astype(BF16).astype(np.float64)
        tab = tab + np.where(w == nb + term, piece, 0.0)
        rest = rest - piece
    return jnp.asarray(tab.astype(np.float32)).astype(BF16)


def _moba_kernel(jt_ref, it_ref, qt_ref, k_ref, vt_ref, kmean_ref, gt_ref, o_ref,
                 qaug_ref, m_ref, acc_ref, cmax_ref, *s_refs):
    t = pl.program_id(1)
    j = jt_ref[t]
    i = it_ref[t]
    blk = MOBA_BLOCK
    dh = HEAD_DIM_MOBA
    nb = kmean_ref.shape[1]
    nbat = qt_ref.shape[0]
    first = i == j
    last = jnp.logical_or(i == j - 1, j == 0)

    def key_tile(bb, h):
        return k_ref[bb, :, 128 * h:128 * (h + 1)]

    def rows(h):
        return slice(dh * h, dh * (h + 1))

    ones_tail = jnp.ones((16, blk), BF16)

    def values_t(bb, h):
        return jnp.concatenate([vt_ref[bb, 0, rows(h), :], ones_tail], axis=0)

    @pl.when(first)
    def _():
        blk_i = lax.broadcasted_iota(jnp.int32, (nb, blk), 0)
        lane_m = lax.broadcasted_iota(jnp.int32, (nb, 2 * dh), 1)
        key_c = lax.broadcasted_iota(jnp.int32, (blk, blk), 0)
        qry_r = lax.broadcasted_iota(jnp.int32, (blk, blk), 1)
        causal = key_c <= qry_r
        ones_rows = jnp.where(blk_i < N_POS_TERMS, 1.0, 0.0)
        gates = {}
        for bb in range(nbat):
            for h in range(N_HEADS_MOBA):
                pr, hh = divmod(h, 2)
                qt_pair = qt_ref[bb, 128 * pr:128 * (pr + 1), :]
                in_head = (lane_m < dh) if hh == 0 else (lane_m >= dh)
                km = jnp.where(in_head, kmean_ref[bb, :, 128 * pr:128 * (pr + 1)], 0.0).astype(BF16)
                gates[bb, h] = jnp.dot(km, qt_pair, preferred_element_type=F32)
        for bb in range(nbat):
            for h in range(N_HEADS_MOBA):
                pr, hh = divmod(h, 2)
                gate = gates[bb, h]
                rank = jnp.zeros((nb, blk), jnp.int32)
                for i2 in range(nb):
                    g2 = gate[i2:i2 + 1, :]
                    beats = (g2 > gate) | ((g2 == gate) & (i2 < blk_i))
                    rank = rank + jnp.where(beats & (i2 < j), 1, 0)
                keep = ((blk_i < j) & (rank < MOBA_TOPK)) | (blk_i == j)
                sel_bias = jnp.where(keep, 0.0, MASK_BIAS)
                extra = jnp.concatenate([sel_bias, ones_rows, jnp.zeros((dh - 16, blk), F32)], axis=0)
                extra = extra.astype(BF16)
                qt_h = qt_ref[bb, dh * h:dh * (h + 1), :]
                qaug_ref[bb, h] = jnp.concatenate([qt_h, extra] if hh == 0 else [extra, qt_h], axis=0)

        for bb in range(nbat):
            for h in range(N_HEADS_MOBA):
                s = jnp.dot(key_tile(bb, h), qaug_ref[bb, h], preferred_element_type=F32)
                s = jnp.where(causal, s, MASK_BIAS)
                s_refs[bb][h] = s
                m_ref[bb, h:h + 1, :] = jnp.max(s, axis=0, keepdims=True)

        for bb in range(nbat):
            for h in range(N_HEADS_MOBA):
                p0 = jnp.exp2(s_refs[bb][h] - m_ref[bb, h:h + 1, :])
                acc_ref[bb, h] = jnp.dot(values_t(bb, h), p0.astype(BF16), preferred_element_type=F32)

    @pl.when(jnp.logical_not(first))
    def _():
        for bb in range(nbat):
            for h in range(N_HEADS_MOBA):
                s = jnp.dot(key_tile(bb, h), qaug_ref[bb, h], preferred_element_type=F32)
                s_refs[bb][h] = s
                cmax_ref[bb, h:h + 1, :] = jnp.max(s, axis=0, keepdims=True)
        for bb in range(nbat):
            for h in range(N_HEADS_MOBA):
                m_run = m_ref[bb, h:h + 1, :]
                m_new = jnp.maximum(m_run, cmax_ref[bb, h:h + 1, :])
                alpha = jnp.exp2(m_run - m_new)
                pr_ = jnp.exp2(s_refs[bb][h] - m_new)
                m_ref[bb, h:h + 1, :] = m_new
                acc_ref[bb, h] = alpha * acc_ref[bb, h] + jnp.dot(values_t(bb, h), pr_.astype(BF16),
                                                                  preferred_element_type=F32)

    @pl.when(last)
    def _():
        for bb in range(nbat):
            for h in range(N_HEADS_MOBA):
                acc = acc_ref[bb, h]
                out = acc[0:dh] / acc[dh:dh + 1] * gt_ref[bb, rows(h), :].astype(F32)
                o_ref[bb, rows(h), :] = out.astype(BF16)


def _moba(qbt, kaug, vbt, kmean, gbt):
    b, dmb, s = qbt.shape
    blk = MOBA_BLOCK
    nb = s // blk
    kw = kaug.shape[2]
    nbat = MOBA_BATCH
    assert nb == 8, "selection extras assume 8 key blocks"
    jt, it = _moba_steps(nb)
    qmap = lambda bi, t, jt_r, it_r: (bi, 0, jt_r[t])
    grid_spec = pltpu.PrefetchScalarGridSpec(
        num_scalar_prefetch=2,
        grid=(b // nbat, len(jt)),
        in_specs=[pl.BlockSpec((nbat, dmb, blk), qmap),
                  pl.BlockSpec((nbat, blk, kw), lambda bi, t, jt_r, it_r: (bi, it_r[t], 0)),
                  pl.BlockSpec((nbat, 1, dmb, blk), lambda bi, t, jt_r, it_r: (bi, it_r[t], 0, 0)),
                  pl.BlockSpec((nbat, nb, dmb), lambda bi, t, jt_r, it_r: (bi, 0, 0)),
                  pl.BlockSpec((nbat, dmb, blk), qmap)],
        out_specs=pl.BlockSpec((nbat, dmb, blk), qmap),
        scratch_shapes=[pltpu.VMEM((nbat, N_HEADS_MOBA, 2 * HEAD_DIM_MOBA, blk), BF16),
                        pltpu.VMEM((nbat, N_HEADS_MOBA, blk), F32),
                        pltpu.VMEM((nbat, N_HEADS_MOBA, HEAD_DIM_MOBA + 16, blk), F32),
                        pltpu.VMEM((nbat, N_HEADS_MOBA, blk), F32)]
        + [pltpu.VMEM((N_HEADS_MOBA, blk, blk), F32) for _ in range(nbat)],
    )
    return pl.pallas_call(
        _moba_kernel,
        grid_spec=grid_spec,
        out_shape=jax.ShapeDtypeStruct((b, dmb, s), BF16),
        compiler_params=pltpu.CompilerParams(dimension_semantics=("arbitrary", "arbitrary"),
                                             vmem_limit_bytes=VMEM_LIMIT),
        name="moba",
    )(jnp.asarray(jt, jnp.int32), jnp.asarray(it, jnp.int32), qbt, kaug, vbt, kmean, gbt)


def _out_kernel(x_ref, gate_ref, oa_ref, obt_ref, wa_ref, wb_ref, gf_ref, o_ref):
    y = (jnp.dot(oa_ref[0], wa_ref[...], preferred_element_type=F32)
         + _dot_tn(obt_ref[0], wb_ref[...]))
    r = x_ref[0] + gate_ref[0] * y
    o_ref[0] = r * lax.rsqrt(jnp.mean(r * r, axis=-1, keepdims=True) + EPS) * gf_ref[...]


def _out(x, mod, out_a, out_bt, w_a, w_b, g_final):
    b, s, d = x.shape
    tm = OUT_TILE
    row = lambda bi, j: (bi, j, 0)
    full2 = lambda bi, j: (0, 0)
    return pl.pallas_call(
        _out_kernel,
        grid=(b, s // tm),
        in_specs=[pl.BlockSpec((1, tm, d), row),
                  pl.BlockSpec((1, 1, d), lambda bi, j: (bi, 0, 2)),
                  pl.BlockSpec((1, tm, D_MLSTM), row),
                  pl.BlockSpec((1, D_MOBA, tm), lambda bi, j: (bi, 0, j)),
                  pl.BlockSpec(w_a.shape, full2),
                  pl.BlockSpec(w_b.shape, full2),
                  pl.BlockSpec((1, d), full2)],
        out_specs=pl.BlockSpec((1, tm, d), row),
        out_shape=jax.ShapeDtypeStruct((b, s, d), x.dtype),
        compiler_params=pltpu.CompilerParams(dimension_semantics=("arbitrary", "arbitrary"),
                                             vmem_limit_bytes=VMEM_LIMIT),
        name="out",
    )(x, mod, out_a, out_bt, w_a, w_b, g_final.reshape(1, d))


def _layer(x, c, w_ada, b_ada, g_norm, w_in, conv_w, conv_b, b_igate, b_fgate, g_mlstm_head, w_out,
           g_final):
    b = x.shape[0]
    mod = _ada(c, w_ada, b_ada).reshape(b, 1, 3 * D_MODEL)

    cols = lambda name: w_in[:, _OFF[name][0]:_OFF[name][1]]
    wnn = jnp.concatenate([cols("qm"), cols("km"), cols("vm"), cols("om"), cols("zm"), cols("kb")],
                          axis=1).astype(BF16)
    wnt = jnp.concatenate([cols("qb"), cols("vb"), cols("zb")], axis=1).T.astype(BF16)
    zpad = jnp.zeros((D_MODEL, 4), w_in.dtype)
    wif = jnp.concatenate([cols("im"), zpad, cols("fm"), jnp.zeros((D_MODEL, 116), w_in.dtype)],
                          axis=1).astype(BF16)
    bif = jnp.concatenate([b_igate.astype(F32), jnp.zeros((4,), F32), b_fgate.astype(F32),
                           jnp.zeros((116,), F32)]).reshape(1, 128)

    qm, km, vm, ga, gates, qbt, kaug, kmean, vbt, gbt = _proj(
        x, mod, g_norm, wnn, wnt, wif, conv_w, conv_b, bif)
    out_a = _mlstm(qm, km, vm, ga, gates, g_mlstm_head)
    out_bt = _moba(qbt, kaug, vbt, kmean, gbt)
    w_a = w_out[:D_MLSTM].astype(BF16)
    w_b = w_out[D_MLSTM:].astype(BF16)
    return _out(x, mod, out_a, out_bt, w_a, w_b, g_final)


def kernel(x, c, w_ada, b_ada, g_norm, w_in, conv_w, conv_b, b_igate, b_fgate, g_mlstm_head, w_out,
           g_final):
    assert w_ada.shape[0] == 1, "single-layer trunk"
    return _layer(x, c, w_ada[0], b_ada[0], g_norm[0], w_in[0], conv_w[0], conv_b[0], b_igate[0],
                  b_fgate[0], g_mlstm_head[0], w_out[0], g_final)
```

```python
import jax
import jax.numpy as jnp
import numpy as np
from jax import lax
from jax.experimental import pallas as pl
from jax.experimental.pallas import tpu as pltpu

F32 = jnp.float32
BF16 = jnp.bfloat16

D_MODEL = 1024
D_MLSTM = 512
N_HEADS_MLSTM = 4
HEAD_DIM_MLSTM = 128
D_MOBA = 512
N_HEADS_MOBA = 8
HEAD_DIM_MOBA = 64
CONV_K = 4
MOBA_BLOCK = 256
MOBA_TOPK = 3
EPS = 1e-6

ROW_TILE = 256
PROJ_CHUNK = 256
MLSTM_CHUNK = 128
MLSTM_STEP_CHUNKS = 4
MOBA_BATCH = 2
OUT_TILE = 512
MASK_BIAS = -1e30
LOG2E = 1.4426950408889634
N_POS_TERMS = 3
VMEM_LIMIT = 56 * 1024 * 1024

_OFF = {}
_o = 0
for _name, _size in (("qm", 512), ("km", 512), ("vm", 512), ("om", 512), ("im", 4), ("fm", 4),
                     ("zm", 512), ("qb", 512), ("kb", 512), ("vb", 512), ("zb", 512)):
    _OFF[_name] = (_o, _o + _size)
    _o += _size


def _silu(v):
    return v * jax.nn.sigmoid(v)


def _dot_nt(a, b):
    return lax.dot_general(a, b, (((1,), (1,)), ((), ())), preferred_element_type=F32)


def _dot_tn(a, b):
    return lax.dot_general(a, b, (((0,), (0,)), ((), ())), preferred_element_type=F32)


def _ada_kernel(c_ref, w_ref, b_ref, o_ref):
    a = _silu(c_ref[...])
    o_ref[...] = jnp.dot(a, w_ref[...], preferred_element_type=F32,
                         precision=lax.Precision.HIGHEST) + b_ref[...]


def _ada(c, w_ada, b_ada):
    b, d = c.shape
    n = w_ada.shape[1]
    tn = 1024
    return pl.pallas_call(
        _ada_kernel,
        grid=(n // tn,),
        in_specs=[pl.BlockSpec((b, d), lambda i: (0, 0)),
                  pl.BlockSpec((d, tn), lambda i: (0, i)),
                  pl.BlockSpec((1, tn), lambda i: (0, i))],
        out_specs=pl.BlockSpec((b, tn), lambda i: (0, i)),
        out_shape=jax.ShapeDtypeStruct((b, n), F32),
        compiler_params=pltpu.CompilerParams(dimension_semantics=("arbitrary",),
                                             vmem_limit_bytes=VMEM_LIMIT),
        name="ada",
    )(c, w_ada, b_ada.reshape(1, n))


def _chunk_scan(v, combine, fill):
    pos = lax.broadcasted_iota(jnp.int32, v.shape, 1) & (MLSTM_CHUNK - 1)
    d = 1
    while d < MLSTM_CHUNK:
        shifted = pltpu.roll(v, d, axis=1)
        v = combine(v, jnp.where(pos >= d, shifted, fill))
        d *= 2
    return v


def _proj_kernel(x_ref, shift_ref, scale_ref, gn_ref, wnn_ref, wnt_ref, wif_ref, cw_ref, cb_ref, bif_ref,
                 kx_ref,
                 qm_ref, km_ref, vm_ref, ga_ref, gates_ref, qbt_ref, kaug_ref, kmean_ref, vbt_ref, gbt_ref,
                 ext_ref):
    j = pl.program_id(1)
    tm = x_ref.shape[1]

    @pl.when(j == 0)
    def _():
        ext_ref[0:8, :] = jnp.zeros((8, 2 * D_MLSTM), F32)

    x = x_ref[0]
    ms = jnp.mean(x * x, axis=-1, keepdims=True)
    h = x * lax.rsqrt(ms + EPS) * gn_ref[...] * (1.0 + scale_ref[0]) + shift_ref[0]
    hb = h.astype(BF16)

    ifc = jnp.dot(hb, wif_ref[...], preferred_element_type=F32) + bif_ref[...]
    ift = jnp.transpose(ifc)[0:16, :]
    log_i = ift[0:8]
    f_pre = ift[8:16]
    log_f = jnp.minimum(f_pre, 0.0) - jnp.log1p(jnp.exp(-jnp.abs(f_pre)))
    cum_f = _chunk_scan(log_f, jnp.add, 0.0)
    g = log_i - cum_f
    gates_ref[0, 0:8, :] = cum_f
    gates_ref[0, 8:16, :] = g
    gates_ref[0, 16:24, :] = _chunk_scan(g, jnp.maximum, -jnp.inf)

    def nn(lo, hi):
        return jnp.dot(hb, wnn_ref[:, lo:hi], preferred_element_type=F32)

    def nt(lo, hi):
        return _dot_nt(wnt_ref[lo:hi, :], hb)

    cw = PROJ_CHUNK

    def conv_chunk(lo):
        cs = slice(lo, lo + cw)
        qk = nn(lo, lo + cw)
        ext_ref[8:8 + tm, cs] = qk
        conv = cb_ref[:, cs] + qk * cw_ref[CONV_K - 1:CONV_K, cs]
        for tap in range(CONV_K - 1):
            conv = conv + ext_ref[pl.ds(8 - (CONV_K - 1) + tap, tm), cs] * cw_ref[tap:tap + 1, cs]
        ext_ref[0:8, cs] = qk[tm - 8:tm, :]
        act = _silu(conv)
        if lo < D_MLSTM:
            qm_ref[0, :, lo:lo + cw] = act.astype(BF16)
        else:
            km_ref[0, :, lo - D_MLSTM:lo - D_MLSTM + cw] = (act * (HEAD_DIM_MLSTM ** -0.5)).astype(BF16)

    def value_chunk(lo):
        vm_ref[0, lo:lo + cw, :] = nt(1536 + lo, 1536 + lo + cw).astype(BF16)

    def gate_chunk(lo):
        ga_ref[0, lo:lo + cw, :] = (jax.nn.sigmoid(nt(2048 + lo, 2048 + lo + cw))
                                   * _silu(nt(2560 + lo, 2560 + lo + cw))).astype(BF16)

    lane = lax.broadcasted_iota(jnp.int32, (tm, 128), 1)

    def key_chunk(lo):
        kb = nn(1024 + lo, 1024 + lo + cw)
        kmean_ref[0, pl.ds(j, 1), lo:lo + cw] = jnp.mean(kb, axis=0, keepdims=True)
        kb2 = (kb * LOG2E).astype(BF16)
        for pr in range(lo // 128, (lo + cw) // 128):
            for hh in range(2):
                hd = 2 * pr + hh
                is_key = (lane < HEAD_DIM_MOBA) if hh == 0 else (lane >= HEAD_DIM_MOBA)
                grp = slice(128 * hd, 128 * (hd + 1))
                kaug_ref[0, :, grp] = jnp.where(is_key, kb2[:, 128 * pr - lo:128 * (pr + 1) - lo],
                                                kx_ref[0, :, grp])

    def qt_chunk(lo):
        qbt_ref[0, lo:lo + cw, :] = (nt(lo, lo + cw) * (HEAD_DIM_MOBA ** -0.5)).astype(BF16)

    def vt_chunk(lo):
        vbt_ref[0, 0, lo:lo + cw, :] = nt(512 + lo, 512 + lo + cw).astype(BF16)

    def gt_chunk(lo):
        gbt_ref[0, lo:lo + cw, :] = _silu(nt(1024 + lo, 1024 + lo + cw)).astype(BF16)

    heavy = ([(conv_chunk, lo) for lo in range(0, 2 * D_MLSTM, cw)]
             + [(gate_chunk, lo) for lo in range(0, D_MLSTM, cw)]
             + [(key_chunk, lo) for lo in range(0, D_MOBA, cw)])
    light = ([(qt_chunk, lo) for lo in range(0, D_MOBA, cw)]
             + [(vt_chunk, lo) for lo in range(0, D_MOBA, cw)]
             + [(value_chunk, lo) for lo in range(0, D_MLSTM, cw)]
             + [(gt_chunk, lo) for lo in range(0, D_MOBA, cw)])
    for idx in range(max(len(heavy), len(light))):
        for group in (heavy, light):
            if idx < len(group):
                fn, lo = group[idx]
                fn(lo)


def _proj(x, mod, g_norm, wnn, wnt, wif, conv_w, conv_b, bif):
    b, s, d = x.shape
    tm = ROW_TILE
    nb = s // MOBA_BLOCK
    assert tm == MOBA_BLOCK
    kx = _moba_key_extras(nb)
    kw = N_HEADS_MOBA * 128
    row = lambda bi, j: (bi, j, 0)
    col = lambda bi, j: (bi, 0, j)
    full2 = lambda bi, j: (0, 0)
    out_shapes = (
        jax.ShapeDtypeStruct((b, s, D_MLSTM), BF16),
        jax.ShapeDtypeStruct((b, s, D_MLSTM), BF16),
        jax.ShapeDtypeStruct((b, D_MLSTM, s), BF16),
        jax.ShapeDtypeStruct((b, D_MLSTM, s), BF16),
        jax.ShapeDtypeStruct((b, 24, s), F32),
        jax.ShapeDtypeStruct((b, D_MOBA, s), BF16),
        jax.ShapeDtypeStruct((b, s, kw), BF16),
        jax.ShapeDtypeStruct((b, nb, D_MOBA), F32),
        jax.ShapeDtypeStruct((b, nb, D_MOBA, MOBA_BLOCK), BF16),
        jax.ShapeDtypeStruct((b, D_MOBA, s), BF16),
    )
    out_specs = (
        pl.BlockSpec((1, tm, D_MLSTM), row),
        pl.BlockSpec((1, tm, D_MLSTM), row),
        pl.BlockSpec((1, D_MLSTM, tm), col),
        pl.BlockSpec((1, D_MLSTM, tm), col),
        pl.BlockSpec((1, 24, tm), col),
        pl.BlockSpec((1, D_MOBA, tm), col),
        pl.BlockSpec((1, tm, kw), row),
        pl.BlockSpec((1, nb, D_MOBA), lambda bi, j: (bi, 0, 0)),
        pl.BlockSpec((1, 1, D_MOBA, tm), lambda bi, j: (bi, j, 0, 0)),
        pl.BlockSpec((1, D_MOBA, tm), col),
    )
    in_specs = [
        pl.BlockSpec((1, tm, d), row),
        pl.BlockSpec((1, 1, d), lambda bi, j: (bi, 0, 0)),
        pl.BlockSpec((1, 1, d), lambda bi, j: (bi, 0, 1)),
        pl.BlockSpec((1, d), full2),
        pl.BlockSpec(wnn.shape, full2),
        pl.BlockSpec(wnt.shape, full2),
        pl.BlockSpec(wif.shape, full2),
        pl.BlockSpec(conv_w.shape, full2),
        pl.BlockSpec((1, 2 * D_MLSTM), full2),
        pl.BlockSpec((1, 128), full2),
        pl.BlockSpec((1, MOBA_BLOCK, kw), lambda bi, j: (j, 0, 0)),
    ]
    return pl.pallas_call(
        _proj_kernel,
        grid=(b, s // tm),
        in_specs=in_specs,
        out_specs=out_specs,
        out_shape=out_shapes,
        scratch_shapes=[pltpu.VMEM((tm + 8, 2 * D_MLSTM), F32)],
        compiler_params=pltpu.CompilerParams(dimension_semantics=("arbitrary", "arbitrary"),
                                             vmem_limit_bytes=VMEM_LIMIT),
        name="proj",
    )(x, mod, mod, g_norm.reshape(1, d), wnn, wnt, wif, conv_w, conv_b.reshape(1, -1), bif, kx)


def _col_bcast(row, n):
    return jnp.transpose(jnp.broadcast_to(row, (n, n)))


def _mlstm_kernel(q_ref, k_ref, vt_ref, gat_ref, gates_ref, gh_ref, o_ref, st_ref, m_ref):
    ci = pl.program_id(1)
    L = MLSTM_CHUNK
    dh = HEAD_DIM_MLSTM
    assert L == dh == 128

    @pl.when(ci == 0)
    def _():
        st_ref[...] = jnp.zeros(st_ref.shape, F32)
        m_ref[...] = jnp.zeros(m_ref.shape, F32)

    s_idx = lax.broadcasted_iota(jnp.int32, (L, L), 0)
    t_idx = lax.broadcasted_iota(jnp.int32, (L, L), 1)
    causal = s_idx <= t_idx
    ones_tail = jnp.ones((16, L), BF16)

    heads = range(N_HEADS_MLSTM)
    sls = [slice(hd * dh, (hd + 1) * dh) for hd in heads]

    m_prev = m_ref[:, 0:1]
    rowv = []
    for c in range(MLSTM_STEP_CHUNKS):
        ts = slice(c * L, (c + 1) * L)
        cum_f = gates_ref[0, 0:8, ts]
        g = gates_ref[0, 8:16, ts]
        g_cmax = gates_ref[0, 16:24, ts]
        f_tot = cum_f[:, L - 1:L]
        g_max = g_cmax[:, L - 1:L]
        a_max = f_tot + g_max
        a_row = -jnp.maximum(m_prev, g_cmax)
        m_new = jnp.maximum(f_tot + m_prev, a_max)
        rowv.append(dict(
            ts=ts, g=g, a_row=a_row,
            w=jnp.exp(g - g_max),
            inter=jnp.exp(a_row + m_prev),
            floor=jnp.exp(a_row - cum_f),
            s_old=jnp.exp(f_tot + m_prev - m_new), s_new=jnp.exp(a_max - m_new)))
        m_prev = m_new
    m_ref[...] = jnp.broadcast_to(m_prev, m_ref.shape)

    s_kq = {}
    for c, rv in enumerate(rowv):
        for hd in heads:
            q = q_ref[0, rv["ts"], sls[hd]]
            k = k_ref[0, rv["ts"], sls[hd]]
            g_col = _col_bcast(rv["g"][hd:hd + 1, :], L)
            decay = jnp.exp(jnp.where(causal, g_col + rv["a_row"][hd:hd + 1, :], -jnp.inf))
            s_kq[c, hd] = (_dot_nt(k, q) * decay).astype(BF16)

    st = [st_ref[hd] for hd in heads]
    for c, rv in enumerate(rowv):
        ts = rv["ts"]
        vts = [jnp.concatenate([vt_ref[0, sls[hd], ts], ones_tail], axis=0) for hd in heads]
        carried = [_dot_nt(st[hd].astype(BF16), q_ref[0, ts, sls[hd]]) for hd in heads]
        intra = [jnp.dot(vts[hd], s_kq[c, hd], preferred_element_type=F32) for hd in heads]
        for hd in heads:
            mixed = intra[hd] + rv["inter"][hd:hd + 1, :] * carried[hd]
            den = mixed[dh:dh + 1, :]
            hh = mixed[0:dh] / jnp.maximum(jnp.abs(den), rv["floor"][hd:hd + 1, :])
            hh = hh * lax.rsqrt(jnp.mean(hh * hh, axis=0, keepdims=True) + EPS)
            o_ref[0, sls[hd], ts] = (hh * gh_ref[sls[hd], :] * gat_ref[0, sls[hd], ts].astype(F32)).astype(BF16)
        for hd in heads:
            vw = (vts[hd].astype(F32) * rv["w"][hd:hd + 1, :]).astype(BF16)
            st_loc = jnp.dot(vw, k_ref[0, ts, sls[hd]], preferred_element_type=F32)
            st[hd] = rv["s_old"][hd:hd + 1, :] * st[hd] + rv["s_new"][hd:hd + 1, :] * st_loc

    for hd in heads:
        st_ref[hd] = st[hd]


def _mlstm(qm, km, vmt, gat, gates, g_head):
    b, s, dm = qm.shape
    L = MLSTM_CHUNK * MLSTM_STEP_CHUNKS
    row = lambda bi, ci: (bi, ci, 0)
    col = lambda bi, ci: (bi, 0, ci)
    g_bc = jnp.broadcast_to(g_head.astype(F32).reshape(dm, 1), (dm, 128))
    return pl.pallas_call(
        _mlstm_kernel,
        grid=(b, s // L),
        in_specs=[pl.BlockSpec((1, L, dm), row), pl.BlockSpec((1, L, dm), row),
                  pl.BlockSpec((1, dm, L), col), pl.BlockSpec((1, dm, L), col),
                  pl.BlockSpec((1, 24, L), col),
                  pl.BlockSpec((dm, 128), lambda bi, ci: (0, 0))],
        out_specs=pl.BlockSpec((1, dm, L), col),
        out_shape=jax.ShapeDtypeStruct((b, dm, s), BF16),
        scratch_shapes=[pltpu.VMEM((N_HEADS_MLSTM, HEAD_DIM_MLSTM + 16, HEAD_DIM_MLSTM), F32),
                        pltpu.VMEM((8, 128), F32)],
        compiler_params=pltpu.CompilerParams(dimension_semantics=("arbitrary", "arbitrary"),
                                             vmem_limit_bytes=VMEM_LIMIT),
        name="mlstm",
    )(qm, km, vmt, gat, gates, g_bc)


def _moba_steps(nb):
    jt, it = [], []
    for j in range(nb):
        for i in [j] + list(range(j)):
            jt.append(j)
            it.append(i)
    return jt, it


def _moba_key_extras(nb):
    blk = MOBA_BLOCK
    lane = np.arange(N_HEADS_MOBA * 128)
    head = lane // 128
    w = (lane % 128 - np.where(head % 2 == 0, HEAD_DIM_MOBA, 0))[None, None, :]
    slope = (2.0 ** -(head + 1.0))[None, None, :]
    i = np.arange(nb)[:, None, None]
    c = np.arange(blk)[None, :, None]
    tab = np.where(w == i, 1.0, 0.0)
    rest = LOG2E * slope * (i * blk + c)
    for term in range(N_POS_TERMS):
        piece = rest.astype(np.float32).astype(BF16).astype(np.float64)
        tab = tab + np.where(w == nb + term, piece, 0.0)
        rest = rest - piece
    return jnp.asarray(tab.astype(np.float32)).astype(BF16)


def _moba_kernel(jt_ref, it_ref, qt_ref, k_ref, vt_ref, kmean_ref, gt_ref, o_ref,
                 qaug_ref, m_ref, acc_ref, cmax_ref, *s_refs):
    t = pl.program_id(1)
    j = jt_ref[t]
    i = it_ref[t]
    blk = MOBA_BLOCK
    dh = HEAD_DIM_MOBA
    nb = kmean_ref.shape[1]
    nbat = qt_ref.shape[0]
    first = i == j
    last = jnp.logical_or(i == j - 1, j == 0)

    def key_tile(bb, h):
        return k_ref[bb, :, 128 * h:128 * (h + 1)]

    def rows(h):
        return slice(dh * h, dh * (h + 1))

    ones_tail = jnp.ones((16, blk), BF16)

    def values_t(bb, h):
        return jnp.concatenate([vt_ref[bb, 0, rows(h), :], ones_tail], axis=0)

    @pl.when(first)
    def _():
        blk_i = lax.broadcasted_iota(jnp.int32, (nb, blk), 0)
        lane_m = lax.broadcasted_iota(jnp.int32, (nb, 2 * dh), 1)
        key_c = lax.broadcasted_iota(jnp.int32, (blk, blk), 0)
        qry_r = lax.broadcasted_iota(jnp.int32, (blk, blk), 1)
        causal = key_c <= qry_r
        ones_rows = jnp.where(blk_i < N_POS_TERMS, 1.0, 0.0)
        gates = {}
        for bb in range(nbat):
            for h in range(N_HEADS_MOBA):
                pr, hh = divmod(h, 2)
                qt_pair = qt_ref[bb, 128 * pr:128 * (pr + 1), :]
                in_head = (lane_m < dh) if hh == 0 else (lane_m >= dh)
                km = jnp.where(in_head, kmean_ref[bb, :, 128 * pr:128 * (pr + 1)], 0.0).astype(BF16)
                gates[bb, h] = jnp.dot(km, qt_pair, preferred_element_type=F32)
        for bb in range(nbat):
            for h in range(N_HEADS_MOBA):
                pr, hh = divmod(h, 2)
                gate = gates[bb, h]
                rank = jnp.zeros((nb, blk), jnp.int32)
                for i2 in range(nb):
                    g2 = gate[i2:i2 + 1, :]
                    beats = (g2 > gate) | ((g2 == gate) & (i2 < blk_i))
                    rank = rank + jnp.where(beats & (i2 < j), 1, 0)
                keep = ((blk_i < j) & (rank < MOBA_TOPK)) | (blk_i == j)
                sel_bias = jnp.where(keep, 0.0, MASK_BIAS)
                extra = jnp.concatenate([sel_bias, ones_rows, jnp.zeros((dh - 16, blk), F32)], axis=0)
                extra = extra.astype(BF16)
                qt_h = qt_ref[bb, dh * h:dh * (h + 1), :]
                qaug_ref[bb, h] = jnp.concatenate([qt_h, extra] if hh == 0 else [extra, qt_h], axis=0)

        for bb in range(nbat):
            for h in range(N_HEADS_MOBA):
                s = jnp.dot(key_tile(bb, h), qaug_ref[bb, h], preferred_element_type=F32)
                s = jnp.where(causal, s, MASK_BIAS)
                s_refs[bb][h] = s
                m_ref[bb, h:h + 1, :] = jnp.max(s, axis=0, keepdims=True)

        for bb in range(nbat):
            for h in range(N_HEADS_MOBA):
                p0 = jnp.exp2(s_refs[bb][h] - m_ref[bb, h:h + 1, :])
                acc_ref[bb, h] = jnp.dot(values_t(bb, h), p0.astype(BF16), preferred_element_type=F32)

    @pl.when(jnp.logical_not(first))
    def _():
        for bb in range(nbat):
            for h in range(N_HEADS_MOBA):
                s = jnp.dot(key_tile(bb, h), qaug_ref[bb, h], preferred_element_type=F32)
                s_refs[bb][h] = s
                cmax_ref[bb, h:h + 1, :] = jnp.max(s, axis=0, keepdims=True)
        for bb in range(nbat):
            for h in range(N_HEADS_MOBA):
                m_run = m_ref[bb, h:h + 1, :]
                m_new = jnp.maximum(m_run, cmax_ref[bb, h:h + 1, :])
                alpha = jnp.exp2(m_run - m_new)
                pr_ = jnp.exp2(s_refs[bb][h] - m_new)
                m_ref[bb, h:h + 1, :] = m_new
                acc_ref[bb, h] = alpha * acc_ref[bb, h] + jnp.dot(values_t(bb, h), pr_.astype(BF16),
                                                                  preferred_element_type=F32)

    @pl.when(last)
    def _():
        for bb in range(nbat):
            for h in range(N_HEADS_MOBA):
                acc = acc_ref[bb, h]
                out = acc[0:dh] / acc[dh:dh + 1] * gt_ref[bb, rows(h), :].astype(F32)
                o_ref[bb, rows(h), :] = out.astype(BF16)


def _moba(qbt, kaug, vbt, kmean, gbt):
    b, dmb, s = qbt.shape
    blk = MOBA_BLOCK
    nb = s // blk
    kw = kaug.shape[2]
    nbat = MOBA_BATCH
    assert nb == 8, "selection extras assume 8 key blocks"
    jt, it = _moba_steps(nb)
    qmap = lambda bi, t, jt_r, it_r: (bi, 0, jt_r[t])
    grid_spec = pltpu.PrefetchScalarGridSpec(
        num_scalar_prefetch=2,
        grid=(b // nbat, len(jt)),
        in_specs=[pl.BlockSpec((nbat, dmb, blk), qmap),
                  pl.BlockSpec((nbat, blk, kw), lambda bi, t, jt_r, it_r: (bi, it_r[t], 0)),
                  pl.BlockSpec((nbat, 1, dmb, blk), lambda bi, t, jt_r, it_r: (bi, it_r[t], 0, 0)),
                  pl.BlockSpec((nbat, nb, dmb), lambda bi, t, jt_r, it_r: (bi, 0, 0)),
                  pl.BlockSpec((nbat, dmb, blk), qmap)],
        out_specs=pl.BlockSpec((nbat, dmb, blk), qmap),
        scratch_shapes=[pltpu.VMEM((nbat, N_HEADS_MOBA, 2 * HEAD_DIM_MOBA, blk), BF16),
                        pltpu.VMEM((nbat, N_HEADS_MOBA, blk), F32),
                        pltpu.VMEM((nbat, N_HEADS_MOBA, HEAD_DIM_MOBA + 16, blk), F32),
                        pltpu.VMEM((nbat, N_HEADS_MOBA, blk), F32)]
        + [pltpu.VMEM((N_HEADS_MOBA, blk, blk), F32) for _ in range(nbat)],
    )
    return pl.pallas_call(
        _moba_kernel,
        grid_spec=grid_spec,
        out_shape=jax.ShapeDtypeStruct((b, dmb, s), BF16),
        compiler_params=pltpu.CompilerParams(dimension_semantics=("arbitrary", "arbitrary"),
                                             vmem_limit_bytes=VMEM_LIMIT),
        name="moba",
    )(jnp.asarray(jt, jnp.int32), jnp.asarray(it, jnp.int32), qbt, kaug, vbt, kmean, gbt)


def _out_kernel(x_ref, gate_ref, oat_ref, obt_ref, wa_ref, wb_ref, gf_ref, o_ref):
    y = _dot_tn(oat_ref[0], wa_ref[...]) + _dot_tn(obt_ref[0], wb_ref[...])
    r = x_ref[0] + gate_ref[0] * y
    o_ref[0] = r * lax.rsqrt(jnp.mean(r * r, axis=-1, keepdims=True) + EPS) * gf_ref[...]


def _out(x, mod, out_a, out_bt, w_a, w_b, g_final):
    b, s, d = x.shape
    tm = OUT_TILE
    row = lambda bi, j: (bi, j, 0)
    full2 = lambda bi, j: (0, 0)
    return pl.pallas_call(
        _out_kernel,
        grid=(b, s // tm),
        in_specs=[pl.BlockSpec((1, tm, d), row),
                  pl.BlockSpec((1, 1, d), lambda bi, j: (bi, 0, 2)),
                  pl.BlockSpec((1, D_MLSTM, tm), lambda bi, j: (bi, 0, j)),
                  pl.BlockSpec((1, D_MOBA, tm), lambda bi, j: (bi, 0, j)),
                  pl.BlockSpec(w_a.shape, full2),
                  pl.BlockSpec(w_b.shape, full2),
                  pl.BlockSpec((1, d), full2)],
        out_specs=pl.BlockSpec((1, tm, d), row),
        out_shape=jax.ShapeDtypeStruct((b, s, d), x.dtype),
        compiler_params=pltpu.CompilerParams(dimension_semantics=("arbitrary", "arbitrary"),
                                             vmem_limit_bytes=VMEM_LIMIT),
        name="out",
    )(x, mod, out_a, out_bt, w_a, w_b, g_final.reshape(1, d))


def _layer(x, c, w_ada, b_ada, g_norm, w_in, conv_w, conv_b, b_igate, b_fgate, g_mlstm_head, w_out,
           g_final):
    b = x.shape[0]
    mod = _ada(c, w_ada, b_ada).reshape(b, 1, 3 * D_MODEL)

    cols = lambda name: w_in[:, _OFF[name][0]:_OFF[name][1]]
    wnn = jnp.concatenate([cols("qm"), cols("km"), cols("kb")], axis=1).astype(BF16)
    wnt = jnp.concatenate([cols("qb"), cols("vb"), cols("zb"), cols("vm"), cols("om"), cols("zm")],
                          axis=1).T.astype(BF16)
    zpad = jnp.zeros((D_MODEL, 4), w_in.dtype)
    wif = jnp.concatenate([cols("im"), zpad, cols("fm"), jnp.zeros((D_MODEL, 116), w_in.dtype)],
                          axis=1).astype(BF16)
    bif = jnp.concatenate([b_igate.astype(F32), jnp.zeros((4,), F32), b_fgate.astype(F32),
                           jnp.zeros((116,), F32)]).reshape(1, 128)

    qm, km, vm, ga, gates, qbt, kaug, kmean, vbt, gbt = _proj(
        x, mod, g_norm, wnn, wnt, wif, conv_w, conv_b, bif)
    out_a = _mlstm(qm, km, vm, ga, gates, g_mlstm_head)
    out_bt = _moba(qbt, kaug, vbt, kmean, gbt)
    w_a = w_out[:D_MLSTM].astype(BF16)
    w_b = w_out[D_MLSTM:].astype(BF16)
    return _out(x, mod, out_a, out_bt, w_a, w_b, g_final)


def kernel(x, c, w_ada, b_ada, g_norm, w_in, conv_w, conv_b, b_igate, b_fgate, g_mlstm_head, w_out,
           g_final):
    assert w_ada.shape[0] == 1, "single-layer trunk"
    return _layer(x, c, w_ada[0], b_ada[0], g_norm[0], w_in[0], conv_w[0], conv_b[0], b_igate[0],
                  b_fgate[0], g_mlstm_head[0], w_out[0], g_final)
```

```python
import jax
import jax.numpy as jnp
import numpy as np
from jax import lax
from jax.experimental import pallas as pl
from jax.experimental.pallas import tpu as pltpu

F32 = jnp.float32
BF16 = jnp.bfloat16

D_MODEL = 1024
D_MLSTM = 512
N_HEADS_MLSTM = 4
HEAD_DIM_MLSTM = 128
D_MOBA = 512
N_HEADS_MOBA = 8
HEAD_DIM_MOBA = 64
CONV_K = 4
MOBA_BLOCK = 256
MOBA_TOPK = 3
EPS = 1e-6

ROW_TILE = 256
PROJ_CHUNK = 256
MLSTM_CHUNK = 128
MLSTM_STEP_CHUNKS = 4
MOBA_BATCH = 2
OUT_TILE = 512
MASK_BIAS = -1e30
LOG2E = 1.4426950408889634
N_POS_TERMS = 3
VMEM_LIMIT = 56 * 1024 * 1024

_OFF = {}
_o = 0
for _name, _size in (("qm", 512), ("km", 512), ("vm", 512), ("om", 512), ("im", 4), ("fm", 4),
                     ("zm", 512), ("qb", 512), ("kb", 512), ("vb", 512), ("zb", 512)):
    _OFF[_name] = (_o, _o + _size)
    _o += _size
_NN_GROUPS = ("qm", "km", "kb")
_NT_GROUPS = ("qb", "vb", "zb", "vm", "om", "zm")


def _silu(v):
    return v * jax.nn.sigmoid(v)


def _dot_nt(a, b):
    return lax.dot_general(a, b, (((1,), (1,)), ((), ())), preferred_element_type=F32)


def _dot_tn(a, b):
    return lax.dot_general(a, b, (((0,), (0,)), ((), ())), preferred_element_type=F32)


def _ada_kernel(c_ref, w_ref, b_ref, o_ref):
    a = _silu(c_ref[...])
    o_ref[...] = jnp.dot(a, w_ref[...], preferred_element_type=F32,
                         precision=lax.Precision.HIGHEST) + b_ref[...]


def _ada(c, w_ada, b_ada):
    b, d = c.shape
    n = w_ada.shape[1]
    tn = 1024
    return pl.pallas_call(
        _ada_kernel,
        grid=(n // tn,),
        in_specs=[pl.BlockSpec((b, d), lambda i: (0, 0)),
                  pl.BlockSpec((d, tn), lambda i: (0, i)),
                  pl.BlockSpec((1, tn), lambda i: (0, i))],
        out_specs=pl.BlockSpec((b, tn), lambda i: (0, i)),
        out_shape=jax.ShapeDtypeStruct((b, n), F32),
        compiler_params=pltpu.CompilerParams(dimension_semantics=("arbitrary",),
                                             vmem_limit_bytes=VMEM_LIMIT),
        name="ada",
    )(c, w_ada, b_ada.reshape(1, n))


def _chunk_scan(v, combine, fill):
    pos = lax.broadcasted_iota(jnp.int32, v.shape, 1) & (MLSTM_CHUNK - 1)
    d = 1
    while d < MLSTM_CHUNK:
        shifted = pltpu.roll(v, d, axis=1)
        v = combine(v, jnp.where(pos >= d, shifted, fill))
        d *= 2
    return v


def _proj_kernel(x_ref, shift_ref, scale_ref, gn_ref, w_ref, cw_ref, cb_ref, bif_ref, kx_ref,
                 qm_ref, km_ref, vm_ref, ga_ref, gates_ref, qbt_ref, kaug_ref, kmean_ref, vbt_ref, gbt_ref,
                 ext_ref, wnn_ref, wnt_ref, wif_ref):
    j = pl.program_id(1)
    tm = x_ref.shape[1]

    @pl.when(jnp.logical_and(pl.program_id(0) == 0, j == 0))
    def _():
        for g, name in enumerate(_NN_GROUPS):
            lo = _OFF[name][0]
            for c in range(0, 512, 256):
                wnn_ref[:, 512 * g + c:512 * g + c + 256] = w_ref[:, lo + c:lo + c + 256].astype(BF16)
        for g, name in enumerate(_NT_GROUPS):
            lo = _OFF[name][0]
            for c in range(0, 512, 256):
                wnt_ref[512 * g + c:512 * g + c + 256, :] = jnp.transpose(
                    w_ref[:, lo + c:lo + c + 256]).astype(BF16)
        blk = w_ref[:, _OFF["im"][0]:_OFF["im"][0] + 128]
        lane_w = lax.broadcasted_iota(jnp.int32, blk.shape, 1)
        f_cols = pltpu.roll(blk, 4, axis=1)
        wif_ref[...] = jnp.where(lane_w < 4, blk,
                                 jnp.where((lane_w >= 8) & (lane_w < 12), f_cols, 0.0)).astype(BF16)

    @pl.when(j == 0)
    def _():
        ext_ref[0:8, :] = jnp.zeros((8, 2 * D_MLSTM), F32)

    x = x_ref[0]
    ms = jnp.mean(x * x, axis=-1, keepdims=True)
    h = x * lax.rsqrt(ms + EPS) * gn_ref[...] * (1.0 + scale_ref[0]) + shift_ref[0]
    hb = h.astype(BF16)

    ifc = jnp.dot(hb, wif_ref[...], preferred_element_type=F32) + bif_ref[...]
    ift = jnp.transpose(ifc)[0:16, :]
    log_i = ift[0:8]
    f_pre = ift[8:16]
    log_f = jnp.minimum(f_pre, 0.0) - jnp.log1p(jnp.exp(-jnp.abs(f_pre)))
    cum_f = _chunk_scan(log_f, jnp.add, 0.0)
    g = log_i - cum_f
    gates_ref[0, 0:8, :] = cum_f
    gates_ref[0, 8:16, :] = g
    gates_ref[0, 16:24, :] = _chunk_scan(g, jnp.maximum, -jnp.inf)

    def nn(lo, hi):
        return jnp.dot(hb, wnn_ref[:, lo:hi], preferred_element_type=F32)

    def nt(lo, hi):
        return _dot_nt(wnt_ref[lo:hi, :], hb)

    cw = PROJ_CHUNK

    def conv_chunk(lo):
        cs = slice(lo, lo + cw)
        qk = nn(lo, lo + cw)
        ext_ref[8:8 + tm, cs] = qk
        conv = cb_ref[:, cs] + qk * cw_ref[CONV_K - 1:CONV_K, cs]
        for tap in range(CONV_K - 1):
            conv = conv + ext_ref[pl.ds(8 - (CONV_K - 1) + tap, tm), cs] * cw_ref[tap:tap + 1, cs]
        ext_ref[0:8, cs] = qk[tm - 8:tm, :]
        act = _silu(conv)
        if lo < D_MLSTM:
            qm_ref[0, :, lo:lo + cw] = act.astype(BF16)
        else:
            km_ref[0, :, lo - D_MLSTM:lo - D_MLSTM + cw] = (act * (HEAD_DIM_MLSTM ** -0.5)).astype(BF16)

    def value_chunk(lo):
        vm_ref[0, lo:lo + cw, :] = nt(1536 + lo, 1536 + lo + cw).astype(BF16)

    def gate_chunk(lo):
        ga_ref[0, lo:lo + cw, :] = (jax.nn.sigmoid(nt(2048 + lo, 2048 + lo + cw))
                                   * _silu(nt(2560 + lo, 2560 + lo + cw))).astype(BF16)

    lane = lax.broadcasted_iota(jnp.int32, (tm, 128), 1)

    def key_chunk(lo):
        kb = nn(1024 + lo, 1024 + lo + cw)
        kmean_ref[0, pl.ds(j, 1), lo:lo + cw] = jnp.mean(kb, axis=0, keepdims=True)
        kb2 = (kb * LOG2E).astype(BF16)
        for pr in range(lo // 128, (lo + cw) // 128):
            for hh in range(2):
                hd = 2 * pr + hh
                is_key = (lane < HEAD_DIM_MOBA) if hh == 0 else (lane >= HEAD_DIM_MOBA)
                grp = slice(128 * hd, 128 * (hd + 1))
                kaug_ref[0, :, grp] = jnp.where(is_key, kb2[:, 128 * pr - lo:128 * (pr + 1) - lo],
                                                kx_ref[0, :, grp])

    def qt_chunk(lo):
        qbt_ref[0, lo:lo + cw, :] = (nt(lo, lo + cw) * (HEAD_DIM_MOBA ** -0.5)).astype(BF16)

    def vt_chunk(lo):
        vbt_ref[0, 0, lo:lo + cw, :] = nt(512 + lo, 512 + lo + cw).astype(BF16)

    def gt_chunk(lo):
        gbt_ref[0, lo:lo + cw, :] = _silu(nt(1024 + lo, 1024 + lo + cw)).astype(BF16)

    heavy = ([(conv_chunk, lo) for lo in range(0, 2 * D_MLSTM, cw)]
             + [(gate_chunk, lo) for lo in range(0, D_MLSTM, cw)]
             + [(key_chunk, lo) for lo in range(0, D_MOBA, cw)])
    light = ([(qt_chunk, lo) for lo in range(0, D_MOBA, cw)]
             + [(vt_chunk, lo) for lo in range(0, D_MOBA, cw)]
             + [(value_chunk, lo) for lo in range(0, D_MLSTM, cw)]
             + [(gt_chunk, lo) for lo in range(0, D_MOBA, cw)])
    for idx in range(max(len(heavy), len(light))):
        for group in (heavy, light):
            if idx < len(group):
                fn, lo = group[idx]
                fn(lo)


def _proj(x, mod, g_norm, w_in, conv_w, conv_b, bif):
    b, s, d = x.shape
    tm = ROW_TILE
    nb = s // MOBA_BLOCK
    assert tm == MOBA_BLOCK
    kx = _moba_key_extras(nb)
    kw = N_HEADS_MOBA * 128
    row = lambda bi, j: (bi, j, 0)
    col = lambda bi, j: (bi, 0, j)
    full2 = lambda bi, j: (0, 0)
    out_shapes = (
        jax.ShapeDtypeStruct((b, s, D_MLSTM), BF16),
        jax.ShapeDtypeStruct((b, s, D_MLSTM), BF16),
        jax.ShapeDtypeStruct((b, D_MLSTM, s), BF16),
        jax.ShapeDtypeStruct((b, D_MLSTM, s), BF16),
        jax.ShapeDtypeStruct((b, 24, s), F32),
        jax.ShapeDtypeStruct((b, D_MOBA, s), BF16),
        jax.ShapeDtypeStruct((b, s, kw), BF16),
        jax.ShapeDtypeStruct((b, nb, D_MOBA), F32),
        jax.ShapeDtypeStruct((b, nb, D_MOBA, MOBA_BLOCK), BF16),
        jax.ShapeDtypeStruct((b, D_MOBA, s), BF16),
    )
    out_specs = (
        pl.BlockSpec((1, tm, D_MLSTM), row),
        pl.BlockSpec((1, tm, D_MLSTM), row),
        pl.BlockSpec((1, D_MLSTM, tm), col),
        pl.BlockSpec((1, D_MLSTM, tm), col),
        pl.BlockSpec((1, 24, tm), col),
        pl.BlockSpec((1, D_MOBA, tm), col),
        pl.BlockSpec((1, tm, kw), row),
        pl.BlockSpec((1, nb, D_MOBA), lambda bi, j: (bi, 0, 0)),
        pl.BlockSpec((1, 1, D_MOBA, tm), lambda bi, j: (bi, j, 0, 0)),
        pl.BlockSpec((1, D_MOBA, tm), col),
    )
    in_specs = [
        pl.BlockSpec((1, tm, d), row),
        pl.BlockSpec((1, 1, d), lambda bi, j: (bi, 0, 0)),
        pl.BlockSpec((1, 1, d), lambda bi, j: (bi, 0, 1)),
        pl.BlockSpec((1, d), full2),
        pl.BlockSpec(w_in.shape, full2, pipeline_mode=pl.Buffered(1)),
        pl.BlockSpec(conv_w.shape, full2),
        pl.BlockSpec((1, 2 * D_MLSTM), full2),
        pl.BlockSpec((1, 128), full2),
        pl.BlockSpec((1, MOBA_BLOCK, kw), lambda bi, j: (j, 0, 0)),
    ]
    return pl.pallas_call(
        _proj_kernel,
        grid=(b, s // tm),
        in_specs=in_specs,
        out_specs=out_specs,
        out_shape=out_shapes,
        scratch_shapes=[pltpu.VMEM((tm + 8, 2 * D_MLSTM), F32),
                        pltpu.VMEM((d, 512 * len(_NN_GROUPS)), BF16),
                        pltpu.VMEM((512 * len(_NT_GROUPS), d), BF16),
                        pltpu.VMEM((d, 128), BF16)],
        compiler_params=pltpu.CompilerParams(dimension_semantics=("arbitrary", "arbitrary"),
                                             vmem_limit_bytes=VMEM_LIMIT),
        name="proj",
    )(x, mod, mod, g_norm.reshape(1, d), w_in, conv_w, conv_b.reshape(1, -1), bif, kx)


def _col_bcast(row, n):
    return jnp.transpose(jnp.broadcast_to(row, (n, n)))


def _mlstm_kernel(q_ref, k_ref, vt_ref, gat_ref, gates_ref, gh_ref, o_ref, st_ref, m_ref):
    ci = pl.program_id(1)
    L = MLSTM_CHUNK
    dh = HEAD_DIM_MLSTM
    assert L == dh == 128

    @pl.when(ci == 0)
    def _():
        st_ref[...] = jnp.zeros(st_ref.shape, F32)
        m_ref[...] = jnp.zeros(m_ref.shape, F32)

    s_idx = lax.broadcasted_iota(jnp.int32, (L, L), 0)
    t_idx = lax.broadcasted_iota(jnp.int32, (L, L), 1)
    causal = s_idx <= t_idx
    ones_tail = jnp.ones((16, L), BF16)

    heads = range(N_HEADS_MLSTM)
    sls = [slice(hd * dh, (hd + 1) * dh) for hd in heads]

    m_prev = m_ref[:, 0:1]
    rowv = []
    for c in range(MLSTM_STEP_CHUNKS):
        ts = slice(c * L, (c + 1) * L)
        cum_f = gates_ref[0, 0:8, ts]
        g = gates_ref[0, 8:16, ts]
        g_cmax = gates_ref[0, 16:24, ts]
        f_tot = cum_f[:, L - 1:L]
        g_max = g_cmax[:, L - 1:L]
        a_max = f_tot + g_max
        a_row = -jnp.maximum(m_prev, g_cmax)
        m_new = jnp.maximum(f_tot + m_prev, a_max)
        rowv.append(dict(
            ts=ts, g=g, a_row=a_row,
            w=jnp.exp(g - g_max),
            inter=jnp.exp(a_row + m_prev),
            floor=jnp.exp(a_row - cum_f),
            s_old=jnp.exp(f_tot + m_prev - m_new), s_new=jnp.exp(a_max - m_new)))
        m_prev = m_new
    m_ref[...] = jnp.broadcast_to(m_prev, m_ref.shape)

    s_kq = {}
    for c, rv in enumerate(rowv):
        for hd in heads:
            q = q_ref[0, rv["ts"], sls[hd]]
            k = k_ref[0, rv["ts"], sls[hd]]
            g_col = _col_bcast(rv["g"][hd:hd + 1, :], L)
            decay = jnp.exp(jnp.where(causal, g_col + rv["a_row"][hd:hd + 1, :], -jnp.inf))
            s_kq[c, hd] = (_dot_nt(k, q) * decay).astype(BF16)

    st = [st_ref[hd] for hd in heads]
    for c, rv in enumerate(rowv):
        ts = rv["ts"]
        vts = [jnp.concatenate([vt_ref[0, sls[hd], ts], ones_tail], axis=0) for hd in heads]
        carried = [_dot_nt(st[hd].astype(BF16), q_ref[0, ts, sls[hd]]) for hd in heads]
        intra = [jnp.dot(vts[hd], s_kq[c, hd], preferred_element_type=F32) for hd in heads]
        for hd in heads:
            mixed = intra[hd] + rv["inter"][hd:hd + 1, :] * carried[hd]
            den = mixed[dh:dh + 1, :]
            hh = mixed[0:dh] / jnp.maximum(jnp.abs(den), rv["floor"][hd:hd + 1, :])
            hh = hh * lax.rsqrt(jnp.mean(hh * hh, axis=0, keepdims=True) + EPS)
            o_ref[0, sls[hd], ts] = (hh * gh_ref[sls[hd], :] * gat_ref[0, sls[hd], ts].astype(F32)).astype(BF16)
        for hd in heads:
            vw = (vts[hd].astype(F32) * rv["w"][hd:hd + 1, :]).astype(BF16)
            st_loc = jnp.dot(vw, k_ref[0, ts, sls[hd]], preferred_element_type=F32)
            st[hd] = rv["s_old"][hd:hd + 1, :] * st[hd] + rv["s_new"][hd:hd + 1, :] * st_loc

    for hd in heads:
        st_ref[hd] = st[hd]


def _mlstm(qm, km, vmt, gat, gates, g_head):
    b, s, dm = qm.shape
    L = MLSTM_CHUNK * MLSTM_STEP_CHUNKS
    row = lambda bi, ci: (bi, ci, 0)
    col = lambda bi, ci: (bi, 0, ci)
    g_bc = jnp.broadcast_to(g_head.astype(F32).reshape(dm, 1), (dm, 128))
    return pl.pallas_call(
        _mlstm_kernel,
        grid=(b, s // L),
        in_specs=[pl.BlockSpec((1, L, dm), row), pl.BlockSpec((1, L, dm), row),
                  pl.BlockSpec((1, dm, L), col), pl.BlockSpec((1, dm, L), col),
                  pl.BlockSpec((1, 24, L), col),
                  pl.BlockSpec((dm, 128), lambda bi, ci: (0, 0))],
        out_specs=pl.BlockSpec((1, dm, L), col),
        out_shape=jax.ShapeDtypeStruct((b, dm, s), BF16),
        scratch_shapes=[pltpu.VMEM((N_HEADS_MLSTM, HEAD_DIM_MLSTM + 16, HEAD_DIM_MLSTM), F32),
                        pltpu.VMEM((8, 128), F32)],
        compiler_params=pltpu.CompilerParams(dimension_semantics=("arbitrary", "arbitrary"),
                                             vmem_limit_bytes=VMEM_LIMIT),
        name="mlstm",
    )(qm, km, vmt, gat, gates, g_bc)


def _moba_steps(nb):
    jt, it = [], []
    for j in range(nb):
        for i in [j] + list(range(j)):
            jt.append(j)
            it.append(i)
    return jt, it


def _moba_key_extras(nb):
    blk = MOBA_BLOCK
    lane = np.arange(N_HEADS_MOBA * 128)
    head = lane // 128
    w = (lane % 128 - np.where(head % 2 == 0, HEAD_DIM_MOBA, 0))[None, None, :]
    slope = (2.0 ** -(head + 1.0))[None, None, :]
    i = np.arange(nb)[:, None, None]
    c = np.arange(blk)[None, :, None]
    tab = np.where(w == i, 1.0, 0.0)
    rest = LOG2E * slope * (i * blk + c)
    for term in range(N_POS_TERMS):
        piece = rest.astype(np.float32).astype(BF16).astype(np.float64)
        tab = tab + np.where(w == nb + term, piece, 0.0)
        rest = rest - piece
    return jnp.asarray(tab.astype(np.float32).astype(BF16))


def _moba_kernel(jt_ref, it_ref, qt_ref, k_ref, vt_ref, kmean_ref, gt_ref, o_ref,
                 qaug_ref, m_ref, acc_ref, cmax_ref, *s_refs):
    t = pl.program_id(1)
    j = jt_ref[t]
    i = it_ref[t]
    blk = MOBA_BLOCK
    dh = HEAD_DIM_MOBA
    nb = kmean_ref.shape[1]
    nbat = qt_ref.shape[0]
    first = i == j
    last = jnp.logical_or(i == j - 1, j == 0)

    def key_tile(bb, h):
        return k_ref[bb, :, 128 * h:128 * (h + 1)]

    def rows(h):
        return slice(dh * h, dh * (h + 1))

    ones_tail = jnp.ones((16, blk), BF16)

    def values_t(bb, h):
        return jnp.concatenate([vt_ref[bb, 0, rows(h), :], ones_tail], axis=0)

    @pl.when(first)
    def _():
        blk_i = lax.broadcasted_iota(jnp.int32, (nb, blk), 0)
        lane_m = lax.broadcasted_iota(jnp.int32, (nb, 2 * dh), 1)
        key_c = lax.broadcasted_iota(jnp.int32, (blk, blk), 0)
        qry_r = lax.broadcasted_iota(jnp.int32, (blk, blk), 1)
        causal = key_c <= qry_r
        ones_rows = jnp.where(blk_i < N_POS_TERMS, 1.0, 0.0)
        gates = {}
        for bb in range(nbat):
            for h in range(N_HEADS_MOBA):
                pr, hh = divmod(h, 2)
                qt_pair = qt_ref[bb, 128 * pr:128 * (pr + 1), :]
                in_head = (lane_m < dh) if hh == 0 else (lane_m >= dh)
                km = jnp.where(in_head, kmean_ref[bb, :, 128 * pr:128 * (pr + 1)], 0.0).astype(BF16)
                gates[bb, h] = jnp.dot(km, qt_pair, preferred_element_type=F32)
        for bb in range(nbat):
            for h in range(N_HEADS_MOBA):
                pr, hh = divmod(h, 2)
                gate = gates[bb, h]
                rank = jnp.zeros((nb, blk), jnp.int32)
                for i2 in range(nb):
                    g2 = gate[i2:i2 + 1, :]
                    beats = (g2 > gate) | ((g2 == gate) & (i2 < blk_i))
                    rank = rank + jnp.where(beats & (i2 < j), 1, 0)
                keep = ((blk_i < j) & (rank < MOBA_TOPK)) | (blk_i == j)
                sel_bias = jnp.where(keep, 0.0, MASK_BIAS)
                extra = jnp.concatenate([sel_bias, ones_rows, jnp.zeros((dh - 16, blk), F32)], axis=0)
                extra = extra.astype(BF16)
                qt_h = qt_ref[bb, dh * h:dh * (h + 1), :]
                qaug_ref[bb, h] = jnp.concatenate([qt_h, extra] if hh == 0 else [extra, qt_h], axis=0)

        for bb in range(nbat):
            for h in range(N_HEADS_MOBA):
                s = jnp.dot(key_tile(bb, h), qaug_ref[bb, h], preferred_element_type=F32)
                s = jnp.where(causal, s, MASK_BIAS)
                s_refs[bb][h] = s
                m_ref[bb, h:h + 1, :] = jnp.max(s, axis=0, keepdims=True)

        for bb in range(nbat):
            for h in range(N_HEADS_MOBA):
                p0 = jnp.exp2(s_refs[bb][h] - m_ref[bb, h:h + 1, :])
                acc_ref[bb, h] = jnp.dot(values_t(bb, h), p0.astype(BF16), preferred_element_type=F32)

    @pl.when(jnp.logical_not(first))
    def _():
        for bb in range(nbat):
            for h in range(N_HEADS_MOBA):
                s = jnp.dot(key_tile(bb, h), qaug_ref[bb, h], preferred_element_type=F32)
                s_refs[bb][h] = s
                cmax_ref[bb, h:h + 1, :] = jnp.max(s, axis=0, keepdims=True)
        for bb in range(nbat):
            for h in range(N_HEADS_MOBA):
                m_run = m_ref[bb, h:h + 1, :]
                m_new = jnp.maximum(m_run, cmax_ref[bb, h:h + 1, :])
                alpha = jnp.exp2(m_run - m_new)
                pr_ = jnp.exp2(s_refs[bb][h] - m_new)
                m_ref[bb, h:h + 1, :] = m_new
                acc_ref[bb, h] = alpha * acc_ref[bb, h] + jnp.dot(values_t(bb, h), pr_.astype(BF16),
                                                                  preferred_element_type=F32)

    @pl.when(last)
    def _():
        for bb in range(nbat):
            for h in range(N_HEADS_MOBA):
                acc = acc_ref[bb, h]
                out = acc[0:dh] / acc[dh:dh + 1] * gt_ref[bb, rows(h), :].astype(F32)
                o_ref[bb, rows(h), :] = out.astype(BF16)


def _moba(qbt, kaug, vbt, kmean, gbt):
    b, dmb, s = qbt.shape
    blk = MOBA_BLOCK
    nb = s // blk
    kw = kaug.shape[2]
    nbat = MOBA_BATCH
    assert nb == 8, "selection extras assume 8 key blocks"
    jt, it = _moba_steps(nb)
    qmap = lambda bi, t, jt_r, it_r: (bi, 0, jt_r[t])
    grid_spec = pltpu.PrefetchScalarGridSpec(
        num_scalar_prefetch=2,
        grid=(b // nbat, len(jt)),
        in_specs=[pl.BlockSpec((nbat, dmb, blk), qmap),
                  pl.BlockSpec((nbat, blk, kw), lambda bi, t, jt_r, it_r: (bi, it_r[t], 0)),
                  pl.BlockSpec((nbat, 1, dmb, blk), lambda bi, t, jt_r, it_r: (bi, it_r[t], 0, 0)),
                  pl.BlockSpec((nbat, nb, dmb), lambda bi, t, jt_r, it_r: (bi, 0, 0)),
                  pl.BlockSpec((nbat, dmb, blk), qmap)],
        out_specs=pl.BlockSpec((nbat, dmb, blk), qmap),
        scratch_shapes=[pltpu.VMEM((nbat, N_HEADS_MOBA, 2 * HEAD_DIM_MOBA, blk), BF16),
                        pltpu.VMEM((nbat, N_HEADS_MOBA, blk), F32),
                        pltpu.VMEM((nbat, N_HEADS_MOBA, HEAD_DIM_MOBA + 16, blk), F32),
                        pltpu.VMEM((nbat, N_HEADS_MOBA, blk), F32)]
        + [pltpu.VMEM((N_HEADS_MOBA, blk, blk), F32) for _ in range(nbat)],
    )
    return pl.pallas_call(
        _moba_kernel,
        grid_spec=grid_spec,
        out_shape=jax.ShapeDtypeStruct((b, dmb, s), BF16),
        compiler_params=pltpu.CompilerParams(dimension_semantics=("arbitrary", "arbitrary"),
                                             vmem_limit_bytes=VMEM_LIMIT),
        name="moba",
    )(jnp.asarray(jt, jnp.int32), jnp.asarray(it, jnp.int32), qbt, kaug, vbt, kmean, gbt)


def _out_kernel(x_ref, gate_ref, oat_ref, obt_ref, w_ref, gf_ref, o_ref, wb_ref):
    @pl.when(jnp.logical_and(pl.program_id(0) == 0, pl.program_id(1) == 0))
    def _():
        for r in range(0, w_ref.shape[0], 256):
            wb_ref[r:r + 256, :] = w_ref[r:r + 256, :].astype(BF16)

    y = (_dot_tn(oat_ref[0], wb_ref[0:D_MLSTM, :])
         + _dot_tn(obt_ref[0], wb_ref[D_MLSTM:D_MLSTM + D_MOBA, :]))
    r = x_ref[0] + gate_ref[0] * y
    o_ref[0] = r * lax.rsqrt(jnp.mean(r * r, axis=-1, keepdims=True) + EPS) * gf_ref[...]


def _out(x, mod, out_at, out_bt, w_out, g_final):
    b, s, d = x.shape
    tm = OUT_TILE
    row = lambda bi, j: (bi, j, 0)
    col = lambda bi, j: (bi, 0, j)
    full2 = lambda bi, j: (0, 0)
    return pl.pallas_call(
        _out_kernel,
        grid=(b, s // tm),
        in_specs=[pl.BlockSpec((1, tm, d), row),
                  pl.BlockSpec((1, 1, d), lambda bi, j: (bi, 0, 2)),
                  pl.BlockSpec((1, D_MLSTM, tm), col),
                  pl.BlockSpec((1, D_MOBA, tm), col),
                  pl.BlockSpec(w_out.shape, full2, pipeline_mode=pl.Buffered(1)),
                  pl.BlockSpec((1, d), full2)],
        out_specs=pl.BlockSpec((1, tm, d), row),
        out_shape=jax.ShapeDtypeStruct((b, s, d), x.dtype),
        scratch_shapes=[pltpu.VMEM(w_out.shape, BF16)],
        compiler_params=pltpu.CompilerParams(dimension_semantics=("arbitrary", "arbitrary"),
                                             vmem_limit_bytes=VMEM_LIMIT),
        name="out",
    )(x, mod, out_at, out_bt, w_out, g_final.reshape(1, d))


def _layer(x, c, w_ada, b_ada, g_norm, w_in, conv_w, conv_b, b_igate, b_fgate, g_mlstm_head, w_out,
           g_final):
    b = x.shape[0]
    mod = _ada(c, w_ada, b_ada).reshape(b, 1, 3 * D_MODEL)

    bif = jnp.concatenate([b_igate.astype(F32), jnp.zeros((4,), F32), b_fgate.astype(F32),
                           jnp.zeros((116,), F32)]).reshape(1, 128)

    qm, km, vmt, gat, gates, qbt, kaug, kmean, vbt, gbt = _proj(
        x, mod, g_norm, w_in, conv_w, conv_b, bif)
    out_at = _mlstm(qm, km, vmt, gat, gates, g_mlstm_head)
    out_bt = _moba(qbt, kaug, vbt, kmean, gbt)
    return _out(x, mod, out_at, out_bt, w_out, g_final)


def kernel(x, c, w_ada, b_ada, g_norm, w_in, conv_w, conv_b, b_igate, b_fgate, g_mlstm_head, w_out,
           g_final):
    assert w_ada.shape[0] == 1, "single-layer trunk"
    return _layer(x, c, w_ada[0], b_ada[0], g_norm[0], w_in[0], conv_w[0], conv_b[0], b_igate[0],
                  b_fgate[0], g_mlstm_head[0], w_out[0], g_final)
```

```python
import jax
import jax.numpy as jnp
import numpy as np
from jax import lax
from jax.experimental import pallas as pl
from jax.experimental.pallas import tpu as pltpu

F32 = jnp.float32
BF16 = jnp.bfloat16

D_MODEL = 1024
D_MLSTM = 512
N_HEADS_MLSTM = 4
HEAD_DIM_MLSTM = 128
D_MOBA = 512
N_HEADS_MOBA = 8
HEAD_DIM_MOBA = 64
CONV_K = 4
MOBA_BLOCK = 256
MOBA_TOPK = 3
EPS = 1e-6

ROW_TILE = 256
PROJ_CHUNK = 256
MLSTM_CHUNK = 128
MLSTM_STEP_CHUNKS = 4
MOBA_BATCH = 2
OUT_TILE = 512
MASK_BIAS = -1e30
LOG2E = 1.4426950408889634
N_POS_TERMS = 3
VMEM_LIMIT = 56 * 1024 * 1024

_OFF = {}
_o = 0
for _name, _size in (("qm", 512), ("km", 512), ("vm", 512), ("om", 512), ("im", 4), ("fm", 4),
                     ("zm", 512), ("qb", 512), ("kb", 512), ("vb", 512), ("zb", 512)):
    _OFF[_name] = (_o, _o + _size)
    _o += _size
_NN_GROUPS = ("qm", "km", "kb")
_NT_GROUPS = ("qb", "vb", "zb", "vm", "om", "zm")


def _silu(v):
    return v * jax.nn.sigmoid(v)


def _dot_nt(a, b):
    return lax.dot_general(a, b, (((1,), (1,)), ((), ())), preferred_element_type=F32)


def _dot_tn(a, b):
    return lax.dot_general(a, b, (((0,), (0,)), ((), ())), preferred_element_type=F32)


def _ada_kernel(c_ref, w_ref, b_ref, o_ref):
    a = _silu(c_ref[...])
    o_ref[...] = jnp.dot(a, w_ref[...], preferred_element_type=F32,
                         precision=lax.Precision.HIGHEST) + b_ref[...]


def _ada(c, w_ada, b_ada):
    b, d = c.shape
    n = w_ada.shape[1]
    tn = 1024
    return pl.pallas_call(
        _ada_kernel,
        grid=(n // tn,),
        in_specs=[pl.BlockSpec((b, d), lambda i: (0, 0)),
                  pl.BlockSpec((d, tn), lambda i: (0, i)),
                  pl.BlockSpec((1, tn), lambda i: (0, i))],
        out_specs=pl.BlockSpec((b, tn), lambda i: (0, i)),
        out_shape=jax.ShapeDtypeStruct((b, n), F32),
        compiler_params=pltpu.CompilerParams(dimension_semantics=("arbitrary",),
                                             vmem_limit_bytes=VMEM_LIMIT),
        name="ada",
    )(c, w_ada, b_ada.reshape(1, n))


def _chunk_scan(v, combine, fill):
    pos = lax.broadcasted_iota(jnp.int32, v.shape, 1) & (MLSTM_CHUNK - 1)
    d = 1
    while d < MLSTM_CHUNK:
        shifted = pltpu.roll(v, d, axis=1)
        v = combine(v, jnp.where(pos >= d, shifted, fill))
        d *= 2
    return v


def _proj_kernel(x_ref, shift_ref, scale_ref, gn_ref, w_ref, cw_ref, cb_ref, bif_ref, kx_ref,
                 qm_ref, km_ref, vm_ref, ga_ref, gates_ref, qbt_ref, kaug_ref, kmean_ref, vbt_ref, gbt_ref,
                 ext_ref, wnn_ref, wnt_ref, wif_ref):
    j = pl.program_id(1)
    tm = x_ref.shape[1]

    @pl.when(jnp.logical_and(pl.program_id(0) == 0, j == 0))
    def _():
        for g, name in enumerate(_NN_GROUPS):
            lo = _OFF[name][0]
            for c in range(0, 512, 256):
                wnn_ref[:, 512 * g + c:512 * g + c + 256] = jnp.transpose(
                    w_ref[lo + c:lo + c + 256, :]).astype(BF16)
        for g, name in enumerate(_NT_GROUPS):
            lo = _OFF[name][0]
            for c in range(0, 512, 256):
                wnt_ref[512 * g + c:512 * g + c + 256, :] = w_ref[lo + c:lo + c + 256, :].astype(BF16)
        blk = jnp.transpose(w_ref[_OFF["im"][0]:_OFF["im"][0] + 128, :])
        lane_w = lax.broadcasted_iota(jnp.int32, blk.shape, 1)
        f_cols = pltpu.roll(blk, 4, axis=1)
        wif_ref[...] = jnp.where(lane_w < 4, blk,
                                 jnp.where((lane_w >= 8) & (lane_w < 12), f_cols, 0.0)).astype(BF16)

    @pl.when(j == 0)
    def _():
        ext_ref[0:8, :] = jnp.zeros((8, 2 * D_MLSTM), F32)

    x = x_ref[0]
    ms = jnp.mean(x * x, axis=-1, keepdims=True)
    h = x * lax.rsqrt(ms + EPS) * gn_ref[...] * (1.0 + scale_ref[0]) + shift_ref[0]
    hb = h.astype(BF16)

    ifc = jnp.dot(hb, wif_ref[...], preferred_element_type=F32) + bif_ref[...]
    ift = jnp.transpose(ifc)[0:16, :]
    log_i = ift[0:8]
    f_pre = ift[8:16]
    log_f = jnp.minimum(f_pre, 0.0) - jnp.log1p(jnp.exp(-jnp.abs(f_pre)))
    cum_f = _chunk_scan(log_f, jnp.add, 0.0)
    g = log_i - cum_f
    gates_ref[0, 0:8, :] = cum_f
    gates_ref[0, 8:16, :] = g
    gates_ref[0, 16:24, :] = _chunk_scan(g, jnp.maximum, -jnp.inf)

    def nn(lo, hi):
        return jnp.dot(hb, wnn_ref[:, lo:hi], preferred_element_type=F32)

    def nt(lo, hi):
        return _dot_nt(wnt_ref[lo:hi, :], hb)

    cw = PROJ_CHUNK

    def conv_chunk(lo):
        cs = slice(lo, lo + cw)
        qk = nn(lo, lo + cw)
        ext_ref[8:8 + tm, cs] = qk
        conv = cb_ref[:, cs] + qk * cw_ref[CONV_K - 1:CONV_K, cs]
        for tap in range(CONV_K - 1):
            conv = conv + ext_ref[pl.ds(8 - (CONV_K - 1) + tap, tm), cs] * cw_ref[tap:tap + 1, cs]
        ext_ref[0:8, cs] = qk[tm - 8:tm, :]
        act = _silu(conv)
        if lo < D_MLSTM:
            qm_ref[0, :, lo:lo + cw] = act.astype(BF16)
        else:
            km_ref[0, :, lo - D_MLSTM:lo - D_MLSTM + cw] = (act * (HEAD_DIM_MLSTM ** -0.5)).astype(BF16)

    def value_chunk(lo):
        vm_ref[0, lo:lo + cw, :] = nt(1536 + lo, 1536 + lo + cw).astype(BF16)

    def gate_chunk(lo):
        ga_ref[0, lo:lo + cw, :] = (jax.nn.sigmoid(nt(2048 + lo, 2048 + lo + cw))
                                   * _silu(nt(2560 + lo, 2560 + lo + cw))).astype(BF16)

    lane = lax.broadcasted_iota(jnp.int32, (tm, 128), 1)

    def key_chunk(lo):
        kb = nn(1024 + lo, 1024 + lo + cw)
        kmean_ref[0, pl.ds(j, 1), lo:lo + cw] = jnp.mean(kb, axis=0, keepdims=True)
        kb2 = (kb * LOG2E).astype(BF16)
        for pr in range(lo // 128, (lo + cw) // 128):
            for hh in range(2):
                hd = 2 * pr + hh
                is_key = (lane < HEAD_DIM_MOBA) if hh == 0 else (lane >= HEAD_DIM_MOBA)
                grp = slice(128 * hd, 128 * (hd + 1))
                kaug_ref[0, :, grp] = jnp.where(is_key, kb2[:, 128 * pr - lo:128 * (pr + 1) - lo],
                                                kx_ref[0, :, grp])

    def qt_chunk(lo):
        qbt_ref[0, lo:lo + cw, :] = (nt(lo, lo + cw) * (HEAD_DIM_MOBA ** -0.5)).astype(BF16)

    def vt_chunk(lo):
        vbt_ref[0, 0, lo:lo + cw, :] = nt(512 + lo, 512 + lo + cw).astype(BF16)

    def gt_chunk(lo):
        gbt_ref[0, lo:lo + cw, :] = _silu(nt(1024 + lo, 1024 + lo + cw)).astype(BF16)

    heavy = ([(conv_chunk, lo) for lo in range(0, 2 * D_MLSTM, cw)]
             + [(gate_chunk, lo) for lo in range(0, D_MLSTM, cw)]
             + [(key_chunk, lo) for lo in range(0, D_MOBA, cw)])
    light = ([(qt_chunk, lo) for lo in range(0, D_MOBA, cw)]
             + [(vt_chunk, lo) for lo in range(0, D_MOBA, cw)]
             + [(value_chunk, lo) for lo in range(0, D_MLSTM, cw)]
             + [(gt_chunk, lo) for lo in range(0, D_MOBA, cw)])
    for idx in range(max(len(heavy), len(light))):
        for group in (heavy, light):
            if idx < len(group):
                fn, lo = group[idx]
                fn(lo)


def _proj(x, mod, g_norm, w_in, conv_w, conv_b, bif):
    b, s, d = x.shape
    tm = ROW_TILE
    nb = s // MOBA_BLOCK
    assert tm == MOBA_BLOCK
    kx = _moba_key_extras(nb)
    kw = N_HEADS_MOBA * 128
    row = lambda bi, j: (bi, j, 0)
    col = lambda bi, j: (bi, 0, j)
    full2 = lambda bi, j: (0, 0)
    out_shapes = (
        jax.ShapeDtypeStruct((b, s, D_MLSTM), BF16),
        jax.ShapeDtypeStruct((b, s, D_MLSTM), BF16),
        jax.ShapeDtypeStruct((b, D_MLSTM, s), BF16),
        jax.ShapeDtypeStruct((b, D_MLSTM, s), BF16),
        jax.ShapeDtypeStruct((b, 24, s), F32),
        jax.ShapeDtypeStruct((b, D_MOBA, s), BF16),
        jax.ShapeDtypeStruct((b, s, kw), BF16),
        jax.ShapeDtypeStruct((b, nb, D_MOBA), F32),
        jax.ShapeDtypeStruct((b, nb, D_MOBA, MOBA_BLOCK), BF16),
        jax.ShapeDtypeStruct((b, D_MOBA, s), BF16),
    )
    out_specs = (
        pl.BlockSpec((1, tm, D_MLSTM), row),
        pl.BlockSpec((1, tm, D_MLSTM), row),
        pl.BlockSpec((1, D_MLSTM, tm), col),
        pl.BlockSpec((1, D_MLSTM, tm), col),
        pl.BlockSpec((1, 24, tm), col),
        pl.BlockSpec((1, D_MOBA, tm), col),
        pl.BlockSpec((1, tm, kw), row),
        pl.BlockSpec((1, nb, D_MOBA), lambda bi, j: (bi, 0, 0)),
        pl.BlockSpec((1, 1, D_MOBA, tm), lambda bi, j: (bi, j, 0, 0)),
        pl.BlockSpec((1, D_MOBA, tm), col),
    )
    in_specs = [
        pl.BlockSpec((1, tm, d), row),
        pl.BlockSpec((1, 1, d), lambda bi, j: (bi, 0, 0)),
        pl.BlockSpec((1, 1, d), lambda bi, j: (bi, 0, 1)),
        pl.BlockSpec((1, d), full2),
        pl.BlockSpec(w_in.shape, full2, pipeline_mode=pl.Buffered(1)),
        pl.BlockSpec(conv_w.shape, full2),
        pl.BlockSpec((1, 2 * D_MLSTM), full2),
        pl.BlockSpec((1, 128), full2),
        pl.BlockSpec((1, MOBA_BLOCK, kw), lambda bi, j: (j, 0, 0)),
    ]
    return pl.pallas_call(
        _proj_kernel,
        grid=(b, s // tm),
        in_specs=in_specs,
        out_specs=out_specs,
        out_shape=out_shapes,
        scratch_shapes=[pltpu.VMEM((tm + 8, 2 * D_MLSTM), F32),
                        pltpu.VMEM((d, 512 * len(_NN_GROUPS)), BF16),
                        pltpu.VMEM((512 * len(_NT_GROUPS), d), BF16),
                        pltpu.VMEM((d, 128), BF16)],
        compiler_params=pltpu.CompilerParams(dimension_semantics=("arbitrary", "arbitrary"),
                                             vmem_limit_bytes=VMEM_LIMIT),
        name="proj",
    )(x, mod, mod, g_norm.reshape(1, d), w_in, conv_w, conv_b.reshape(1, -1), bif, kx)


def _col_bcast(row, n):
    return jnp.transpose(jnp.broadcast_to(row, (n, n)))


def _mlstm_kernel(q_ref, k_ref, vt_ref, gat_ref, gates_ref, gh_ref, o_ref, st_ref, m_ref):
    ci = pl.program_id(1)
    L = MLSTM_CHUNK
    dh = HEAD_DIM_MLSTM
    assert L == dh == 128

    @pl.when(ci == 0)
    def _():
        st_ref[...] = jnp.zeros(st_ref.shape, F32)
        m_ref[...] = jnp.zeros(m_ref.shape, F32)

    s_idx = lax.broadcasted_iota(jnp.int32, (L, L), 0)
    t_idx = lax.broadcasted_iota(jnp.int32, (L, L), 1)
    causal = s_idx <= t_idx
    ones_tail = jnp.ones((16, L), BF16)

    heads = range(N_HEADS_MLSTM)
    sls = [slice(hd * dh, (hd + 1) * dh) for hd in heads]

    m_prev = m_ref[:, 0:1]
    rowv = []
    for c in range(MLSTM_STEP_CHUNKS):
        ts = slice(c * L, (c + 1) * L)
        cum_f = gates_ref[0, 0:8, ts]
        g = gates_ref[0, 8:16, ts]
        g_cmax = gates_ref[0, 16:24, ts]
        f_tot = cum_f[:, L - 1:L]
        g_max = g_cmax[:, L - 1:L]
        a_max = f_tot + g_max
        a_row = -jnp.maximum(m_prev, g_cmax)
        m_new = jnp.maximum(f_tot + m_prev, a_max)
        rowv.append(dict(
            ts=ts, g=g, a_row=a_row,
            w=jnp.exp(g - g_max),
            inter=jnp.exp(a_row + m_prev),
            floor=jnp.exp(a_row - cum_f),
            s_old=jnp.exp(f_tot + m_prev - m_new), s_new=jnp.exp(a_max - m_new)))
        m_prev = m_new
    m_ref[...] = jnp.broadcast_to(m_prev, m_ref.shape)

    s_kq = {}
    for c, rv in enumerate(rowv):
        for hd in heads:
            q = q_ref[0, rv["ts"], sls[hd]]
            k = k_ref[0, rv["ts"], sls[hd]]
            g_col = _col_bcast(rv["g"][hd:hd + 1, :], L)
            decay = jnp.exp(jnp.where(causal, g_col + rv["a_row"][hd:hd + 1, :], -jnp.inf))
            s_kq[c, hd] = (_dot_nt(k, q) * decay).astype(BF16)

    st = [st_ref[hd] for hd in heads]
    for c, rv in enumerate(rowv):
        ts = rv["ts"]
        vts = [jnp.concatenate([vt_ref[0, sls[hd], ts], ones_tail], axis=0) for hd in heads]
        carried = [_dot_nt(st[hd].astype(BF16), q_ref[0, ts, sls[hd]]) for hd in heads]
        intra = [jnp.dot(vts[hd], s_kq[c, hd], preferred_element_type=F32) for hd in heads]
        for hd in heads:
            mixed = intra[hd] + rv["inter"][hd:hd + 1, :] * carried[hd]
            den = mixed[dh:dh + 1, :]
            hh = mixed[0:dh] / jnp.maximum(jnp.abs(den), rv["floor"][hd:hd + 1, :])
            hh = hh * lax.rsqrt(jnp.mean(hh * hh, axis=0, keepdims=True) + EPS)
            o_ref[0, sls[hd], ts] = (hh * gh_ref[sls[hd], :] * gat_ref[0, sls[hd], ts].astype(F32)).astype(BF16)
        for hd in heads:
            vw = (vts[hd].astype(F32) * rv["w"][hd:hd + 1, :]).astype(BF16)
            st_loc = jnp.dot(vw, k_ref[0, ts, sls[hd]], preferred_element_type=F32)
            st[hd] = rv["s_old"][hd:hd + 1, :] * st[hd] + rv["s_new"][hd:hd + 1, :] * st_loc

    for hd in heads:
        st_ref[hd] = st[hd]


def _mlstm(qm, km, vmt, gat, gates, g_head):
    b, s, dm = qm.shape
    L = MLSTM_CHUNK * MLSTM_STEP_CHUNKS
    row = lambda bi, ci: (bi, ci, 0)
    col = lambda bi, ci: (bi, 0, ci)
    g_bc = jnp.broadcast_to(g_head.astype(F32).reshape(dm, 1), (dm, 128))
    return pl.pallas_call(
        _mlstm_kernel,
        grid=(b, s // L),
        in_specs=[pl.BlockSpec((1, L, dm), row), pl.BlockSpec((1, L, dm), row),
                  pl.BlockSpec((1, dm, L), col), pl.BlockSpec((1, dm, L), col),
                  pl.BlockSpec((1, 24, L), col),
                  pl.BlockSpec((dm, 128), lambda bi, ci: (0, 0))],
        out_specs=pl.BlockSpec((1, dm, L), col),
        out_shape=jax.ShapeDtypeStruct((b, dm, s), BF16),
        scratch_shapes=[pltpu.VMEM((N_HEADS_MLSTM, HEAD_DIM_MLSTM + 16, HEAD_DIM_MLSTM), F32),
                        pltpu.VMEM((8, 128), F32)],
        compiler_params=pltpu.CompilerParams(dimension_semantics=("arbitrary", "arbitrary"),
                                             vmem_limit_bytes=VMEM_LIMIT),
        name="mlstm",
    )(qm, km, vmt, gat, gates, g_bc)


def _moba_steps(nb):
    jt, it = [], []
    for j in range(nb):
        for i in [j] + list(range(j)):
            jt.append(j)
            it.append(i)
    return jt, it


def _moba_key_extras(nb):
    blk = MOBA_BLOCK
    lane = np.arange(N_HEADS_MOBA * 128)
    head = lane // 128
    w = (lane % 128 - np.where(head % 2 == 0, HEAD_DIM_MOBA, 0))[None, None, :]
    slope = (2.0 ** -(head + 1.0))[None, None, :]
    i = np.arange(nb)[:, None, None]
    c = np.arange(blk)[None, :, None]
    tab = np.where(w == i, 1.0, 0.0)
    rest = LOG2E * slope * (i * blk + c)
    for term in range(N_POS_TERMS):
        piece = rest.astype(np.float32).astype(BF16).astype(np.float64)
        tab = tab + np.where(w == nb + term, piece, 0.0)
        rest = rest - piece
    return jnp.asarray(tab.astype(np.float32).astype(BF16))


def _moba_kernel(jt_ref, it_ref, qt_ref, k_ref, vt_ref, kmean_ref, gt_ref, o_ref,
                 qaug_ref, m_ref, acc_ref, cmax_ref, *s_refs):
    t = pl.program_id(1)
    j = jt_ref[t]
    i = it_ref[t]
    blk = MOBA_BLOCK
    dh = HEAD_DIM_MOBA
    nb = kmean_ref.shape[1]
    nbat = qt_ref.shape[0]
    first = i == j
    last = jnp.logical_or(i == j - 1, j == 0)

    def key_tile(bb, h):
        return k_ref[bb, :, 128 * h:128 * (h + 1)]

    def rows(h):
        return slice(dh * h, dh * (h + 1))

    ones_tail = jnp.ones((16, blk), BF16)

    def values_t(bb, h):
        return jnp.concatenate([vt_ref[bb, 0, rows(h), :], ones_tail], axis=0)

    @pl.when(first)
    def _():
        blk_i = lax.broadcasted_iota(jnp.int32, (nb, blk), 0)
        lane_m = lax.broadcasted_iota(jnp.int32, (nb, 2 * dh), 1)
        key_c = lax.broadcasted_iota(jnp.int32, (blk, blk), 0)
        qry_r = lax.broadcasted_iota(jnp.int32, (blk, blk), 1)
        causal = key_c <= qry_r
        ones_rows = jnp.where(blk_i < N_POS_TERMS, 1.0, 0.0)
        gates = {}
        for bb in range(nbat):
            for h in range(N_HEADS_MOBA):
                pr, hh = divmod(h, 2)
                qt_pair = qt_ref[bb, 128 * pr:128 * (pr + 1), :]
                in_head = (lane_m < dh) if hh == 0 else (lane_m >= dh)
                km = jnp.where(in_head, kmean_ref[bb, :, 128 * pr:128 * (pr + 1)], 0.0).astype(BF16)
                gates[bb, h] = jnp.dot(km, qt_pair, preferred_element_type=F32)
        for bb in range(nbat):
            for h in range(N_HEADS_MOBA):
                pr, hh = divmod(h, 2)
                gate = gates[bb, h]
                rank = jnp.zeros((nb, blk), jnp.int32)
                for i2 in range(nb):
                    g2 = gate[i2:i2 + 1, :]
                    beats = (g2 > gate) | ((g2 == gate) & (i2 < blk_i))
                    rank = rank + jnp.where(beats & (i2 < j), 1, 0)
                keep = ((blk_i < j) & (rank < MOBA_TOPK)) | (blk_i == j)
                sel_bias = jnp.where(keep, 0.0, MASK_BIAS)
                extra = jnp.concatenate([sel_bias, ones_rows, jnp.zeros((dh - 16, blk), F32)], axis=0)
                extra = extra.astype(BF16)
                qt_h = qt_ref[bb, dh * h:dh * (h + 1), :]
                qaug_ref[bb, h] = jnp.concatenate([qt_h, extra] if hh == 0 else [extra, qt_h], axis=0)

        for bb in range(nbat):
            for h in range(N_HEADS_MOBA):
                s = jnp.dot(key_tile(bb, h), qaug_ref[bb, h], preferred_element_type=F32)
                s = jnp.where(causal, s, MASK_BIAS)
                s_refs[bb][h] = s
                m_ref[bb, h:h + 1, :] = jnp.max(s, axis=0, keepdims=True)

        for bb in range(nbat):
            for h in range(N_HEADS_MOBA):
                p0 = jnp.exp2(s_refs[bb][h] - m_ref[bb, h:h + 1, :])
                acc_ref[bb, h] = jnp.dot(values_t(bb, h), p0.astype(BF16), preferred_element_type=F32)

    @pl.when(jnp.logical_not(first))
    def _():
        for bb in range(nbat):
            for h in range(N_HEADS_MOBA):
                s = jnp.dot(key_tile(bb, h), qaug_ref[bb, h], preferred_element_type=F32)
                s_refs[bb][h] = s
                cmax_ref[bb, h:h + 1, :] = jnp.max(s, axis=0, keepdims=True)
        for bb in range(nbat):
            for h in range(N_HEADS_MOBA):
                m_run = m_ref[bb, h:h + 1, :]
                m_new = jnp.maximum(m_run, cmax_ref[bb, h:h + 1, :])
                alpha = jnp.exp2(m_run - m_new)
                pr_ = jnp.exp2(s_refs[bb][h] - m_new)
                m_ref[bb, h:h + 1, :] = m_new
                acc_ref[bb, h] = alpha * acc_ref[bb, h] + jnp.dot(values_t(bb, h), pr_.astype(BF16),
                                                                  preferred_element_type=F32)

    @pl.when(last)
    def _():
        for bb in range(nbat):
            for h in range(N_HEADS_MOBA):
                acc = acc_ref[bb, h]
                out = acc[0:dh] / acc[dh:dh + 1] * gt_ref[bb, rows(h), :].astype(F32)
                o_ref[bb, rows(h), :] = out.astype(BF16)


def _moba(qbt, kaug, vbt, kmean, gbt):
    b, dmb, s = qbt.shape
    blk = MOBA_BLOCK
    nb = s // blk
    kw = kaug.shape[2]
    nbat = MOBA_BATCH
    assert nb == 8, "selection extras assume 8 key blocks"
    jt, it = _moba_steps(nb)
    qmap = lambda bi, t, jt_r, it_r: (bi, 0, jt_r[t])
    grid_spec = pltpu.PrefetchScalarGridSpec(
        num_scalar_prefetch=2,
        grid=(b // nbat, len(jt)),
        in_specs=[pl.BlockSpec((nbat, dmb, blk), qmap),
                  pl.BlockSpec((nbat, blk, kw), lambda bi, t, jt_r, it_r: (bi, it_r[t], 0)),
                  pl.BlockSpec((nbat, 1, dmb, blk), lambda bi, t, jt_r, it_r: (bi, it_r[t], 0, 0)),
                  pl.BlockSpec((nbat, nb, dmb), lambda bi, t, jt_r, it_r: (bi, 0, 0)),
                  pl.BlockSpec((nbat, dmb, blk), qmap)],
        out_specs=pl.BlockSpec((nbat, dmb, blk), qmap),
        scratch_shapes=[pltpu.VMEM((nbat, N_HEADS_MOBA, 2 * HEAD_DIM_MOBA, blk), BF16),
                        pltpu.VMEM((nbat, N_HEADS_MOBA, blk), F32),
                        pltpu.VMEM((nbat, N_HEADS_MOBA, HEAD_DIM_MOBA + 16, blk), F32),
                        pltpu.VMEM((nbat, N_HEADS_MOBA, blk), F32)]
        + [pltpu.VMEM((N_HEADS_MOBA, blk, blk), F32) for _ in range(nbat)],
    )
    return pl.pallas_call(
        _moba_kernel,
        grid_spec=grid_spec,
        out_shape=jax.ShapeDtypeStruct((b, dmb, s), BF16),
        compiler_params=pltpu.CompilerParams(dimension_semantics=("arbitrary", "arbitrary"),
                                             vmem_limit_bytes=VMEM_LIMIT),
        name="moba",
    )(jnp.asarray(jt, jnp.int32), jnp.asarray(it, jnp.int32), qbt, kaug, vbt, kmean, gbt)


def _out_kernel(x_ref, gate_ref, oat_ref, obt_ref, w_ref, gf_ref, o_ref, wb_ref):
    @pl.when(jnp.logical_and(pl.program_id(0) == 0, pl.program_id(1) == 0))
    def _():
        for r in range(0, w_ref.shape[0], 256):
            wb_ref[r:r + 256, :] = w_ref[r:r + 256, :].astype(BF16)

    y = (_dot_tn(oat_ref[0], wb_ref[0:D_MLSTM, :])
         + _dot_tn(obt_ref[0], wb_ref[D_MLSTM:D_MLSTM + D_MOBA, :]))
    r = x_ref[0] + gate_ref[0] * y
    o_ref[0] = r * lax.rsqrt(jnp.mean(r * r, axis=-1, keepdims=True) + EPS) * gf_ref[...]


def _out(x, mod, out_at, out_bt, w_out, g_final):
    b, s, d = x.shape
    tm = OUT_TILE
    row = lambda bi, j: (bi, j, 0)
    col = lambda bi, j: (bi, 0, j)
    full2 = lambda bi, j: (0, 0)
    return pl.pallas_call(
        _out_kernel,
        grid=(b, s // tm),
        in_specs=[pl.BlockSpec((1, tm, d), row),
                  pl.BlockSpec((1, 1, d), lambda bi, j: (bi, 0, 2)),
                  pl.BlockSpec((1, D_MLSTM, tm), col),
                  pl.BlockSpec((1, D_MOBA, tm), col),
                  pl.BlockSpec(w_out.shape, full2, pipeline_mode=pl.Buffered(1)),
                  pl.BlockSpec((1, d), full2)],
        out_specs=pl.BlockSpec((1, tm, d), row),
        out_shape=jax.ShapeDtypeStruct((b, s, d), x.dtype),
        scratch_shapes=[pltpu.VMEM(w_out.shape, BF16)],
        compiler_params=pltpu.CompilerParams(dimension_semantics=("arbitrary", "arbitrary"),
                                             vmem_limit_bytes=VMEM_LIMIT),
        name="out",
    )(x, mod, out_at, out_bt, w_out, g_final.reshape(1, d))


def _layer(x, c, w_ada, b_ada, g_norm, w_in, conv_w, conv_b, b_igate, b_fgate, g_mlstm_head, w_out,
           g_final):
    b = x.shape[0]
    mod = _ada(c, w_ada, b_ada).reshape(b, 1, 3 * D_MODEL)

    bif = jnp.concatenate([b_igate.astype(F32), jnp.zeros((4,), F32), b_fgate.astype(F32),
                           jnp.zeros((116,), F32)]).reshape(1, 128)

    qm, km, vmt, gat, gates, qbt, kaug, kmean, vbt, gbt = _proj(
        x, mod, g_norm, jnp.transpose(w_in), conv_w, conv_b, bif)
    out_at = _mlstm(qm, km, vmt, gat, gates, g_mlstm_head)
    out_bt = _moba(qbt, kaug, vbt, kmean, gbt)
    return _out(x, mod, out_at, out_bt, w_out, g_final)


def kernel(x, c, w_ada, b_ada, g_norm, w_in, conv_w, conv_b, b_igate, b_fgate, g_mlstm_head, w_out,
           g_final):
    assert w_ada.shape[0] == 1, "single-layer trunk"
    return _layer(x, c, w_ada[0], b_ada[0], g_norm[0], w_in[0], conv_w[0], conv_b[0], b_igate[0],
                  b_fgate[0], g_mlstm_head[0], w_out[0], g_final)
```

```python
import jax
import jax.numpy as jnp
import numpy as np
from jax import lax
from jax.experimental import pallas as pl
from jax.experimental.pallas import tpu as pltpu

F32 = jnp.float32
BF16 = jnp.bfloat16

D_MODEL = 1024
D_MLSTM = 512
N_HEADS_MLSTM = 4
HEAD_DIM_MLSTM = 128
D_MOBA = 512
N_HEADS_MOBA = 8
HEAD_DIM_MOBA = 64
CONV_K = 4
MOBA_BLOCK = 256
MOBA_TOPK = 3
EPS = 1e-6

ROW_TILE = 256
PROJ_CHUNK = 256
MLSTM_CHUNK = 128
MLSTM_STEP_CHUNKS = 4
MOBA_BATCH = 2
OUT_TILE = 512
MASK_BIAS = -1e30
LOG2E = 1.4426950408889634
N_POS_TERMS = 3
VMEM_LIMIT = 56 * 1024 * 1024

_OFF = {}
_o = 0
for _name, _size in (("qm", 512), ("km", 512), ("vm", 512), ("om", 512), ("im", 4), ("fm", 4),
                     ("zm", 512), ("qb", 512), ("kb", 512), ("vb", 512), ("zb", 512)):
    _OFF[_name] = (_o, _o + _size)
    _o += _size
_NN_GROUPS = ("qm", "km", "kb")
_NT_GROUPS = ("qb", "vb", "zb", "vm", "om", "zm")


def _silu(v):
    return v * jax.nn.sigmoid(v)


def _dot_nt(a, b):
    return lax.dot_general(a, b, (((1,), (1,)), ((), ())), preferred_element_type=F32)


def _dot_tn(a, b):
    return lax.dot_general(a, b, (((0,), (0,)), ((), ())), preferred_element_type=F32)


def _ada_kernel(c_ref, w_ref, b_ref, o_ref):
    a = _silu(c_ref[...])
    o_ref[...] = jnp.dot(a, w_ref[...], preferred_element_type=F32,
                         precision=lax.Precision.HIGHEST) + b_ref[...]


def _ada(c, w_ada, b_ada):
    b, d = c.shape
    n = w_ada.shape[1]
    tn = 1024
    return pl.pallas_call(
        _ada_kernel,
        grid=(n // tn,),
        in_specs=[pl.BlockSpec((b, d), lambda i: (0, 0)),
                  pl.BlockSpec((d, tn), lambda i: (0, i)),
                  pl.BlockSpec((1, tn), lambda i: (0, i))],
        out_specs=pl.BlockSpec((b, tn), lambda i: (0, i)),
        out_shape=jax.ShapeDtypeStruct((b, n), F32),
        compiler_params=pltpu.CompilerParams(dimension_semantics=("arbitrary",),
                                             vmem_limit_bytes=VMEM_LIMIT),
        name="ada",
    )(c, w_ada, b_ada.reshape(1, n))


def _chunk_scan(v, combine, fill):
    pos = lax.broadcasted_iota(jnp.int32, v.shape, 1) & (MLSTM_CHUNK - 1)
    d = 1
    while d < MLSTM_CHUNK:
        shifted = pltpu.roll(v, d, axis=1)
        v = combine(v, jnp.where(pos >= d, shifted, fill))
        d *= 2
    return v


def _proj_kernel(x_ref, shift_ref, scale_ref, gn_ref, w_ref, cw_ref, cb_ref, bif_ref, kx_ref,
                 qm_ref, km_ref, vm_ref, ga_ref, gates_ref, qbt_ref, kaug_ref, kmean_ref, vbt_ref, gbt_ref,
                 ext_ref, wnn_ref, wnt_ref, wif_ref):
    j = pl.program_id(1)
    tm = x_ref.shape[1]

    @pl.when(jnp.logical_and(pl.program_id(0) == 0, j == 0))
    def _():
        for g, name in enumerate(_NN_GROUPS):
            lo = _OFF[name][0]
            for c in range(0, 512, 256):
                wnn_ref[:, 512 * g + c:512 * g + c + 256] = jnp.transpose(
                    w_ref[lo + c:lo + c + 256, :]).astype(BF16)
        for g, name in enumerate(_NT_GROUPS):
            lo = _OFF[name][0]
            for c in range(0, 512, 256):
                wnt_ref[512 * g + c:512 * g + c + 256, :] = w_ref[lo + c:lo + c + 256, :].astype(BF16)
        blk = jnp.transpose(w_ref[_OFF["im"][0]:_OFF["im"][0] + 128, :])
        lane_w = lax.broadcasted_iota(jnp.int32, blk.shape, 1)
        f_cols = pltpu.roll(blk, 4, axis=1)
        wif_ref[...] = jnp.where(lane_w < 4, blk,
                                 jnp.where((lane_w >= 8) & (lane_w < 12), f_cols, 0.0)).astype(BF16)

    @pl.when(j == 0)
    def _():
        ext_ref[0:8, :] = jnp.zeros((8, 2 * D_MLSTM), F32)

    x = x_ref[0]
    ms = jnp.mean(x * x, axis=-1, keepdims=True)
    h = x * lax.rsqrt(ms + EPS) * gn_ref[...] * (1.0 + scale_ref[0]) + shift_ref[0]
    hb = h.astype(BF16)

    ifc = jnp.dot(hb, wif_ref[...], preferred_element_type=F32) + bif_ref[...]
    ift = jnp.transpose(ifc)[0:16, :]
    log_i = ift[0:8]
    f_pre = ift[8:16]
    log_f = jnp.minimum(f_pre, 0.0) - jnp.log1p(jnp.exp(-jnp.abs(f_pre)))
    cum_f = _chunk_scan(log_f, jnp.add, 0.0)
    g = log_i - cum_f
    gates_ref[0, 0:8, :] = cum_f
    gates_ref[0, 8:16, :] = g
    gates_ref[0, 16:24, :] = _chunk_scan(g, jnp.maximum, -jnp.inf)

    def nn(lo, hi):
        return jnp.dot(hb, wnn_ref[:, lo:hi], preferred_element_type=F32)

    def nt(lo, hi):
        return _dot_nt(wnt_ref[lo:hi, :], hb)

    cw = PROJ_CHUNK

    def conv_chunk(lo):
        cs = slice(lo, lo + cw)
        qk = nn(lo, lo + cw)
        ext_ref[8:8 + tm, cs] = qk
        conv = cb_ref[:, cs] + qk * cw_ref[CONV_K - 1:CONV_K, cs]
        for tap in range(CONV_K - 1):
            conv = conv + ext_ref[pl.ds(8 - (CONV_K - 1) + tap, tm), cs] * cw_ref[tap:tap + 1, cs]
        ext_ref[0:8, cs] = qk[tm - 8:tm, :]
        act = _silu(conv)
        if lo < D_MLSTM:
            qm_ref[0, :, lo:lo + cw] = act.astype(BF16)
        else:
            km_ref[0, :, lo - D_MLSTM:lo - D_MLSTM + cw] = (act * (HEAD_DIM_MLSTM ** -0.5)).astype(BF16)

    def value_chunk(lo):
        vm_ref[0, lo:lo + cw, :] = nt(1536 + lo, 1536 + lo + cw).astype(BF16)

    def gate_chunk(lo):
        ga_ref[0, lo:lo + cw, :] = (jax.nn.sigmoid(nt(2048 + lo, 2048 + lo + cw))
                                   * _silu(nt(2560 + lo, 2560 + lo + cw))).astype(BF16)

    lane = lax.broadcasted_iota(jnp.int32, (tm, 128), 1)

    def key_chunk(lo):
        kb = nn(1024 + lo, 1024 + lo + cw)
        kmean_ref[0, pl.ds(j, 1), lo:lo + cw] = jnp.mean(kb, axis=0, keepdims=True)
        kb2 = (kb * LOG2E).astype(BF16)
        for pr in range(lo // 128, (lo + cw) // 128):
            for hh in range(2):
                hd = 2 * pr + hh
                is_key = (lane < HEAD_DIM_MOBA) if hh == 0 else (lane >= HEAD_DIM_MOBA)
                grp = slice(128 * hd, 128 * (hd + 1))
                kaug_ref[0, :, grp] = jnp.where(is_key, kb2[:, 128 * pr - lo:128 * (pr + 1) - lo],
                                                kx_ref[0, :, grp])

    def qt_chunk(lo):
        qbt_ref[0, lo:lo + cw, :] = (nt(lo, lo + cw) * (HEAD_DIM_MOBA ** -0.5)).astype(BF16)

    def vt_chunk(lo):
        vbt_ref[0, 0, lo:lo + cw, :] = nt(512 + lo, 512 + lo + cw).astype(BF16)

    def gt_chunk(lo):
        gbt_ref[0, lo:lo + cw, :] = _silu(nt(1024 + lo, 1024 + lo + cw)).astype(BF16)

    heavy = ([(conv_chunk, lo) for lo in range(0, 2 * D_MLSTM, cw)]
             + [(gate_chunk, lo) for lo in range(0, D_MLSTM, cw)]
             + [(key_chunk, lo) for lo in range(0, D_MOBA, cw)])
    light = ([(qt_chunk, lo) for lo in range(0, D_MOBA, cw)]
             + [(vt_chunk, lo) for lo in range(0, D_MOBA, cw)]
             + [(value_chunk, lo) for lo in range(0, D_MLSTM, cw)]
             + [(gt_chunk, lo) for lo in range(0, D_MOBA, cw)])
    for idx in range(max(len(heavy), len(light))):
        for group in (heavy, light):
            if idx < len(group):
                fn, lo = group[idx]
                fn(lo)


def _proj(x, mod, g_norm, w_in, conv_w, conv_b, bif):
    b, s, d = x.shape
    tm = ROW_TILE
    nb = s // MOBA_BLOCK
    assert tm == MOBA_BLOCK
    kx = _moba_key_extras(nb)
    kw = N_HEADS_MOBA * 128
    row = lambda bi, j: (bi, j, 0)
    col = lambda bi, j: (bi, 0, j)
    full2 = lambda bi, j: (0, 0)
    out_shapes = (
        jax.ShapeDtypeStruct((b, s, D_MLSTM), BF16),
        jax.ShapeDtypeStruct((b, s, D_MLSTM), BF16),
        jax.ShapeDtypeStruct((b, D_MLSTM, s), BF16),
        jax.ShapeDtypeStruct((b, D_MLSTM, s), BF16),
        jax.ShapeDtypeStruct((b, 24, s), F32),
        jax.ShapeDtypeStruct((b, D_MOBA, s), BF16),
        jax.ShapeDtypeStruct((b, s, kw), BF16),
        jax.ShapeDtypeStruct((b, nb, D_MOBA), F32),
        jax.ShapeDtypeStruct((b, nb, D_MOBA, MOBA_BLOCK), BF16),
        jax.ShapeDtypeStruct((b, D_MOBA, s), BF16),
    )
    out_specs = (
        pl.BlockSpec((1, tm, D_MLSTM), row),
        pl.BlockSpec((1, tm, D_MLSTM), row),
        pl.BlockSpec((1, D_MLSTM, tm), col),
        pl.BlockSpec((1, D_MLSTM, tm), col),
        pl.BlockSpec((1, 24, tm), col),
        pl.BlockSpec((1, D_MOBA, tm), col),
        pl.BlockSpec((1, tm, kw), row),
        pl.BlockSpec((1, nb, D_MOBA), lambda bi, j: (bi, 0, 0)),
        pl.BlockSpec((1, 1, D_MOBA, tm), lambda bi, j: (bi, j, 0, 0)),
        pl.BlockSpec((1, D_MOBA, tm), col),
    )
    in_specs = [
        pl.BlockSpec((1, tm, d), row),
        pl.BlockSpec((1, 1, d), lambda bi, j: (bi, 0, 0)),
        pl.BlockSpec((1, 1, d), lambda bi, j: (bi, 0, 1)),
        pl.BlockSpec((1, d), full2),
        pl.BlockSpec(w_in.shape, full2, pipeline_mode=pl.Buffered(1)),
        pl.BlockSpec(conv_w.shape, full2),
        pl.BlockSpec((1, 2 * D_MLSTM), full2),
        pl.BlockSpec((1, 128), full2),
        pl.BlockSpec((1, MOBA_BLOCK, kw), lambda bi, j: (j, 0, 0)),
    ]
    return pl.pallas_call(
        _proj_kernel,
        grid=(b, s // tm),
        in_specs=in_specs,
        out_specs=out_specs,
        out_shape=out_shapes,
        scratch_shapes=[pltpu.VMEM((tm + 8, 2 * D_MLSTM), F32),
                        pltpu.VMEM((d, 512 * len(_NN_GROUPS)), BF16),
                        pltpu.VMEM((512 * len(_NT_GROUPS), d), BF16),
                        pltpu.VMEM((d, 128), BF16)],
        compiler_params=pltpu.CompilerParams(dimension_semantics=("arbitrary", "arbitrary"),
                                             vmem_limit_bytes=VMEM_LIMIT),
        name="proj",
    )(x, mod, mod, g_norm.reshape(1, d), w_in, conv_w, conv_b.reshape(1, -1), bif, kx)


def _col_bcast(row, n):
    return jnp.transpose(jnp.broadcast_to(row, (n, n)))


def _mlstm_kernel(q_ref, k_ref, vt_ref, gat_ref, gates_ref, gh_ref, o_ref, st_ref, m_ref):
    ci = pl.program_id(1)
    L = MLSTM_CHUNK
    dh = HEAD_DIM_MLSTM
    assert L == dh == 128

    @pl.when(ci == 0)
    def _():
        st_ref[...] = jnp.zeros(st_ref.shape, F32)
        m_ref[...] = jnp.zeros(m_ref.shape, F32)

    s_idx = lax.broadcasted_iota(jnp.int32, (L, L), 0)
    t_idx = lax.broadcasted_iota(jnp.int32, (L, L), 1)
    causal = s_idx <= t_idx
    ones_tail = jnp.ones((16, L), BF16)

    heads = range(N_HEADS_MLSTM)
    sls = [slice(hd * dh, (hd + 1) * dh) for hd in heads]

    m_prev = m_ref[:, 0:1]
    rowv = []
    for c in range(MLSTM_STEP_CHUNKS):
        ts = slice(c * L, (c + 1) * L)
        cum_f = gates_ref[0, 0:8, ts]
        g = gates_ref[0, 8:16, ts]
        g_cmax = gates_ref[0, 16:24, ts]
        f_tot = cum_f[:, L - 1:L]
        g_max = g_cmax[:, L - 1:L]
        a_max = f_tot + g_max
        a_row = -jnp.maximum(m_prev, g_cmax)
        m_new = jnp.maximum(f_tot + m_prev, a_max)
        rowv.append(dict(
            ts=ts, g=g, a_row=a_row,
            w=jnp.exp(g - g_max),
            inter=jnp.exp(a_row + m_prev),
            floor=jnp.exp(a_row - cum_f),
            s_old=jnp.exp(f_tot + m_prev - m_new), s_new=jnp.exp(a_max - m_new)))
        m_prev = m_new
    m_ref[...] = jnp.broadcast_to(m_prev, m_ref.shape)

    s_kq = {}
    for c, rv in enumerate(rowv):
        for hd in heads:
            q = q_ref[0, rv["ts"], sls[hd]]
            k = k_ref[0, rv["ts"], sls[hd]]
            g_col = _col_bcast(rv["g"][hd:hd + 1, :], L)
            decay = jnp.exp(jnp.where(causal, g_col + rv["a_row"][hd:hd + 1, :], -jnp.inf))
            s_kq[c, hd] = (_dot_nt(k, q) * decay).astype(BF16)

    st = [st_ref[hd] for hd in heads]
    for c, rv in enumerate(rowv):
        ts = rv["ts"]
        vts = [jnp.concatenate([vt_ref[0, sls[hd], ts], ones_tail], axis=0) for hd in heads]
        carried = [_dot_nt(st[hd].astype(BF16), q_ref[0, ts, sls[hd]]) for hd in heads]
        intra = [jnp.dot(vts[hd], s_kq[c, hd], preferred_element_type=F32) for hd in heads]
        for hd in heads:
            mixed = intra[hd] + rv["inter"][hd:hd + 1, :] * carried[hd]
            den = mixed[dh:dh + 1, :]
            hh = mixed[0:dh] / jnp.maximum(jnp.abs(den), rv["floor"][hd:hd + 1, :])
            hh = hh * lax.rsqrt(jnp.mean(hh * hh, axis=0, keepdims=True) + EPS)
            o_ref[0, sls[hd], ts] = (hh * gh_ref[sls[hd], :] * gat_ref[0, sls[hd], ts].astype(F32)).astype(BF16)
        for hd in heads:
            vw = (vts[hd].astype(F32) * rv["w"][hd:hd + 1, :]).astype(BF16)
            st_loc = jnp.dot(vw, k_ref[0, ts, sls[hd]], preferred_element_type=F32)
            st[hd] = rv["s_old"][hd:hd + 1, :] * st[hd] + rv["s_new"][hd:hd + 1, :] * st_loc

    for hd in heads:
        st_ref[hd] = st[hd]


def _mlstm(qm, km, vmt, gat, gates, g_head):
    b, s, dm = qm.shape
    L = MLSTM_CHUNK * MLSTM_STEP_CHUNKS
    row = lambda bi, ci: (bi, ci, 0)
    col = lambda bi, ci: (bi, 0, ci)
    g_bc = jnp.broadcast_to(g_head.astype(F32).reshape(dm, 1), (dm, 128))
    return pl.pallas_call(
        _mlstm_kernel,
        grid=(b, s // L),
        in_specs=[pl.BlockSpec((1, L, dm), row), pl.BlockSpec((1, L, dm), row),
                  pl.BlockSpec((1, dm, L), col), pl.BlockSpec((1, dm, L), col),
                  pl.BlockSpec((1, 24, L), col),
                  pl.BlockSpec((dm, 128), lambda bi, ci: (0, 0))],
        out_specs=pl.BlockSpec((1, dm, L), col),
        out_shape=jax.ShapeDtypeStruct((b, dm, s), BF16),
        scratch_shapes=[pltpu.VMEM((N_HEADS_MLSTM, HEAD_DIM_MLSTM + 16, HEAD_DIM_MLSTM), F32),
                        pltpu.VMEM((8, 128), F32)],
        compiler_params=pltpu.CompilerParams(dimension_semantics=("arbitrary", "arbitrary"),
                                             vmem_limit_bytes=VMEM_LIMIT),
        name="mlstm",
    )(qm, km, vmt, gat, gates, g_bc)


def _moba_steps(nb):
    jt, it = [], []
    for j in range(nb):
        for i in [j] + list(range(j)):
            jt.append(j)
            it.append(i)
    return jt, it


def _moba_key_extras(nb):
    blk = MOBA_BLOCK
    lane = np.arange(N_HEADS_MOBA * 128)
    head = lane // 128
    w = (lane % 128 - np.where(head % 2 == 0, HEAD_DIM_MOBA, 0))[None, None, :]
    slope = (2.0 ** -(head + 1.0))[None, None, :]
    i = np.arange(nb)[:, None, None]
    c = np.arange(blk)[None, :, None]
    tab = np.where(w == i, 1.0, 0.0)
    rest = LOG2E * slope * (i * blk + c)
    for term in range(N_POS_TERMS):
        piece = rest.astype(np.float32).astype(BF16).astype(np.float64)
        tab = tab + np.where(w == nb + term, piece, 0.0)
        rest = rest - piece
    return jnp.asarray(tab.astype(np.float32).astype(BF16))


def _moba_kernel(jt_ref, it_ref, qt_ref, k_ref, vt_ref, kmean_ref, gt_ref, x_ref, gate_ref, oat_ref, w_ref,
                 gf_ref, o_ref,
                 qaug_ref, m_ref, acc_ref, cmax_ref, wb_ref, obt_ref, *s_refs):
    @pl.when(jnp.logical_and(pl.program_id(0) == 0, pl.program_id(1) == 0))
    def _():
        for r in range(0, w_ref.shape[0], 256):
            wb_ref[r:r + 256, :] = w_ref[r:r + 256, :].astype(BF16)

    t = pl.program_id(1)
    j = jt_ref[t]
    i = it_ref[t]
    blk = MOBA_BLOCK
    dh = HEAD_DIM_MOBA
    nb = kmean_ref.shape[1]
    nbat = qt_ref.shape[0]
    first = i == j
    last = jnp.logical_or(i == j - 1, j == 0)

    def key_tile(bb, h):
        return k_ref[bb, :, 128 * h:128 * (h + 1)]

    def rows(h):
        return slice(dh * h, dh * (h + 1))

    ones_tail = jnp.ones((16, blk), BF16)

    def values_t(bb, h):
        return jnp.concatenate([vt_ref[bb, 0, rows(h), :], ones_tail], axis=0)

    @pl.when(first)
    def _():
        blk_i = lax.broadcasted_iota(jnp.int32, (nb, blk), 0)
        lane_m = lax.broadcasted_iota(jnp.int32, (nb, 2 * dh), 1)
        key_c = lax.broadcasted_iota(jnp.int32, (blk, blk), 0)
        qry_r = lax.broadcasted_iota(jnp.int32, (blk, blk), 1)
        causal = key_c <= qry_r
        ones_rows = jnp.where(blk_i < N_POS_TERMS, 1.0, 0.0)
        gates = {}
        for bb in range(nbat):
            for h in range(N_HEADS_MOBA):
                pr, hh = divmod(h, 2)
                qt_pair = qt_ref[bb, 128 * pr:128 * (pr + 1), :]
                in_head = (lane_m < dh) if hh == 0 else (lane_m >= dh)
                km = jnp.where(in_head, kmean_ref[bb, :, 128 * pr:128 * (pr + 1)], 0.0).astype(BF16)
                gates[bb, h] = jnp.dot(km, qt_pair, preferred_element_type=F32)
        for bb in range(nbat):
            for h in range(N_HEADS_MOBA):
                pr, hh = divmod(h, 2)
                gate = gates[bb, h]
                rank = jnp.zeros((nb, blk), jnp.int32)
                for i2 in range(nb):
                    g2 = gate[i2:i2 + 1, :]
                    beats = (g2 > gate) | ((g2 == gate) & (i2 < blk_i))
                    rank = rank + jnp.where(beats & (i2 < j), 1, 0)
                keep = ((blk_i < j) & (rank < MOBA_TOPK)) | (blk_i == j)
                sel_bias = jnp.where(keep, 0.0, MASK_BIAS)
                extra = jnp.concatenate([sel_bias, ones_rows, jnp.zeros((dh - 16, blk), F32)], axis=0)
                extra = extra.astype(BF16)
                qt_h = qt_ref[bb, dh * h:dh * (h + 1), :]
                qaug_ref[bb, h] = jnp.concatenate([qt_h, extra] if hh == 0 else [extra, qt_h], axis=0)

        for bb in range(nbat):
            for h in range(N_HEADS_MOBA):
                s = jnp.dot(key_tile(bb, h), qaug_ref[bb, h], preferred_element_type=F32)
                s = jnp.where(causal, s, MASK_BIAS)
                s_refs[bb][h] = s
                m_ref[bb, h:h + 1, :] = jnp.max(s, axis=0, keepdims=True)

        for bb in range(nbat):
            for h in range(N_HEADS_MOBA):
                p0 = jnp.exp2(s_refs[bb][h] - m_ref[bb, h:h + 1, :])
                acc_ref[bb, h] = jnp.dot(values_t(bb, h), p0.astype(BF16), preferred_element_type=F32)

    @pl.when(jnp.logical_not(first))
    def _():
        for bb in range(nbat):
            for h in range(N_HEADS_MOBA):
                s = jnp.dot(key_tile(bb, h), qaug_ref[bb, h], preferred_element_type=F32)
                s_refs[bb][h] = s
                cmax_ref[bb, h:h + 1, :] = jnp.max(s, axis=0, keepdims=True)
        for bb in range(nbat):
            for h in range(N_HEADS_MOBA):
                m_run = m_ref[bb, h:h + 1, :]
                m_new = jnp.maximum(m_run, cmax_ref[bb, h:h + 1, :])
                alpha = jnp.exp2(m_run - m_new)
                pr_ = jnp.exp2(s_refs[bb][h] - m_new)
                m_ref[bb, h:h + 1, :] = m_new
                acc_ref[bb, h] = alpha * acc_ref[bb, h] + jnp.dot(values_t(bb, h), pr_.astype(BF16),
                                                                  preferred_element_type=F32)

    @pl.when(last)
    def _():
        for bb in range(nbat):
            for h in range(N_HEADS_MOBA):
                acc = acc_ref[bb, h]
                out = acc[0:dh] / acc[dh:dh + 1] * gt_ref[bb, rows(h), :].astype(F32)
                obt_ref[bb, rows(h), :] = out.astype(BF16)
        for bb in range(nbat):
            y = (_dot_tn(oat_ref[bb], wb_ref[0:D_MLSTM, :])
                 + _dot_tn(obt_ref[bb], wb_ref[D_MLSTM:D_MLSTM + D_MOBA, :]))
            r = x_ref[bb] + gate_ref[bb] * y
            o_ref[bb] = r * lax.rsqrt(jnp.mean(r * r, axis=-1, keepdims=True) + EPS) * gf_ref[...]


def _moba_out(qbt, kaug, vbt, kmean, gbt, x, mod, out_at, w_out, g_final):
    b, dmb, s = qbt.shape
    d = x.shape[2]
    blk = MOBA_BLOCK
    nb = s // blk
    kw = kaug.shape[2]
    nbat = MOBA_BATCH
    assert nb == 8, "selection extras assume 8 key blocks"
    jt, it = _moba_steps(nb)
    qmap = lambda bi, t, jt_r, it_r: (bi, 0, jt_r[t])
    rmap = lambda bi, t, jt_r, it_r: (bi, jt_r[t], 0)
    full2 = lambda bi, t, jt_r, it_r: (0, 0)
    grid_spec = pltpu.PrefetchScalarGridSpec(
        num_scalar_prefetch=2,
        grid=(b // nbat, len(jt)),
        in_specs=[pl.BlockSpec((nbat, dmb, blk), qmap),
                  pl.BlockSpec((nbat, blk, kw), lambda bi, t, jt_r, it_r: (bi, it_r[t], 0)),
                  pl.BlockSpec((nbat, 1, dmb, blk), lambda bi, t, jt_r, it_r: (bi, it_r[t], 0, 0)),
                  pl.BlockSpec((nbat, nb, dmb), lambda bi, t, jt_r, it_r: (bi, 0, 0)),
                  pl.BlockSpec((nbat, dmb, blk), qmap),
                  pl.BlockSpec((nbat, blk, d), rmap),
                  pl.BlockSpec((nbat, 1, d), lambda bi, t, jt_r, it_r: (bi, 0, 2)),
                  pl.BlockSpec((nbat, D_MLSTM, blk), qmap),
                  pl.BlockSpec(w_out.shape, full2, pipeline_mode=pl.Buffered(1)),
                  pl.BlockSpec((1, d), full2)],
        out_specs=pl.BlockSpec((nbat, blk, d), rmap),
        scratch_shapes=[pltpu.VMEM((nbat, N_HEADS_MOBA, 2 * HEAD_DIM_MOBA, blk), BF16),
                        pltpu.VMEM((nbat, N_HEADS_MOBA, blk), F32),
                        pltpu.VMEM((nbat, N_HEADS_MOBA, HEAD_DIM_MOBA + 16, blk), F32),
                        pltpu.VMEM((nbat, N_HEADS_MOBA, blk), F32),
                        pltpu.VMEM(w_out.shape, BF16),
                        pltpu.VMEM((nbat, dmb, blk), BF16)]
        + [pltpu.VMEM((N_HEADS_MOBA, blk, blk), F32) for _ in range(nbat)],
    )
    return pl.pallas_call(
        _moba_kernel,
        grid_spec=grid_spec,
        out_shape=jax.ShapeDtypeStruct((b, s, d), x.dtype),
        compiler_params=pltpu.CompilerParams(dimension_semantics=("arbitrary", "arbitrary"),
                                             vmem_limit_bytes=VMEM_LIMIT),
        name="moba_out",
    )(jnp.asarray(jt, jnp.int32), jnp.asarray(it, jnp.int32), qbt, kaug, vbt, kmean, gbt,
      x, mod, out_at, w_out, g_final.reshape(1, d))


def _layer(x, c, w_ada, b_ada, g_norm, w_in, conv_w, conv_b, b_igate, b_fgate, g_mlstm_head, w_out,
           g_final):
    b = x.shape[0]
    mod = _ada(c, w_ada, b_ada).reshape(b, 1, 3 * D_MODEL)

    bif = jnp.concatenate([b_igate.astype(F32), jnp.zeros((4,), F32), b_fgate.astype(F32),
                           jnp.zeros((116,), F32)]).reshape(1, 128)

    qm, km, vmt, gat, gates, qbt, kaug, kmean, vbt, gbt = _proj(
        x, mod, g_norm, jnp.transpose(w_in), conv_w, conv_b, bif)
    out_at = _mlstm(qm, km, vmt, gat, gates, g_mlstm_head)
    return _moba_out(qbt, kaug, vbt, kmean, gbt, x, mod, out_at, w_out, g_final)


def kernel(x, c, w_ada, b_ada, g_norm, w_in, conv_w, conv_b, b_igate, b_fgate, g_mlstm_head, w_out,
           g_final):
    assert w_ada.shape[0] == 1, "single-layer trunk"
    return _layer(x, c, w_ada[0], b_ada[0], g_norm[0], w_in[0], conv_w[0], conv_b[0], b_igate[0],
                  b_fgate[0], g_mlstm_head[0], w_out[0], g_final)
```

```python
import jax
import jax.numpy as jnp
import numpy as np
from jax import lax
from jax.experimental import pallas as pl
from jax.experimental.pallas import tpu as pltpu

F32 = jnp.float32
BF16 = jnp.bfloat16

D_MODEL = 1024
D_MLSTM = 512
N_HEADS_MLSTM = 4
HEAD_DIM_MLSTM = 128
D_MOBA = 512
N_HEADS_MOBA = 8
HEAD_DIM_MOBA = 64
CONV_K = 4
MOBA_BLOCK = 256
MOBA_TOPK = 3
EPS = 1e-6

ROW_TILE = 256
PROJ_CHUNK = 256
MLSTM_CHUNK = 128
MLSTM_STEP_CHUNKS = 8
MOBA_BATCH = 4
MASK_BIAS = -1e30
LOG2E = 1.4426950408889634
N_POS_TERMS = 3
VMEM_LIMIT = 56 * 1024 * 1024

_OFF = {}
_o = 0
for _name, _size in (("qm", 512), ("km", 512), ("vm", 512), ("om", 512), ("im", 4), ("fm", 4),
                     ("zm", 512), ("qb", 512), ("kb", 512), ("vb", 512), ("zb", 512)):
    _OFF[_name] = (_o, _o + _size)
    _o += _size
_NN_GROUPS = ("qm", "km", "kb")
_NT_GROUPS = ("qb", "vb", "zb", "vm", "om", "zm")


def _silu(v):
    return v * jax.nn.sigmoid(v)


def _dot_nt(a, b):
    return lax.dot_general(a, b, (((1,), (1,)), ((), ())), preferred_element_type=F32)


def _dot_tn(a, b):
    return lax.dot_general(a, b, (((0,), (0,)), ((), ())), preferred_element_type=F32)


def _ada_kernel(c_ref, w_ref, b_ref, o_ref):
    a = _silu(c_ref[...])
    o_ref[...] = jnp.dot(a, w_ref[...], preferred_element_type=F32,
                         precision=lax.Precision.HIGHEST) + b_ref[...]


def _ada(c, w_ada, b_ada):
    b, d = c.shape
    n = w_ada.shape[1]
    tn = 1024
    return pl.pallas_call(
        _ada_kernel,
        grid=(n // tn,),
        in_specs=[pl.BlockSpec((b, d), lambda i: (0, 0)),
                  pl.BlockSpec((d, tn), lambda i: (0, i)),
                  pl.BlockSpec((1, tn), lambda i: (0, i))],
        out_specs=pl.BlockSpec((b, tn), lambda i: (0, i)),
        out_shape=jax.ShapeDtypeStruct((b, n), F32),
        compiler_params=pltpu.CompilerParams(dimension_semantics=("arbitrary",),
                                             vmem_limit_bytes=VMEM_LIMIT),
        name="ada",
    )(c, w_ada, b_ada.reshape(1, n))


def _chunk_scan(v, combine, fill):
    pos = lax.broadcasted_iota(jnp.int32, v.shape, 1) & (MLSTM_CHUNK - 1)
    d = 1
    while d < MLSTM_CHUNK:
        shifted = pltpu.roll(v, d, axis=1)
        v = combine(v, jnp.where(pos >= d, shifted, fill))
        d *= 2
    return v


def _proj_kernel(x_ref, shift_ref, scale_ref, gn_ref, w_ref, cw_ref, cb_ref, bif_ref, kx_ref,
                 qm_ref, km_ref, vm_ref, ga_ref, gates_ref, qbt_ref, kaug_ref, kmean_ref, vbt_ref, gbt_ref,
                 ext_ref, wnn_ref, wnt_ref, wif_ref):
    j = pl.program_id(1)
    tm = x_ref.shape[1]

    @pl.when(jnp.logical_and(pl.program_id(0) == 0, j == 0))
    def _():
        for g, name in enumerate(_NN_GROUPS):
            lo = _OFF[name][0]
            for c in range(0, 512, 256):
                wnn_ref[:, 512 * g + c:512 * g + c + 256] = jnp.transpose(
                    w_ref[lo + c:lo + c + 256, :]).astype(BF16)
        for g, name in enumerate(_NT_GROUPS):
            lo = _OFF[name][0]
            for c in range(0, 512, 256):
                wnt_ref[512 * g + c:512 * g + c + 256, :] = w_ref[lo + c:lo + c + 256, :].astype(BF16)
        blk = jnp.transpose(w_ref[_OFF["im"][0]:_OFF["im"][0] + 128, :])
        lane_w = lax.broadcasted_iota(jnp.int32, blk.shape, 1)
        f_cols = pltpu.roll(blk, 4, axis=1)
        wif_ref[...] = jnp.where(lane_w < 4, blk,
                                 jnp.where((lane_w >= 8) & (lane_w < 12), f_cols, 0.0)).astype(BF16)

    @pl.when(j == 0)
    def _():
        ext_ref[0:8, :] = jnp.zeros((8, 2 * D_MLSTM), F32)

    x = x_ref[0]
    ms = jnp.mean(x * x, axis=-1, keepdims=True)
    h = x * lax.rsqrt(ms + EPS) * gn_ref[...] * (1.0 + scale_ref[0]) + shift_ref[0]
    hb = h.astype(BF16)

    ifc = jnp.dot(hb, wif_ref[...], preferred_element_type=F32) + bif_ref[...]
    ift = jnp.transpose(ifc)[0:16, :]
    log_i = ift[0:8]
    f_pre = ift[8:16]
    log_f = jnp.minimum(f_pre, 0.0) - jnp.log1p(jnp.exp(-jnp.abs(f_pre)))
    cum_f = _chunk_scan(log_f, jnp.add, 0.0)
    g = log_i - cum_f
    gates_ref[0, 0:8, :] = cum_f
    gates_ref[0, 8:16, :] = g
    gates_ref[0, 16:24, :] = _chunk_scan(g, jnp.maximum, -jnp.inf)

    def nn(lo, hi):
        return jnp.dot(hb, wnn_ref[:, lo:hi], preferred_element_type=F32)

    def nt(lo, hi):
        return _dot_nt(wnt_ref[lo:hi, :], hb)

    cw = PROJ_CHUNK

    def conv_chunk(lo):
        cs = slice(lo, lo + cw)
        qk = nn(lo, lo + cw)
        ext_ref[8:8 + tm, cs] = qk
        conv = cb_ref[:, cs] + qk * cw_ref[CONV_K - 1:CONV_K, cs]
        for tap in range(CONV_K - 1):
            conv = conv + ext_ref[pl.ds(8 - (CONV_K - 1) + tap, tm), cs] * cw_ref[tap:tap + 1, cs]
        ext_ref[0:8, cs] = qk[tm - 8:tm, :]
        act = _silu(conv)
        if lo < D_MLSTM:
            qm_ref[0, :, lo:lo + cw] = act.astype(BF16)
        else:
            km_ref[0, :, lo - D_MLSTM:lo - D_MLSTM + cw] = (act * (HEAD_DIM_MLSTM ** -0.5)).astype(BF16)

    def value_chunk(lo):
        vm_ref[0, lo:lo + cw, :] = nt(1536 + lo, 1536 + lo + cw).astype(BF16)

    def gate_chunk(lo):
        ga_ref[0, lo:lo + cw, :] = (jax.nn.sigmoid(nt(2048 + lo, 2048 + lo + cw))
                                   * _silu(nt(2560 + lo, 2560 + lo + cw))).astype(BF16)

    lane = lax.broadcasted_iota(jnp.int32, (tm, 128), 1)

    def key_chunk(lo):
        kb = nn(1024 + lo, 1024 + lo + cw)
        kmean_ref[0, pl.ds(j, 1), lo:lo + cw] = jnp.mean(kb, axis=0, keepdims=True)
        kb2 = (kb * LOG2E).astype(BF16)
        for pr in range(lo // 128, (lo + cw) // 128):
            for hh in range(2):
                hd = 2 * pr + hh
                is_key = (lane < HEAD_DIM_MOBA) if hh == 0 else (lane >= HEAD_DIM_MOBA)
                grp = slice(128 * hd, 128 * (hd + 1))
                kaug_ref[0, :, grp] = jnp.where(is_key, kb2[:, 128 * pr - lo:128 * (pr + 1) - lo],
                                                kx_ref[0, :, grp])

    def qt_chunk(lo):
        qbt_ref[0, lo:lo + cw, :] = (nt(lo, lo + cw) * (HEAD_DIM_MOBA ** -0.5)).astype(BF16)

    def vt_chunk(lo):
        vbt_ref[0, 0, lo:lo + cw, :] = nt(512 + lo, 512 + lo + cw).astype(BF16)

    def gt_chunk(lo):
        gbt_ref[0, lo:lo + cw, :] = _silu(nt(1024 + lo, 1024 + lo + cw)).astype(BF16)

    heavy = ([(conv_chunk, lo) for lo in range(0, 2 * D_MLSTM, cw)]
             + [(gate_chunk, lo) for lo in range(0, D_MLSTM, cw)]
             + [(key_chunk, lo) for lo in range(0, D_MOBA, cw)])
    light = ([(qt_chunk, lo) for lo in range(0, D_MOBA, cw)]
             + [(vt_chunk, lo) for lo in range(0, D_MOBA, cw)]
             + [(value_chunk, lo) for lo in range(0, D_MLSTM, cw)]
             + [(gt_chunk, lo) for lo in range(0, D_MOBA, cw)])
    for idx in range(max(len(heavy), len(light))):
        for group in (heavy, light):
            if idx < len(group):
                fn, lo = group[idx]
                fn(lo)


def _proj(x, mod, g_norm, w_in, conv_w, conv_b, bif):
    b, s, d = x.shape
    tm = ROW_TILE
    nb = s // MOBA_BLOCK
    assert tm == MOBA_BLOCK
    kx = _moba_key_extras(nb)
    kw = N_HEADS_MOBA * 128
    row = lambda bi, j: (bi, j, 0)
    col = lambda bi, j: (bi, 0, j)
    full2 = lambda bi, j: (0, 0)
    out_shapes = (
        jax.ShapeDtypeStruct((b, s, D_MLSTM), BF16),
        jax.ShapeDtypeStruct((b, s, D_MLSTM), BF16),
        jax.ShapeDtypeStruct((b, D_MLSTM, s), BF16),
        jax.ShapeDtypeStruct((b, D_MLSTM, s), BF16),
        jax.ShapeDtypeStruct((b, 24, s), F32),
        jax.ShapeDtypeStruct((b, D_MOBA, s), BF16),
        jax.ShapeDtypeStruct((b, s, kw), BF16),
        jax.ShapeDtypeStruct((b, nb, D_MOBA), F32),
        jax.ShapeDtypeStruct((b, nb, D_MOBA, MOBA_BLOCK), BF16),
        jax.ShapeDtypeStruct((b, D_MOBA, s), BF16),
    )
    out_specs = (
        pl.BlockSpec((1, tm, D_MLSTM), row),
        pl.BlockSpec((1, tm, D_MLSTM), row),
        pl.BlockSpec((1, D_MLSTM, tm), col),
        pl.BlockSpec((1, D_MLSTM, tm), col),
        pl.BlockSpec((1, 24, tm), col),
        pl.BlockSpec((1, D_MOBA, tm), col),
        pl.BlockSpec((1, tm, kw), row),
        pl.BlockSpec((1, nb, D_MOBA), lambda bi, j: (bi, 0, 0)),
        pl.BlockSpec((1, 1, D_MOBA, tm), lambda bi, j: (bi, j, 0, 0)),
        pl.BlockSpec((1, D_MOBA, tm), col),
    )
    in_specs = [
        pl.BlockSpec((1, tm, d), row),
        pl.BlockSpec((1, 1, d), lambda bi, j: (bi, 0, 0)),
        pl.BlockSpec((1, 1, d), lambda bi, j: (bi, 0, 1)),
        pl.BlockSpec((1, d), full2),
        pl.BlockSpec(w_in.shape, full2, pipeline_mode=pl.Buffered(1)),
        pl.BlockSpec(conv_w.shape, full2),
        pl.BlockSpec((1, 2 * D_MLSTM), full2),
        pl.BlockSpec((1, 128), full2),
        pl.BlockSpec((1, MOBA_BLOCK, kw), lambda bi, j: (j, 0, 0)),
    ]
    return pl.pallas_call(
        _proj_kernel,
        grid=(b, s // tm),
        in_specs=in_specs,
        out_specs=out_specs,
        out_shape=out_shapes,
        scratch_shapes=[pltpu.VMEM((tm + 8, 2 * D_MLSTM), F32),
                        pltpu.VMEM((d, 512 * len(_NN_GROUPS)), BF16),
                        pltpu.VMEM((512 * len(_NT_GROUPS), d), BF16),
                        pltpu.VMEM((d, 128), BF16)],
        compiler_params=pltpu.CompilerParams(dimension_semantics=("arbitrary", "arbitrary"),
                                             vmem_limit_bytes=VMEM_LIMIT),
        name="proj",
    )(x, mod, mod, g_norm.reshape(1, d), w_in, conv_w, conv_b.reshape(1, -1), bif, kx)


def _col_bcast(row, n):
    return jnp.transpose(jnp.broadcast_to(row, (n, n)))


def _mlstm_kernel(q_ref, k_ref, vt_ref, gat_ref, gates_ref, gh_ref, o_ref, st_ref, m_ref):
    ci = pl.program_id(1)
    L = MLSTM_CHUNK
    dh = HEAD_DIM_MLSTM
    assert L == dh == 128

    @pl.when(ci == 0)
    def _():
        st_ref[...] = jnp.zeros(st_ref.shape, F32)
        m_ref[...] = jnp.zeros(m_ref.shape, F32)

    s_idx = lax.broadcasted_iota(jnp.int32, (L, L), 0)
    t_idx = lax.broadcasted_iota(jnp.int32, (L, L), 1)
    causal = s_idx <= t_idx
    ones_tail = jnp.ones((16, L), BF16)

    heads = range(N_HEADS_MLSTM)
    sls = [slice(hd * dh, (hd + 1) * dh) for hd in heads]

    m_prev = m_ref[:, 0:1]
    rowv = []
    for c in range(MLSTM_STEP_CHUNKS):
        ts = slice(c * L, (c + 1) * L)
        cum_f = gates_ref[0, 0:8, ts]
        g = gates_ref[0, 8:16, ts]
        g_cmax = gates_ref[0, 16:24, ts]
        f_tot = cum_f[:, L - 1:L]
        g_max = g_cmax[:, L - 1:L]
        a_max = f_tot + g_max
        a_row = -jnp.maximum(m_prev, g_cmax)
        m_new = jnp.maximum(f_tot + m_prev, a_max)
        rowv.append(dict(
            ts=ts, g=g, a_row=a_row,
            w=jnp.exp(g - g_max),
            inter=jnp.exp(a_row + m_prev),
            floor=jnp.exp(a_row - cum_f),
            s_old=jnp.exp(f_tot + m_prev - m_new), s_new=jnp.exp(a_max - m_new)))
        m_prev = m_new
    m_ref[...] = jnp.broadcast_to(m_prev, m_ref.shape)

    s_kq = {}
    for c, rv in enumerate(rowv):
        for hd in heads:
            q = q_ref[0, rv["ts"], sls[hd]]
            k = k_ref[0, rv["ts"], sls[hd]]
            g_col = _col_bcast(rv["g"][hd:hd + 1, :], L)
            decay = jnp.exp(jnp.where(causal, g_col + rv["a_row"][hd:hd + 1, :], -jnp.inf))
            s_kq[c, hd] = (_dot_nt(k, q) * decay).astype(BF16)

    st = [st_ref[hd] for hd in heads]
    for c, rv in enumerate(rowv):
        ts = rv["ts"]
        vts = [jnp.concatenate([vt_ref[0, sls[hd], ts], ones_tail], axis=0) for hd in heads]
        carried = [_dot_nt(st[hd].astype(BF16), q_ref[0, ts, sls[hd]]) for hd in heads]
        intra = [jnp.dot(vts[hd], s_kq[c, hd], preferred_element_type=F32) for hd in heads]
        for hd in heads:
            mixed = intra[hd] + rv["inter"][hd:hd + 1, :] * carried[hd]
            den = mixed[dh:dh + 1, :]
            hh = mixed[0:dh] / jnp.maximum(jnp.abs(den), rv["floor"][hd:hd + 1, :])
            hh = hh * lax.rsqrt(jnp.mean(hh * hh, axis=0, keepdims=True) + EPS)
            o_ref[0, sls[hd], ts] = (hh * gh_ref[sls[hd], :] * gat_ref[0, sls[hd], ts].astype(F32)).astype(BF16)
        for hd in heads:
            vw = (vts[hd].astype(F32) * rv["w"][hd:hd + 1, :]).astype(BF16)
            st_loc = jnp.dot(vw, k_ref[0, ts, sls[hd]], preferred_element_type=F32)
            st[hd] = rv["s_old"][hd:hd + 1, :] * st[hd] + rv["s_new"][hd:hd + 1, :] * st_loc

    for hd in heads:
        st_ref[hd] = st[hd]


def _mlstm(qm, km, vmt, gat, gates, g_head):
    b, s, dm = qm.shape
    L = MLSTM_CHUNK * MLSTM_STEP_CHUNKS
    row = lambda bi, ci: (bi, ci, 0)
    col = lambda bi, ci: (bi, 0, ci)
    g_bc = jnp.broadcast_to(g_head.astype(F32).reshape(dm, 1), (dm, 128))
    return pl.pallas_call(
        _mlstm_kernel,
        grid=(b, s // L),
        in_specs=[pl.BlockSpec((1, L, dm), row), pl.BlockSpec((1, L, dm), row),
                  pl.BlockSpec((1, dm, L), col), pl.BlockSpec((1, dm, L), col),
                  pl.BlockSpec((1, 24, L), col),
                  pl.BlockSpec((dm, 128), lambda bi, ci: (0, 0))],
        out_specs=pl.BlockSpec((1, dm, L), col),
        out_shape=jax.ShapeDtypeStruct((b, dm, s), BF16),
        scratch_shapes=[pltpu.VMEM((N_HEADS_MLSTM, HEAD_DIM_MLSTM + 16, HEAD_DIM_MLSTM), F32),
                        pltpu.VMEM((8, 128), F32)],
        compiler_params=pltpu.CompilerParams(dimension_semantics=("arbitrary", "arbitrary"),
                                             vmem_limit_bytes=VMEM_LIMIT),
        name="mlstm",
    )(qm, km, vmt, gat, gates, g_bc)


def _moba_steps(nb):
    jt, it = [], []
    for j in range(nb):
        for i in [j] + list(range(j)):
            jt.append(j)
            it.append(i)
    return jt, it


def _moba_key_extras(nb):
    blk = MOBA_BLOCK
    lane = np.arange(N_HEADS_MOBA * 128)
    head = lane // 128
    w = (lane % 128 - np.where(head % 2 == 0, HEAD_DIM_MOBA, 0))[None, None, :]
    slope = (2.0 ** -(head + 1.0))[None, None, :]
    i = np.arange(nb)[:, None, None]
    c = np.arange(blk)[None, :, None]
    tab = np.where(w == i, 1.0, 0.0)
    rest = LOG2E * slope * (i * blk + c)
    for term in range(N_POS_TERMS):
        piece = rest.astype(np.float32).astype(BF16).astype(np.float64)
        tab = tab + np.where(w == nb + term, piece, 0.0)
        rest = rest - piece
    return jnp.asarray(tab.astype(np.float32).astype(BF16))


def _moba_kernel(jt_ref, it_ref, qt_ref, k_ref, vt_ref, kmean_ref, gt_ref, x_ref, gate_ref, oat_ref, w_ref,
                 gf_ref, o_ref,
                 qaug_ref, m_ref, acc_ref, cmax_ref, wb_ref, obt_ref, *s_refs):
    @pl.when(jnp.logical_and(pl.program_id(0) == 0, pl.program_id(1) == 0))
    def _():
        for r in range(0, w_ref.shape[0], 256):
            wb_ref[r:r + 256, :] = w_ref[r:r + 256, :].astype(BF16)

    t = pl.program_id(1)
    j = jt_ref[t]
    i = it_ref[t]
    blk = MOBA_BLOCK
    dh = HEAD_DIM_MOBA
    nb = kmean_ref.shape[1]
    nbat = qt_ref.shape[0]
    first = i == j
    last = jnp.logical_or(i == j - 1, j == 0)

    def key_tile(bb, h):
        return k_ref[bb, :, 128 * h:128 * (h + 1)]

    def rows(h):
        return slice(dh * h, dh * (h + 1))

    ones_tail = jnp.ones((16, blk), BF16)

    def values_t(bb, h):
        return jnp.concatenate([vt_ref[bb, 0, rows(h), :], ones_tail], axis=0)

    @pl.when(first)
    def _():
        blk_i = lax.broadcasted_iota(jnp.int32, (nb, blk), 0)
        lane_m = lax.broadcasted_iota(jnp.int32, (nb, 2 * dh), 1)
        key_c = lax.broadcasted_iota(jnp.int32, (blk, blk), 0)
        qry_r = lax.broadcasted_iota(jnp.int32, (blk, blk), 1)
        causal = key_c <= qry_r
        ones_rows = jnp.where(blk_i < N_POS_TERMS, 1.0, 0.0)
        gates = {}
        for bb in range(nbat):
            for h in range(N_HEADS_MOBA):
                pr, hh = divmod(h, 2)
                qt_pair = qt_ref[bb, 128 * pr:128 * (pr + 1), :]
                in_head = (lane_m < dh) if hh == 0 else (lane_m >= dh)
                km = jnp.where(in_head, kmean_ref[bb, :, 128 * pr:128 * (pr + 1)], 0.0).astype(BF16)
                gates[bb, h] = jnp.dot(km, qt_pair, preferred_element_type=F32)
        for bb in range(nbat):
            for h in range(N_HEADS_MOBA):
                pr, hh = divmod(h, 2)
                gate = gates[bb, h]
                rank = jnp.zeros((nb, blk), jnp.int32)
                for i2 in range(nb):
                    g2 = gate[i2:i2 + 1, :]
                    beats = (g2 > gate) | ((g2 == gate) & (i2 < blk_i))
                    rank = rank + jnp.where(beats & (i2 < j), 1, 0)
                keep = ((blk_i < j) & (rank < MOBA_TOPK)) | (blk_i == j)
                sel_bias = jnp.where(keep, 0.0, MASK_BIAS)
                extra = jnp.concatenate([sel_bias, ones_rows, jnp.zeros((dh - 16, blk), F32)], axis=0)
                extra = extra.astype(BF16)
                qt_h = qt_ref[bb, dh * h:dh * (h + 1), :]
                qaug_ref[bb, h] = jnp.concatenate([qt_h, extra] if hh == 0 else [extra, qt_h], axis=0)

        for bb in range(nbat):
            for h in range(N_HEADS_MOBA):
                s = jnp.dot(key_tile(bb, h), qaug_ref[bb, h], preferred_element_type=F32)
                s = jnp.where(causal, s, MASK_BIAS)
                s_refs[bb][h] = s
                m_ref[bb, h:h + 1, :] = jnp.max(s, axis=0, keepdims=True)

        for bb in range(nbat):
            for h in range(N_HEADS_MOBA):
                p0 = jnp.exp2(s_refs[bb][h] - m_ref[bb, h:h + 1, :])
                acc_ref[bb, h] = jnp.dot(values_t(bb, h), p0.astype(BF16), preferred_element_type=F32)

    @pl.when(jnp.logical_not(first))
    def _():
        for bb in range(nbat):
            for h in range(N_HEADS_MOBA):
                s = jnp.dot(key_tile(bb, h), qaug_ref[bb, h], preferred_element_type=F32)
                s_refs[bb][h] = s
                cmax_ref[bb, h:h + 1, :] = jnp.max(s, axis=0, keepdims=True)
        for bb in range(nbat):
            for h in range(N_HEADS_MOBA):
                m_run = m_ref[bb, h:h + 1, :]
                m_new = jnp.maximum(m_run, cmax_ref[bb, h:h + 1, :])
                alpha = jnp.exp2(m_run - m_new)
                pr_ = jnp.exp2(s_refs[bb][h] - m_new)
                m_ref[bb, h:h + 1, :] = m_new
                acc_ref[bb, h] = alpha * acc_ref[bb, h] + jnp.dot(values_t(bb, h), pr_.astype(BF16),
                                                                  preferred_element_type=F32)

    @pl.when(last)
    def _():
        for bb in range(nbat):
            for h in range(N_HEADS_MOBA):
                acc = acc_ref[bb, h]
                out = acc[0:dh] / acc[dh:dh + 1] * gt_ref[bb, rows(h), :].astype(F32)
                obt_ref[bb, rows(h), :] = out.astype(BF16)
        for bb in range(nbat):
            y = (_dot_tn(oat_ref[bb], wb_ref[0:D_MLSTM, :])
                 + _dot_tn(obt_ref[bb], wb_ref[D_MLSTM:D_MLSTM + D_MOBA, :]))
            r = x_ref[bb] + gate_ref[bb] * y
            o_ref[bb] = r * lax.rsqrt(jnp.mean(r * r, axis=-1, keepdims=True) + EPS) * gf_ref[...]


def _moba_out(qbt, kaug, vbt, kmean, gbt, x, mod, out_at, w_out, g_final):
    b, dmb, s = qbt.shape
    d = x.shape[2]
    blk = MOBA_BLOCK
    nb = s // blk
    kw = kaug.shape[2]
    nbat = MOBA_BATCH
    assert nb == 8, "selection extras assume 8 key blocks"
    jt, it = _moba_steps(nb)
    qmap = lambda bi, t, jt_r, it_r: (bi, 0, jt_r[t])
    rmap = lambda bi, t, jt_r, it_r: (bi, jt_r[t], 0)
    full2 = lambda bi, t, jt_r, it_r: (0, 0)
    grid_spec = pltpu.PrefetchScalarGridSpec(
        num_scalar_prefetch=2,
        grid=(b // nbat, len(jt)),
        in_specs=[pl.BlockSpec((nbat, dmb, blk), qmap),
                  pl.BlockSpec((nbat, blk, kw), lambda bi, t, jt_r, it_r: (bi, it_r[t], 0)),
                  pl.BlockSpec((nbat, 1, dmb, blk), lambda bi, t, jt_r, it_r: (bi, it_r[t], 0, 0)),
                  pl.BlockSpec((nbat, nb, dmb), lambda bi, t, jt_r, it_r: (bi, 0, 0)),
                  pl.BlockSpec((nbat, dmb, blk), qmap),
                  pl.BlockSpec((nbat, blk, d), rmap),
                  pl.BlockSpec((nbat, 1, d), lambda bi, t, jt_r, it_r: (bi, 0, 2)),
                  pl.BlockSpec((nbat, D_MLSTM, blk), qmap),
                  pl.BlockSpec(w_out.shape, full2, pipeline_mode=pl.Buffered(1)),
                  pl.BlockSpec((1, d), full2)],
        out_specs=pl.BlockSpec((nbat, blk, d), rmap),
        scratch_shapes=[pltpu.VMEM((nbat, N_HEADS_MOBA, 2 * HEAD_DIM_MOBA, blk), BF16),
                        pltpu.VMEM((nbat, N_HEADS_MOBA, blk), F32),
                        pltpu.VMEM((nbat, N_HEADS_MOBA, HEAD_DIM_MOBA + 16, blk), F32),
                        pltpu.VMEM((nbat, N_HEADS_MOBA, blk), F32),
                        pltpu.VMEM(w_out.shape, BF16),
                        pltpu.VMEM((nbat, dmb, blk), BF16)]
        + [pltpu.VMEM((N_HEADS_MOBA, blk, blk), F32) for _ in range(nbat)],
    )
    return pl.pallas_call(
        _moba_kernel,
        grid_spec=grid_spec,
        out_shape=jax.ShapeDtypeStruct((b, s, d), x.dtype),
        compiler_params=pltpu.CompilerParams(dimension_semantics=("arbitrary", "arbitrary"),
                                             vmem_limit_bytes=VMEM_LIMIT),
        name="moba_out",
    )(jnp.asarray(jt, jnp.int32), jnp.asarray(it, jnp.int32), qbt, kaug, vbt, kmean, gbt,
      x, mod, out_at, w_out, g_final.reshape(1, d))


def _layer(x, c, w_ada, b_ada, g_norm, w_in, conv_w, conv_b, b_igate, b_fgate, g_mlstm_head, w_out,
           g_final):
    b = x.shape[0]
    mod = _ada(c, w_ada, b_ada).reshape(b, 1, 3 * D_MODEL)

    bif = jnp.concatenate([b_igate.astype(F32), jnp.zeros((4,), F32), b_fgate.astype(F32),
                           jnp.zeros((116,), F32)]).reshape(1, 128)

    qm, km, vmt, gat, gates, qbt, kaug, kmean, vbt, gbt = _proj(
        x, mod, g_norm, jnp.transpose(w_in), conv_w, conv_b, bif)
    out_at = _mlstm(qm, km, vmt, gat, gates, g_mlstm_head)
    return _moba_out(qbt, kaug, vbt, kmean, gbt, x, mod, out_at, w_out, g_final)


def kernel(x, c, w_ada, b_ada, g_norm, w_in, conv_w, conv_b, b_igate, b_fgate, g_mlstm_head, w_out,
           g_final):
    assert w_ada.shape[0] == 1, "single-layer trunk"
    return _layer(x, c, w_ada[0], b_ada[0], g_norm[0], w_in[0], conv_w[0], conv_b[0], b_igate[0],
                  b_fgate[0], g_mlstm_head[0], w_out[0], g_final)
```

```python
import jax
import jax.numpy as jnp
import numpy as np
from jax import lax
from jax.experimental import pallas as pl
from jax.experimental.pallas import tpu as pltpu

F32 = jnp.float32
BF16 = jnp.bfloat16

D_MODEL = 1024
D_MLSTM = 512
N_HEADS_MLSTM = 4
HEAD_DIM_MLSTM = 128
D_MOBA = 512
N_HEADS_MOBA = 8
HEAD_DIM_MOBA = 64
CONV_K = 4
MOBA_BLOCK = 256
MOBA_TOPK = 3
EPS = 1e-6

ROW_TILE = 512
PROJ_CHUNK = 256
MLSTM_CHUNK = 128
MLSTM_STEP_CHUNKS = 8
MOBA_BATCH = 4
MASK_BIAS = -1e30
LOG2E = 1.4426950408889634
N_POS_TERMS = 3
VMEM_LIMIT = 56 * 1024 * 1024

_OFF = {}
_o = 0
for _name, _size in (("qm", 512), ("km", 512), ("vm", 512), ("om", 512), ("im", 4), ("fm", 4),
                     ("zm", 512), ("qb", 512), ("kb", 512), ("vb", 512), ("zb", 512)):
    _OFF[_name] = (_o, _o + _size)
    _o += _size
_NN_GROUPS = ("qm", "km", "kb")
_NT_GROUPS = ("qb", "vb", "zb", "vm", "om", "zm")


def _silu(v):
    return v * jax.nn.sigmoid(v)


def _dot_nt(a, b):
    return lax.dot_general(a, b, (((1,), (1,)), ((), ())), preferred_element_type=F32)


def _dot_tn(a, b):
    return lax.dot_general(a, b, (((0,), (0,)), ((), ())), preferred_element_type=F32)


def _ada_kernel(c_ref, w_ref, b_ref, o_ref):
    a = _silu(c_ref[...])
    o_ref[...] = jnp.dot(a, w_ref[...], preferred_element_type=F32,
                         precision=lax.Precision.HIGHEST) + b_ref[...]


def _ada(c, w_ada, b_ada):
    b, d = c.shape
    n = w_ada.shape[1]
    tn = 1024
    return pl.pallas_call(
        _ada_kernel,
        grid=(n // tn,),
        in_specs=[pl.BlockSpec((b, d), lambda i: (0, 0)),
                  pl.BlockSpec((d, tn), lambda i: (0, i)),
                  pl.BlockSpec((1, tn), lambda i: (0, i))],
        out_specs=pl.BlockSpec((b, tn), lambda i: (0, i)),
        out_shape=jax.ShapeDtypeStruct((b, n), F32),
        compiler_params=pltpu.CompilerParams(dimension_semantics=("arbitrary",),
                                             vmem_limit_bytes=VMEM_LIMIT),
        name="ada",
    )(c, w_ada, b_ada.reshape(1, n))


def _chunk_scan(v, combine, fill):
    pos = lax.broadcasted_iota(jnp.int32, v.shape, 1) & (MLSTM_CHUNK - 1)
    d = 1
    while d < MLSTM_CHUNK:
        shifted = pltpu.roll(v, d, axis=1)
        v = combine(v, jnp.where(pos >= d, shifted, fill))
        d *= 2
    return v


def _proj_kernel(x_ref, shift_ref, scale_ref, gn_ref, w_ref, cw_ref, cb_ref, bif_ref, kx_ref,
                 qm_ref, km_ref, vm_ref, ga_ref, gates_ref, qbt_ref, kaug_ref, kmean_ref, vbt_ref, gbt_ref,
                 ext_ref, wnn_ref, wnt_ref, wif_ref):
    j = pl.program_id(1)
    tm = x_ref.shape[1]

    @pl.when(jnp.logical_and(pl.program_id(0) == 0, j == 0))
    def _():
        for g, name in enumerate(_NN_GROUPS):
            lo = _OFF[name][0]
            for c in range(0, 512, 256):
                wnn_ref[:, 512 * g + c:512 * g + c + 256] = jnp.transpose(
                    w_ref[lo + c:lo + c + 256, :]).astype(BF16)
        for g, name in enumerate(_NT_GROUPS):
            lo = _OFF[name][0]
            for c in range(0, 512, 256):
                wnt_ref[512 * g + c:512 * g + c + 256, :] = w_ref[lo + c:lo + c + 256, :].astype(BF16)
        blk = jnp.transpose(w_ref[_OFF["im"][0]:_OFF["im"][0] + 128, :])
        lane_w = lax.broadcasted_iota(jnp.int32, blk.shape, 1)
        f_cols = pltpu.roll(blk, 4, axis=1)
        wif_ref[...] = jnp.where(lane_w < 4, blk,
                                 jnp.where((lane_w >= 8) & (lane_w < 12), f_cols, 0.0)).astype(BF16)

    @pl.when(j == 0)
    def _():
        ext_ref[0:8, :] = jnp.zeros((8, 2 * D_MLSTM), F32)

    ts = MOBA_BLOCK
    nsub = tm // ts
    cw = PROJ_CHUNK
    halo = CONV_K - 1
    lane = lax.broadcasted_iota(jnp.int32, (ts, 128), 1)

    hbs = []
    for u in range(nsub):
        x = x_ref[0, u * ts:(u + 1) * ts, :]
        ms = jnp.mean(x * x, axis=-1, keepdims=True)
        h = x * lax.rsqrt(ms + EPS) * gn_ref[...] * (1.0 + scale_ref[0]) + shift_ref[0]
        hbs.append(h.astype(BF16))

    for u in range(nsub):
        hb = hbs[u]
        rs = slice(u * ts, (u + 1) * ts)

        ifc = jnp.dot(hb, wif_ref[...], preferred_element_type=F32) + bif_ref[...]
        ift = jnp.transpose(ifc)[0:16, :]
        log_i = ift[0:8]
        f_pre = ift[8:16]
        log_f = jnp.minimum(f_pre, 0.0) - jnp.log1p(jnp.exp(-jnp.abs(f_pre)))
        cum_f = _chunk_scan(log_f, jnp.add, 0.0)
        g = log_i - cum_f
        gates_ref[0, 0:8, rs] = cum_f
        gates_ref[0, 8:16, rs] = g
        gates_ref[0, 16:24, rs] = _chunk_scan(g, jnp.maximum, -jnp.inf)

        def nn(lo, hi, hb=hb):
            return jnp.dot(hb, wnn_ref[:, lo:hi], preferred_element_type=F32)

        def nt(lo, hi, hb=hb):
            return _dot_nt(wnt_ref[lo:hi, :], hb)

        def conv_chunk(lo, u=u, rs=rs, nn=nn):
            cs = slice(lo, lo + cw)
            base = 8 + u * ts
            qk = nn(lo, lo + cw)
            ext_ref[base:base + ts, cs] = qk
            conv = cb_ref[:, cs] + qk * cw_ref[CONV_K - 1:CONV_K, cs]
            for tap in range(halo):
                conv = conv + ext_ref[pl.ds(base - halo + tap, ts), cs] * cw_ref[tap:tap + 1, cs]
            if u == nsub - 1:
                ext_ref[0:8, cs] = qk[ts - 8:ts, :]
            act = _silu(conv)
            if lo < D_MLSTM:
                qm_ref[0, rs, lo:lo + cw] = act.astype(BF16)
            else:
                km_ref[0, rs, lo - D_MLSTM:lo - D_MLSTM + cw] = (act * (HEAD_DIM_MLSTM ** -0.5)).astype(BF16)

        def value_chunk(lo, rs=rs, nt=nt):
            vm_ref[0, lo:lo + cw, rs] = nt(1536 + lo, 1536 + lo + cw).astype(BF16)

        def gate_chunk(lo, rs=rs, nt=nt):
            ga_ref[0, lo:lo + cw, rs] = (jax.nn.sigmoid(nt(2048 + lo, 2048 + lo + cw))
                                        * _silu(nt(2560 + lo, 2560 + lo + cw))).astype(BF16)

        def key_chunk(lo, u=u, rs=rs, nn=nn):
            kb = nn(1024 + lo, 1024 + lo + cw)
            kmean_ref[0, pl.ds(j * nsub + u, 1), lo:lo + cw] = jnp.mean(kb, axis=0, keepdims=True)
            kb2 = (kb * LOG2E).astype(BF16)
            for pr in range(lo // 128, (lo + cw) // 128):
                for hh in range(2):
                    hd = 2 * pr + hh
                    is_key = (lane < HEAD_DIM_MOBA) if hh == 0 else (lane >= HEAD_DIM_MOBA)
                    grp = slice(128 * hd, 128 * (hd + 1))
                    kaug_ref[0, rs, grp] = jnp.where(is_key, kb2[:, 128 * pr - lo:128 * (pr + 1) - lo],
                                                     kx_ref[u, :, grp])

        def qt_chunk(lo, rs=rs, nt=nt):
            qbt_ref[0, lo:lo + cw, rs] = (nt(lo, lo + cw) * (HEAD_DIM_MOBA ** -0.5)).astype(BF16)

        def vt_chunk(lo, u=u, nt=nt):
            vbt_ref[0, u, lo:lo + cw, :] = nt(512 + lo, 512 + lo + cw).astype(BF16)

        def gt_chunk(lo, rs=rs, nt=nt):
            gbt_ref[0, lo:lo + cw, rs] = _silu(nt(1024 + lo, 1024 + lo + cw)).astype(BF16)

        heavy = ([(conv_chunk, lo) for lo in range(0, 2 * D_MLSTM, cw)]
                 + [(gate_chunk, lo) for lo in range(0, D_MLSTM, cw)]
                 + [(key_chunk, lo) for lo in range(0, D_MOBA, cw)])
        light = ([(qt_chunk, lo) for lo in range(0, D_MOBA, cw)]
                 + [(vt_chunk, lo) for lo in range(0, D_MOBA, cw)]
                 + [(value_chunk, lo) for lo in range(0, D_MLSTM, cw)]
                 + [(gt_chunk, lo) for lo in range(0, D_MOBA, cw)])
        for idx in range(max(len(heavy), len(light))):
            for group in (heavy, light):
                if idx < len(group):
                    fn, lo = group[idx]
                    fn(lo)


def _proj(x, mod, g_norm, w_in, conv_w, conv_b, bif):
    b, s, d = x.shape
    tm = ROW_TILE
    nb = s // MOBA_BLOCK
    nsub = tm // MOBA_BLOCK
    assert tm % MOBA_BLOCK == 0
    kx = _moba_key_extras(nb)
    kw = N_HEADS_MOBA * 128
    row = lambda bi, j: (bi, j, 0)
    col = lambda bi, j: (bi, 0, j)
    full2 = lambda bi, j: (0, 0)
    out_shapes = (
        jax.ShapeDtypeStruct((b, s, D_MLSTM), BF16),
        jax.ShapeDtypeStruct((b, s, D_MLSTM), BF16),
        jax.ShapeDtypeStruct((b, D_MLSTM, s), BF16),
        jax.ShapeDtypeStruct((b, D_MLSTM, s), BF16),
        jax.ShapeDtypeStruct((b, 24, s), F32),
        jax.ShapeDtypeStruct((b, D_MOBA, s), BF16),
        jax.ShapeDtypeStruct((b, s, kw), BF16),
        jax.ShapeDtypeStruct((b, nb, D_MOBA), F32),
        jax.ShapeDtypeStruct((b, nb, D_MOBA, MOBA_BLOCK), BF16),
        jax.ShapeDtypeStruct((b, D_MOBA, s), BF16),
    )
    out_specs = (
        pl.BlockSpec((1, tm, D_MLSTM), row),
        pl.BlockSpec((1, tm, D_MLSTM), row),
        pl.BlockSpec((1, D_MLSTM, tm), col),
        pl.BlockSpec((1, D_MLSTM, tm), col),
        pl.BlockSpec((1, 24, tm), col),
        pl.BlockSpec((1, D_MOBA, tm), col),
        pl.BlockSpec((1, tm, kw), row),
        pl.BlockSpec((1, nb, D_MOBA), lambda bi, j: (bi, 0, 0)),
        pl.BlockSpec((1, nsub, D_MOBA, MOBA_BLOCK), lambda bi, j: (bi, j, 0, 0)),
        pl.BlockSpec((1, D_MOBA, tm), col),
    )
    in_specs = [
        pl.BlockSpec((1, tm, d), row),
        pl.BlockSpec((1, 1, d), lambda bi, j: (bi, 0, 0)),
        pl.BlockSpec((1, 1, d), lambda bi, j: (bi, 0, 1)),
        pl.BlockSpec((1, d), full2),
        pl.BlockSpec(w_in.shape, full2, pipeline_mode=pl.Buffered(1)),
        pl.BlockSpec(conv_w.shape, full2),
        pl.BlockSpec((1, 2 * D_MLSTM), full2),
        pl.BlockSpec((1, 128), full2),
        pl.BlockSpec((nsub, MOBA_BLOCK, kw), lambda bi, j: (j, 0, 0)),
    ]
    return pl.pallas_call(
        _proj_kernel,
        grid=(b, s // tm),
        in_specs=in_specs,
        out_specs=out_specs,
        out_shape=out_shapes,
        scratch_shapes=[pltpu.VMEM((tm + 8, 2 * D_MLSTM), F32),
                        pltpu.VMEM((d, 512 * len(_NN_GROUPS)), BF16),
                        pltpu.VMEM((512 * len(_NT_GROUPS), d), BF16),
                        pltpu.VMEM((d, 128), BF16)],
        compiler_params=pltpu.CompilerParams(dimension_semantics=("arbitrary", "arbitrary"),
                                             vmem_limit_bytes=VMEM_LIMIT),
        name="proj",
    )(x, mod, mod, g_norm.reshape(1, d), w_in, conv_w, conv_b.reshape(1, -1), bif, kx)


def _col_bcast(row, n):
    return jnp.transpose(jnp.broadcast_to(row, (n, n)))


def _mlstm_kernel(q_ref, k_ref, vt_ref, gat_ref, gates_ref, gh_ref, o_ref, st_ref, m_ref):
    ci = pl.program_id(1)
    L = MLSTM_CHUNK
    dh = HEAD_DIM_MLSTM
    assert L == dh == 128

    @pl.when(ci == 0)
    def _():
        st_ref[...] = jnp.zeros(st_ref.shape, F32)
        m_ref[...] = jnp.zeros(m_ref.shape, F32)

    s_idx = lax.broadcasted_iota(jnp.int32, (L, L), 0)
    t_idx = lax.broadcasted_iota(jnp.int32, (L, L), 1)
    causal = s_idx <= t_idx
    ones_tail = jnp.ones((16, L), BF16)

    heads = range(N_HEADS_MLSTM)
    sls = [slice(hd * dh, (hd + 1) * dh) for hd in heads]

    m_prev = m_ref[:, 0:1]
    rowv = []
    for c in range(MLSTM_STEP_CHUNKS):
        ts = slice(c * L, (c + 1) * L)
        cum_f = gates_ref[0, 0:8, ts]
        g = gates_ref[0, 8:16, ts]
        g_cmax = gates_ref[0, 16:24, ts]
        f_tot = cum_f[:, L - 1:L]
        g_max = g_cmax[:, L - 1:L]
        a_max = f_tot + g_max
        a_row = -jnp.maximum(m_prev, g_cmax)
        m_new = jnp.maximum(f_tot + m_prev, a_max)
        rowv.append(dict(
            ts=ts, g=g, a_row=a_row,
            w=jnp.exp(g - g_max),
            inter=jnp.exp(a_row + m_prev),
            floor=jnp.exp(a_row - cum_f),
            s_old=jnp.exp(f_tot + m_prev - m_new), s_new=jnp.exp(a_max - m_new)))
        m_prev = m_new
    m_ref[...] = jnp.broadcast_to(m_prev, m_ref.shape)

    s_kq = {}
    for c, rv in enumerate(rowv):
        for hd in heads:
            q = q_ref[0, rv["ts"], sls[hd]]
            k = k_ref[0, rv["ts"], sls[hd]]
            g_col = _col_bcast(rv["g"][hd:hd + 1, :], L)
            decay = jnp.exp(jnp.where(causal, g_col + rv["a_row"][hd:hd + 1, :], -jnp.inf))
            s_kq[c, hd] = (_dot_nt(k, q) * decay).astype(BF16)

    st = [st_ref[hd] for hd in heads]
    for c, rv in enumerate(rowv):
        ts = rv["ts"]
        vts = [jnp.concatenate([vt_ref[0, sls[hd], ts], ones_tail], axis=0) for hd in heads]
        carried = [_dot_nt(st[hd].astype(BF16), q_ref[0, ts, sls[hd]]) for hd in heads]
        intra = [jnp.dot(vts[hd], s_kq[c, hd], preferred_element_type=F32) for hd in heads]
        for hd in heads:
            mixed = intra[hd] + rv["inter"][hd:hd + 1, :] * carried[hd]
            den = mixed[dh:dh + 1, :]
            hh = mixed[0:dh] / jnp.maximum(jnp.abs(den), rv["floor"][hd:hd + 1, :])
            hh = hh * lax.rsqrt(jnp.mean(hh * hh, axis=0, keepdims=True) + EPS)
            o_ref[0, sls[hd], ts] = (hh * gh_ref[sls[hd], :] * gat_ref[0, sls[hd], ts].astype(F32)).astype(BF16)
        for hd in heads:
            vw = (vts[hd].astype(F32) * rv["w"][hd:hd + 1, :]).astype(BF16)
            st_loc = jnp.dot(vw, k_ref[0, ts, sls[hd]], preferred_element_type=F32)
            st[hd] = rv["s_old"][hd:hd + 1, :] * st[hd] + rv["s_new"][hd:hd + 1, :] * st_loc

    for hd in heads:
        st_ref[hd] = st[hd]


def _mlstm(qm, km, vmt, gat, gates, g_head):
    b, s, dm = qm.shape
    L = MLSTM_CHUNK * MLSTM_STEP_CHUNKS
    row = lambda bi, ci: (bi, ci, 0)
    col = lambda bi, ci: (bi, 0, ci)
    g_bc = jnp.broadcast_to(g_head.astype(F32).reshape(dm, 1), (dm, 128))
    return pl.pallas_call(
        _mlstm_kernel,
        grid=(b, s // L),
        in_specs=[pl.BlockSpec((1, L, dm), row), pl.BlockSpec((1, L, dm), row),
                  pl.BlockSpec((1, dm, L), col), pl.BlockSpec((1, dm, L), col),
                  pl.BlockSpec((1, 24, L), col),
                  pl.BlockSpec((dm, 128), lambda bi, ci: (0, 0))],
        out_specs=pl.BlockSpec((1, dm, L), col),
        out_shape=jax.ShapeDtypeStruct((b, dm, s), BF16),
        scratch_shapes=[pltpu.VMEM((N_HEADS_MLSTM, HEAD_DIM_MLSTM + 16, HEAD_DIM_MLSTM), F32),
                        pltpu.VMEM((8, 128), F32)],
        compiler_params=pltpu.CompilerParams(dimension_semantics=("arbitrary", "arbitrary"),
                                             vmem_limit_bytes=VMEM_LIMIT),
        name="mlstm",
    )(qm, km, vmt, gat, gates, g_bc)


def _moba_steps(nb):
    jt, it = [], []
    for j in range(nb):
        for i in [j] + list(range(j)):
            jt.append(j)
            it.append(i)
    return jt, it


def _moba_key_extras(nb):
    blk = MOBA_BLOCK
    lane = np.arange(N_HEADS_MOBA * 128)
    head = lane // 128
    w = (lane % 128 - np.where(head % 2 == 0, HEAD_DIM_MOBA, 0))[None, None, :]
    slope = (2.0 ** -(head + 1.0))[None, None, :]
    i = np.arange(nb)[:, None, None]
    c = np.arange(blk)[None, :, None]
    tab = np.where(w == i, 1.0, 0.0)
    rest = LOG2E * slope * (i * blk + c)
    for term in range(N_POS_TERMS):
        piece = rest.astype(np.float32).astype(BF16).astype(np.float64)
        tab = tab + np.where(w == nb + term, piece, 0.0)
        rest = rest - piece
    return jnp.asarray(tab.astype(np.float32).astype(BF16))


def _moba_kernel(jt_ref, it_ref, qt_ref, k_ref, vt_ref, kmean_ref, gt_ref, x_ref, gate_ref, oat_ref, w_ref,
                 gf_ref, o_ref,
                 qaug_ref, m_ref, acc_ref, cmax_ref, wb_ref, obt_ref, *s_refs):
    @pl.when(jnp.logical_and(pl.program_id(0) == 0, pl.program_id(1) == 0))
    def _():
        for r in range(0, w_ref.shape[0], 256):
            wb_ref[r:r + 256, :] = w_ref[r:r + 256, :].astype(BF16)

    t = pl.program_id(1)
    j = jt_ref[t]
    i = it_ref[t]
    blk = MOBA_BLOCK
    dh = HEAD_DIM_MOBA
    nb = kmean_ref.shape[1]
    nbat = qt_ref.shape[0]
    first = i == j
    last = jnp.logical_or(i == j - 1, j == 0)

    def key_tile(bb, h):
        return k_ref[bb, :, 128 * h:128 * (h + 1)]

    def rows(h):
        return slice(dh * h, dh * (h + 1))

    ones_tail = jnp.ones((16, blk), BF16)

    def values_t(bb, h):
        return jnp.concatenate([vt_ref[bb, 0, rows(h), :], ones_tail], axis=0)

    @pl.when(first)
    def _():
        blk_i = lax.broadcasted_iota(jnp.int32, (nb, blk), 0)
        lane_m = lax.broadcasted_iota(jnp.int32, (nb, 2 * dh), 1)
        key_c = lax.broadcasted_iota(jnp.int32, (blk, blk), 0)
        qry_r = lax.broadcasted_iota(jnp.int32, (blk, blk), 1)
        causal = key_c <= qry_r
        ones_rows = jnp.where(blk_i < N_POS_TERMS, 1.0, 0.0)
        gates = {}
        for bb in range(nbat):
            for h in range(N_HEADS_MOBA):
                pr, hh = divmod(h, 2)
                qt_pair = qt_ref[bb, 128 * pr:128 * (pr + 1), :]
                in_head = (lane_m < dh) if hh == 0 else (lane_m >= dh)
                km = jnp.where(in_head, kmean_ref[bb, :, 128 * pr:128 * (pr + 1)], 0.0).astype(BF16)
                gates[bb, h] = jnp.dot(km, qt_pair, preferred_element_type=F32)
        for bb in range(nbat):
            for h in range(N_HEADS_MOBA):
                pr, hh = divmod(h, 2)
                gate = gates[bb, h]
                rank = jnp.zeros((nb, blk), jnp.int32)
                for i2 in range(nb):
                    g2 = gate[i2:i2 + 1, :]
                    beats = (g2 > gate) | ((g2 == gate) & (i2 < blk_i))
                    rank = rank + jnp.where(beats & (i2 < j), 1, 0)
                keep = ((blk_i < j) & (rank < MOBA_TOPK)) | (blk_i == j)
                sel_bias = jnp.where(keep, 0.0, MASK_BIAS)
                extra = jnp.concatenate([sel_bias, ones_rows, jnp.zeros((dh - 16, blk), F32)], axis=0)
                extra = extra.astype(BF16)
                qt_h = qt_ref[bb, dh * h:dh * (h + 1), :]
                qaug_ref[bb, h] = jnp.concatenate([qt_h, extra] if hh == 0 else [extra, qt_h], axis=0)

        for bb in range(nbat):
            for h in range(N_HEADS_MOBA):
                s = jnp.dot(key_tile(bb, h), qaug_ref[bb, h], preferred_element_type=F32)
                s = jnp.where(causal, s, MASK_BIAS)
                s_refs[bb][h] = s
                m_ref[bb, h:h + 1, :] = jnp.max(s, axis=0, keepdims=True)

        for bb in range(nbat):
            for h in range(N_HEADS_MOBA):
                p0 = jnp.exp2(s_refs[bb][h] - m_ref[bb, h:h + 1, :])
                acc_ref[bb, h] = jnp.dot(values_t(bb, h), p0.astype(BF16), preferred_element_type=F32)

    @pl.when(jnp.logical_not(first))
    def _():
        for bb in range(nbat):
            for h in range(N_HEADS_MOBA):
                s = jnp.dot(key_tile(bb, h), qaug_ref[bb, h], preferred_element_type=F32)
                s_refs[bb][h] = s
                cmax_ref[bb, h:h + 1, :] = jnp.max(s, axis=0, keepdims=True)
        for bb in range(nbat):
            for h in range(N_HEADS_MOBA):
                m_run = m_ref[bb, h:h + 1, :]
                m_new = jnp.maximum(m_run, cmax_ref[bb, h:h + 1, :])
                alpha = jnp.exp2(m_run - m_new)
                pr_ = jnp.exp2(s_refs[bb][h] - m_new)
                m_ref[bb, h:h + 1, :] = m_new
                acc_ref[bb, h] = alpha * acc_ref[bb, h] + jnp.dot(values_t(bb, h), pr_.astype(BF16),
                                                                  preferred_element_type=F32)

    @pl.when(last)
    def _():
        for bb in range(nbat):
            for h in range(N_HEADS_MOBA):
                acc = acc_ref[bb, h]
                out = acc[0:dh] / acc[dh:dh + 1] * gt_ref[bb, rows(h), :].astype(F32)
                obt_ref[bb, rows(h), :] = out.astype(BF16)
        for bb in range(nbat):
            y = (_dot_tn(oat_ref[bb], wb_ref[0:D_MLSTM, :])
                 + _dot_tn(obt_ref[bb], wb_ref[D_MLSTM:D_MLSTM + D_MOBA, :]))
            r = x_ref[bb] + gate_ref[bb] * y
            o_ref[bb] = r * lax.rsqrt(jnp.mean(r * r, axis=-1, keepdims=True) + EPS) * gf_ref[...]


def _moba_out(qbt, kaug, vbt, kmean, gbt, x, mod, out_at, w_out, g_final):
    b, dmb, s = qbt.shape
    d = x.shape[2]
    blk = MOBA_BLOCK
    nb = s // blk
    kw = kaug.shape[2]
    nbat = MOBA_BATCH
    assert nb == 8, "selection extras assume 8 key blocks"
    jt, it = _moba_steps(nb)
    qmap = lambda bi, t, jt_r, it_r: (bi, 0, jt_r[t])
    rmap = lambda bi, t, jt_r, it_r: (bi, jt_r[t], 0)
    full2 = lambda bi, t, jt_r, it_r: (0, 0)
    grid_spec = pltpu.PrefetchScalarGridSpec(
        num_scalar_prefetch=2,
        grid=(b // nbat, len(jt)),
        in_specs=[pl.BlockSpec((nbat, dmb, blk), qmap),
                  pl.BlockSpec((nbat, blk, kw), lambda bi, t, jt_r, it_r: (bi, it_r[t], 0)),
                  pl.BlockSpec((nbat, 1, dmb, blk), lambda bi, t, jt_r, it_r: (bi, it_r[t], 0, 0)),
                  pl.BlockSpec((nbat, nb, dmb), lambda bi, t, jt_r, it_r: (bi, 0, 0)),
                  pl.BlockSpec((nbat, dmb, blk), qmap),
                  pl.BlockSpec((nbat, blk, d), rmap),
                  pl.BlockSpec((nbat, 1, d), lambda bi, t, jt_r, it_r: (bi, 0, 2)),
                  pl.BlockSpec((nbat, D_MLSTM, blk), qmap),
                  pl.BlockSpec(w_out.shape, full2, pipeline_mode=pl.Buffered(1)),
                  pl.BlockSpec((1, d), full2)],
        out_specs=pl.BlockSpec((nbat, blk, d), rmap),
        scratch_shapes=[pltpu.VMEM((nbat, N_HEADS_MOBA, 2 * HEAD_DIM_MOBA, blk), BF16),
                        pltpu.VMEM((nbat, N_HEADS_MOBA, blk), F32),
                        pltpu.VMEM((nbat, N_HEADS_MOBA, HEAD_DIM_MOBA + 16, blk), F32),
                        pltpu.VMEM((nbat, N_HEADS_MOBA, blk), F32),
                        pltpu.VMEM(w_out.shape, BF16),
                        pltpu.VMEM((nbat, dmb, blk), BF16)]
        + [pltpu.VMEM((N_HEADS_MOBA, blk, blk), F32) for _ in range(nbat)],
    )
    return pl.pallas_call(
        _moba_kernel,
        grid_spec=grid_spec,
        out_shape=jax.ShapeDtypeStruct((b, s, d), x.dtype),
        compiler_params=pltpu.CompilerParams(dimension_semantics=("arbitrary", "arbitrary"),
                                             vmem_limit_bytes=VMEM_LIMIT),
        name="moba_out",
    )(jnp.asarray(jt, jnp.int32), jnp.asarray(it, jnp.int32), qbt, kaug, vbt, kmean, gbt,
      x, mod, out_at, w_out, g_final.reshape(1, d))


def _layer(x, c, w_ada, b_ada, g_norm, w_in, conv_w, conv_b, b_igate, b_fgate, g_mlstm_head, w_out,
           g_final):
    b = x.shape[0]
    mod = _ada(c, w_ada, b_ada).reshape(b, 1, 3 * D_MODEL)

    bif = jnp.concatenate([b_igate.astype(F32), jnp.zeros((4,), F32), b_fgate.astype(F32),
                           jnp.zeros((116,), F32)]).reshape(1, 128)

    qm, km, vmt, gat, gates, qbt, kaug, kmean, vbt, gbt = _proj(
        x, mod, g_norm, jnp.transpose(w_in), conv_w, conv_b, bif)
    out_at = _mlstm(qm, km, vmt, gat, gates, g_mlstm_head)
    return _moba_out(qbt, kaug, vbt, kmean, gbt, x, mod, out_at, w_out, g_final)


def kernel(x, c, w_ada, b_ada, g_norm, w_in, conv_w, conv_b, b_igate, b_fgate, g_mlstm_head, w_out,
           g_final):
    assert w_ada.shape[0] == 1, "single-layer trunk"
    return _layer(x, c, w_ada[0], b_ada[0], g_norm[0], w_in[0], conv_w[0], conv_b[0], b_igate[0],
                  b_fgate[0], g_mlstm_head[0], w_out[0], g_final)
```

```python
import jax
import jax.numpy as jnp
import numpy as np
from jax import lax
from jax.experimental import pallas as pl
from jax.experimental.pallas import tpu as pltpu

F32 = jnp.float32
BF16 = jnp.bfloat16

D_MODEL = 1024
D_MLSTM = 512
N_HEADS_MLSTM = 4
HEAD_DIM_MLSTM = 128
D_MOBA = 512
N_HEADS_MOBA = 8
HEAD_DIM_MOBA = 64
CONV_K = 4
MOBA_BLOCK = 256
MOBA_TOPK = 3
EPS = 1e-6

ROW_TILE = 512
PROJ_CHUNK = 256
MLSTM_CHUNK = 128
MOBA_BATCH = 4
MASK_BIAS = -1e30
LOG2E = 1.4426950408889634
N_POS_TERMS = 3
VMEM_LIMIT = 56 * 1024 * 1024
_TILES_PER_SEQ = 2048 // ROW_TILE
_N_ROW_TILES = 8 * _TILES_PER_SEQ

_OFF = {}
_o = 0
for _name, _size in (("qm", 512), ("km", 512), ("vm", 512), ("om", 512), ("im", 4), ("fm", 4),
                     ("zm", 512), ("qb", 512), ("kb", 512), ("vb", 512), ("zb", 512)):
    _OFF[_name] = (_o, _o + _size)
    _o += _size
_NN_GROUPS = ("qm", "km", "kb")
_NT_GROUPS = ("qb", "vb", "zb", "vm", "om", "zm")


def _silu(v):
    return v * jax.nn.sigmoid(v)


def _dot_nt(a, b):
    return lax.dot_general(a, b, (((1,), (1,)), ((), ())), preferred_element_type=F32)


def _dot_tn(a, b):
    return lax.dot_general(a, b, (((0,), (0,)), ((), ())), preferred_element_type=F32)


def _ada_kernel(c_ref, w_ref, b_ref, o_ref):
    a = _silu(c_ref[...])
    o_ref[...] = jnp.dot(a, w_ref[...], preferred_element_type=F32,
                         precision=lax.Precision.HIGHEST) + b_ref[...]


def _ada(c, w_ada, b_ada):
    b, d = c.shape
    n = w_ada.shape[1]
    tn = 1024
    return pl.pallas_call(
        _ada_kernel,
        grid=(n // tn,),
        in_specs=[pl.BlockSpec((b, d), lambda i: (0, 0)),
                  pl.BlockSpec((d, tn), lambda i: (0, i)),
                  pl.BlockSpec((1, tn), lambda i: (0, i))],
        out_specs=pl.BlockSpec((b, tn), lambda i: (0, i)),
        out_shape=jax.ShapeDtypeStruct((b, n), F32),
        compiler_params=pltpu.CompilerParams(dimension_semantics=("arbitrary",),
                                             vmem_limit_bytes=VMEM_LIMIT),
        name="ada",
    )(c, w_ada, b_ada.reshape(1, n))


def _chunk_scan(v, combine, fill):
    pos = lax.broadcasted_iota(jnp.int32, v.shape, 1) & (MLSTM_CHUNK - 1)
    d = 1
    while d < MLSTM_CHUNK:
        shifted = pltpu.roll(v, d, axis=1)
        v = combine(v, jnp.where(pos >= d, shifted, fill))
        d *= 2
    return v


def _col_bcast(row, n):
    return jnp.transpose(jnp.broadcast_to(row, (n, n)))


def _interleave(major, minor):
    out, done = [], 0
    for idx, task in enumerate(major):
        out.append(task)
        want = (idx + 1) * len(minor) // len(major)
        out.extend(minor[done:want])
        done = want
    return out


def _proj_mlstm_kernel(x_ref, shift_ref, scale_ref, gn_ref, w_ref, cw_ref, cb_ref, bif_ref, kx_ref, gh_ref,
                       qbt_ref, kaug_ref, kmean_ref, vbt_ref, gbt_ref, oat_ref,
                       ext_ref, wnn_ref, wnt_ref, wif_ref,
                       qp_ref, kp_ref, vp_ref, gp_ref, tp_ref,
                       qm_ref, km_ref, vm_ref, gm_ref, tm_ref, st_ref, m_ref):
    t = pl.program_id(0)
    tm = x_ref.shape[1]
    ts = MOBA_BLOCK
    nsub = tm // ts
    cw = PROJ_CHUNK
    halo = CONV_K - 1
    L = MLSTM_CHUNK
    dh = HEAD_DIM_MLSTM
    npair = N_HEADS_MLSTM // 2
    nchunk = tm // L
    jp = jnp.minimum(t, _N_ROW_TILES - 1) % _TILES_PER_SEQ
    jm = jnp.maximum(t - 1, 0) % _TILES_PER_SEQ

    @pl.when(t == 0)
    def _():
        for g, name in enumerate(_NN_GROUPS):
            lo = _OFF[name][0]
            for c in range(0, 512, 256):
                wnn_ref[:, 512 * g + c:512 * g + c + 256] = jnp.transpose(
                    w_ref[lo + c:lo + c + 256, :]).astype(BF16)
        for g, name in enumerate(_NT_GROUPS):
            lo = _OFF[name][0]
            for c in range(0, 512, 256):
                wnt_ref[512 * g + c:512 * g + c + 256, :] = w_ref[lo + c:lo + c + 256, :].astype(BF16)
        blk = jnp.transpose(w_ref[_OFF["im"][0]:_OFF["im"][0] + 128, :])
        lane_w = lax.broadcasted_iota(jnp.int32, blk.shape, 1)
        f_cols = pltpu.roll(blk, 4, axis=1)
        wif_ref[...] = jnp.where(lane_w < 4, blk,
                                 jnp.where((lane_w >= 8) & (lane_w < 12), f_cols, 0.0)).astype(BF16)
        for ref in (qp_ref, kp_ref, vp_ref, gp_ref, tp_ref, st_ref, m_ref):
            ref[...] = jnp.zeros(ref.shape, ref.dtype)

    @pl.when(jp == 0)
    def _():
        ext_ref[0:8, :] = jnp.zeros((8, 2 * D_MLSTM), F32)

    @pl.when(jm == 0)
    def _():
        st_ref[...] = jnp.zeros(st_ref.shape, F32)
        m_ref[...] = jnp.zeros(m_ref.shape, F32)

    qm_ref[...] = qp_ref[...]
    km_ref[...] = kp_ref[...]
    vm_ref[...] = vp_ref[...]
    gm_ref[...] = gp_ref[...]
    tm_ref[...] = tp_ref[...]

    s_idx = lax.broadcasted_iota(jnp.int32, (L, 2 * L), 0)
    t_idx = lax.broadcasted_iota(jnp.int32, (L, 2 * L), 1) & (L - 1)
    causal = s_idx <= t_idx
    lane2 = lax.broadcasted_iota(jnp.int32, (1, 2 * L), 1)
    ones_tail = jnp.ones((16, L), BF16)
    zeros_blk = jnp.zeros((L, dh), BF16)

    def pair_row(v, p):
        return jnp.concatenate([v[2 * p:2 * p + 1, :], v[2 * p + 1:2 * p + 2, :]], axis=1)

    def pair_scalar(v, p):
        return jnp.where(lane2 < L, v[2 * p:2 * p + 1, :], v[2 * p + 1:2 * p + 2, :])

    def block_diag(a, b):
        return jnp.concatenate([jnp.concatenate([a, zeros_blk], axis=1),
                                jnp.concatenate([zeros_blk, b], axis=1)], axis=0)

    m_prev = m_ref[:, 0:1]
    rowv = []
    for c in range(nchunk):
        cs_ = slice(c * L, (c + 1) * L)
        cum_f = tm_ref[0:8, cs_]
        g = tm_ref[8:16, cs_]
        g_cmax = tm_ref[16:24, cs_]
        f_tot = cum_f[:, L - 1:L]
        g_max = g_cmax[:, L - 1:L]
        a_max = f_tot + g_max
        a_row = -jnp.maximum(m_prev, g_cmax)
        m_new = jnp.maximum(f_tot + m_prev, a_max)
        rowv.append(dict(
            ts=cs_, g=g, a_row=a_row,
            w=jnp.exp(g - g_max),
            inter=jnp.exp(a_row + m_prev),
            floor=jnp.exp(a_row - cum_f),
            s_old=jnp.exp(f_tot + m_prev - m_new), s_new=jnp.exp(a_max - m_new)))
        m_prev = m_new
    m_ref[...] = jnp.broadcast_to(m_prev, m_ref.shape)

    def q_diag(cs_, p):
        return block_diag(qm_ref[cs_, 2 * p * dh:(2 * p + 1) * dh],
                          qm_ref[cs_, (2 * p + 1) * dh:(2 * p + 2) * dh])

    s_kq, vts, carried, intra = {}, {}, {}, {}
    st = {p: st_ref[p] for p in range(npair)}
    mtasks = []

    def score_task(c, p):
        rv = rowv[c]
        k2 = km_ref[rv["ts"], 2 * p * dh:(2 * p + 2) * dh]
        g_col = jnp.concatenate([_col_bcast(rv["g"][2 * p:2 * p + 1, :], L),
                                 _col_bcast(rv["g"][2 * p + 1:2 * p + 2, :], L)], axis=1)
        decay = jnp.exp(jnp.where(causal, g_col + pair_row(rv["a_row"], p), -jnp.inf))
        s_kq[c, p] = (_dot_nt(k2, q_diag(rv["ts"], p)) * decay).astype(BF16)

    def mix_task(c, p):
        cs_ = rowv[c]["ts"]
        vt2 = jnp.concatenate(
            [jnp.concatenate([vm_ref[(2 * p + e) * dh:(2 * p + e + 1) * dh, cs_], ones_tail], axis=0)
             for e in range(2)], axis=1)
        vts[c, p] = vt2
        carried[c, p] = _dot_nt(st[p].astype(BF16), q_diag(cs_, p))
        intra[c, p] = jnp.dot(vt2, block_diag(s_kq[c, p][:, 0:L], s_kq[c, p][:, L:2 * L]),
                              preferred_element_type=F32)

    def finish_task(c, p):
        rv = rowv[c]
        cs_ = rv["ts"]
        mixed = intra[c, p] + pair_row(rv["inter"], p) * carried[c, p]
        den = mixed[dh:dh + 1, :]
        hh = mixed[0:dh] / jnp.maximum(jnp.abs(den), pair_row(rv["floor"], p))
        hh = hh * lax.rsqrt(jnp.mean(hh * hh, axis=0, keepdims=True) + EPS)
        for e in range(2):
            sl = slice((2 * p + e) * dh, (2 * p + e + 1) * dh)
            oat_ref[0, sl, cs_] = (hh[:, e * L:(e + 1) * L] * gh_ref[sl, :]
                                   * gm_ref[sl, cs_].astype(F32)).astype(BF16)

    def state_task(c, p):
        rv = rowv[c]
        cs_ = rv["ts"]
        vw = (vts[c, p].astype(F32) * pair_row(rv["w"], p)).astype(BF16)
        k_bd = block_diag(km_ref[cs_, 2 * p * dh:(2 * p + 1) * dh],
                          km_ref[cs_, (2 * p + 1) * dh:(2 * p + 2) * dh])
        st_loc = jnp.dot(vw, k_bd, preferred_element_type=F32)
        st[p] = pair_scalar(rv["s_old"], p) * st[p] + pair_scalar(rv["s_new"], p) * st_loc

    for c in range(nchunk):
        for p in range(npair):
            mtasks.append((score_task, c, p))
    for c in range(nchunk):
        for fn in (mix_task, finish_task, state_task):
            for p in range(npair):
                mtasks.append((fn, c, p))

    lane = lax.broadcasted_iota(jnp.int32, (ts, 128), 1)
    hbs = []
    for u in range(nsub):
        x = x_ref[0, u * ts:(u + 1) * ts, :]
        ms = jnp.mean(x * x, axis=-1, keepdims=True)
        h = x * lax.rsqrt(ms + EPS) * gn_ref[...] * (1.0 + scale_ref[0]) + shift_ref[0]
        hbs.append(h.astype(BF16))

    ptasks = []
    for u in range(nsub):
        hb = hbs[u]
        rs = slice(u * ts, (u + 1) * ts)

        def gates_task(_a, _b, hb=hb, rs=rs):
            ifc = jnp.dot(hb, wif_ref[...], preferred_element_type=F32) + bif_ref[...]
            ift = jnp.transpose(ifc)[0:16, :]
            log_i = ift[0:8]
            f_pre = ift[8:16]
            log_f = jnp.minimum(f_pre, 0.0) - jnp.log1p(jnp.exp(-jnp.abs(f_pre)))
            cum_f = _chunk_scan(log_f, jnp.add, 0.0)
            g = log_i - cum_f
            tp_ref[0:8, rs] = cum_f
            tp_ref[8:16, rs] = g
            tp_ref[16:24, rs] = _chunk_scan(g, jnp.maximum, -jnp.inf)

        def nn(lo, hi, hb=hb):
            return jnp.dot(hb, wnn_ref[:, lo:hi], preferred_element_type=F32)

        def nt(lo, hi, hb=hb):
            return _dot_nt(wnt_ref[lo:hi, :], hb)

        def conv_chunk(lo, _b, u=u, rs=rs, nn=nn):
            cs = slice(lo, lo + cw)
            base = 8 + u * ts
            qk = nn(lo, lo + cw)
            ext_ref[base:base + ts, cs] = qk
            conv = cb_ref[:, cs] + qk * cw_ref[CONV_K - 1:CONV_K, cs]
            for tap in range(halo):
                conv = conv + ext_ref[pl.ds(base - halo + tap, ts), cs] * cw_ref[tap:tap + 1, cs]
            if u == nsub - 1:
                ext_ref[0:8, cs] = qk[ts - 8:ts, :]
            act = _silu(conv)
            if lo < D_MLSTM:
                qp_ref[rs, lo:lo + cw] = act.astype(BF16)
            else:
                kp_ref[rs, lo - D_MLSTM:lo - D_MLSTM + cw] = (act * (HEAD_DIM_MLSTM ** -0.5)).astype(BF16)

        def value_chunk(lo, _b, rs=rs, nt=nt):
            vp_ref[lo:lo + cw, rs] = nt(1536 + lo, 1536 + lo + cw).astype(BF16)

        def gate_chunk(lo, _b, rs=rs, nt=nt):
            gp_ref[lo:lo + cw, rs] = (jax.nn.sigmoid(nt(2048 + lo, 2048 + lo + cw))
                                     * _silu(nt(2560 + lo, 2560 + lo + cw))).astype(BF16)

        def key_chunk(lo, _b, u=u, rs=rs, nn=nn):
            kb = nn(1024 + lo, 1024 + lo + cw)
            kmean_ref[0, pl.ds(jp * nsub + u, 1), lo:lo + cw] = jnp.mean(kb, axis=0, keepdims=True)
            kb2 = (kb * LOG2E).astype(BF16)
            for pr in range(lo // 128, (lo + cw) // 128):
                for hh in range(2):
                    hd = 2 * pr + hh
                    is_key = (lane < HEAD_DIM_MOBA) if hh == 0 else (lane >= HEAD_DIM_MOBA)
                    grp = slice(128 * hd, 128 * (hd + 1))
                    kaug_ref[0, rs, grp] = jnp.where(is_key, kb2[:, 128 * pr - lo:128 * (pr + 1) - lo],
                                                     kx_ref[u, :, grp])

        def qt_chunk(lo, _b, rs=rs, nt=nt):
            qbt_ref[0, lo:lo + cw, rs] = (nt(lo, lo + cw) * (HEAD_DIM_MOBA ** -0.5)).astype(BF16)

        def vt_chunk(lo, _b, u=u, nt=nt):
            vbt_ref[0, u, lo:lo + cw, :] = nt(512 + lo, 512 + lo + cw).astype(BF16)

        def gt_chunk(lo, _b, rs=rs, nt=nt):
            gbt_ref[0, lo:lo + cw, rs] = _silu(nt(1024 + lo, 1024 + lo + cw)).astype(BF16)

        heavy = ([(conv_chunk, lo, 0) for lo in range(0, 2 * D_MLSTM, cw)]
                 + [(gate_chunk, lo, 0) for lo in range(0, D_MLSTM, cw)]
                 + [(key_chunk, lo, 0) for lo in range(0, D_MOBA, cw)])
        light = ([(qt_chunk, lo, 0) for lo in range(0, D_MOBA, cw)]
                 + [(vt_chunk, lo, 0) for lo in range(0, D_MOBA, cw)]
                 + [(value_chunk, lo, 0) for lo in range(0, D_MLSTM, cw)]
                 + [(gt_chunk, lo, 0) for lo in range(0, D_MOBA, cw)])
        ptasks.append((gates_task, 0, 0))
        for idx in range(max(len(heavy), len(light))):
            for group in (heavy, light):
                if idx < len(group):
                    ptasks.append(group[idx])

    for fn, a0, a1 in _interleave(ptasks, mtasks):
        fn(a0, a1)

    for p in range(npair):
        st_ref[p] = st[p]


def _proj_mlstm(x, mod, g_norm, w_in, conv_w, conv_b, bif, g_head):
    b, s, d = x.shape
    tm = ROW_TILE
    nb = s // MOBA_BLOCK
    nsub = tm // MOBA_BLOCK
    assert tm % MOBA_BLOCK == 0 and (b * s) // tm == _N_ROW_TILES and s // tm == _TILES_PER_SEQ
    kx = _moba_key_extras(nb)
    kw = N_HEADS_MOBA * 128
    dm = D_MLSTM
    g_bc = jnp.broadcast_to(g_head.astype(F32).reshape(dm, 1), (dm, 128))
    tps = _TILES_PER_SEQ
    last = _N_ROW_TILES - 1
    pt = lambda t: jnp.minimum(t, last)
    mt = lambda t: jnp.maximum(t - 1, 0)
    full2 = lambda t: (0, 0)
    out_shapes = (
        jax.ShapeDtypeStruct((b, D_MOBA, s), BF16),
        jax.ShapeDtypeStruct((b, s, kw), BF16),
        jax.ShapeDtypeStruct((b, nb, D_MOBA), F32),
        jax.ShapeDtypeStruct((b, nb, D_MOBA, MOBA_BLOCK), BF16),
        jax.ShapeDtypeStruct((b, D_MOBA, s), BF16),
        jax.ShapeDtypeStruct((b, dm, s), BF16),
    )
    out_specs = (
        pl.BlockSpec((1, D_MOBA, tm), lambda t: (pt(t) // tps, 0, pt(t) % tps)),
        pl.BlockSpec((1, tm, kw), lambda t: (pt(t) // tps, pt(t) % tps, 0)),
        pl.BlockSpec((1, nb, D_MOBA), lambda t: (pt(t) // tps, 0, 0)),
        pl.BlockSpec((1, nsub, D_MOBA, MOBA_BLOCK), lambda t: (pt(t) // tps, pt(t) % tps, 0, 0)),
        pl.BlockSpec((1, D_MOBA, tm), lambda t: (pt(t) // tps, 0, pt(t) % tps)),
        pl.BlockSpec((1, dm, tm), lambda t: (mt(t) // tps, 0, mt(t) % tps)),
    )
    in_specs = [
        pl.BlockSpec((1, tm, d), lambda t: (pt(t) // tps, pt(t) % tps, 0)),
        pl.BlockSpec((1, 1, d), lambda t: (pt(t) // tps, 0, 0)),
        pl.BlockSpec((1, 1, d), lambda t: (pt(t) // tps, 0, 1)),
        pl.BlockSpec((1, d), full2),
        pl.BlockSpec(w_in.shape, full2, pipeline_mode=pl.Buffered(1)),
        pl.BlockSpec(conv_w.shape, full2),
        pl.BlockSpec((1, 2 * dm), full2),
        pl.BlockSpec((1, 128), full2),
        pl.BlockSpec((nsub, MOBA_BLOCK, kw), lambda t: (pt(t) % tps, 0, 0)),
        pl.BlockSpec((dm, 128), full2),
    ]
    operand_scratch = [pltpu.VMEM((tm, dm), BF16), pltpu.VMEM((tm, dm), BF16),
                       pltpu.VMEM((dm, tm), BF16), pltpu.VMEM((dm, tm), BF16),
                       pltpu.VMEM((24, tm), F32)]
    return pl.pallas_call(
        _proj_mlstm_kernel,
        grid=(_N_ROW_TILES + 1,),
        in_specs=in_specs,
        out_specs=out_specs,
        out_shape=out_shapes,
        scratch_shapes=[pltpu.VMEM((tm + 8, 2 * dm), F32),
                        pltpu.VMEM((d, 512 * len(_NN_GROUPS)), BF16),
                        pltpu.VMEM((512 * len(_NT_GROUPS), d), BF16),
                        pltpu.VMEM((d, 128), BF16)]
        + operand_scratch + operand_scratch
        + [pltpu.VMEM((N_HEADS_MLSTM // 2, HEAD_DIM_MLSTM + 16, 2 * HEAD_DIM_MLSTM), F32),
           pltpu.VMEM((8, 128), F32)],
        compiler_params=pltpu.CompilerParams(dimension_semantics=("arbitrary",),
                                             vmem_limit_bytes=VMEM_LIMIT),
        name="proj_mlstm",
    )(x, mod, mod, g_norm.reshape(1, d), w_in, conv_w, conv_b.reshape(1, -1), bif, kx, g_bc)


def _moba_steps(nb):
    jt, it = [], []
    for j in range(nb):
        for i in [j] + list(range(j)):
            jt.append(j)
            it.append(i)
    return jt, it


def _moba_key_extras(nb):
    blk = MOBA_BLOCK
    lane = np.arange(N_HEADS_MOBA * 128)
    head = lane // 128
    w = (lane % 128 - np.where(head % 2 == 0, HEAD_DIM_MOBA, 0))[None, None, :]
    slope = (2.0 ** -(head + 1.0))[None, None, :]
    i = np.arange(nb)[:, None, None]
    c = np.arange(blk)[None, :, None]
    tab = np.where(w == i, 1.0, 0.0)
    rest = LOG2E * slope * (i * blk + c)
    for term in range(N_POS_TERMS):
        piece = rest.astype(np.float32).astype(BF16).astype(np.float64)
        tab = tab + np.where(w == nb + term, piece, 0.0)
        rest = rest - piece
    return jnp.asarray(tab.astype(np.float32).astype(BF16))


def _moba_kernel(jt_ref, it_ref, qt_ref, k_ref, vt_ref, kmean_ref, gt_ref, x_ref, gate_ref, oat_ref, w_ref,
                 gf_ref, o_ref,
                 qaug_ref, m_ref, acc_ref, cmax_ref, wb_ref, obt_ref, *s_refs):
    @pl.when(jnp.logical_and(pl.program_id(0) == 0, pl.program_id(1) == 0))
    def _():
        for r in range(0, w_ref.shape[0], 256):
            wb_ref[r:r + 256, :] = w_ref[r:r + 256, :].astype(BF16)

    t = pl.program_id(1)
    j = jt_ref[t]
    i = it_ref[t]
    blk = MOBA_BLOCK
    dh = HEAD_DIM_MOBA
    nb = kmean_ref.shape[1]
    nbat = qt_ref.shape[0]
    first = i == j
    last = jnp.logical_or(i == j - 1, j == 0)

    def key_tile(bb, h):
        return k_ref[bb, :, 128 * h:128 * (h + 1)]

    def rows(h):
        return slice(dh * h, dh * (h + 1))

    ones_tail = jnp.ones((16, blk), BF16)

    def values_t(bb, h):
        return jnp.concatenate([vt_ref[bb, 0, rows(h), :], ones_tail], axis=0)

    @pl.when(first)
    def _():
        blk_i = lax.broadcasted_iota(jnp.int32, (nb, blk), 0)
        lane_m = lax.broadcasted_iota(jnp.int32, (nb, 2 * dh), 1)
        key_c = lax.broadcasted_iota(jnp.int32, (blk, blk), 0)
        qry_r = lax.broadcasted_iota(jnp.int32, (blk, blk), 1)
        causal = key_c <= qry_r
        ones_rows = jnp.where(blk_i < N_POS_TERMS, 1.0, 0.0)
        gates = {}
        for bb in range(nbat):
            for h in range(N_HEADS_MOBA):
                pr, hh = divmod(h, 2)
                qt_pair = qt_ref[bb, 128 * pr:128 * (pr + 1), :]
                in_head = (lane_m < dh) if hh == 0 else (lane_m >= dh)
                km = jnp.where(in_head, kmean_ref[bb, :, 128 * pr:128 * (pr + 1)], 0.0).astype(BF16)
                gates[bb, h] = jnp.dot(km, qt_pair, preferred_element_type=F32)
        for bb in range(nbat):
            for h in range(N_HEADS_MOBA):
                pr, hh = divmod(h, 2)
                gate = gates[bb, h]
                rank = jnp.zeros((nb, blk), jnp.int32)
                for i2 in range(nb):
                    g2 = gate[i2:i2 + 1, :]
                    beats = (g2 > gate) | ((g2 == gate) & (i2 < blk_i))
                    rank = rank + jnp.where(beats & (i2 < j), 1, 0)
                keep = ((blk_i < j) & (rank < MOBA_TOPK)) | (blk_i == j)
                sel_bias = jnp.where(keep, 0.0, MASK_BIAS)
                extra = jnp.concatenate([sel_bias, ones_rows, jnp.zeros((dh - 16, blk), F32)], axis=0)
                extra = extra.astype(BF16)
                qt_h = qt_ref[bb, dh * h:dh * (h + 1), :]
                qaug_ref[bb, h] = jnp.concatenate([qt_h, extra] if hh == 0 else [extra, qt_h], axis=0)

        for bb in range(nbat):
            for h in range(N_HEADS_MOBA):
                s = jnp.dot(key_tile(bb, h), qaug_ref[bb, h], preferred_element_type=F32)
                s = jnp.where(causal, s, MASK_BIAS)
                s_refs[bb][h] = s
                m_ref[bb, h:h + 1, :] = jnp.max(s, axis=0, keepdims=True)

        for bb in range(nbat):
            for h in range(N_HEADS_MOBA):
                p0 = jnp.exp2(s_refs[bb][h] - m_ref[bb, h:h + 1, :])
                acc_ref[bb, h] = jnp.dot(values_t(bb, h), p0.astype(BF16), preferred_element_type=F32)

    @pl.when(jnp.logical_not(first))
    def _():
        for bb in range(nbat):
            for h in range(N_HEADS_MOBA):
                s = jnp.dot(key_tile(bb, h), qaug_ref[bb, h], preferred_element_type=F32)
                s_refs[bb][h] = s
                cmax_ref[bb, h:h + 1, :] = jnp.max(s, axis=0, keepdims=True)
        for bb in range(nbat):
            for h in range(N_HEADS_MOBA):
                m_run = m_ref[bb, h:h + 1, :]
                m_new = jnp.maximum(m_run, cmax_ref[bb, h:h + 1, :])
                alpha = jnp.exp2(m_run - m_new)
                pr_ = jnp.exp2(s_refs[bb][h] - m_new)
                m_ref[bb, h:h + 1, :] = m_new
                acc_ref[bb, h] = alpha * acc_ref[bb, h] + jnp.dot(values_t(bb, h), pr_.astype(BF16),
                                                                  preferred_element_type=F32)

    @pl.when(last)
    def _():
        for bb in range(nbat):
            for h in range(N_HEADS_MOBA):
                acc = acc_ref[bb, h]
                out = acc[0:dh] / acc[dh:dh + 1] * gt_ref[bb, rows(h), :].astype(F32)
                obt_ref[bb, rows(h), :] = out.astype(BF16)
        for bb in range(nbat):
            y = (_dot_tn(oat_ref[bb], wb_ref[0:D_MLSTM, :])
                 + _dot_tn(obt_ref[bb], wb_ref[D_MLSTM:D_MLSTM + D_MOBA, :]))
            r = x_ref[bb] + gate_ref[bb] * y
            o_ref[bb] = r * lax.rsqrt(jnp.mean(r * r, axis=-1, keepdims=True) + EPS) * gf_ref[...]


def _moba_out(qbt, kaug, vbt, kmean, gbt, x, mod, out_at, w_out, g_final):
    b, dmb, s = qbt.shape
    d = x.shape[2]
    blk = MOBA_BLOCK
    nb = s // blk
    kw = kaug.shape[2]
    nbat = MOBA_BATCH
    assert nb == 8, "selection extras assume 8 key blocks"
    jt, it = _moba_steps(nb)
    qmap = lambda bi, t, jt_r, it_r: (bi, 0, jt_r[t])
    rmap = lambda bi, t, jt_r, it_r: (bi, jt_r[t], 0)
    full2 = lambda bi, t, jt_r, it_r: (0, 0)
    grid_spec = pltpu.PrefetchScalarGridSpec(
        num_scalar_prefetch=2,
        grid=(b // nbat, len(jt)),
        in_specs=[pl.BlockSpec((nbat, dmb, blk), qmap),
                  pl.BlockSpec((nbat, blk, kw), lambda bi, t, jt_r, it_r: (bi, it_r[t], 0)),
                  pl.BlockSpec((nbat, 1, dmb, blk), lambda bi, t, jt_r, it_r: (bi, it_r[t], 0, 0)),
                  pl.BlockSpec((nbat, nb, dmb), lambda bi, t, jt_r, it_r: (bi, 0, 0)),
                  pl.BlockSpec((nbat, dmb, blk), qmap),
                  pl.BlockSpec((nbat, blk, d), rmap),
                  pl.BlockSpec((nbat, 1, d), lambda bi, t, jt_r, it_r: (bi, 0, 2)),
                  pl.BlockSpec((nbat, D_MLSTM, blk), qmap),
                  pl.BlockSpec(w_out.shape, full2, pipeline_mode=pl.Buffered(1)),
                  pl.BlockSpec((1, d), full2)],
        out_specs=pl.BlockSpec((nbat, blk, d), rmap),
        scratch_shapes=[pltpu.VMEM((nbat, N_HEADS_MOBA, 2 * HEAD_DIM_MOBA, blk), BF16),
                        pltpu.VMEM((nbat, N_HEADS_MOBA, blk), F32),
                        pltpu.VMEM((nbat, N_HEADS_MOBA, HEAD_DIM_MOBA + 16, blk), F32),
                        pltpu.VMEM((nbat, N_HEADS_MOBA, blk), F32),
                        pltpu.VMEM(w_out.shape, BF16),
                        pltpu.VMEM((nbat, dmb, blk), BF16)]
        + [pltpu.VMEM((N_HEADS_MOBA, blk, blk), F32) for _ in range(nbat)],
    )
    return pl.pallas_call(
        _moba_kernel,
        grid_spec=grid_spec,
        out_shape=jax.ShapeDtypeStruct((b, s, d), x.dtype),
        compiler_params=pltpu.CompilerParams(dimension_semantics=("arbitrary", "arbitrary"),
                                             vmem_limit_bytes=VMEM_LIMIT),
        name="moba_out",
    )(jnp.asarray(jt, jnp.int32), jnp.asarray(it, jnp.int32), qbt, kaug, vbt, kmean, gbt,
      x, mod, out_at, w_out, g_final.reshape(1, d))


def _layer(x, c, w_ada, b_ada, g_norm, w_in, conv_w, conv_b, b_igate, b_fgate, g_mlstm_head, w_out,
           g_final):
    b = x.shape[0]
    mod = _ada(c, w_ada, b_ada).reshape(b, 1, 3 * D_MODEL)

    bif = jnp.concatenate([b_igate.astype(F32), jnp.zeros((4,), F32), b_fgate.astype(F32),
                           jnp.zeros((116,), F32)]).reshape(1, 128)

    qbt, kaug, kmean, vbt, gbt, out_at = _proj_mlstm(
        x, mod, g_norm, jnp.transpose(w_in), conv_w, conv_b, bif, g_mlstm_head)
    return _moba_out(qbt, kaug, vbt, kmean, gbt, x, mod, out_at, w_out, g_final)


def kernel(x, c, w_ada, b_ada, g_norm, w_in, conv_w, conv_b, b_igate, b_fgate, g_mlstm_head, w_out,
           g_final):
    assert w_ada.shape[0] == 1, "single-layer trunk"
    return _layer(x, c, w_ada[0], b_ada[0], g_norm[0], w_in[0], conv_w[0], conv_b[0], b_igate[0],
                  b_fgate[0], g_mlstm_head[0], w_out[0], g_final)
```

```python
import jax
import jax.numpy as jnp
import numpy as np
from jax import lax
from jax.experimental import pallas as pl
from jax.experimental.pallas import tpu as pltpu

F32 = jnp.float32
BF16 = jnp.bfloat16

D_MODEL = 1024
D_MLSTM = 512
N_HEADS_MLSTM = 4
HEAD_DIM_MLSTM = 128
D_MOBA = 512
N_HEADS_MOBA = 8
HEAD_DIM_MOBA = 64
CONV_K = 4
MOBA_BLOCK = 256
MOBA_TOPK = 3
EPS = 1e-6

ROW_TILE = 512
PROJ_CHUNK = 256
MLSTM_CHUNK = 128
MOBA_BATCH = 4
MASK_BIAS = -1e30
LOG2E = 1.4426950408889634
N_POS_TERMS = 3
VMEM_LIMIT = 56 * 1024 * 1024
_TILES_PER_SEQ = 2048 // ROW_TILE
_N_ROW_TILES = 8 * _TILES_PER_SEQ

_OFF = {}
_o = 0
for _name, _size in (("qm", 512), ("km", 512), ("vm", 512), ("om", 512), ("im", 4), ("fm", 4),
                     ("zm", 512), ("qb", 512), ("kb", 512), ("vb", 512), ("zb", 512)):
    _OFF[_name] = (_o, _o + _size)
    _o += _size
_NN_GROUPS = ("qm", "km", "kb")
_NT_GROUPS = ("qb", "vb", "zb", "vm", "om", "zm")


def _silu(v):
    return v * jax.nn.sigmoid(v)


def _dot_nt(a, b):
    return lax.dot_general(a, b, (((1,), (1,)), ((), ())), preferred_element_type=F32)


def _dot_tn(a, b):
    return lax.dot_general(a, b, (((0,), (0,)), ((), ())), preferred_element_type=F32)


def _ada_kernel(c_ref, w_ref, b_ref, o_ref):
    a = _silu(c_ref[...])
    w = w_ref[...]
    a_hi = a.astype(BF16)
    a_lo = (a - a_hi.astype(F32)).astype(BF16)
    w_hi = w.astype(BF16)
    w_lo = (w - w_hi.astype(F32)).astype(BF16)
    acc = jnp.dot(jnp.concatenate([a_hi, a_lo], axis=0), w_hi, preferred_element_type=F32)
    n = a.shape[0]
    o_ref[...] = (acc[0:n] + acc[n:2 * n] + jnp.dot(a_hi, w_lo, preferred_element_type=F32)) + b_ref[...]


def _ada(c, w_ada, b_ada):
    b, d = c.shape
    n = w_ada.shape[1]
    tn = 1024
    return pl.pallas_call(
        _ada_kernel,
        grid=(n // tn,),
        in_specs=[pl.BlockSpec((b, d), lambda i: (0, 0)),
                  pl.BlockSpec((d, tn), lambda i: (0, i)),
                  pl.BlockSpec((1, tn), lambda i: (0, i))],
        out_specs=pl.BlockSpec((b, tn), lambda i: (0, i)),
        out_shape=jax.ShapeDtypeStruct((b, n), F32),
        compiler_params=pltpu.CompilerParams(dimension_semantics=("arbitrary",),
                                             vmem_limit_bytes=VMEM_LIMIT),
        name="ada",
    )(c, w_ada, b_ada.reshape(1, n))


def _chunk_scan(v, combine, fill):
    pos = lax.broadcasted_iota(jnp.int32, v.shape, 1) & (MLSTM_CHUNK - 1)
    d = 1
    while d < MLSTM_CHUNK:
        shifted = pltpu.roll(v, d, axis=1)
        v = combine(v, jnp.where(pos >= d, shifted, fill))
        d *= 2
    return v


def _col_bcast(row, n):
    return jnp.transpose(jnp.broadcast_to(row, (n, n)))


def _interleave(major, minor):
    out, done = [], 0
    for idx, task in enumerate(major):
        out.append(task)
        want = (idx + 1) * len(minor) // len(major)
        out.extend(minor[done:want])
        done = want
    return out


def _proj_mlstm_kernel(x_ref, shift_ref, scale_ref, gn_ref, w_ref, cw_ref, cb_ref, bif_ref, kx_ref, gh_ref,
                       qbt_ref, kaug_ref, kmean_ref, vbt_ref, gbt_ref, oat_ref,
                       ext_ref, wnn_ref, wnt_ref, wif_ref,
                       qp_ref, kp_ref, vp_ref, gp_ref, tp_ref,
                       qm_ref, km_ref, vm_ref, gm_ref, tm_ref, st_ref, m_ref):
    t = pl.program_id(0)
    tm = x_ref.shape[1]
    ts = MOBA_BLOCK
    nsub = tm // ts
    cw = PROJ_CHUNK
    halo = CONV_K - 1
    L = MLSTM_CHUNK
    dh = HEAD_DIM_MLSTM
    npair = N_HEADS_MLSTM // 2
    nchunk = tm // L
    jp = jnp.minimum(t, _N_ROW_TILES - 1) % _TILES_PER_SEQ
    jm = jnp.maximum(t - 1, 0) % _TILES_PER_SEQ

    @pl.when(t == 0)
    def _():
        for g, name in enumerate(_NN_GROUPS):
            lo = _OFF[name][0]
            for c in range(0, 512, 256):
                wnn_ref[:, 512 * g + c:512 * g + c + 256] = jnp.transpose(
                    w_ref[lo + c:lo + c + 256, :]).astype(BF16)
        for g, name in enumerate(_NT_GROUPS):
            lo = _OFF[name][0]
            for c in range(0, 512, 256):
                wnt_ref[512 * g + c:512 * g + c + 256, :] = w_ref[lo + c:lo + c + 256, :].astype(BF16)
        blk = jnp.transpose(w_ref[_OFF["im"][0]:_OFF["im"][0] + 128, :])
        lane_w = lax.broadcasted_iota(jnp.int32, blk.shape, 1)
        f_cols = pltpu.roll(blk, 4, axis=1)
        wif_ref[...] = jnp.where(lane_w < 4, blk,
                                 jnp.where((lane_w >= 8) & (lane_w < 12), f_cols, 0.0)).astype(BF16)
        for ref in (qp_ref, kp_ref, vp_ref, gp_ref, tp_ref, st_ref, m_ref):
            ref[...] = jnp.zeros(ref.shape, ref.dtype)

    @pl.when(jp == 0)
    def _():
        ext_ref[0:8, :] = jnp.zeros((8, 2 * D_MLSTM), F32)

    @pl.when(jm == 0)
    def _():
        st_ref[...] = jnp.zeros(st_ref.shape, F32)
        m_ref[...] = jnp.zeros(m_ref.shape, F32)

    qm_ref[...] = qp_ref[...]
    km_ref[...] = kp_ref[...]
    vm_ref[...] = vp_ref[...]
    gm_ref[...] = gp_ref[...]
    tm_ref[...] = tp_ref[...]

    s_idx = lax.broadcasted_iota(jnp.int32, (L, 2 * L), 0)
    t_idx = lax.broadcasted_iota(jnp.int32, (L, 2 * L), 1) & (L - 1)
    causal = s_idx <= t_idx
    lane2 = lax.broadcasted_iota(jnp.int32, (1, 2 * L), 1)
    ones_tail = jnp.ones((16, L), BF16)
    zeros_blk = jnp.zeros((L, dh), BF16)

    def pair_row(v, p):
        return jnp.concatenate([v[2 * p:2 * p + 1, :], v[2 * p + 1:2 * p + 2, :]], axis=1)

    def pair_scalar(v, p):
        return jnp.where(lane2 < L, v[2 * p:2 * p + 1, :], v[2 * p + 1:2 * p + 2, :])

    def block_diag(a, b):
        return jnp.concatenate([jnp.concatenate([a, zeros_blk], axis=1),
                                jnp.concatenate([zeros_blk, b], axis=1)], axis=0)

    m_prev = m_ref[:, 0:1]
    rowv = []
    for c in range(nchunk):
        cs_ = slice(c * L, (c + 1) * L)
        cum_f = tm_ref[0:8, cs_]
        g = tm_ref[8:16, cs_]
        g_cmax = tm_ref[16:24, cs_]
        f_tot = cum_f[:, L - 1:L]
        g_max = g_cmax[:, L - 1:L]
        a_max = f_tot + g_max
        a_row = -jnp.maximum(m_prev, g_cmax)
        m_new = jnp.maximum(f_tot + m_prev, a_max)
        rowv.append(dict(
            ts=cs_, g=g, a_row=a_row,
            w=jnp.exp(g - g_max),
            inter=jnp.exp(a_row + m_prev),
            floor=jnp.exp(a_row - cum_f),
            s_old=jnp.exp(f_tot + m_prev - m_new), s_new=jnp.exp(a_max - m_new)))
        m_prev = m_new
    m_ref[...] = jnp.broadcast_to(m_prev, m_ref.shape)

    def q_diag(cs_, p):
        return block_diag(qm_ref[cs_, 2 * p * dh:(2 * p + 1) * dh],
                          qm_ref[cs_, (2 * p + 1) * dh:(2 * p + 2) * dh])

    s_kq, vts, carried, intra = {}, {}, {}, {}
    st = {p: st_ref[p] for p in range(npair)}
    mtasks = []

    def score_task(c, p):
        rv = rowv[c]
        k2 = km_ref[rv["ts"], 2 * p * dh:(2 * p + 2) * dh]
        g_col = jnp.concatenate([_col_bcast(rv["g"][2 * p:2 * p + 1, :], L),
                                 _col_bcast(rv["g"][2 * p + 1:2 * p + 2, :], L)], axis=1)
        decay = jnp.exp(jnp.where(causal, g_col + pair_row(rv["a_row"], p), -jnp.inf))
        s_kq[c, p] = (_dot_nt(k2, q_diag(rv["ts"], p)) * decay).astype(BF16)

    def mix_task(c, p):
        cs_ = rowv[c]["ts"]
        vt2 = jnp.concatenate(
            [jnp.concatenate([vm_ref[(2 * p + e) * dh:(2 * p + e + 1) * dh, cs_], ones_tail], axis=0)
             for e in range(2)], axis=1)
        vts[c, p] = vt2
        carried[c, p] = _dot_nt(st[p].astype(BF16), q_diag(cs_, p))
        intra[c, p] = jnp.dot(vt2, block_diag(s_kq[c, p][:, 0:L], s_kq[c, p][:, L:2 * L]),
                              preferred_element_type=F32)

    def finish_task(c, p):
        rv = rowv[c]
        cs_ = rv["ts"]
        mixed = intra[c, p] + pair_row(rv["inter"], p) * carried[c, p]
        den = mixed[dh:dh + 1, :]
        hh = mixed[0:dh] / jnp.maximum(jnp.abs(den), pair_row(rv["floor"], p))
        hh = hh * lax.rsqrt(jnp.mean(hh * hh, axis=0, keepdims=True) + EPS)
        for e in range(2):
            sl = slice((2 * p + e) * dh, (2 * p + e + 1) * dh)
            oat_ref[0, sl, cs_] = (hh[:, e * L:(e + 1) * L] * gh_ref[sl, :]
                                   * gm_ref[sl, cs_].astype(F32)).astype(BF16)

    def state_task(c, p):
        rv = rowv[c]
        cs_ = rv["ts"]
        vw = (vts[c, p].astype(F32) * pair_row(rv["w"], p)).astype(BF16)
        k_bd = block_diag(km_ref[cs_, 2 * p * dh:(2 * p + 1) * dh],
                          km_ref[cs_, (2 * p + 1) * dh:(2 * p + 2) * dh])
        st_loc = jnp.dot(vw, k_bd, preferred_element_type=F32)
        st[p] = pair_scalar(rv["s_old"], p) * st[p] + pair_scalar(rv["s_new"], p) * st_loc

    for c in range(nchunk):
        for p in range(npair):
            mtasks.append((score_task, c, p))
    for c in range(nchunk):
        for fn in (mix_task, finish_task, state_task):
            for p in range(npair):
                mtasks.append((fn, c, p))

    head_start = 2 * npair
    for fn, a0, a1 in mtasks[:head_start]:
        fn(a0, a1)
    mtasks = mtasks[head_start:]

    lane = lax.broadcasted_iota(jnp.int32, (ts, 128), 1)
    hbs = []
    for u in range(nsub):
        x = x_ref[0, u * ts:(u + 1) * ts, :]
        ms = jnp.mean(x * x, axis=-1, keepdims=True)
        h = x * lax.rsqrt(ms + EPS) * gn_ref[...] * (1.0 + scale_ref[0]) + shift_ref[0]
        hbs.append(h.astype(BF16))

    ptasks = []
    for u in range(nsub):
        hb = hbs[u]
        rs = slice(u * ts, (u + 1) * ts)

        def gates_task(_a, _b, hb=hb, rs=rs):
            ifc = jnp.dot(hb, wif_ref[...], preferred_element_type=F32) + bif_ref[...]
            ift = jnp.transpose(ifc)[0:16, :]
            log_i = ift[0:8]
            f_pre = ift[8:16]
            log_f = jnp.minimum(f_pre, 0.0) - jnp.log1p(jnp.exp(-jnp.abs(f_pre)))
            cum_f = _chunk_scan(log_f, jnp.add, 0.0)
            g = log_i - cum_f
            tp_ref[0:8, rs] = cum_f
            tp_ref[8:16, rs] = g
            tp_ref[16:24, rs] = _chunk_scan(g, jnp.maximum, -jnp.inf)

        def nn(lo, hi, hb=hb):
            return jnp.dot(hb, wnn_ref[:, lo:hi], preferred_element_type=F32)

        def nt(lo, hi, hb=hb):
            return _dot_nt(wnt_ref[lo:hi, :], hb)

        def conv_chunk(lo, _b, u=u, rs=rs, nn=nn):
            cs = slice(lo, lo + cw)
            base = 8 + u * ts
            qk = nn(lo, lo + cw)
            ext_ref[base:base + ts, cs] = qk
            conv = cb_ref[:, cs] + qk * cw_ref[CONV_K - 1:CONV_K, cs]
            for tap in range(halo):
                conv = conv + ext_ref[pl.ds(base - halo + tap, ts), cs] * cw_ref[tap:tap + 1, cs]
            if u == nsub - 1:
                ext_ref[0:8, cs] = qk[ts - 8:ts, :]
            act = _silu(conv)
            if lo < D_MLSTM:
                qp_ref[rs, lo:lo + cw] = act.astype(BF16)
            else:
                kp_ref[rs, lo - D_MLSTM:lo - D_MLSTM + cw] = (act * (HEAD_DIM_MLSTM ** -0.5)).astype(BF16)

        def value_chunk(lo, _b, rs=rs, nt=nt):
            vp_ref[lo:lo + cw, rs] = nt(1536 + lo, 1536 + lo + cw).astype(BF16)

        def gate_chunk(lo, _b, rs=rs, nt=nt):
            gp_ref[lo:lo + cw, rs] = (jax.nn.sigmoid(nt(2048 + lo, 2048 + lo + cw))
                                     * _silu(nt(2560 + lo, 2560 + lo + cw))).astype(BF16)

        def key_chunk(lo, _b, u=u, rs=rs, nn=nn):
            kb = nn(1024 + lo, 1024 + lo + cw)
            kmean_ref[0, pl.ds(jp * nsub + u, 1), lo:lo + cw] = jnp.mean(kb, axis=0, keepdims=True)
            kb2 = (kb * LOG2E).astype(BF16)
            for pr in range(lo // 128, (lo + cw) // 128):
                for hh in range(2):
                    hd = 2 * pr + hh
                    is_key = (lane < HEAD_DIM_MOBA) if hh == 0 else (lane >= HEAD_DIM_MOBA)
                    grp = slice(128 * hd, 128 * (hd + 1))
                    kaug_ref[0, rs, grp] = jnp.where(is_key, kb2[:, 128 * pr - lo:128 * (pr + 1) - lo],
                                                     kx_ref[u, :, grp])

        def qt_chunk(lo, _b, rs=rs, nt=nt):
            qbt_ref[0, lo:lo + cw, rs] = (nt(lo, lo + cw) * (HEAD_DIM_MOBA ** -0.5)).astype(BF16)

        def vt_chunk(lo, _b, u=u, nt=nt):
            vbt_ref[0, u, lo:lo + cw, :] = nt(512 + lo, 512 + lo + cw).astype(BF16)

        def gt_chunk(lo, _b, rs=rs, nt=nt):
            gbt_ref[0, lo:lo + cw, rs] = _silu(nt(1024 + lo, 1024 + lo + cw)).astype(BF16)

        heavy = ([(conv_chunk, lo, 0) for lo in range(0, 2 * D_MLSTM, cw)]
                 + [(gate_chunk, lo, 0) for lo in range(0, D_MLSTM, cw)]
                 + [(key_chunk, lo, 0) for lo in range(0, D_MOBA, cw)])
        light = ([(qt_chunk, lo, 0) for lo in range(0, D_MOBA, cw)]
                 + [(vt_chunk, lo, 0) for lo in range(0, D_MOBA, cw)]
                 + [(value_chunk, lo, 0) for lo in range(0, D_MLSTM, cw)]
                 + [(gt_chunk, lo, 0) for lo in range(0, D_MOBA, cw)])
        ptasks.append((gates_task, 0, 0))
        for idx in range(max(len(heavy), len(light))):
            for group in (heavy, light):
                if idx < len(group):
                    ptasks.append(group[idx])

    for fn, a0, a1 in _interleave(ptasks, mtasks):
        fn(a0, a1)

    for p in range(npair):
        st_ref[p] = st[p]


def _proj_mlstm(x, mod, g_norm, w_in, conv_w, conv_b, bif, g_head):
    b, s, d = x.shape
    tm = ROW_TILE
    nb = s // MOBA_BLOCK
    nsub = tm // MOBA_BLOCK
    assert tm % MOBA_BLOCK == 0 and (b * s) // tm == _N_ROW_TILES and s // tm == _TILES_PER_SEQ
    kx = _moba_key_extras(nb)
    kw = N_HEADS_MOBA * 128
    dm = D_MLSTM
    g_bc = jnp.broadcast_to(g_head.astype(F32).reshape(dm, 1), (dm, 128))
    tps = _TILES_PER_SEQ
    last = _N_ROW_TILES - 1
    pt = lambda t: jnp.minimum(t, last)
    mt = lambda t: jnp.maximum(t - 1, 0)
    full2 = lambda t: (0, 0)
    out_shapes = (
        jax.ShapeDtypeStruct((b, D_MOBA, s), BF16),
        jax.ShapeDtypeStruct((b, s, kw), BF16),
        jax.ShapeDtypeStruct((b, nb, D_MOBA), F32),
        jax.ShapeDtypeStruct((b, nb, D_MOBA, MOBA_BLOCK), BF16),
        jax.ShapeDtypeStruct((b, D_MOBA, s), BF16),
        jax.ShapeDtypeStruct((b, dm, s), BF16),
    )
    out_specs = (
        pl.BlockSpec((1, D_MOBA, tm), lambda t: (pt(t) // tps, 0, pt(t) % tps)),
        pl.BlockSpec((1, tm, kw), lambda t: (pt(t) // tps, pt(t) % tps, 0)),
        pl.BlockSpec((1, nb, D_MOBA), lambda t: (pt(t) // tps, 0, 0)),
        pl.BlockSpec((1, nsub, D_MOBA, MOBA_BLOCK), lambda t: (pt(t) // tps, pt(t) % tps, 0, 0)),
        pl.BlockSpec((1, D_MOBA, tm), lambda t: (pt(t) // tps, 0, pt(t) % tps)),
        pl.BlockSpec((1, dm, tm), lambda t: (mt(t) // tps, 0, mt(t) % tps)),
    )
    in_specs = [
        pl.BlockSpec((1, tm, d), lambda t: (pt(t) // tps, pt(t) % tps, 0)),
        pl.BlockSpec((1, 1, d), lambda t: (pt(t) // tps, 0, 0)),
        pl.BlockSpec((1, 1, d), lambda t: (pt(t) // tps, 0, 1)),
        pl.BlockSpec((1, d), full2),
        pl.BlockSpec(w_in.shape, full2, pipeline_mode=pl.Buffered(1)),
        pl.BlockSpec(conv_w.shape, full2),
        pl.BlockSpec((1, 2 * dm), full2),
        pl.BlockSpec((1, 128), full2),
        pl.BlockSpec((nsub, MOBA_BLOCK, kw), lambda t: (pt(t) % tps, 0, 0)),
        pl.BlockSpec((dm, 128), full2),
    ]
    operand_scratch = [pltpu.VMEM((tm, dm), BF16), pltpu.VMEM((tm, dm), BF16),
                       pltpu.VMEM((dm, tm), BF16), pltpu.VMEM((dm, tm), BF16),
                       pltpu.VMEM((24, tm), F32)]
    return pl.pallas_call(
        _proj_mlstm_kernel,
        grid=(_N_ROW_TILES + 1,),
        in_specs=in_specs,
        out_specs=out_specs,
        out_shape=out_shapes,
        scratch_shapes=[pltpu.VMEM((tm + 8, 2 * dm), F32),
                        pltpu.VMEM((d, 512 * len(_NN_GROUPS)), BF16),
                        pltpu.VMEM((512 * len(_NT_GROUPS), d), BF16),
                        pltpu.VMEM((d, 128), BF16)]
        + operand_scratch + operand_scratch
        + [pltpu.VMEM((N_HEADS_MLSTM // 2, HEAD_DIM_MLSTM + 16, 2 * HEAD_DIM_MLSTM), F32),
           pltpu.VMEM((8, 128), F32)],
        compiler_params=pltpu.CompilerParams(dimension_semantics=("arbitrary",),
                                             vmem_limit_bytes=VMEM_LIMIT),
        name="proj_mlstm",
    )(x, mod, mod, g_norm.reshape(1, d), w_in, conv_w, conv_b.reshape(1, -1), bif, kx, g_bc)


def _moba_steps(nb):
    jt, it = [], []
    for j in range(nb):
        for i in [j] + list(range(j)):
            jt.append(j)
            it.append(i)
    return jt, it


def _moba_key_extras(nb):
    blk = MOBA_BLOCK
    lane = np.arange(N_HEADS_MOBA * 128)
    head = lane // 128
    w = (lane % 128 - np.where(head % 2 == 0, HEAD_DIM_MOBA, 0))[None, None, :]
    slope = (2.0 ** -(head + 1.0))[None, None, :]
    i = np.arange(nb)[:, None, None]
    c = np.arange(blk)[None, :, None]
    tab = np.where(w == i, 1.0, 0.0)
    rest = LOG2E * slope * (i * blk + c)
    for term in range(N_POS_TERMS):
        piece = rest.astype(np.float32).astype(BF16).astype(np.float64)
        tab = tab + np.where(w == nb + term, piece, 0.0)
        rest = rest - piece
    return jnp.asarray(tab.astype(np.float32).astype(BF16))


def _moba_kernel(jt_ref, it_ref, qt_ref, k_ref, vt_ref, kmean_ref, gt_ref, x_ref, gate_ref, oat_ref, w_ref,
                 gf_ref, o_ref,
                 qaug_ref, m_ref, acc_ref, cmax_ref, wb_ref, obt_ref, *s_refs):
    @pl.when(jnp.logical_and(pl.program_id(0) == 0, pl.program_id(1) == 0))
    def _():
        for r in range(0, w_ref.shape[0], 256):
            wb_ref[r:r + 256, :] = w_ref[r:r + 256, :].astype(BF16)

    t = pl.program_id(1)
    j = jt_ref[t]
    i = it_ref[t]
    blk = MOBA_BLOCK
    dh = HEAD_DIM_MOBA
    nb = kmean_ref.shape[1]
    nbat = qt_ref.shape[0]
    first = i == j
    last = jnp.logical_or(i == j - 1, j == 0)

    def key_tile(bb, h):
        return k_ref[bb, :, 128 * h:128 * (h + 1)]

    def rows(h):
        return slice(dh * h, dh * (h + 1))

    ones_tail = jnp.ones((16, blk), BF16)

    def values_t(bb, h):
        return jnp.concatenate([vt_ref[bb, 0, rows(h), :], ones_tail], axis=0)

    @pl.when(first)
    def _():
        blk_i = lax.broadcasted_iota(jnp.int32, (nb, blk), 0)
        lane_m = lax.broadcasted_iota(jnp.int32, (nb, 2 * dh), 1)
        key_c = lax.broadcasted_iota(jnp.int32, (blk, blk), 0)
        qry_r = lax.broadcasted_iota(jnp.int32, (blk, blk), 1)
        causal = key_c <= qry_r
        ones_rows = jnp.where(blk_i < N_POS_TERMS, 1.0, 0.0)
        gates = {}
        for bb in range(nbat):
            for h in range(N_HEADS_MOBA):
                pr, hh = divmod(h, 2)
                qt_pair = qt_ref[bb, 128 * pr:128 * (pr + 1), :]
                in_head = (lane_m < dh) if hh == 0 else (lane_m >= dh)
                km = jnp.where(in_head, kmean_ref[bb, :, 128 * pr:128 * (pr + 1)], 0.0).astype(BF16)
                gates[bb, h] = jnp.dot(km, qt_pair, preferred_element_type=F32)
        for bb in range(nbat):
            for h in range(N_HEADS_MOBA):
                pr, hh = divmod(h, 2)
                gate = gates[bb, h]
                rank = jnp.zeros((nb, blk), jnp.int32)
                for i2 in range(nb):
                    g2 = gate[i2:i2 + 1, :]
                    beats = (g2 > gate) | ((g2 == gate) & (i2 < blk_i))
                    rank = rank + jnp.where(beats & (i2 < j), 1, 0)
                keep = ((blk_i < j) & (rank < MOBA_TOPK)) | (blk_i == j)
                sel_bias = jnp.where(keep, 0.0, MASK_BIAS)
                extra = jnp.concatenate([sel_bias, ones_rows, jnp.zeros((dh - 16, blk), F32)], axis=0)
                extra = extra.astype(BF16)
                qt_h = qt_ref[bb, dh * h:dh * (h + 1), :]
                qaug_ref[bb, h] = jnp.concatenate([qt_h, extra] if hh == 0 else [extra, qt_h], axis=0)

        for bb in range(nbat):
            for h in range(N_HEADS_MOBA):
                s = jnp.dot(key_tile(bb, h), qaug_ref[bb, h], preferred_element_type=F32)
                s = jnp.where(causal, s, MASK_BIAS)
                s_refs[bb][h] = s
                m_ref[bb, h:h + 1, :] = jnp.max(s, axis=0, keepdims=True)

        for bb in range(nbat):
            for h in range(N_HEADS_MOBA):
                p0 = jnp.exp2(s_refs[bb][h] - m_ref[bb, h:h + 1, :])
                acc_ref[bb, h] = jnp.dot(values_t(bb, h), p0.astype(BF16), preferred_element_type=F32)

    @pl.when(jnp.logical_not(first))
    def _():
        def score_phase(bb):
            for h in range(N_HEADS_MOBA):
                s = jnp.dot(key_tile(bb, h), qaug_ref[bb, h], preferred_element_type=F32)
                s_refs[bb][h] = s
                cmax_ref[bb, h:h + 1, :] = jnp.max(s, axis=0, keepdims=True)

        def update_phase(bb):
            for h in range(N_HEADS_MOBA):
                m_run = m_ref[bb, h:h + 1, :]
                m_new = jnp.maximum(m_run, cmax_ref[bb, h:h + 1, :])
                alpha = jnp.exp2(m_run - m_new)
                pr_ = jnp.exp2(s_refs[bb][h] - m_new)
                m_ref[bb, h:h + 1, :] = m_new
                acc_ref[bb, h] = alpha * acc_ref[bb, h] + jnp.dot(values_t(bb, h), pr_.astype(BF16),
                                                                  preferred_element_type=F32)

        for step in range(nbat + 1):
            if step < nbat:
                score_phase(step)
            if step >= 1:
                update_phase(step - 1)

    @pl.when(last)
    def _():
        for bb in range(nbat):
            for h in range(N_HEADS_MOBA):
                acc = acc_ref[bb, h]
                out = acc[0:dh] / acc[dh:dh + 1] * gt_ref[bb, rows(h), :].astype(F32)
                obt_ref[bb, rows(h), :] = out.astype(BF16)
        for bb in range(nbat):
            y = (_dot_tn(oat_ref[bb], wb_ref[0:D_MLSTM, :])
                 + _dot_tn(obt_ref[bb], wb_ref[D_MLSTM:D_MLSTM + D_MOBA, :]))
            r = x_ref[bb] + gate_ref[bb] * y
            o_ref[bb] = r * lax.rsqrt(jnp.mean(r * r, axis=-1, keepdims=True) + EPS) * gf_ref[...]


def _moba_out(qbt, kaug, vbt, kmean, gbt, x, mod, out_at, w_out, g_final):
    b, dmb, s = qbt.shape
    d = x.shape[2]
    blk = MOBA_BLOCK
    nb = s // blk
    kw = kaug.shape[2]
    nbat = MOBA_BATCH
    assert nb == 8, "selection extras assume 8 key blocks"
    jt, it = _moba_steps(nb)
    qmap = lambda bi, t, jt_r, it_r: (bi, 0, jt_r[t])
    rmap = lambda bi, t, jt_r, it_r: (bi, jt_r[t], 0)
    full2 = lambda bi, t, jt_r, it_r: (0, 0)
    grid_spec = pltpu.PrefetchScalarGridSpec(
        num_scalar_prefetch=2,
        grid=(b // nbat, len(jt)),
        in_specs=[pl.BlockSpec((nbat, dmb, blk), qmap),
                  pl.BlockSpec((nbat, blk, kw), lambda bi, t, jt_r, it_r: (bi, it_r[t], 0)),
                  pl.BlockSpec((nbat, 1, dmb, blk), lambda bi, t, jt_r, it_r: (bi, it_r[t], 0, 0)),
                  pl.BlockSpec((nbat, nb, dmb), lambda bi, t, jt_r, it_r: (bi, 0, 0)),
                  pl.BlockSpec((nbat, dmb, blk), qmap),
                  pl.BlockSpec((nbat, blk, d), rmap),
                  pl.BlockSpec((nbat, 1, d), lambda bi, t, jt_r, it_r: (bi, 0, 2)),
                  pl.BlockSpec((nbat, D_MLSTM, blk), qmap),
                  pl.BlockSpec(w_out.shape, full2, pipeline_mode=pl.Buffered(1)),
                  pl.BlockSpec((1, d), full2)],
        out_specs=pl.BlockSpec((nbat, blk, d), rmap),
        scratch_shapes=[pltpu.VMEM((nbat, N_HEADS_MOBA, 2 * HEAD_DIM_MOBA, blk), BF16),
                        pltpu.VMEM((nbat, N_HEADS_MOBA, blk), F32),
                        pltpu.VMEM((nbat, N_HEADS_MOBA, HEAD_DIM_MOBA + 16, blk), F32),
                        pltpu.VMEM((nbat, N_HEADS_MOBA, blk), F32),
                        pltpu.VMEM(w_out.shape, BF16),
                        pltpu.VMEM((nbat, dmb, blk), BF16)]
        + [pltpu.VMEM((N_HEADS_MOBA, blk, blk), F32) for _ in range(nbat)],
    )
    return pl.pallas_call(
        _moba_kernel,
        grid_spec=grid_spec,
        out_shape=jax.ShapeDtypeStruct((b, s, d), x.dtype),
        compiler_params=pltpu.CompilerParams(dimension_semantics=("arbitrary", "arbitrary"),
                                             vmem_limit_bytes=VMEM_LIMIT),
        name="moba_out",
    )(jnp.asarray(jt, jnp.int32), jnp.asarray(it, jnp.int32), qbt, kaug, vbt, kmean, gbt,
      x, mod, out_at, w_out, g_final.reshape(1, d))


def _layer(x, c, w_ada, b_ada, g_norm, w_in, conv_w, conv_b, b_igate, b_fgate, g_mlstm_head, w_out,
           g_final):
    b = x.shape[0]
    mod = _ada(c, w_ada, b_ada).reshape(b, 1, 3 * D_MODEL)

    bif = jnp.concatenate([b_igate.astype(F32), jnp.zeros((4,), F32), b_fgate.astype(F32),
                           jnp.zeros((116,), F32)]).reshape(1, 128)

    qbt, kaug, kmean, vbt, gbt, out_at = _proj_mlstm(
        x, mod, g_norm, jnp.transpose(w_in), conv_w, conv_b, bif, g_mlstm_head)
    return _moba_out(qbt, kaug, vbt, kmean, gbt, x, mod, out_at, w_out, g_final)


def kernel(x, c, w_ada, b_ada, g_norm, w_in, conv_w, conv_b, b_igate, b_fgate, g_mlstm_head, w_out,
           g_final):
    assert w_ada.shape[0] == 1, "single-layer trunk"
    return _layer(x, c, w_ada[0], b_ada[0], g_norm[0], w_in[0], conv_w[0], conv_b[0], b_igate[0],
                  b_fgate[0], g_mlstm_head[0], w_out[0], g_final)
```

```python
import jax
import jax.numpy as jnp
import numpy as np
from jax import lax
from jax.experimental import pallas as pl
from jax.experimental.pallas import tpu as pltpu

F32 = jnp.float32
BF16 = jnp.bfloat16

D_MODEL = 1024
D_MLSTM = 512
N_HEADS_MLSTM = 4
HEAD_DIM_MLSTM = 128
D_MOBA = 512
N_HEADS_MOBA = 8
HEAD_DIM_MOBA = 64
CONV_K = 4
MOBA_BLOCK = 256
MOBA_TOPK = 3
EPS = 1e-6

ROW_TILE = 512
PROJ_CHUNK = 256
MLSTM_CHUNK = 128
MOBA_BATCH = 4
MASK_BIAS = -1e30
LOG2E = 1.4426950408889634
N_POS_TERMS = 3
VMEM_LIMIT = 56 * 1024 * 1024
_TILES_PER_SEQ = 2048 // ROW_TILE
_N_ROW_TILES = 8 * _TILES_PER_SEQ

_OFF = {}
_o = 0
for _name, _size in (("qm", 512), ("km", 512), ("vm", 512), ("om", 512), ("im", 4), ("fm", 4),
                     ("zm", 512), ("qb", 512), ("kb", 512), ("vb", 512), ("zb", 512)):
    _OFF[_name] = (_o, _o + _size)
    _o += _size
_NN_GROUPS = ("qm", "km", "kb")
_NT_GROUPS = ("qb", "vb", "zb", "vm", "om", "zm")
_GATE_ROWS = 16


def _silu(v):
    return v * jax.nn.sigmoid(v)


def _dot_nt(a, b):
    return lax.dot_general(a, b, (((1,), (1,)), ((), ())), preferred_element_type=F32)


def _dot_tn(a, b):
    return lax.dot_general(a, b, (((0,), (0,)), ((), ())), preferred_element_type=F32)


def _ada_kernel(c_ref, w_ref, b_ref, o_ref):
    a = _silu(c_ref[...])
    w = w_ref[...]
    a_hi = a.astype(BF16)
    a_lo = (a - a_hi.astype(F32)).astype(BF16)
    w_hi = w.astype(BF16)
    w_lo = (w - w_hi.astype(F32)).astype(BF16)
    acc = jnp.dot(jnp.concatenate([a_hi, a_lo], axis=0), w_hi, preferred_element_type=F32)
    n = a.shape[0]
    o_ref[...] = (acc[0:n] + acc[n:2 * n] + jnp.dot(a_hi, w_lo, preferred_element_type=F32)) + b_ref[...]


def _ada(c, w_ada, b_ada):
    b, d = c.shape
    n = w_ada.shape[1]
    tn = 1024
    return pl.pallas_call(
        _ada_kernel,
        grid=(n // tn,),
        in_specs=[pl.BlockSpec((b, d), lambda i: (0, 0)),
                  pl.BlockSpec((d, tn), lambda i: (0, i)),
                  pl.BlockSpec((1, tn), lambda i: (0, i))],
        out_specs=pl.BlockSpec((b, tn), lambda i: (0, i)),
        out_shape=jax.ShapeDtypeStruct((b, n), F32),
        compiler_params=pltpu.CompilerParams(dimension_semantics=("arbitrary",),
                                             vmem_limit_bytes=VMEM_LIMIT),
        name="ada",
    )(c, w_ada, b_ada.reshape(1, n))


def _chunk_scan(v, combine, fill):
    pos = lax.broadcasted_iota(jnp.int32, v.shape, 1) & (MLSTM_CHUNK - 1)
    d = 1
    while d < MLSTM_CHUNK:
        shifted = pltpu.roll(v, d, axis=1)
        v = combine(v, jnp.where(pos >= d, shifted, fill))
        d *= 2
    return v


def _col_bcast(row, n):
    return jnp.transpose(jnp.broadcast_to(row, (n, n)))


def _interleave(major, minor):
    out, done = [], 0
    for idx, task in enumerate(major):
        out.append(task)
        want = (idx + 1) * len(minor) // len(major)
        out.extend(minor[done:want])
        done = want
    return out


def _proj_mlstm_kernel(x_ref, shift_ref, scale_ref, gn_ref, w_ref, cw_ref, cb_ref, bif_ref, kx_ref, gh_ref,
                       qbt_ref, kaug_ref, kmean_ref, vbt_ref, gbt_ref, oat_ref,
                       ext_ref, wnn_ref, wnt_ref,
                       qp_ref, kp_ref, vp_ref, gp_ref, tp_ref,
                       qm_ref, km_ref, vm_ref, gm_ref, tm_ref, st_ref, m_ref):
    t = pl.program_id(0)
    tm = x_ref.shape[1]
    ts = MOBA_BLOCK
    nsub = tm // ts
    cw = PROJ_CHUNK
    halo = CONV_K - 1
    L = MLSTM_CHUNK
    dh = HEAD_DIM_MLSTM
    npair = N_HEADS_MLSTM // 2
    nchunk = tm // L
    jp = jnp.minimum(t, _N_ROW_TILES - 1) % _TILES_PER_SEQ
    jm = jnp.maximum(t - 1, 0) % _TILES_PER_SEQ

    @pl.when(t == 0)
    def _():
        for g, name in enumerate(_NN_GROUPS):
            lo = _OFF[name][0]
            for c in range(0, 512, 256):
                wnn_ref[:, 512 * g + c:512 * g + c + 256] = jnp.transpose(
                    w_ref[lo + c:lo + c + 256, :]).astype(BF16)
        for g, name in enumerate(_NT_GROUPS):
            lo = _OFF[name][0]
            for c in range(0, 512, 256):
                r0 = _GATE_ROWS + 512 * g + c
                wnt_ref[r0:r0 + 256, :] = w_ref[lo + c:lo + c + 256, :].astype(BF16)
        gi = _OFF["im"][0]
        row_w = lax.broadcasted_iota(jnp.int32, (_GATE_ROWS, w_ref.shape[1]), 0)
        wnt_ref[0:_GATE_ROWS, :] = jnp.where(
            row_w < 4, w_ref[gi:gi + _GATE_ROWS, :],
            jnp.where((row_w >= 8) & (row_w < 12), w_ref[gi - 4:gi - 4 + _GATE_ROWS, :], 0.0)).astype(BF16)
        for ref in (qp_ref, kp_ref, vp_ref, gp_ref, tp_ref, st_ref, m_ref):
            ref[...] = jnp.zeros(ref.shape, ref.dtype)

    @pl.when(jp == 0)
    def _():
        ext_ref[0:8, :] = jnp.zeros((8, 2 * D_MLSTM), F32)

    @pl.when(jm == 0)
    def _():
        st_ref[...] = jnp.zeros(st_ref.shape, F32)
        m_ref[...] = jnp.zeros(m_ref.shape, F32)

    qm_ref[...] = qp_ref[...]
    km_ref[...] = kp_ref[...]
    vm_ref[...] = vp_ref[...]
    gm_ref[...] = gp_ref[...]
    tm_ref[...] = tp_ref[...]

    s_idx = lax.broadcasted_iota(jnp.int32, (L, 2 * L), 0)
    t_idx = lax.broadcasted_iota(jnp.int32, (L, 2 * L), 1) & (L - 1)
    causal = s_idx <= t_idx
    lane2 = lax.broadcasted_iota(jnp.int32, (1, 2 * L), 1)
    ones_tail = jnp.ones((16, L), BF16)
    zeros_blk = jnp.zeros((L, dh), BF16)

    def pair_row(v, p):
        return jnp.concatenate([v[2 * p:2 * p + 1, :], v[2 * p + 1:2 * p + 2, :]], axis=1)

    def pair_scalar(v, p):
        return jnp.where(lane2 < L, v[2 * p:2 * p + 1, :], v[2 * p + 1:2 * p + 2, :])

    def block_diag(a, b):
        return jnp.concatenate([jnp.concatenate([a, zeros_blk], axis=1),
                                jnp.concatenate([zeros_blk, b], axis=1)], axis=0)

    m_prev = m_ref[:, 0:1]
    rowv = []
    for c in range(nchunk):
        cs_ = slice(c * L, (c + 1) * L)
        cum_f = tm_ref[0:8, cs_]
        g = tm_ref[8:16, cs_]
        g_cmax = tm_ref[16:24, cs_]
        f_tot = cum_f[:, L - 1:L]
        g_max = g_cmax[:, L - 1:L]
        a_max = f_tot + g_max
        a_row = -jnp.maximum(m_prev, g_cmax)
        m_new = jnp.maximum(f_tot + m_prev, a_max)
        rowv.append(dict(
            ts=cs_, g=g, a_row=a_row,
            w=jnp.exp(g - g_max),
            inter=jnp.exp(a_row + m_prev),
            floor=jnp.exp(a_row - cum_f),
            s_old=jnp.exp(f_tot + m_prev - m_new), s_new=jnp.exp(a_max - m_new)))
        m_prev = m_new
    m_ref[...] = jnp.broadcast_to(m_prev, m_ref.shape)

    def q_diag(cs_, p):
        return block_diag(qm_ref[cs_, 2 * p * dh:(2 * p + 1) * dh],
                          qm_ref[cs_, (2 * p + 1) * dh:(2 * p + 2) * dh])

    s_kq, vts, carried, intra = {}, {}, {}, {}
    st = {p: st_ref[p] for p in range(npair)}
    mtasks = []

    def score_task(c, p):
        rv = rowv[c]
        k2 = km_ref[rv["ts"], 2 * p * dh:(2 * p + 2) * dh]
        g_col = jnp.concatenate([_col_bcast(rv["g"][2 * p:2 * p + 1, :], L),
                                 _col_bcast(rv["g"][2 * p + 1:2 * p + 2, :], L)], axis=1)
        decay = jnp.exp(jnp.where(causal, g_col + pair_row(rv["a_row"], p), -jnp.inf))
        s_kq[c, p] = (_dot_nt(k2, q_diag(rv["ts"], p)) * decay).astype(BF16)

    def mix_task(c, p):
        cs_ = rowv[c]["ts"]
        vt2 = jnp.concatenate(
            [jnp.concatenate([vm_ref[(2 * p + e) * dh:(2 * p + e + 1) * dh, cs_], ones_tail], axis=0)
             for e in range(2)], axis=1)
        vts[c, p] = vt2
        carried[c, p] = _dot_nt(st[p].astype(BF16), q_diag(cs_, p))
        intra[c, p] = jnp.dot(vt2, block_diag(s_kq[c, p][:, 0:L], s_kq[c, p][:, L:2 * L]),
                              preferred_element_type=F32)

    def finish_task(c, p):
        rv = rowv[c]
        cs_ = rv["ts"]
        mixed = intra[c, p] + pair_row(rv["inter"], p) * carried[c, p]
        den = mixed[dh:dh + 1, :]
        hh = mixed[0:dh] / jnp.maximum(jnp.abs(den), pair_row(rv["floor"], p))
        hh = hh * lax.rsqrt(jnp.mean(hh * hh, axis=0, keepdims=True) + EPS)
        for e in range(2):
            sl = slice((2 * p + e) * dh, (2 * p + e + 1) * dh)
            oat_ref[0, sl, cs_] = (hh[:, e * L:(e + 1) * L] * gh_ref[sl, :]
                                   * gm_ref[sl, cs_].astype(F32)).astype(BF16)

    def state_task(c, p):
        rv = rowv[c]
        cs_ = rv["ts"]
        vw = (vts[c, p].astype(F32) * pair_row(rv["w"], p)).astype(BF16)
        k_bd = block_diag(km_ref[cs_, 2 * p * dh:(2 * p + 1) * dh],
                          km_ref[cs_, (2 * p + 1) * dh:(2 * p + 2) * dh])
        st_loc = jnp.dot(vw, k_bd, preferred_element_type=F32)
        st[p] = pair_scalar(rv["s_old"], p) * st[p] + pair_scalar(rv["s_new"], p) * st_loc

    for c in range(nchunk):
        for p in range(npair):
            mtasks.append((score_task, c, p))
    for c in range(nchunk):
        for fn in (mix_task, finish_task, state_task):
            for p in range(npair):
                mtasks.append((fn, c, p))

    head_start = 2 * npair
    for fn, a0, a1 in mtasks[:head_start]:
        fn(a0, a1)
    mtasks = mtasks[head_start:]

    lane = lax.broadcasted_iota(jnp.int32, (ts, 128), 1)
    hbs = []
    for u in range(nsub):
        x = x_ref[0, u * ts:(u + 1) * ts, :]
        ms = jnp.mean(x * x, axis=-1, keepdims=True)
        h = x * lax.rsqrt(ms + EPS) * gn_ref[...] * (1.0 + scale_ref[0]) + shift_ref[0]
        hbs.append(h.astype(BF16))

    ptasks = []
    for u in range(nsub):
        hb = hbs[u]
        rs = slice(u * ts, (u + 1) * ts)

        def scan_gates(ift, rs=rs):
            ift = ift + bif_ref[...]
            log_i = ift[0:8]
            f_pre = ift[8:16]
            log_f = jnp.minimum(f_pre, 0.0) - jnp.log1p(jnp.exp(-jnp.abs(f_pre)))
            cum_f = _chunk_scan(log_f, jnp.add, 0.0)
            g = log_i - cum_f
            tp_ref[0:8, rs] = cum_f
            tp_ref[8:16, rs] = g
            tp_ref[16:24, rs] = _chunk_scan(g, jnp.maximum, -jnp.inf)

        def nn(lo, hi, hb=hb):
            return jnp.dot(hb, wnn_ref[:, lo:hi], preferred_element_type=F32)

        def nt(lo, hi, hb=hb):
            return _dot_nt(wnt_ref[_GATE_ROWS + lo:_GATE_ROWS + hi, :], hb)

        def conv_chunk(lo, _b, u=u, rs=rs, nn=nn):
            cs = slice(lo, lo + cw)
            base = 8 + u * ts
            qk = nn(lo, lo + cw)
            ext_ref[base:base + ts, cs] = qk
            conv = cb_ref[:, cs] + qk * cw_ref[CONV_K - 1:CONV_K, cs]
            for tap in range(halo):
                conv = conv + ext_ref[pl.ds(base - halo + tap, ts), cs] * cw_ref[tap:tap + 1, cs]
            if u == nsub - 1:
                ext_ref[0:8, cs] = qk[ts - 8:ts, :]
            act = _silu(conv)
            if lo < D_MLSTM:
                qp_ref[rs, lo:lo + cw] = act.astype(BF16)
            else:
                kp_ref[rs, lo - D_MLSTM:lo - D_MLSTM + cw] = (act * (HEAD_DIM_MLSTM ** -0.5)).astype(BF16)

        def value_chunk(lo, _b, rs=rs, nt=nt):
            vp_ref[lo:lo + cw, rs] = nt(1536 + lo, 1536 + lo + cw).astype(BF16)

        def gate_chunk(lo, _b, rs=rs, nt=nt):
            gp_ref[lo:lo + cw, rs] = (jax.nn.sigmoid(nt(2048 + lo, 2048 + lo + cw))
                                     * _silu(nt(2560 + lo, 2560 + lo + cw))).astype(BF16)

        def key_chunk(lo, _b, u=u, rs=rs, nn=nn):
            kb = nn(1024 + lo, 1024 + lo + cw)
            kmean_ref[0, pl.ds(jp * nsub + u, 1), lo:lo + cw] = jnp.mean(kb, axis=0, keepdims=True)
            kb2 = (kb * LOG2E).astype(BF16)
            for pr in range(lo // 128, (lo + cw) // 128):
                for hh in range(2):
                    hd = 2 * pr + hh
                    is_key = (lane < HEAD_DIM_MOBA) if hh == 0 else (lane >= HEAD_DIM_MOBA)
                    grp = slice(128 * hd, 128 * (hd + 1))
                    kaug_ref[0, rs, grp] = jnp.where(is_key, kb2[:, 128 * pr - lo:128 * (pr + 1) - lo],
                                                     kx_ref[u, :, grp])

        def qt_chunk(lo, _b, rs=rs, nt=nt, scan_gates=scan_gates):
            if lo == 0:
                both = nt(-_GATE_ROWS, cw)
                scan_gates(both[0:_GATE_ROWS])
                qt = both[_GATE_ROWS:_GATE_ROWS + cw]
            else:
                qt = nt(lo, lo + cw)
            qbt_ref[0, lo:lo + cw, rs] = (qt * (HEAD_DIM_MOBA ** -0.5)).astype(BF16)

        def vt_chunk(lo, _b, u=u, nt=nt):
            vbt_ref[0, u, lo:lo + cw, :] = nt(512 + lo, 512 + lo + cw).astype(BF16)

        def gt_chunk(lo, _b, rs=rs, nt=nt):
            gbt_ref[0, lo:lo + cw, rs] = _silu(nt(1024 + lo, 1024 + lo + cw)).astype(BF16)

        heavy = ([(conv_chunk, lo, 0) for lo in range(0, 2 * D_MLSTM, cw)]
                 + [(gate_chunk, lo, 0) for lo in range(0, D_MLSTM, cw)]
                 + [(key_chunk, lo, 0) for lo in range(0, D_MOBA, cw)])
        light = ([(qt_chunk, lo, 0) for lo in range(0, D_MOBA, cw)]
                 + [(vt_chunk, lo, 0) for lo in range(0, D_MOBA, cw)]
                 + [(value_chunk, lo, 0) for lo in range(0, D_MLSTM, cw)]
                 + [(gt_chunk, lo, 0) for lo in range(0, D_MOBA, cw)])
        for idx in range(max(len(heavy), len(light))):
            for group in (light, heavy):
                if idx < len(group):
                    ptasks.append(group[idx])

    for fn, a0, a1 in _interleave(ptasks, mtasks):
        fn(a0, a1)

    for p in range(npair):
        st_ref[p] = st[p]


def _proj_mlstm(x, mod, g_norm, w_in, conv_w, conv_b, bif, g_head):
    b, s, d = x.shape
    tm = ROW_TILE
    nb = s // MOBA_BLOCK
    nsub = tm // MOBA_BLOCK
    assert tm % MOBA_BLOCK == 0 and (b * s) // tm == _N_ROW_TILES and s // tm == _TILES_PER_SEQ
    kx = _moba_key_extras(nb)
    kw = N_HEADS_MOBA * 128
    dm = D_MLSTM
    g_bc = jnp.broadcast_to(g_head.astype(F32).reshape(dm, 1), (dm, 128))
    tps = _TILES_PER_SEQ
    last = _N_ROW_TILES - 1
    pt = lambda t: jnp.minimum(t, last)
    mt = lambda t: jnp.maximum(t - 1, 0)
    full2 = lambda t: (0, 0)
    out_shapes = (
        jax.ShapeDtypeStruct((b, D_MOBA, s), BF16),
        jax.ShapeDtypeStruct((b, s, kw), BF16),
        jax.ShapeDtypeStruct((b, nb, D_MOBA), F32),
        jax.ShapeDtypeStruct((b, nb, D_MOBA, MOBA_BLOCK), BF16),
        jax.ShapeDtypeStruct((b, D_MOBA, s), BF16),
        jax.ShapeDtypeStruct((b, dm, s), BF16),
    )
    out_specs = (
        pl.BlockSpec((1, D_MOBA, tm), lambda t: (pt(t) // tps, 0, pt(t) % tps)),
        pl.BlockSpec((1, tm, kw), lambda t: (pt(t) // tps, pt(t) % tps, 0)),
        pl.BlockSpec((1, nb, D_MOBA), lambda t: (pt(t) // tps, 0, 0)),
        pl.BlockSpec((1, nsub, D_MOBA, MOBA_BLOCK), lambda t: (pt(t) // tps, pt(t) % tps, 0, 0)),
        pl.BlockSpec((1, D_MOBA, tm), lambda t: (pt(t) // tps, 0, pt(t) % tps)),
        pl.BlockSpec((1, dm, tm), lambda t: (mt(t) // tps, 0, mt(t) % tps)),
    )
    in_specs = [
        pl.BlockSpec((1, tm, d), lambda t: (pt(t) // tps, pt(t) % tps, 0)),
        pl.BlockSpec((1, 1, d), lambda t: (pt(t) // tps, 0, 0)),
        pl.BlockSpec((1, 1, d), lambda t: (pt(t) // tps, 0, 1)),
        pl.BlockSpec((1, d), full2),
        pl.BlockSpec(w_in.shape, full2, pipeline_mode=pl.Buffered(1)),
        pl.BlockSpec(conv_w.shape, full2),
        pl.BlockSpec((1, 2 * dm), full2),
        pl.BlockSpec((_GATE_ROWS, 1), full2),
        pl.BlockSpec((nsub, MOBA_BLOCK, kw), lambda t: (pt(t) % tps, 0, 0)),
        pl.BlockSpec((dm, 128), full2),
    ]
    operand_scratch = [pltpu.VMEM((tm, dm), BF16), pltpu.VMEM((tm, dm), BF16),
                       pltpu.VMEM((dm, tm), BF16), pltpu.VMEM((dm, tm), BF16),
                       pltpu.VMEM((24, tm), F32)]
    return pl.pallas_call(
        _proj_mlstm_kernel,
        grid=(_N_ROW_TILES + 1,),
        in_specs=in_specs,
        out_specs=out_specs,
        out_shape=out_shapes,
        scratch_shapes=[pltpu.VMEM((tm + 8, 2 * dm), F32),
                        pltpu.VMEM((d, 512 * len(_NN_GROUPS)), BF16),
                        pltpu.VMEM((_GATE_ROWS + 512 * len(_NT_GROUPS), d), BF16)]
        + operand_scratch + operand_scratch
        + [pltpu.VMEM((N_HEADS_MLSTM // 2, HEAD_DIM_MLSTM + 16, 2 * HEAD_DIM_MLSTM), F32),
           pltpu.VMEM((8, 128), F32)],
        compiler_params=pltpu.CompilerParams(dimension_semantics=("arbitrary",),
                                             vmem_limit_bytes=VMEM_LIMIT),
        name="proj_mlstm",
    )(x, mod, mod, g_norm.reshape(1, d), w_in, conv_w, conv_b.reshape(1, -1), bif, kx, g_bc)


def _moba_steps(nb):
    jt, it = [], []
    for j in range(nb):
        for i in [j] + list(range(j)):
            jt.append(j)
            it.append(i)
    return jt, it


def _moba_key_extras(nb):
    blk = MOBA_BLOCK
    lane = np.arange(N_HEADS_MOBA * 128)
    head = lane // 128
    w = (lane % 128 - np.where(head % 2 == 0, HEAD_DIM_MOBA, 0))[None, None, :]
    slope = (2.0 ** -(head + 1.0))[None, None, :]
    i = np.arange(nb)[:, None, None]
    c = np.arange(blk)[None, :, None]
    tab = np.where(w == i, 1.0, 0.0)
    rest = LOG2E * slope * (i * blk + c)
    for term in range(N_POS_TERMS):
        piece = rest.astype(np.float32).astype(BF16).astype(np.float64)
        tab = tab + np.where(w == nb + term, piece, 0.0)
        rest = rest - piece
    return jnp.asarray(tab.astype(np.float32).astype(BF16))


def _moba_kernel(jt_ref, it_ref, qt_ref, k_ref, vt_ref, kmean_ref, gt_ref, x_ref, gate_ref, oat_ref, w_ref,
                 gf_ref, o_ref,
                 qaug_ref, m_ref, acc_ref, cmax_ref, wb_ref, obt_ref, *s_refs):
    @pl.when(jnp.logical_and(pl.program_id(0) == 0, pl.program_id(1) == 0))
    def _():
        for r in range(0, w_ref.shape[0], 256):
            wb_ref[r:r + 256, :] = w_ref[r:r + 256, :].astype(BF16)

    t = pl.program_id(1)
    j = jt_ref[t]
    i = it_ref[t]
    blk = MOBA_BLOCK
    dh = HEAD_DIM_MOBA
    nb = kmean_ref.shape[1]
    nbat = qt_ref.shape[0]
    first = i == j
    last = jnp.logical_or(i == j - 1, j == 0)

    def key_tile(bb, h):
        return k_ref[bb, :, 128 * h:128 * (h + 1)]

    def rows(h):
        return slice(dh * h, dh * (h + 1))

    ones_tail = jnp.ones((16, blk), BF16)

    def values_t(bb, h):
        return jnp.concatenate([vt_ref[bb, 0, rows(h), :], ones_tail], axis=0)

    @pl.when(first)
    def _():
        blk_i = lax.broadcasted_iota(jnp.int32, (nb, blk), 0)
        lane_m = lax.broadcasted_iota(jnp.int32, (nb, 2 * dh), 1)
        key_c = lax.broadcasted_iota(jnp.int32, (blk, blk), 0)
        qry_r = lax.broadcasted_iota(jnp.int32, (blk, blk), 1)
        causal = key_c <= qry_r
        ones_rows = jnp.where(blk_i < N_POS_TERMS, 1.0, 0.0)
        gates = {}
        for bb in range(nbat):
            for h in range(N_HEADS_MOBA):
                pr, hh = divmod(h, 2)
                qt_pair = qt_ref[bb, 128 * pr:128 * (pr + 1), :]
                in_head = (lane_m < dh) if hh == 0 else (lane_m >= dh)
                km = jnp.where(in_head, kmean_ref[bb, :, 128 * pr:128 * (pr + 1)], 0.0).astype(BF16)
                gates[bb, h] = jnp.dot(km, qt_pair, preferred_element_type=F32)
        for bb in range(nbat):
            for h in range(N_HEADS_MOBA):
                pr, hh = divmod(h, 2)
                gate = gates[bb, h]
                rank = jnp.zeros((nb, blk), jnp.int32)
                for i2 in range(nb):
                    g2 = gate[i2:i2 + 1, :]
                    beats = (g2 > gate) | ((g2 == gate) & (i2 < blk_i))
                    rank = rank + jnp.where(beats & (i2 < j), 1, 0)
                keep = ((blk_i < j) & (rank < MOBA_TOPK)) | (blk_i == j)
                sel_bias = jnp.where(keep, 0.0, MASK_BIAS)
                extra = jnp.concatenate([sel_bias, ones_rows, jnp.zeros((dh - 16, blk), F32)], axis=0)
                extra = extra.astype(BF16)
                qt_h = qt_ref[bb, dh * h:dh * (h + 1), :]
                qaug_ref[bb, h] = jnp.concatenate([qt_h, extra] if hh == 0 else [extra, qt_h], axis=0)

        for bb in range(nbat):
            for h in range(N_HEADS_MOBA):
                s = jnp.dot(key_tile(bb, h), qaug_ref[bb, h], preferred_element_type=F32)
                s = jnp.where(causal, s, MASK_BIAS)
                s_refs[bb][h] = s
                m_ref[bb, h:h + 1, :] = jnp.max(s, axis=0, keepdims=True)

        for bb in range(nbat):
            for h in range(N_HEADS_MOBA):
                p0 = jnp.exp2(s_refs[bb][h] - m_ref[bb, h:h + 1, :])
                acc_ref[bb, h] = jnp.dot(values_t(bb, h), p0.astype(BF16), preferred_element_type=F32)

    @pl.when(jnp.logical_not(first))
    def _():
        def score_phase(bb, hs):
            for h in hs:
                s = jnp.dot(key_tile(bb, h), qaug_ref[bb, h], preferred_element_type=F32)
                s_refs[bb][h] = s
                cmax_ref[bb, h:h + 1, :] = jnp.max(s, axis=0, keepdims=True)

        def update_phase(bb, hs):
            for h in hs:
                m_run = m_ref[bb, h:h + 1, :]
                m_new = jnp.maximum(m_run, cmax_ref[bb, h:h + 1, :])
                alpha = jnp.exp2(m_run - m_new)
                pr_ = jnp.exp2(s_refs[bb][h] - m_new)
                m_ref[bb, h:h + 1, :] = m_new
                acc_ref[bb, h] = alpha * acc_ref[bb, h] + jnp.dot(values_t(bb, h), pr_.astype(BF16),
                                                                  preferred_element_type=F32)

        half = N_HEADS_MOBA // 2
        units = [(bb, range(g * half, (g + 1) * half)) for bb in range(nbat) for g in range(2)]
        lag = 2
        for step in range(len(units) + lag):
            if step < len(units):
                score_phase(*units[step])
            if step >= lag:
                update_phase(*units[step - lag])

    @pl.when(last)
    def _():
        for bb in range(nbat):
            for h in range(N_HEADS_MOBA):
                acc = acc_ref[bb, h]
                out = acc[0:dh] / acc[dh:dh + 1] * gt_ref[bb, rows(h), :].astype(F32)
                obt_ref[bb, rows(h), :] = out.astype(BF16)
        for bb in range(nbat):
            y = (_dot_tn(oat_ref[bb], wb_ref[0:D_MLSTM, :])
                 + _dot_tn(obt_ref[bb], wb_ref[D_MLSTM:D_MLSTM + D_MOBA, :]))
            r = x_ref[bb] + gate_ref[bb] * y
            o_ref[bb] = r * lax.rsqrt(jnp.mean(r * r, axis=-1, keepdims=True) + EPS) * gf_ref[...]


def _moba_out(qbt, kaug, vbt, kmean, gbt, x, mod, out_at, w_out, g_final):
    b, dmb, s = qbt.shape
    d = x.shape[2]
    blk = MOBA_BLOCK
    nb = s // blk
    kw = kaug.shape[2]
    nbat = MOBA_BATCH
    assert nb == 8, "selection extras assume 8 key blocks"
    jt, it = _moba_steps(nb)
    qmap = lambda bi, t, jt_r, it_r: (bi, 0, jt_r[t])
    rmap = lambda bi, t, jt_r, it_r: (bi, jt_r[t], 0)
    full2 = lambda bi, t, jt_r, it_r: (0, 0)
    grid_spec = pltpu.PrefetchScalarGridSpec(
        num_scalar_prefetch=2,
        grid=(b // nbat, len(jt)),
        in_specs=[pl.BlockSpec((nbat, dmb, blk), qmap),
                  pl.BlockSpec((nbat, blk, kw), lambda bi, t, jt_r, it_r: (bi, it_r[t], 0)),
                  pl.BlockSpec((nbat, 1, dmb, blk), lambda bi, t, jt_r, it_r: (bi, it_r[t], 0, 0)),
                  pl.BlockSpec((nbat, nb, dmb), lambda bi, t, jt_r, it_r: (bi, 0, 0)),
                  pl.BlockSpec((nbat, dmb, blk), qmap),
                  pl.BlockSpec((nbat, blk, d), rmap),
                  pl.BlockSpec((nbat, 1, d), lambda bi, t, jt_r, it_r: (bi, 0, 2)),
                  pl.BlockSpec((nbat, D_MLSTM, blk), qmap),
                  pl.BlockSpec(w_out.shape, full2, pipeline_mode=pl.Buffered(1)),
                  pl.BlockSpec((1, d), full2)],
        out_specs=pl.BlockSpec((nbat, blk, d), rmap),
        scratch_shapes=[pltpu.VMEM((nbat, N_HEADS_MOBA, 2 * HEAD_DIM_MOBA, blk), BF16),
                        pltpu.VMEM((nbat, N_HEADS_MOBA, blk), F32),
                        pltpu.VMEM((nbat, N_HEADS_MOBA, HEAD_DIM_MOBA + 16, blk), F32),
                        pltpu.VMEM((nbat, N_HEADS_MOBA, blk), F32),
                        pltpu.VMEM(w_out.shape, BF16),
                        pltpu.VMEM((nbat, dmb, blk), BF16)]
        + [pltpu.VMEM((N_HEADS_MOBA, blk, blk), F32) for _ in range(nbat)],
    )
    return pl.pallas_call(
        _moba_kernel,
        grid_spec=grid_spec,
        out_shape=jax.ShapeDtypeStruct((b, s, d), x.dtype),
        compiler_params=pltpu.CompilerParams(dimension_semantics=("arbitrary", "arbitrary"),
                                             vmem_limit_bytes=VMEM_LIMIT),
        name="moba_out",
    )(jnp.asarray(jt, jnp.int32), jnp.asarray(it, jnp.int32), qbt, kaug, vbt, kmean, gbt,
      x, mod, out_at, w_out, g_final.reshape(1, d))


def _layer(x, c, w_ada, b_ada, g_norm, w_in, conv_w, conv_b, b_igate, b_fgate, g_mlstm_head, w_out,
           g_final):
    b = x.shape[0]
    mod = _ada(c, w_ada, b_ada).reshape(b, 1, 3 * D_MODEL)

    z4 = jnp.zeros((4,), F32)
    bif = jnp.concatenate([b_igate.astype(F32), z4, b_fgate.astype(F32), z4]).reshape(_GATE_ROWS, 1)

    qbt, kaug, kmean, vbt, gbt, out_at = _proj_mlstm(
        x, mod, g_norm, jnp.transpose(w_in), conv_w, conv_b, bif, g_mlstm_head)
    return _moba_out(qbt, kaug, vbt, kmean, gbt, x, mod, out_at, w_out, g_final)


def kernel(x, c, w_ada, b_ada, g_norm, w_in, conv_w, conv_b, b_igate, b_fgate, g_mlstm_head, w_out,
           g_final):
    assert w_ada.shape[0] == 1, "single-layer trunk"
    return _layer(x, c, w_ada[0], b_ada[0], g_norm[0], w_in[0], conv_w[0], conv_b[0], b_igate[0],
                  b_fgate[0], g_mlstm_head[0], w_out[0], g_final)
```

```python
import jax
import jax.numpy as jnp
import numpy as np
from jax import lax
from jax.experimental import pallas as pl
from jax.experimental.pallas import tpu as pltpu

F32 = jnp.float32
BF16 = jnp.bfloat16

D_MODEL = 1024
D_MLSTM = 512
N_HEADS_MLSTM = 4
HEAD_DIM_MLSTM = 128
D_MOBA = 512
N_HEADS_MOBA = 8
HEAD_DIM_MOBA = 64
CONV_K = 4
MOBA_BLOCK = 256
MOBA_TOPK = 3
EPS = 1e-6

ROW_TILE = 512
PROJ_CHUNK = 256
MLSTM_CHUNK = 128
MOBA_BATCH = 4
MASK_BIAS = -1e30
LOG2E = 1.4426950408889634
N_POS_TERMS = 3
VMEM_LIMIT = 56 * 1024 * 1024
_TILES_PER_SEQ = 2048 // ROW_TILE
_N_ROW_TILES = 8 * _TILES_PER_SEQ

_OFF = {}
_o = 0
for _name, _size in (("qm", 512), ("km", 512), ("vm", 512), ("om", 512), ("im", 4), ("fm", 4),
                     ("zm", 512), ("qb", 512), ("kb", 512), ("vb", 512), ("zb", 512)):
    _OFF[_name] = (_o, _o + _size)
    _o += _size
_NN_GROUPS = ("qm", "km", "kb")
_NT_GROUPS = ("qb", "vb", "zb", "vm", "om", "zm")
_GATE_ROWS = 16


def _silu(v):
    return v * jax.nn.sigmoid(v)


def _dot_nt(a, b):
    return lax.dot_general(a, b, (((1,), (1,)), ((), ())), preferred_element_type=F32)


def _dot_tn(a, b):
    return lax.dot_general(a, b, (((0,), (0,)), ((), ())), preferred_element_type=F32)


def _ada_kernel(c_ref, w_ref, b_ref, o_ref):
    a = _silu(c_ref[...])
    w = w_ref[...]
    a_hi = a.astype(BF16)
    a_lo = (a - a_hi.astype(F32)).astype(BF16)
    w_hi = w.astype(BF16)
    w_lo = (w - w_hi.astype(F32)).astype(BF16)
    acc = jnp.dot(jnp.concatenate([a_hi, a_lo], axis=0), w_hi, preferred_element_type=F32)
    n = a.shape[0]
    o_ref[...] = (acc[0:n] + acc[n:2 * n] + jnp.dot(a_hi, w_lo, preferred_element_type=F32)) + b_ref[...]


def _ada(c, w_ada, b_ada):
    b, d = c.shape
    n = w_ada.shape[1]
    tn = 512
    return pl.pallas_call(
        _ada_kernel,
        grid=(n // tn,),
        in_specs=[pl.BlockSpec((b, d), lambda i: (0, 0)),
                  pl.BlockSpec((d, tn), lambda i: (0, i)),
                  pl.BlockSpec((1, tn), lambda i: (0, i))],
        out_specs=pl.BlockSpec((b, tn), lambda i: (0, i)),
        out_shape=jax.ShapeDtypeStruct((b, n), F32),
        compiler_params=pltpu.CompilerParams(dimension_semantics=("arbitrary",),
                                             vmem_limit_bytes=VMEM_LIMIT),
        name="ada",
    )(c, w_ada, b_ada.reshape(1, n))


def _chunk_scan(v, combine, fill):
    pos = lax.broadcasted_iota(jnp.int32, v.shape, 1) & (MLSTM_CHUNK - 1)
    d = 1
    while d < MLSTM_CHUNK:
        shifted = pltpu.roll(v, d, axis=1)
        v = combine(v, jnp.where(pos >= d, shifted, fill))
        d *= 2
    return v


def _col_bcast(row, n):
    return jnp.transpose(jnp.broadcast_to(row, (n, n)))


def _interleave(major, minor):
    out, done = [], 0
    for idx, task in enumerate(major):
        out.append(task)
        want = (idx + 1) * len(minor) // len(major)
        out.extend(minor[done:want])
        done = want
    return out


def _proj_mlstm_kernel(x_ref, mod_ref, gn_ref, w_ref, cw_ref, cb_ref, bif_ref, kx_ref, gh_ref,
                       qbt_ref, kaug_ref, kmean_ref, vbt_ref, gbt_ref, oat_ref,
                       ext_ref, wnn_ref, wnt_ref,
                       qp_ref, kp_ref, vp_ref, gp_ref, tp_ref,
                       qm_ref, km_ref, vm_ref, gm_ref, tm_ref, st_ref, m_ref):
    t = pl.program_id(0)
    tm = x_ref.shape[1]
    ts = MOBA_BLOCK
    nsub = tm // ts
    cw = PROJ_CHUNK
    halo = CONV_K - 1
    L = MLSTM_CHUNK
    dh = HEAD_DIM_MLSTM
    npair = N_HEADS_MLSTM // 2
    nchunk = tm // L
    tile_p = jnp.minimum(t, _N_ROW_TILES - 1)
    jp = tile_p % _TILES_PER_SEQ
    shift = mod_ref[pl.ds(tile_p // _TILES_PER_SEQ, 1), 0:D_MODEL]
    scale = mod_ref[pl.ds(tile_p // _TILES_PER_SEQ, 1), D_MODEL:2 * D_MODEL]
    jm = jnp.maximum(t - 1, 0) % _TILES_PER_SEQ

    @pl.when(t == 0)
    def _():
        for g, name in enumerate(_NN_GROUPS):
            lo = _OFF[name][0]
            for c in range(0, 512, 256):
                wnn_ref[:, 512 * g + c:512 * g + c + 256] = jnp.transpose(
                    w_ref[lo + c:lo + c + 256, :]).astype(BF16)
        for g, name in enumerate(_NT_GROUPS):
            lo = _OFF[name][0]
            for c in range(0, 512, 256):
                r0 = _GATE_ROWS + 512 * g + c
                rows = w_ref[lo + c:lo + c + 256, :]
                if name == "qb":
                    rows = rows * (HEAD_DIM_MOBA ** -0.5)
                wnt_ref[r0:r0 + 256, :] = rows.astype(BF16)
        gi = _OFF["im"][0]
        row_w = lax.broadcasted_iota(jnp.int32, (_GATE_ROWS, w_ref.shape[1]), 0)
        wnt_ref[0:_GATE_ROWS, :] = jnp.where(
            row_w < 4, w_ref[gi:gi + _GATE_ROWS, :],
            jnp.where((row_w >= 8) & (row_w < 12), w_ref[gi - 4:gi - 4 + _GATE_ROWS, :], 0.0)).astype(BF16)
        for ref in (qp_ref, kp_ref, vp_ref, gp_ref, tp_ref, st_ref, m_ref):
            ref[...] = jnp.zeros(ref.shape, ref.dtype)

    @pl.when(jp == 0)
    def _():
        ext_ref[0:8, :] = jnp.zeros((8, 2 * D_MLSTM), F32)

    @pl.when(jm == 0)
    def _():
        st_ref[...] = jnp.zeros(st_ref.shape, F32)
        m_ref[...] = jnp.zeros(m_ref.shape, F32)

    def step(with_projection):
        qm_ref[...] = qp_ref[...]
        km_ref[...] = kp_ref[...]
        vm_ref[...] = vp_ref[...]
        gm_ref[...] = gp_ref[...]
        tm_ref[...] = tp_ref[...]

        s_idx = lax.broadcasted_iota(jnp.int32, (L, 2 * L), 0)
        t_idx = lax.broadcasted_iota(jnp.int32, (L, 2 * L), 1) & (L - 1)
        causal = s_idx <= t_idx
        lane2 = lax.broadcasted_iota(jnp.int32, (1, 2 * L), 1)
        ones_tail = jnp.ones((16, L), BF16)
        zeros_blk = jnp.zeros((L, dh), BF16)

        def pair_row(v, p):
            return jnp.concatenate([v[2 * p:2 * p + 1, :], v[2 * p + 1:2 * p + 2, :]], axis=1)

        def pair_scalar(v, p):
            return jnp.where(lane2 < L, v[2 * p:2 * p + 1, :], v[2 * p + 1:2 * p + 2, :])

        def block_diag(a, b):
            return jnp.concatenate([jnp.concatenate([a, zeros_blk], axis=1),
                                    jnp.concatenate([zeros_blk, b], axis=1)], axis=0)

        m_prev = m_ref[:, 0:1]
        rowv = []
        for c in range(nchunk):
            cs_ = slice(c * L, (c + 1) * L)
            cum_f = tm_ref[0:8, cs_]
            g = tm_ref[8:16, cs_]
            g_cmax = tm_ref[16:24, cs_]
            f_tot = cum_f[:, L - 1:L]
            g_max = g_cmax[:, L - 1:L]
            a_max = f_tot + g_max
            a_row = -jnp.maximum(m_prev, g_cmax)
            m_new = jnp.maximum(f_tot + m_prev, a_max)
            rowv.append(dict(
                ts=cs_, g=g, a_row=a_row,
                w=jnp.exp(g - g_max),
                inter=jnp.exp(a_row + m_prev),
                floor=jnp.exp(a_row - cum_f),
                s_old=jnp.exp(f_tot + m_prev - m_new), s_new=jnp.exp(a_max - m_new)))
            m_prev = m_new
        m_ref[...] = jnp.broadcast_to(m_prev, m_ref.shape)

        def q_diag(cs_, p):
            return block_diag(qm_ref[cs_, 2 * p * dh:(2 * p + 1) * dh],
                              qm_ref[cs_, (2 * p + 1) * dh:(2 * p + 2) * dh])

        s_kq, vts, carried, intra = {}, {}, {}, {}
        st = {p: st_ref[p] for p in range(npair)}
        mtasks = []

        def score_task(c, p):
            rv = rowv[c]
            k2 = km_ref[rv["ts"], 2 * p * dh:(2 * p + 2) * dh]
            g_col = jnp.concatenate([_col_bcast(rv["g"][2 * p:2 * p + 1, :], L),
                                     _col_bcast(rv["g"][2 * p + 1:2 * p + 2, :], L)], axis=1)
            decay = jnp.exp(jnp.where(causal, g_col + pair_row(rv["a_row"], p), -jnp.inf))
            s_kq[c, p] = (_dot_nt(k2, q_diag(rv["ts"], p)) * decay).astype(BF16)

        def mix_task(c, p):
            cs_ = rowv[c]["ts"]
            vt2 = jnp.concatenate(
                [jnp.concatenate([vm_ref[(2 * p + e) * dh:(2 * p + e + 1) * dh, cs_], ones_tail], axis=0)
                 for e in range(2)], axis=1)
            vts[c, p] = vt2
            carried[c, p] = _dot_nt(st[p].astype(BF16), q_diag(cs_, p))
            intra[c, p] = jnp.dot(vt2, block_diag(s_kq[c, p][:, 0:L], s_kq[c, p][:, L:2 * L]),
                                  preferred_element_type=F32)

        def finish_task(c, p):
            rv = rowv[c]
            cs_ = rv["ts"]
            mixed = intra[c, p] + pair_row(rv["inter"], p) * carried[c, p]
            den = mixed[dh:dh + 1, :]
            hh = mixed[0:dh] / jnp.maximum(jnp.abs(den), pair_row(rv["floor"], p))
            hh = hh * lax.rsqrt(jnp.mean(hh * hh, axis=0, keepdims=True) + EPS)
            for e in range(2):
                sl = slice((2 * p + e) * dh, (2 * p + e + 1) * dh)
                oat_ref[0, sl, cs_] = (hh[:, e * L:(e + 1) * L] * gh_ref[sl, :]
                                       * gm_ref[sl, cs_].astype(F32)).astype(BF16)

        def state_task(c, p):
            rv = rowv[c]
            cs_ = rv["ts"]
            vw = (vts[c, p].astype(F32) * pair_row(rv["w"], p)).astype(BF16)
            k_bd = block_diag(km_ref[cs_, 2 * p * dh:(2 * p + 1) * dh],
                              km_ref[cs_, (2 * p + 1) * dh:(2 * p + 2) * dh])
            st_loc = jnp.dot(vw, k_bd, preferred_element_type=F32)
            st[p] = pair_scalar(rv["s_old"], p) * st[p] + pair_scalar(rv["s_new"], p) * st_loc

        for c in range(nchunk):
            for p in range(npair):
                mtasks.append((score_task, c, p))
        for c in range(nchunk):
            for fn in (mix_task, finish_task, state_task):
                for p in range(npair):
                    mtasks.append((fn, c, p))

        head_start = 2 * npair
        for fn, a0, a1 in mtasks[:head_start]:
            fn(a0, a1)
        mtasks = mtasks[head_start:]

        lane = lax.broadcasted_iota(jnp.int32, (ts, 128), 1)
        hbs = []

        def norms():
            for u in range(nsub):
                x = x_ref[0, u * ts:(u + 1) * ts, :]
                ms = jnp.mean(x * x, axis=-1, keepdims=True)
                h = x * lax.rsqrt(ms + EPS) * gn_ref[...] * (1.0 + scale) + shift
                hbs.append(h.astype(BF16))

        ptasks = []
        for u in range(nsub):
            rs = slice(u * ts, (u + 1) * ts)

            def scan_gates(ift, rs=rs):
                ift = ift + bif_ref[...]
                log_i = ift[0:8]
                f_pre = ift[8:16]
                log_f = jnp.minimum(f_pre, 0.0) - jnp.log1p(jnp.exp(-jnp.abs(f_pre)))
                cum_f = _chunk_scan(log_f, jnp.add, 0.0)
                g = log_i - cum_f
                tp_ref[0:8, rs] = cum_f
                tp_ref[8:16, rs] = g
                tp_ref[16:24, rs] = _chunk_scan(g, jnp.maximum, -jnp.inf)

            def nn(lo, hi, u=u):
                return jnp.dot(hbs[u], wnn_ref[:, lo:hi], preferred_element_type=F32)

            def nt(lo, hi, u=u):
                return _dot_nt(wnt_ref[_GATE_ROWS + lo:_GATE_ROWS + hi, :], hbs[u])

            def conv_chunk(lo, _b, u=u, rs=rs, nn=nn):
                cs = slice(lo, lo + cw)
                base = 8 + u * ts
                qk = nn(lo, lo + cw)
                ext_ref[base:base + ts, cs] = qk
                conv = cb_ref[:, cs] + qk * cw_ref[CONV_K - 1:CONV_K, cs]
                for tap in range(halo):
                    conv = conv + ext_ref[pl.ds(base - halo + tap, ts), cs] * cw_ref[tap:tap + 1, cs]
                if u == nsub - 1:
                    ext_ref[0:8, cs] = qk[ts - 8:ts, :]
                act = _silu(conv)
                if lo < D_MLSTM:
                    qp_ref[rs, lo:lo + cw] = act.astype(BF16)
                else:
                    kp_ref[rs, lo - D_MLSTM:lo - D_MLSTM + cw] = (act * (HEAD_DIM_MLSTM ** -0.5)).astype(BF16)

            def value_chunk(lo, _b, rs=rs, nt=nt):
                vp_ref[lo:lo + cw, rs] = nt(1536 + lo, 1536 + lo + cw).astype(BF16)

            def gate_chunk(lo, _b, rs=rs, nt=nt):
                gp_ref[lo:lo + cw, rs] = (jax.nn.sigmoid(nt(2048 + lo, 2048 + lo + cw))
                                         * _silu(nt(2560 + lo, 2560 + lo + cw))).astype(BF16)

            def key_chunk(lo, _b, u=u, rs=rs, nn=nn):
                kb = nn(1024 + lo, 1024 + lo + cw)
                kmean_ref[0, pl.ds(jp * nsub + u, 1), lo:lo + cw] = jnp.mean(kb, axis=0, keepdims=True)
                kb2 = (kb * LOG2E).astype(BF16)
                for pr in range(lo // 128, (lo + cw) // 128):
                    for hh in range(2):
                        hd = 2 * pr + hh
                        is_key = (lane < HEAD_DIM_MOBA) if hh == 0 else (lane >= HEAD_DIM_MOBA)
                        grp = slice(128 * hd, 128 * (hd + 1))
                        kaug_ref[0, rs, grp] = jnp.where(is_key, kb2[:, 128 * pr - lo:128 * (pr + 1) - lo],
                                                         kx_ref[u, :, grp])

            def qt_chunk(lo, _b, rs=rs, nt=nt, scan_gates=scan_gates):
                if lo == 0:
                    both = nt(-_GATE_ROWS, cw)
                    scan_gates(both[0:_GATE_ROWS])
                    qt = both[_GATE_ROWS:_GATE_ROWS + cw]
                else:
                    qt = nt(lo, lo + cw)
                qbt_ref[0, lo:lo + cw, rs] = qt.astype(BF16)

            def vt_chunk(lo, _b, u=u, nt=nt):
                vbt_ref[0, u, lo:lo + cw, :] = nt(512 + lo, 512 + lo + cw).astype(BF16)

            def gt_chunk(lo, _b, rs=rs, nt=nt):
                gbt_ref[0, lo:lo + cw, rs] = _silu(nt(1024 + lo, 1024 + lo + cw)).astype(BF16)

            heavy = ([(conv_chunk, lo, 0) for lo in range(0, 2 * D_MLSTM, cw)]
                     + [(gate_chunk, lo, 0) for lo in range(0, D_MLSTM, cw)]
                     + [(key_chunk, lo, 0) for lo in range(0, D_MOBA, cw)])
            light = ([(qt_chunk, lo, 0) for lo in range(0, D_MOBA, cw)]
                     + [(vt_chunk, lo, 0) for lo in range(0, D_MOBA, cw)]
                     + [(value_chunk, lo, 0) for lo in range(0, D_MLSTM, cw)]
                     + [(gt_chunk, lo, 0) for lo in range(0, D_MOBA, cw)])
            for idx in range(max(len(heavy), len(light))):
                for group in (light, heavy):
                    if idx < len(group):
                        ptasks.append(group[idx])

        tasks = mtasks
        if with_projection:
            norms()
            tasks = _interleave(ptasks, mtasks)
        for fn, a0, a1 in tasks:
            fn(a0, a1)
        for p in range(npair):
            st_ref[p] = st[p]

    @pl.when(t < _N_ROW_TILES)
    def _():
        step(True)

    @pl.when(t == _N_ROW_TILES)
    def _():
        step(False)


def _proj_mlstm(x, mod, g_norm, w_in, conv_w, conv_b, bif, g_head):
    b, s, d = x.shape
    tm = ROW_TILE
    nb = s // MOBA_BLOCK
    nsub = tm // MOBA_BLOCK
    assert tm % MOBA_BLOCK == 0 and (b * s) // tm == _N_ROW_TILES and s // tm == _TILES_PER_SEQ
    kx = _moba_key_extras(nb)
    kw = N_HEADS_MOBA * 128
    dm = D_MLSTM
    g_bc = jnp.broadcast_to(g_head.astype(F32).reshape(dm, 1), (dm, 128))
    tps = _TILES_PER_SEQ
    last = _N_ROW_TILES - 1
    pt = lambda t: jnp.minimum(t, last)
    mt = lambda t: jnp.maximum(t - 1, 0)
    full2 = lambda t: (0, 0)
    out_shapes = (
        jax.ShapeDtypeStruct((b, D_MOBA, s), BF16),
        jax.ShapeDtypeStruct((b, s, kw), BF16),
        jax.ShapeDtypeStruct((b, nb, D_MOBA), F32),
        jax.ShapeDtypeStruct((b, nb, D_MOBA, MOBA_BLOCK), BF16),
        jax.ShapeDtypeStruct((b, D_MOBA, s), BF16),
        jax.ShapeDtypeStruct((b, dm, s), BF16),
    )
    out_specs = (
        pl.BlockSpec((1, D_MOBA, tm), lambda t: (pt(t) // tps, 0, pt(t) % tps)),
        pl.BlockSpec((1, tm, kw), lambda t: (pt(t) // tps, pt(t) % tps, 0)),
        pl.BlockSpec((1, nb, D_MOBA), lambda t: (pt(t) // tps, 0, 0)),
        pl.BlockSpec((1, nsub, D_MOBA, MOBA_BLOCK), lambda t: (pt(t) // tps, pt(t) % tps, 0, 0)),
        pl.BlockSpec((1, D_MOBA, tm), lambda t: (pt(t) // tps, 0, pt(t) % tps)),
        pl.BlockSpec((1, dm, tm), lambda t: (mt(t) // tps, 0, mt(t) % tps)),
    )
    in_specs = [
        pl.BlockSpec((1, tm, d), lambda t: (pt(t) // tps, pt(t) % tps, 0)),
        pl.BlockSpec(mod.shape, full2),
        pl.BlockSpec((1, d), full2),
        pl.BlockSpec(w_in.shape, full2, pipeline_mode=pl.Buffered(1)),
        pl.BlockSpec(conv_w.shape, full2),
        pl.BlockSpec((1, 2 * dm), full2),
        pl.BlockSpec((_GATE_ROWS, 1), full2),
        pl.BlockSpec((nsub, MOBA_BLOCK, kw), lambda t: (pt(t) % tps, 0, 0)),
        pl.BlockSpec((dm, 128), full2),
    ]
    operand_scratch = [pltpu.VMEM((tm, dm), BF16), pltpu.VMEM((tm, dm), BF16),
                       pltpu.VMEM((dm, tm), BF16), pltpu.VMEM((dm, tm), BF16),
                       pltpu.VMEM((24, tm), F32)]
    return pl.pallas_call(
        _proj_mlstm_kernel,
        grid=(_N_ROW_TILES + 1,),
        in_specs=in_specs,
        out_specs=out_specs,
        out_shape=out_shapes,
        scratch_shapes=[pltpu.VMEM((tm + 8, 2 * dm), F32),
                        pltpu.VMEM((d, 512 * len(_NN_GROUPS)), BF16),
                        pltpu.VMEM((_GATE_ROWS + 512 * len(_NT_GROUPS), d), BF16)]
        + operand_scratch + operand_scratch
        + [pltpu.VMEM((N_HEADS_MLSTM // 2, HEAD_DIM_MLSTM + 16, 2 * HEAD_DIM_MLSTM), F32),
           pltpu.VMEM((8, 128), F32)],
        compiler_params=pltpu.CompilerParams(dimension_semantics=("arbitrary",),
                                             vmem_limit_bytes=VMEM_LIMIT),
        name="proj_mlstm",
    )(x, mod, g_norm.reshape(1, d), w_in, conv_w, conv_b.reshape(1, -1), bif, kx, g_bc)


def _moba_steps(nb):
    jt, it = [], []
    for j in range(nb):
        for i in [j] + list(range(j)):
            jt.append(j)
            it.append(i)
    return jt, it


def _moba_key_extras(nb):
    blk = MOBA_BLOCK
    lane = np.arange(N_HEADS_MOBA * 128)
    head = lane // 128
    w = (lane % 128 - np.where(head % 2 == 0, HEAD_DIM_MOBA, 0))[None, None, :]
    slope = (2.0 ** -(head + 1.0))[None, None, :]
    i = np.arange(nb)[:, None, None]
    c = np.arange(blk)[None, :, None]
    tab = np.where(w == i, 1.0, 0.0)
    rest = LOG2E * slope * (i * blk + c)
    for term in range(N_POS_TERMS):
        piece = rest.astype(np.float32).astype(BF16).astype(np.float64)
        tab = tab + np.where(w == nb + term, piece, 0.0)
        rest = rest - piece
    return jnp.asarray(tab.astype(np.float32).astype(BF16))


def _moba_kernel(jt_ref, it_ref, qt_ref, k_ref, vt_ref, kmean_ref, gt_ref, x_ref, mod_ref, oat_ref, w_ref,
                 gf_ref, o_ref,
                 qaug_ref, m_ref, acc_ref, cmax_ref, wb_ref, obt_ref, *s_refs):
    @pl.when(jnp.logical_and(pl.program_id(0) == 0, pl.program_id(1) == 0))
    def _():
        for r in range(0, w_ref.shape[0], 256):
            wb_ref[r:r + 256, :] = w_ref[r:r + 256, :].astype(BF16)

    t = pl.program_id(1)
    j = jt_ref[t]
    i = it_ref[t]
    blk = MOBA_BLOCK
    dh = HEAD_DIM_MOBA
    nb = kmean_ref.shape[1]
    nbat = qt_ref.shape[0]
    first = i == j
    last = jnp.logical_or(i == j - 1, j == 0)

    def key_tile(bb, h):
        return k_ref[bb, :, 128 * h:128 * (h + 1)]

    def rows(h):
        return slice(dh * h, dh * (h + 1))

    ones_tail = jnp.ones((16, blk), BF16)

    def values_t(bb, h):
        return jnp.concatenate([vt_ref[bb, 0, rows(h), :], ones_tail], axis=0)

    @pl.when(first)
    def _():
        blk_i = lax.broadcasted_iota(jnp.int32, (nb, blk), 0)
        lane_m = lax.broadcasted_iota(jnp.int32, (nb, 2 * dh), 1)
        key_c = lax.broadcasted_iota(jnp.int32, (blk, blk), 0)
        qry_r = lax.broadcasted_iota(jnp.int32, (blk, blk), 1)
        causal = key_c <= qry_r
        ones_rows = jnp.where(blk_i < N_POS_TERMS, 1.0, 0.0)
        gates = {}
        for bb in range(nbat):
            for h in range(N_HEADS_MOBA):
                pr, hh = divmod(h, 2)
                qt_pair = qt_ref[bb, 128 * pr:128 * (pr + 1), :]
                in_head = (lane_m < dh) if hh == 0 else (lane_m >= dh)
                km = jnp.where(in_head, kmean_ref[bb, :, 128 * pr:128 * (pr + 1)], 0.0).astype(BF16)
                gates[bb, h] = jnp.dot(km, qt_pair, preferred_element_type=F32)
        for bb in range(nbat):
            for h in range(N_HEADS_MOBA):
                pr, hh = divmod(h, 2)
                gate = gates[bb, h]
                rank = jnp.zeros((nb, blk), jnp.int32)
                for i2 in range(nb):
                    g2 = gate[i2:i2 + 1, :]
                    beats = (g2 > gate) | ((g2 == gate) & (i2 < blk_i))
                    rank = rank + jnp.where(beats & (i2 < j), 1, 0)
                keep = ((blk_i < j) & (rank < MOBA_TOPK)) | (blk_i == j)
                sel_bias = jnp.where(keep, 0.0, MASK_BIAS)
                extra = jnp.concatenate([sel_bias, ones_rows, jnp.zeros((dh - 16, blk), F32)], axis=0)
                extra = extra.astype(BF16)
                qt_h = qt_ref[bb, dh * h:dh * (h + 1), :]
                qaug_ref[bb, h] = jnp.concatenate([qt_h, extra] if hh == 0 else [extra, qt_h], axis=0)

        for bb in range(nbat):
            for h in range(N_HEADS_MOBA):
                s = jnp.dot(key_tile(bb, h), qaug_ref[bb, h], preferred_element_type=F32)
                s = jnp.where(causal, s, MASK_BIAS)
                s_refs[bb][h] = s
                m_ref[bb, h:h + 1, :] = jnp.max(s, axis=0, keepdims=True)

        for bb in range(nbat):
            for h in range(N_HEADS_MOBA):
                p0 = jnp.exp2(s_refs[bb][h] - m_ref[bb, h:h + 1, :])
                acc_ref[bb, h] = jnp.dot(values_t(bb, h), p0.astype(BF16), preferred_element_type=F32)

    @pl.when(jnp.logical_not(first))
    def _():
        def score_phase(bb, hs):
            for h in hs:
                s = jnp.dot(key_tile(bb, h), qaug_ref[bb, h], preferred_element_type=F32)
                s_refs[bb][h] = s
                cmax_ref[bb, h:h + 1, :] = jnp.max(s, axis=0, keepdims=True)

        def update_phase(bb, hs):
            for h in hs:
                m_run = m_ref[bb, h:h + 1, :]
                m_new = jnp.maximum(m_run, cmax_ref[bb, h:h + 1, :])
                alpha = jnp.exp2(m_run - m_new)
                pr_ = jnp.exp2(s_refs[bb][h] - m_new)
                m_ref[bb, h:h + 1, :] = m_new
                acc_ref[bb, h] = alpha * acc_ref[bb, h] + jnp.dot(values_t(bb, h), pr_.astype(BF16),
                                                                  preferred_element_type=F32)

        half = N_HEADS_MOBA // 2
        units = [(bb, range(g * half, (g + 1) * half)) for bb in range(nbat) for g in range(2)]
        lag = 2
        for step in range(len(units) + lag):
            if step < len(units):
                score_phase(*units[step])
            if step >= lag:
                update_phase(*units[step - lag])

    @pl.when(last)
    def _():
        for bb in range(nbat):
            for h in range(N_HEADS_MOBA):
                acc = acc_ref[bb, h]
                out = acc[0:dh] / acc[dh:dh + 1] * gt_ref[bb, rows(h), :].astype(F32)
                obt_ref[bb, rows(h), :] = out.astype(BF16)
        for bb in range(nbat):
            y = (_dot_tn(oat_ref[bb], wb_ref[0:D_MLSTM, :])
                 + _dot_tn(obt_ref[bb], wb_ref[D_MLSTM:D_MLSTM + D_MOBA, :]))
            gate = mod_ref[pl.ds(pl.program_id(0) * nbat + bb, 1), 2 * D_MODEL:3 * D_MODEL]
            r = x_ref[bb] + gate * y
            o_ref[bb] = r * lax.rsqrt(jnp.mean(r * r, axis=-1, keepdims=True) + EPS) * gf_ref[...]


def _moba_out(qbt, kaug, vbt, kmean, gbt, x, mod, out_at, w_out, g_final):
    b, dmb, s = qbt.shape
    d = x.shape[2]
    blk = MOBA_BLOCK
    nb = s // blk
    kw = kaug.shape[2]
    nbat = MOBA_BATCH
    assert nb == 8, "selection extras assume 8 key blocks"
    jt, it = _moba_steps(nb)
    qmap = lambda bi, t, jt_r, it_r: (bi, 0, jt_r[t])
    rmap = lambda bi, t, jt_r, it_r: (bi, jt_r[t], 0)
    full2 = lambda bi, t, jt_r, it_r: (0, 0)
    grid_spec = pltpu.PrefetchScalarGridSpec(
        num_scalar_prefetch=2,
        grid=(b // nbat, len(jt)),
        in_specs=[pl.BlockSpec((nbat, dmb, blk), qmap),
                  pl.BlockSpec((nbat, blk, kw), lambda bi, t, jt_r, it_r: (bi, it_r[t], 0)),
                  pl.BlockSpec((nbat, 1, dmb, blk), lambda bi, t, jt_r, it_r: (bi, it_r[t], 0, 0)),
                  pl.BlockSpec((nbat, nb, dmb), lambda bi, t, jt_r, it_r: (bi, 0, 0)),
                  pl.BlockSpec((nbat, dmb, blk), qmap),
                  pl.BlockSpec((nbat, blk, d), rmap),
                  pl.BlockSpec(mod.shape, full2),
                  pl.BlockSpec((nbat, D_MLSTM, blk), qmap),
                  pl.BlockSpec(w_out.shape, full2, pipeline_mode=pl.Buffered(1)),
                  pl.BlockSpec((1, d), full2)],
        out_specs=pl.BlockSpec((nbat, blk, d), rmap),
        scratch_shapes=[pltpu.VMEM((nbat, N_HEADS_MOBA, 2 * HEAD_DIM_MOBA, blk), BF16),
                        pltpu.VMEM((nbat, N_HEADS_MOBA, blk), F32),
                        pltpu.VMEM((nbat, N_HEADS_MOBA, HEAD_DIM_MOBA + 16, blk), F32),
                        pltpu.VMEM((nbat, N_HEADS_MOBA, blk), F32),
                        pltpu.VMEM(w_out.shape, BF16),
                        pltpu.VMEM((nbat, dmb, blk), BF16)]
        + [pltpu.VMEM((N_HEADS_MOBA, blk, blk), F32) for _ in range(nbat)],
    )
    return pl.pallas_call(
        _moba_kernel,
        grid_spec=grid_spec,
        out_shape=jax.ShapeDtypeStruct((b, s, d), x.dtype),
        compiler_params=pltpu.CompilerParams(dimension_semantics=("arbitrary", "arbitrary"),
                                             vmem_limit_bytes=VMEM_LIMIT),
        name="moba_out",
    )(jnp.asarray(jt, jnp.int32), jnp.asarray(it, jnp.int32), qbt, kaug, vbt, kmean, gbt,
      x, mod, out_at, w_out, g_final.reshape(1, d))


def _layer(x, c, w_ada, b_ada, g_norm, w_in, conv_w, conv_b, b_igate, b_fgate, g_mlstm_head, w_out,
           g_final):
    mod = _ada(c, w_ada, b_ada)

    z4 = jnp.zeros((4,), F32)
    bif = jnp.concatenate([b_igate.astype(F32), z4, b_fgate.astype(F32), z4]).reshape(_GATE_ROWS, 1)

    qbt, kaug, kmean, vbt, gbt, out_at = _proj_mlstm(
        x, mod, g_norm, jnp.transpose(w_in), conv_w, conv_b, bif, g_mlstm_head)
    return _moba_out(qbt, kaug, vbt, kmean, gbt, x, mod, out_at, w_out, g_final)


def kernel(x, c, w_ada, b_ada, g_norm, w_in, conv_w, conv_b, b_igate, b_fgate, g_mlstm_head, w_out,
           g_final):
    assert w_ada.shape[0] == 1, "single-layer trunk"
    return _layer(x, c, w_ada[0], b_ada[0], g_norm[0], w_in[0], conv_w[0], conv_b[0], b_igate[0],
                  b_fgate[0], g_mlstm_head[0], w_out[0], g_final)
```

```python
import jax
import jax.numpy as jnp
import numpy as np
from jax import lax
from jax.experimental import pallas as pl
from jax.experimental.pallas import tpu as pltpu

F32 = jnp.float32
BF16 = jnp.bfloat16

D_MODEL = 1024
D_MLSTM = 512
N_HEADS_MLSTM = 4
HEAD_DIM_MLSTM = 128
D_MOBA = 512
N_HEADS_MOBA = 8
HEAD_DIM_MOBA = 64
CONV_K = 4
MOBA_BLOCK = 256
MOBA_TOPK = 3
EPS = 1e-6

ROW_TILE = 512
PROJ_CHUNK = 256
MLSTM_CHUNK = 128
MOBA_BATCH = 4
MASK_BIAS = -1e30
LOG2E = 1.4426950408889634
N_POS_TERMS = 3
VMEM_LIMIT = 56 * 1024 * 1024
_TILES_PER_SEQ = 2048 // ROW_TILE
_N_ROW_TILES = 8 * _TILES_PER_SEQ

_OFF = {}
_o = 0
for _name, _size in (("qm", 512), ("km", 512), ("vm", 512), ("om", 512), ("im", 4), ("fm", 4),
                     ("zm", 512), ("qb", 512), ("kb", 512), ("vb", 512), ("zb", 512)):
    _OFF[_name] = (_o, _o + _size)
    _o += _size
_NN_GROUPS = ("qm", "km", "kb")
_NT_GROUPS = ("qb", "vb", "zb", "vm", "om", "zm")
_GATE_ROWS = 16


def _silu(v):
    return v * jax.nn.sigmoid(v)


def _dot_nt(a, b):
    return lax.dot_general(a, b, (((1,), (1,)), ((), ())), preferred_element_type=F32)


def _dot_tn(a, b):
    return lax.dot_general(a, b, (((0,), (0,)), ((), ())), preferred_element_type=F32)


def _ada_kernel(c_ref, w_ref, b_ref, o_ref):
    a = _silu(c_ref[...])
    w = w_ref[...]
    a_hi = a.astype(BF16)
    a_lo = (a - a_hi.astype(F32)).astype(BF16)
    w_hi = w.astype(BF16)
    w_lo = (w - w_hi.astype(F32)).astype(BF16)
    acc = jnp.dot(jnp.concatenate([a_hi, a_lo], axis=0), w_hi, preferred_element_type=F32)
    n = a.shape[0]
    o_ref[...] = (acc[0:n] + acc[n:2 * n] + jnp.dot(a_hi, w_lo, preferred_element_type=F32)) + b_ref[...]


def _ada(c, w_ada, b_ada):
    b, d = c.shape
    n = w_ada.shape[1]
    tn = 512
    return pl.pallas_call(
        _ada_kernel,
        grid=(n // tn,),
        in_specs=[pl.BlockSpec((b, d), lambda i: (0, 0)),
                  pl.BlockSpec((d, tn), lambda i: (0, i)),
                  pl.BlockSpec((1, tn), lambda i: (0, i))],
        out_specs=pl.BlockSpec((b, tn), lambda i: (0, i)),
        out_shape=jax.ShapeDtypeStruct((b, n), F32),
        compiler_params=pltpu.CompilerParams(dimension_semantics=("arbitrary",),
                                             vmem_limit_bytes=VMEM_LIMIT),
        name="ada",
    )(c, w_ada, b_ada.reshape(1, n))


def _chunk_scan(v, combine, fill):
    pos = lax.broadcasted_iota(jnp.int32, v.shape, 1) & (MLSTM_CHUNK - 1)
    d = 1
    while d < MLSTM_CHUNK:
        shifted = pltpu.roll(v, d, axis=1)
        v = combine(v, jnp.where(pos >= d, shifted, fill))
        d *= 2
    return v


def _col_bcast(row, n):
    return jnp.transpose(jnp.broadcast_to(row, (n, n)))


def _interleave(major, minor):
    out, done = [], 0
    for idx, task in enumerate(major):
        out.append(task)
        want = (idx + 1) * len(minor) // len(major)
        out.extend(minor[done:want])
        done = want
    return out


def _proj_mlstm_kernel(x_ref, mod_ref, gn_ref, w_ref, cw_ref, cb_ref, bif_ref, kx_ref, gh_ref,
                       qbt_ref, kaug_ref, kmean_ref, vbt_ref, gbt_ref, oat_ref,
                       ext_ref, wnn_ref, wnt_ref,
                       qp_ref, kp_ref, vp_ref, gp_ref, tp_ref,
                       qm_ref, km_ref, vm_ref, gm_ref, tm_ref, st_ref, m_ref):
    t = pl.program_id(0)
    tm = x_ref.shape[1]
    ts = MOBA_BLOCK
    nsub = tm // ts
    cw = PROJ_CHUNK
    halo = CONV_K - 1
    L = MLSTM_CHUNK
    dh = HEAD_DIM_MLSTM
    npair = N_HEADS_MLSTM // 2
    nchunk = tm // L
    tile_p = jnp.minimum(t, _N_ROW_TILES - 1)
    jp = tile_p % _TILES_PER_SEQ
    shift = mod_ref[pl.ds(tile_p // _TILES_PER_SEQ, 1), 0:D_MODEL]
    scale = mod_ref[pl.ds(tile_p // _TILES_PER_SEQ, 1), D_MODEL:2 * D_MODEL]
    jm = jnp.maximum(t - 1, 0) % _TILES_PER_SEQ

    @pl.when(t == 0)
    def _():
        for g, name in enumerate(_NN_GROUPS):
            lo = _OFF[name][0]
            for c in range(0, 512, 256):
                wnn_ref[:, 512 * g + c:512 * g + c + 256] = jnp.transpose(
                    w_ref[lo + c:lo + c + 256, :]).astype(BF16)
        for g, name in enumerate(_NT_GROUPS):
            lo = _OFF[name][0]
            for c in range(0, 512, 256):
                r0 = _GATE_ROWS + 512 * g + c
                rows = w_ref[lo + c:lo + c + 256, :]
                if name == "qb":
                    rows = rows * (HEAD_DIM_MOBA ** -0.5)
                wnt_ref[r0:r0 + 256, :] = rows.astype(BF16)
        gi = _OFF["im"][0]
        row_w = lax.broadcasted_iota(jnp.int32, (_GATE_ROWS, w_ref.shape[1]), 0)
        wnt_ref[0:_GATE_ROWS, :] = jnp.where(
            row_w < 4, w_ref[gi:gi + _GATE_ROWS, :],
            jnp.where((row_w >= 8) & (row_w < 12), w_ref[gi - 4:gi - 4 + _GATE_ROWS, :], 0.0)).astype(BF16)
        for ref in (qp_ref, kp_ref, vp_ref, gp_ref, tp_ref, st_ref, m_ref):
            ref[...] = jnp.zeros(ref.shape, ref.dtype)

    @pl.when(jp == 0)
    def _():
        ext_ref[0:8, :] = jnp.zeros((8, 2 * D_MLSTM), F32)

    @pl.when(jm == 0)
    def _():
        st_ref[...] = jnp.zeros(st_ref.shape, F32)
        m_ref[...] = jnp.zeros(m_ref.shape, F32)

    def step(with_projection):
        qm_ref[...] = qp_ref[...]
        km_ref[...] = kp_ref[...]
        vm_ref[...] = vp_ref[...]
        gm_ref[...] = gp_ref[...]
        tm_ref[...] = tp_ref[...]

        s_idx = lax.broadcasted_iota(jnp.int32, (L, 2 * L), 0)
        t_idx = lax.broadcasted_iota(jnp.int32, (L, 2 * L), 1) & (L - 1)
        causal = s_idx <= t_idx
        lane2 = lax.broadcasted_iota(jnp.int32, (1, 2 * L), 1)
        ones_tail = jnp.ones((16, L), BF16)
        zeros_blk = jnp.zeros((L, dh), BF16)

        def pair_row(v, p):
            return jnp.concatenate([v[2 * p:2 * p + 1, :], v[2 * p + 1:2 * p + 2, :]], axis=1)

        def pair_scalar(v, p):
            return jnp.where(lane2 < L, v[2 * p:2 * p + 1, :], v[2 * p + 1:2 * p + 2, :])

        def block_diag(a, b):
            return jnp.concatenate([jnp.concatenate([a, zeros_blk], axis=1),
                                    jnp.concatenate([zeros_blk, b], axis=1)], axis=0)

        m_prev = m_ref[:, 0:1]
        rowv = []
        for c in range(nchunk):
            cs_ = slice(c * L, (c + 1) * L)
            cum_f = tm_ref[0:8, cs_]
            g = tm_ref[8:16, cs_]
            g_cmax = tm_ref[16:24, cs_]
            f_tot = cum_f[:, L - 1:L]
            g_max = g_cmax[:, L - 1:L]
            a_max = f_tot + g_max
            a_row = -jnp.maximum(m_prev, g_cmax)
            m_new = jnp.maximum(f_tot + m_prev, a_max)
            rowv.append(dict(
                ts=cs_, g=g, a_row=a_row,
                w=jnp.exp(g - g_max),
                inter=jnp.exp(a_row + m_prev),
                floor=jnp.exp(a_row - cum_f),
                s_old=jnp.exp(f_tot + m_prev - m_new), s_new=jnp.exp(a_max - m_new)))
            m_prev = m_new
        m_ref[...] = jnp.broadcast_to(m_prev, m_ref.shape)

        def q_diag(cs_, p):
            return block_diag(qm_ref[cs_, 2 * p * dh:(2 * p + 1) * dh],
                              qm_ref[cs_, (2 * p + 1) * dh:(2 * p + 2) * dh])

        s_kq, vts, carried, intra = {}, {}, {}, {}
        st = {p: st_ref[p] for p in range(npair)}
        mtasks = []

        def score_task(c, p):
            rv = rowv[c]
            k2 = km_ref[rv["ts"], 2 * p * dh:(2 * p + 2) * dh]
            g_col = jnp.concatenate([_col_bcast(rv["g"][2 * p:2 * p + 1, :], L),
                                     _col_bcast(rv["g"][2 * p + 1:2 * p + 2, :], L)], axis=1)
            decay = jnp.exp(jnp.where(causal, g_col + pair_row(rv["a_row"], p), -jnp.inf))
            s_kq[c, p] = (_dot_nt(k2, q_diag(rv["ts"], p)) * decay).astype(BF16)

        def mix_task(c, p):
            cs_ = rowv[c]["ts"]
            vt2 = jnp.concatenate(
                [jnp.concatenate([vm_ref[(2 * p + e) * dh:(2 * p + e + 1) * dh, cs_], ones_tail], axis=0)
                 for e in range(2)], axis=1)
            vts[c, p] = vt2
            carried[c, p] = _dot_nt(st[p].astype(BF16), q_diag(cs_, p))
            intra[c, p] = jnp.dot(vt2, block_diag(s_kq[c, p][:, 0:L], s_kq[c, p][:, L:2 * L]),
                                  preferred_element_type=F32)

        def finish_task(c, p):
            rv = rowv[c]
            cs_ = rv["ts"]
            mixed = intra[c, p] + pair_row(rv["inter"], p) * carried[c, p]
            den = mixed[dh:dh + 1, :]
            hh = mixed[0:dh] / jnp.maximum(jnp.abs(den), pair_row(rv["floor"], p))
            hh = hh * lax.rsqrt(jnp.mean(hh * hh, axis=0, keepdims=True) + EPS)
            for e in range(2):
                sl = slice((2 * p + e) * dh, (2 * p + e + 1) * dh)
                oat_ref[0, sl, cs_] = (hh[:, e * L:(e + 1) * L] * gh_ref[sl, :]
                                       * gm_ref[sl, cs_].astype(F32)).astype(BF16)

        def state_task(c, p):
            rv = rowv[c]
            cs_ = rv["ts"]
            vw = (vts[c, p].astype(F32) * pair_row(rv["w"], p)).astype(BF16)
            k_bd = block_diag(km_ref[cs_, 2 * p * dh:(2 * p + 1) * dh],
                              km_ref[cs_, (2 * p + 1) * dh:(2 * p + 2) * dh])
            st_loc = jnp.dot(vw, k_bd, preferred_element_type=F32)
            st[p] = pair_scalar(rv["s_old"], p) * st[p] + pair_scalar(rv["s_new"], p) * st_loc

        for c in range(nchunk):
            for p in range(npair):
                mtasks.append((score_task, c, p))
        for c in range(nchunk):
            for fn in (mix_task, finish_task, state_task):
                for p in range(npair):
                    mtasks.append((fn, c, p))

        head_start = 2 * npair
        for fn, a0, a1 in mtasks[:head_start]:
            fn(a0, a1)
        mtasks = mtasks[head_start:]

        lane = lax.broadcasted_iota(jnp.int32, (ts, 128), 1)
        hbs = []

        def norms():
            for u in range(nsub):
                x = x_ref[0, u * ts:(u + 1) * ts, :]
                ms = jnp.mean(x * x, axis=-1, keepdims=True)
                h = x * lax.rsqrt(ms + EPS) * gn_ref[...] * (1.0 + scale) + shift
                hbs.append(h.astype(BF16))

        ptasks = []
        for u in range(nsub):
            rs = slice(u * ts, (u + 1) * ts)

            def scan_gates(ift, rs=rs):
                ift = ift + bif_ref[...]
                log_i = ift[0:8]
                f_pre = ift[8:16]
                log_f = jnp.minimum(f_pre, 0.0) - jnp.log1p(jnp.exp(-jnp.abs(f_pre)))
                cum_f = _chunk_scan(log_f, jnp.add, 0.0)
                g = log_i - cum_f
                tp_ref[0:8, rs] = cum_f
                tp_ref[8:16, rs] = g
                tp_ref[16:24, rs] = _chunk_scan(g, jnp.maximum, -jnp.inf)

            def nn(lo, hi, u=u):
                return jnp.dot(hbs[u], wnn_ref[:, lo:hi], preferred_element_type=F32)

            def nt(lo, hi, u=u):
                return _dot_nt(wnt_ref[_GATE_ROWS + lo:_GATE_ROWS + hi, :], hbs[u])

            def conv_chunk(lo, _b, u=u, rs=rs, nn=nn):
                cs = slice(lo, lo + cw)
                base = 8 + u * ts
                qk = nn(lo, lo + cw)
                ext_ref[base:base + ts, cs] = qk
                conv = cb_ref[:, cs] + qk * cw_ref[CONV_K - 1:CONV_K, cs]
                for tap in range(halo):
                    conv = conv + ext_ref[pl.ds(base - halo + tap, ts), cs] * cw_ref[tap:tap + 1, cs]
                if u == nsub - 1:
                    ext_ref[0:8, cs] = qk[ts - 8:ts, :]
                act = _silu(conv)
                if lo < D_MLSTM:
                    qp_ref[rs, lo:lo + cw] = act.astype(BF16)
                else:
                    kp_ref[rs, lo - D_MLSTM:lo - D_MLSTM + cw] = (act * (HEAD_DIM_MLSTM ** -0.5)).astype(BF16)

            def value_chunk(lo, _b, rs=rs, nt=nt):
                vp_ref[lo:lo + cw, rs] = nt(1536 + lo, 1536 + lo + cw).astype(BF16)

            def gate_chunk(lo, _b, rs=rs, nt=nt):
                gp_ref[lo:lo + cw, rs] = (jax.nn.sigmoid(nt(2048 + lo, 2048 + lo + cw))
                                         * _silu(nt(2560 + lo, 2560 + lo + cw))).astype(BF16)

            def key_chunk(lo, _b, u=u, rs=rs, nn=nn):
                kb = nn(1024 + lo, 1024 + lo + cw)
                kmean_ref[0, pl.ds(jp * nsub + u, 1), lo:lo + cw] = jnp.mean(kb, axis=0, keepdims=True)
                kb2 = (kb * LOG2E).astype(BF16)
                for pr in range(lo // 128, (lo + cw) // 128):
                    for hh in range(2):
                        hd = 2 * pr + hh
                        is_key = (lane < HEAD_DIM_MOBA) if hh == 0 else (lane >= HEAD_DIM_MOBA)
                        grp = slice(128 * hd, 128 * (hd + 1))
                        kaug_ref[0, rs, grp] = jnp.where(is_key, kb2[:, 128 * pr - lo:128 * (pr + 1) - lo],
                                                         kx_ref[u, :, grp])

            def qt_chunk(lo, _b, rs=rs, nt=nt, scan_gates=scan_gates):
                if lo == 0:
                    both = nt(-_GATE_ROWS, cw)
                    scan_gates(both[0:_GATE_ROWS])
                    qt = both[_GATE_ROWS:_GATE_ROWS + cw]
                else:
                    qt = nt(lo, lo + cw)
                qbt_ref[0, lo:lo + cw, rs] = qt.astype(BF16)

            def vt_chunk(lo, _b, u=u, nt=nt):
                vbt_ref[0, u, lo:lo + cw, :] = nt(512 + lo, 512 + lo + cw).astype(BF16)

            def gt_chunk(lo, _b, rs=rs, nt=nt):
                gbt_ref[0, lo:lo + cw, rs] = _silu(nt(1024 + lo, 1024 + lo + cw)).astype(BF16)

            heavy = ([(conv_chunk, lo, 0) for lo in range(0, 2 * D_MLSTM, cw)]
                     + [(gate_chunk, lo, 0) for lo in range(0, D_MLSTM, cw)]
                     + [(key_chunk, lo, 0) for lo in range(0, D_MOBA, cw)])
            light = ([(qt_chunk, lo, 0) for lo in range(0, D_MOBA, cw)]
                     + [(vt_chunk, lo, 0) for lo in range(0, D_MOBA, cw)]
                     + [(value_chunk, lo, 0) for lo in range(0, D_MLSTM, cw)]
                     + [(gt_chunk, lo, 0) for lo in range(0, D_MOBA, cw)])
            for idx in range(max(len(heavy), len(light))):
                for group in (light, heavy):
                    if idx < len(group):
                        ptasks.append(group[idx])

        tasks = mtasks
        if with_projection:
            norms()
            tasks = _interleave(ptasks, mtasks)
        for fn, a0, a1 in tasks:
            fn(a0, a1)
        for p in range(npair):
            st_ref[p] = st[p]

    @pl.when(t < _N_ROW_TILES)
    def _():
        step(True)

    @pl.when(t == _N_ROW_TILES)
    def _():
        step(False)


def _proj_mlstm(x, mod, g_norm, w_in, conv_w, conv_b, bif, g_head):
    b, s, d = x.shape
    tm = ROW_TILE
    nb = s // MOBA_BLOCK
    nsub = tm // MOBA_BLOCK
    assert tm % MOBA_BLOCK == 0 and (b * s) // tm == _N_ROW_TILES and s // tm == _TILES_PER_SEQ
    kx = _moba_key_extras(nb)
    kw = N_HEADS_MOBA * 128
    dm = D_MLSTM
    g_bc = jnp.broadcast_to(g_head.astype(F32).reshape(dm, 1), (dm, 128))
    tps = _TILES_PER_SEQ
    last = _N_ROW_TILES - 1
    pt = lambda t: jnp.minimum(t, last)
    mt = lambda t: jnp.maximum(t - 1, 0)
    full2 = lambda t: (0, 0)
    out_shapes = (
        jax.ShapeDtypeStruct((b, D_MOBA, s), BF16),
        jax.ShapeDtypeStruct((b, s, kw), BF16),
        jax.ShapeDtypeStruct((b, nb, D_MOBA), F32),
        jax.ShapeDtypeStruct((b, nb, D_MOBA, MOBA_BLOCK), BF16),
        jax.ShapeDtypeStruct((b, D_MOBA, s), BF16),
        jax.ShapeDtypeStruct((b, dm, s), BF16),
    )
    out_specs = (
        pl.BlockSpec((1, D_MOBA, tm), lambda t: (pt(t) // tps, 0, pt(t) % tps)),
        pl.BlockSpec((1, tm, kw), lambda t: (pt(t) // tps, pt(t) % tps, 0)),
        pl.BlockSpec((1, nb, D_MOBA), lambda t: (pt(t) // tps, 0, 0)),
        pl.BlockSpec((1, nsub, D_MOBA, MOBA_BLOCK), lambda t: (pt(t) // tps, pt(t) % tps, 0, 0)),
        pl.BlockSpec((1, D_MOBA, tm), lambda t: (pt(t) // tps, 0, pt(t) % tps)),
        pl.BlockSpec((1, dm, tm), lambda t: (mt(t) // tps, 0, mt(t) % tps)),
    )
    in_specs = [
        pl.BlockSpec((1, tm, d), lambda t: (pt(t) // tps, pt(t) % tps, 0)),
        pl.BlockSpec(mod.shape, full2),
        pl.BlockSpec((1, d), full2),
        pl.BlockSpec(w_in.shape, full2, pipeline_mode=pl.Buffered(1)),
        pl.BlockSpec(conv_w.shape, full2),
        pl.BlockSpec((1, 2 * dm), full2),
        pl.BlockSpec((_GATE_ROWS, 1), full2),
        pl.BlockSpec((nsub, MOBA_BLOCK, kw), lambda t: (pt(t) % tps, 0, 0)),
        pl.BlockSpec((dm, 128), full2),
    ]
    operand_scratch = [pltpu.VMEM((tm, dm), BF16), pltpu.VMEM((tm, dm), BF16),
                       pltpu.VMEM((dm, tm), BF16), pltpu.VMEM((dm, tm), BF16),
                       pltpu.VMEM((24, tm), F32)]
    return pl.pallas_call(
        _proj_mlstm_kernel,
        grid=(_N_ROW_TILES + 1,),
        in_specs=in_specs,
        out_specs=out_specs,
        out_shape=out_shapes,
        scratch_shapes=[pltpu.VMEM((tm + 8, 2 * dm), F32),
                        pltpu.VMEM((d, 512 * len(_NN_GROUPS)), BF16),
                        pltpu.VMEM((_GATE_ROWS + 512 * len(_NT_GROUPS), d), BF16)]
        + operand_scratch + operand_scratch
        + [pltpu.VMEM((N_HEADS_MLSTM // 2, HEAD_DIM_MLSTM + 16, 2 * HEAD_DIM_MLSTM), F32),
           pltpu.VMEM((8, 128), F32)],
        compiler_params=pltpu.CompilerParams(dimension_semantics=("arbitrary",),
                                             vmem_limit_bytes=VMEM_LIMIT),
        name="proj_mlstm",
    )(x, mod, g_norm.reshape(1, d), w_in, conv_w, conv_b.reshape(1, -1), bif, kx, g_bc)


def _moba_steps(nb):
    jt, it = [], []
    for j in range(nb):
        for i in [j] + list(range(j)):
            jt.append(j)
            it.append(i)
    return jt, it


def _moba_key_extras(nb):
    blk = MOBA_BLOCK
    lane = np.arange(N_HEADS_MOBA * 128)
    head = lane // 128
    w = (lane % 128 - np.where(head % 2 == 0, HEAD_DIM_MOBA, 0))[None, None, :]
    slope = (2.0 ** -(head + 1.0))[None, None, :]
    i = np.arange(nb)[:, None, None]
    c = np.arange(blk)[None, :, None]
    tab = np.where(w == i, 1.0, 0.0)
    rest = LOG2E * slope * (i * blk + c)
    for term in range(N_POS_TERMS):
        piece = rest.astype(np.float32).astype(BF16).astype(np.float64)
        tab = tab + np.where(w == nb + term, piece, 0.0)
        rest = rest - piece
    return jnp.asarray(tab.astype(np.float32).astype(BF16))


def _moba_kernel(jt_ref, it_ref, qt_ref, k_ref, vt_ref, kmean_ref, gt_ref, x_ref, mod_ref, oat_ref, w_ref,
                 gf_ref, o_ref,
                 qaug_ref, m_ref, acc_ref, cmax_ref, wb_ref, obt_ref, vprev_ref, *s_refs):
    @pl.when(jnp.logical_and(pl.program_id(0) == 0, pl.program_id(1) == 0))
    def _():
        for r in range(0, w_ref.shape[0], 256):
            wb_ref[r:r + 256, :] = w_ref[r:r + 256, :].astype(BF16)

    t = pl.program_id(1)
    j = jt_ref[t]
    i = it_ref[t]
    blk = MOBA_BLOCK
    dh = HEAD_DIM_MOBA
    nb = kmean_ref.shape[1]
    nbat = qt_ref.shape[0]
    first = i == j
    last = jnp.logical_or(i == j - 1, j == 0)

    def key_tile(bb, h):
        return k_ref[bb, :, 128 * h:128 * (h + 1)]

    def rows(h):
        return slice(dh * h, dh * (h + 1))

    ones_tail = jnp.ones((16, blk), BF16)

    def values_t(bb, h):
        return jnp.concatenate([vt_ref[bb, 0, rows(h), :], ones_tail], axis=0)

    late = nbat - 1

    par = pl.program_id(1) & 1

    def late_values_prev(_bb, h):
        return jnp.concatenate([vprev_ref[par, rows(h), :], ones_tail], axis=0)

    def late_values_t(_bb, h):
        return jnp.concatenate([vprev_ref[1 - par, rows(h), :], ones_tail], axis=0)

    def update_phase(bb, hs, values):
        for h in hs:
            m_run = m_ref[bb, h:h + 1, :]
            m_new = jnp.maximum(m_run, cmax_ref[bb, h:h + 1, :])
            alpha = jnp.exp2(m_run - m_new)
            pr_ = jnp.exp2(s_refs[bb][h] - m_new)
            m_ref[bb, h:h + 1, :] = m_new
            acc_ref[bb, h] = alpha * acc_ref[bb, h] + jnp.dot(values(bb, h), pr_.astype(BF16),
                                                              preferred_element_type=F32)

    half = N_HEADS_MOBA // 2
    units = [(bb, range(g * half, (g + 1) * half)) for bb in range(nbat) for g in range(2)]
    lag = 2

    @pl.when(first)
    def _():
        blk_i = lax.broadcasted_iota(jnp.int32, (nb, blk), 0)
        lane_m = lax.broadcasted_iota(jnp.int32, (nb, 2 * dh), 1)
        key_c = lax.broadcasted_iota(jnp.int32, (blk, blk), 0)
        qry_r = lax.broadcasted_iota(jnp.int32, (blk, blk), 1)
        causal = key_c <= qry_r
        ones_rows = jnp.where(blk_i < N_POS_TERMS, 1.0, 0.0)
        gates = {}
        for bb in range(nbat):
            for h in range(N_HEADS_MOBA):
                pr, hh = divmod(h, 2)
                qt_pair = qt_ref[bb, 128 * pr:128 * (pr + 1), :]
                in_head = (lane_m < dh) if hh == 0 else (lane_m >= dh)
                km = jnp.where(in_head, kmean_ref[bb, :, 128 * pr:128 * (pr + 1)], 0.0).astype(BF16)
                gates[bb, h] = jnp.dot(km, qt_pair, preferred_element_type=F32)
        for bb in range(nbat):
            for h in range(N_HEADS_MOBA):
                pr, hh = divmod(h, 2)
                gate = gates[bb, h]
                rank = jnp.zeros((nb, blk), jnp.int32)
                for i2 in range(nb):
                    g2 = gate[i2:i2 + 1, :]
                    beats = (g2 > gate) | ((g2 == gate) & (i2 < blk_i))
                    rank = rank + jnp.where(beats & (i2 < j), 1, 0)
                keep = ((blk_i < j) & (rank < MOBA_TOPK)) | (blk_i == j)
                sel_bias = jnp.where(keep, 0.0, MASK_BIAS)
                extra = jnp.concatenate([sel_bias, ones_rows, jnp.zeros((dh - 16, blk), F32)], axis=0)
                extra = extra.astype(BF16)
                qt_h = qt_ref[bb, dh * h:dh * (h + 1), :]
                qaug_ref[bb, h] = jnp.concatenate([qt_h, extra] if hh == 0 else [extra, qt_h], axis=0)

        for bb in range(nbat):
            for h in range(N_HEADS_MOBA):
                s = jnp.dot(key_tile(bb, h), qaug_ref[bb, h], preferred_element_type=F32)
                s = jnp.where(causal, s, MASK_BIAS)
                s_refs[bb][h] = s
                m_ref[bb, h:h + 1, :] = jnp.max(s, axis=0, keepdims=True)

        for bb in range(late):
            for h in range(N_HEADS_MOBA):
                p0 = jnp.exp2(s_refs[bb][h] - m_ref[bb, h:h + 1, :])
                acc_ref[bb, h] = jnp.dot(values_t(bb, h), p0.astype(BF16), preferred_element_type=F32)
        cmax_ref[late] = m_ref[late]
        acc_ref[late] = jnp.zeros(acc_ref.shape[1:], F32)
        vprev_ref[1 - par] = vt_ref[late, 0]

    @pl.when(jnp.logical_not(first))
    def _():
        def score_phase(bb, hs):
            for h in hs:
                s = jnp.dot(key_tile(bb, h), qaug_ref[bb, h], preferred_element_type=F32)
                s_refs[bb][h] = s
                cmax_ref[bb, h:h + 1, :] = jnp.max(s, axis=0, keepdims=True)

        vprev_ref[1 - par] = vt_ref[late, 0]
        for slot in range(len(units)):
            score_phase(*units[slot])
            if slot < lag:
                update_phase(*units[slot - lag], late_values_prev)
            else:
                update_phase(*units[slot - lag], values_t)

    @pl.when(last)
    def _():
        def finalize(bb):
            for h in range(N_HEADS_MOBA):
                acc = acc_ref[bb, h]
                out = acc[0:dh] / acc[dh:dh + 1] * gt_ref[bb, rows(h), :].astype(F32)
                obt_ref[bb, rows(h), :] = out.astype(BF16)

        for bb in range(late):
            finalize(bb)
        for bb in range(nbat):
            if bb == late:
                finalize(bb)
            y = (_dot_tn(oat_ref[bb], wb_ref[0:D_MLSTM, :])
                 + _dot_tn(obt_ref[bb], wb_ref[D_MLSTM:D_MLSTM + D_MOBA, :]))
            if bb < lag:
                update_phase(*units[bb - lag], late_values_t)
            gate = mod_ref[pl.ds(pl.program_id(0) * nbat + bb, 1), 2 * D_MODEL:3 * D_MODEL]
            r = x_ref[bb] + gate * y
            o_ref[bb] = r * lax.rsqrt(jnp.mean(r * r, axis=-1, keepdims=True) + EPS) * gf_ref[...]


def _moba_out(qbt, kaug, vbt, kmean, gbt, x, mod, out_at, w_out, g_final):
    b, dmb, s = qbt.shape
    d = x.shape[2]
    blk = MOBA_BLOCK
    nb = s // blk
    kw = kaug.shape[2]
    nbat = MOBA_BATCH
    assert nb == 8, "selection extras assume 8 key blocks"
    jt, it = _moba_steps(nb)
    qmap = lambda bi, t, jt_r, it_r: (bi, 0, jt_r[t])
    rmap = lambda bi, t, jt_r, it_r: (bi, jt_r[t], 0)
    full2 = lambda bi, t, jt_r, it_r: (0, 0)
    grid_spec = pltpu.PrefetchScalarGridSpec(
        num_scalar_prefetch=2,
        grid=(b // nbat, len(jt)),
        in_specs=[pl.BlockSpec((nbat, dmb, blk), qmap),
                  pl.BlockSpec((nbat, blk, kw), lambda bi, t, jt_r, it_r: (bi, it_r[t], 0)),
                  pl.BlockSpec((nbat, 1, dmb, blk), lambda bi, t, jt_r, it_r: (bi, it_r[t], 0, 0)),
                  pl.BlockSpec((nbat, nb, dmb), lambda bi, t, jt_r, it_r: (bi, 0, 0)),
                  pl.BlockSpec((nbat, dmb, blk), qmap),
                  pl.BlockSpec((nbat, blk, d), rmap),
                  pl.BlockSpec(mod.shape, full2),
                  pl.BlockSpec((nbat, D_MLSTM, blk), qmap),
                  pl.BlockSpec(w_out.shape, full2, pipeline_mode=pl.Buffered(1)),
                  pl.BlockSpec((1, d), full2)],
        out_specs=pl.BlockSpec((nbat, blk, d), rmap),
        scratch_shapes=[pltpu.VMEM((nbat, N_HEADS_MOBA, 2 * HEAD_DIM_MOBA, blk), BF16),
                        pltpu.VMEM((nbat, N_HEADS_MOBA, blk), F32),
                        pltpu.VMEM((nbat, N_HEADS_MOBA, HEAD_DIM_MOBA + 16, blk), F32),
                        pltpu.VMEM((nbat, N_HEADS_MOBA, blk), F32),
                        pltpu.VMEM(w_out.shape, BF16),
                        pltpu.VMEM((nbat, dmb, blk), BF16),
                        pltpu.VMEM((2, dmb, blk), BF16)]
        + [pltpu.VMEM((N_HEADS_MOBA, blk, blk), F32) for _ in range(nbat)],
    )
    return pl.pallas_call(
        _moba_kernel,
        grid_spec=grid_spec,
        out_shape=jax.ShapeDtypeStruct((b, s, d), x.dtype),
        compiler_params=pltpu.CompilerParams(dimension_semantics=("arbitrary", "arbitrary"),
                                             vmem_limit_bytes=VMEM_LIMIT),
        name="moba_out",
    )(jnp.asarray(jt, jnp.int32), jnp.asarray(it, jnp.int32), qbt, kaug, vbt, kmean, gbt,
      x, mod, out_at, w_out, g_final.reshape(1, d))


def _layer(x, c, w_ada, b_ada, g_norm, w_in, conv_w, conv_b, b_igate, b_fgate, g_mlstm_head, w_out,
           g_final):
    mod = _ada(c, w_ada, b_ada)

    z4 = jnp.zeros((4,), F32)
    bif = jnp.concatenate([b_igate.astype(F32), z4, b_fgate.astype(F32), z4]).reshape(_GATE_ROWS, 1)

    qbt, kaug, kmean, vbt, gbt, out_at = _proj_mlstm(
        x, mod, g_norm, jnp.transpose(w_in), conv_w, conv_b, bif, g_mlstm_head)
    return _moba_out(qbt, kaug, vbt, kmean, gbt, x, mod, out_at, w_out, g_final)


def kernel(x, c, w_ada, b_ada, g_norm, w_in, conv_w, conv_b, b_igate, b_fgate, g_mlstm_head, w_out,
           g_final):
    assert w_ada.shape[0] == 1, "single-layer trunk"
    return _layer(x, c, w_ada[0], b_ada[0], g_norm[0], w_in[0], conv_w[0], conv_b[0], b_igate[0],
                  b_fgate[0], g_mlstm_head[0], w_out[0], g_final)
```

```python
import jax
import jax.numpy as jnp
import numpy as np
from jax import lax
from jax.experimental import pallas as pl
from jax.experimental.pallas import tpu as pltpu

F32 = jnp.float32
BF16 = jnp.bfloat16

D_MODEL = 1024
D_MLSTM = 512
N_HEADS_MLSTM = 4
HEAD_DIM_MLSTM = 128
D_MOBA = 512
N_HEADS_MOBA = 8
HEAD_DIM_MOBA = 64
CONV_K = 4
MOBA_BLOCK = 256
MOBA_TOPK = 3
EPS = 1e-6

ROW_TILE = 512
PROJ_CHUNK = 256
MLSTM_CHUNK = 128
MOBA_BATCH = 4
MOBA_LAG = 7
MASK_BIAS = -1e30
LOG2E = 1.4426950408889634
N_POS_TERMS = 3
VMEM_LIMIT = 56 * 1024 * 1024
_TILES_PER_SEQ = 2048 // ROW_TILE
_N_ROW_TILES = 8 * _TILES_PER_SEQ

_OFF = {}
_o = 0
for _name, _size in (("qm", 512), ("km", 512), ("vm", 512), ("om", 512), ("im", 4), ("fm", 4),
                     ("zm", 512), ("qb", 512), ("kb", 512), ("vb", 512), ("zb", 512)):
    _OFF[_name] = (_o, _o + _size)
    _o += _size
_NN_GROUPS = ("qm", "km", "kb")
_NT_GROUPS = ("qb", "vb", "zb", "vm", "om", "zm")
_GATE_ROWS = 16


def _silu(v):
    return v * jax.nn.sigmoid(v)


def _dot_nt(a, b):
    return lax.dot_general(a, b, (((1,), (1,)), ((), ())), preferred_element_type=F32)


def _dot_tn(a, b):
    return lax.dot_general(a, b, (((0,), (0,)), ((), ())), preferred_element_type=F32)


def _ada_kernel(c_ref, w_ref, b_ref, o_ref):
    a = _silu(c_ref[...])
    w = w_ref[...]
    a_hi = a.astype(BF16)
    a_lo = (a - a_hi.astype(F32)).astype(BF16)
    w_hi = w.astype(BF16)
    w_lo = (w - w_hi.astype(F32)).astype(BF16)
    acc = jnp.dot(jnp.concatenate([a_hi, a_lo], axis=0), w_hi, preferred_element_type=F32)
    n = a.shape[0]
    o_ref[...] = (acc[0:n] + acc[n:2 * n] + jnp.dot(a_hi, w_lo, preferred_element_type=F32)) + b_ref[...]


def _ada(c, w_ada, b_ada):
    b, d = c.shape
    n = w_ada.shape[1]
    tn = 512
    return pl.pallas_call(
        _ada_kernel,
        grid=(n // tn,),
        in_specs=[pl.BlockSpec((b, d), lambda i: (0, 0)),
                  pl.BlockSpec((d, tn), lambda i: (0, i)),
                  pl.BlockSpec((1, tn), lambda i: (0, i))],
        out_specs=pl.BlockSpec((b, tn), lambda i: (0, i)),
        out_shape=jax.ShapeDtypeStruct((b, n), F32),
        compiler_params=pltpu.CompilerParams(dimension_semantics=("arbitrary",),
                                             vmem_limit_bytes=VMEM_LIMIT),
        name="ada",
    )(c, w_ada, b_ada.reshape(1, n))


def _chunk_scan(v, combine, fill):
    pos = lax.broadcasted_iota(jnp.int32, v.shape, 1) & (MLSTM_CHUNK - 1)
    d = 1
    while d < MLSTM_CHUNK:
        shifted = pltpu.roll(v, d, axis=1)
        v = combine(v, jnp.where(pos >= d, shifted, fill))
        d *= 2
    return v


def _col_bcast(row, n):
    return jnp.transpose(jnp.broadcast_to(row, (n, n)))


def _interleave(major, minor):
    out, done = [], 0
    for idx, task in enumerate(major):
        out.append(task)
        want = (idx + 1) * len(minor) // len(major)
        out.extend(minor[done:want])
        done = want
    return out


def _proj_mlstm_kernel(x_ref, mod_ref, gn_ref, w_ref, cw_ref, cb_ref, bif_ref, kx_ref, gh_ref,
                       qbt_ref, kaug_ref, kmean_ref, vbt_ref, gbt_ref, oat_ref,
                       ext_ref, wnn_ref, wnt_ref,
                       qp_ref, kp_ref, vp_ref, gp_ref, tp_ref,
                       qm_ref, km_ref, vm_ref, gm_ref, tm_ref, st_ref, m_ref):
    t = pl.program_id(0)
    tm = x_ref.shape[1]
    ts = MOBA_BLOCK
    nsub = tm // ts
    cw = PROJ_CHUNK
    halo = CONV_K - 1
    L = MLSTM_CHUNK
    dh = HEAD_DIM_MLSTM
    npair = N_HEADS_MLSTM // 2
    nchunk = tm // L
    tile_p = jnp.minimum(t, _N_ROW_TILES - 1)
    jp = tile_p % _TILES_PER_SEQ
    shift = mod_ref[pl.ds(tile_p // _TILES_PER_SEQ, 1), 0:D_MODEL]
    scale = mod_ref[pl.ds(tile_p // _TILES_PER_SEQ, 1), D_MODEL:2 * D_MODEL]
    jm = jnp.maximum(t - 1, 0) % _TILES_PER_SEQ

    @pl.when(t == 0)
    def _():
        for g, name in enumerate(_NN_GROUPS):
            lo = _OFF[name][0]
            for c in range(0, 512, 256):
                wnn_ref[:, 512 * g + c:512 * g + c + 256] = jnp.transpose(
                    w_ref[lo + c:lo + c + 256, :]).astype(BF16)
        for g, name in enumerate(_NT_GROUPS):
            lo = _OFF[name][0]
            for c in range(0, 512, 256):
                r0 = _GATE_ROWS + 512 * g + c
                rows = w_ref[lo + c:lo + c + 256, :]
                if name == "qb":
                    rows = rows * (HEAD_DIM_MOBA ** -0.5)
                wnt_ref[r0:r0 + 256, :] = rows.astype(BF16)
        gi = _OFF["im"][0]
        row_w = lax.broadcasted_iota(jnp.int32, (_GATE_ROWS, w_ref.shape[1]), 0)
        wnt_ref[0:_GATE_ROWS, :] = jnp.where(
            row_w < 4, w_ref[gi:gi + _GATE_ROWS, :],
            jnp.where((row_w >= 8) & (row_w < 12), w_ref[gi - 4:gi - 4 + _GATE_ROWS, :], 0.0)).astype(BF16)
        for ref in (qp_ref, kp_ref, vp_ref, gp_ref, tp_ref, st_ref, m_ref):
            ref[...] = jnp.zeros(ref.shape, ref.dtype)

    @pl.when(jp == 0)
    def _():
        ext_ref[0:8, :] = jnp.zeros((8, 2 * D_MLSTM), F32)

    @pl.when(jm == 0)
    def _():
        st_ref[...] = jnp.zeros(st_ref.shape, F32)
        m_ref[...] = jnp.zeros(m_ref.shape, F32)

    def step(with_projection):
        qm_ref[...] = qp_ref[...]
        km_ref[...] = kp_ref[...]
        vm_ref[...] = vp_ref[...]
        gm_ref[...] = gp_ref[...]
        tm_ref[...] = tp_ref[...]

        s_idx = lax.broadcasted_iota(jnp.int32, (L, 2 * L), 0)
        t_idx = lax.broadcasted_iota(jnp.int32, (L, 2 * L), 1) & (L - 1)
        causal = s_idx <= t_idx
        lane2 = lax.broadcasted_iota(jnp.int32, (1, 2 * L), 1)
        ones_tail = jnp.ones((16, L), BF16)
        zeros_blk = jnp.zeros((L, dh), BF16)

        def pair_row(v, p):
            return jnp.concatenate([v[2 * p:2 * p + 1, :], v[2 * p + 1:2 * p + 2, :]], axis=1)

        def pair_scalar(v, p):
            return jnp.where(lane2 < L, v[2 * p:2 * p + 1, :], v[2 * p + 1:2 * p + 2, :])

        def block_diag(a, b):
            return jnp.concatenate([jnp.concatenate([a, zeros_blk], axis=1),
                                    jnp.concatenate([zeros_blk, b], axis=1)], axis=0)

        m_prev = m_ref[:, 0:1]
        rowv = []
        for c in range(nchunk):
            cs_ = slice(c * L, (c + 1) * L)
            cum_f = tm_ref[0:8, cs_]
            g = tm_ref[8:16, cs_]
            g_cmax = tm_ref[16:24, cs_]
            f_tot = cum_f[:, L - 1:L]
            g_max = g_cmax[:, L - 1:L]
            a_max = f_tot + g_max
            a_row = -jnp.maximum(m_prev, g_cmax)
            m_new = jnp.maximum(f_tot + m_prev, a_max)
            rowv.append(dict(
                ts=cs_, g=g, a_row=a_row,
                w=jnp.exp(g - g_max),
                inter=jnp.exp(a_row + m_prev),
                floor=jnp.exp(a_row - cum_f),
                s_old=jnp.exp(f_tot + m_prev - m_new), s_new=jnp.exp(a_max - m_new)))
            m_prev = m_new
        m_ref[...] = jnp.broadcast_to(m_prev, m_ref.shape)

        def q_diag(cs_, p):
            return block_diag(qm_ref[cs_, 2 * p * dh:(2 * p + 1) * dh],
                              qm_ref[cs_, (2 * p + 1) * dh:(2 * p + 2) * dh])

        s_kq, vts, carried, intra = {}, {}, {}, {}
        st = {p: st_ref[p] for p in range(npair)}
        mtasks = []

        def score_task(c, p):
            rv = rowv[c]
            k2 = km_ref[rv["ts"], 2 * p * dh:(2 * p + 2) * dh]
            g_col = jnp.concatenate([_col_bcast(rv["g"][2 * p:2 * p + 1, :], L),
                                     _col_bcast(rv["g"][2 * p + 1:2 * p + 2, :], L)], axis=1)
            decay = jnp.exp(jnp.where(causal, g_col + pair_row(rv["a_row"], p), -jnp.inf))
            s_kq[c, p] = (_dot_nt(k2, q_diag(rv["ts"], p)) * decay).astype(BF16)

        def mix_task(c, p):
            cs_ = rowv[c]["ts"]
            vt2 = jnp.concatenate(
                [jnp.concatenate([vm_ref[(2 * p + e) * dh:(2 * p + e + 1) * dh, cs_], ones_tail], axis=0)
                 for e in range(2)], axis=1)
            vts[c, p] = vt2
            carried[c, p] = _dot_nt(st[p].astype(BF16), q_diag(cs_, p))
            intra[c, p] = jnp.dot(vt2, block_diag(s_kq[c, p][:, 0:L], s_kq[c, p][:, L:2 * L]),
                                  preferred_element_type=F32)

        def finish_task(c, p):
            rv = rowv[c]
            cs_ = rv["ts"]
            mixed = intra[c, p] + pair_row(rv["inter"], p) * carried[c, p]
            den = mixed[dh:dh + 1, :]
            hh = mixed[0:dh] / jnp.maximum(jnp.abs(den), pair_row(rv["floor"], p))
            hh = hh * lax.rsqrt(jnp.mean(hh * hh, axis=0, keepdims=True) + EPS)
            for e in range(2):
                sl = slice((2 * p + e) * dh, (2 * p + e + 1) * dh)
                oat_ref[0, sl, cs_] = (hh[:, e * L:(e + 1) * L] * gh_ref[sl, :]
                                       * gm_ref[sl, cs_].astype(F32)).astype(BF16)

        def state_task(c, p):
            rv = rowv[c]
            cs_ = rv["ts"]
            vw = (vts[c, p].astype(F32) * pair_row(rv["w"], p)).astype(BF16)
            k_bd = block_diag(km_ref[cs_, 2 * p * dh:(2 * p + 1) * dh],
                              km_ref[cs_, (2 * p + 1) * dh:(2 * p + 2) * dh])
            st_loc = jnp.dot(vw, k_bd, preferred_element_type=F32)
            st[p] = pair_scalar(rv["s_old"], p) * st[p] + pair_scalar(rv["s_new"], p) * st_loc

        for c in range(nchunk):
            for p in range(npair):
                mtasks.append((score_task, c, p))
        for c in range(nchunk):
            for fn in (mix_task, finish_task, state_task):
                for p in range(npair):
                    mtasks.append((fn, c, p))

        head_start = 2 * npair
        for fn, a0, a1 in mtasks[:head_start]:
            fn(a0, a1)
        mtasks = mtasks[head_start:]

        lane = lax.broadcasted_iota(jnp.int32, (ts, 128), 1)
        hbs = []

        def norms():
            for u in range(nsub):
                x = x_ref[0, u * ts:(u + 1) * ts, :]
                ms = jnp.mean(x * x, axis=-1, keepdims=True)
                h = x * lax.rsqrt(ms + EPS) * gn_ref[...] * (1.0 + scale) + shift
                hbs.append(h.astype(BF16))

        ptasks = []
        for u in range(nsub):
            rs = slice(u * ts, (u + 1) * ts)

            def scan_gates(ift, rs=rs):
                ift = ift + bif_ref[...]
                log_i = ift[0:8]
                f_pre = ift[8:16]
                log_f = jnp.minimum(f_pre, 0.0) - jnp.log1p(jnp.exp(-jnp.abs(f_pre)))
                cum_f = _chunk_scan(log_f, jnp.add, 0.0)
                g = log_i - cum_f
                tp_ref[0:8, rs] = cum_f
                tp_ref[8:16, rs] = g
                tp_ref[16:24, rs] = _chunk_scan(g, jnp.maximum, -jnp.inf)

            def nn(lo, hi, u=u):
                return jnp.dot(hbs[u], wnn_ref[:, lo:hi], preferred_element_type=F32)

            def nt(lo, hi, u=u):
                return _dot_nt(wnt_ref[_GATE_ROWS + lo:_GATE_ROWS + hi, :], hbs[u])

            def conv_chunk(lo, _b, u=u, rs=rs, nn=nn):
                cs = slice(lo, lo + cw)
                base = 8 + u * ts
                qk = nn(lo, lo + cw)
                ext_ref[base:base + ts, cs] = qk
                conv = cb_ref[:, cs] + qk * cw_ref[CONV_K - 1:CONV_K, cs]
                for tap in range(halo):
                    conv = conv + ext_ref[pl.ds(base - halo + tap, ts), cs] * cw_ref[tap:tap + 1, cs]
                if u == nsub - 1:
                    ext_ref[0:8, cs] = qk[ts - 8:ts, :]
                act = _silu(conv)
                if lo < D_MLSTM:
                    qp_ref[rs, lo:lo + cw] = act.astype(BF16)
                else:
                    kp_ref[rs, lo - D_MLSTM:lo - D_MLSTM + cw] = (act * (HEAD_DIM_MLSTM ** -0.5)).astype(BF16)

            def value_chunk(lo, _b, rs=rs, nt=nt):
                vp_ref[lo:lo + cw, rs] = nt(1536 + lo, 1536 + lo + cw).astype(BF16)

            def gate_chunk(lo, _b, rs=rs, nt=nt):
                gp_ref[lo:lo + cw, rs] = (jax.nn.sigmoid(nt(2048 + lo, 2048 + lo + cw))
                                         * _silu(nt(2560 + lo, 2560 + lo + cw))).astype(BF16)

            def key_chunk(lo, _b, u=u, rs=rs, nn=nn):
                kb = nn(1024 + lo, 1024 + lo + cw)
                kmean_ref[0, pl.ds(jp * nsub + u, 1), lo:lo + cw] = jnp.mean(kb, axis=0, keepdims=True)
                kb2 = (kb * LOG2E).astype(BF16)
                for pr in range(lo // 128, (lo + cw) // 128):
                    for hh in range(2):
                        hd = 2 * pr + hh
                        is_key = (lane < HEAD_DIM_MOBA) if hh == 0 else (lane >= HEAD_DIM_MOBA)
                        grp = slice(128 * hd, 128 * (hd + 1))
                        kaug_ref[0, rs, grp] = jnp.where(is_key, kb2[:, 128 * pr - lo:128 * (pr + 1) - lo],
                                                         kx_ref[u, :, grp])

            def qt_chunk(lo, _b, rs=rs, nt=nt, scan_gates=scan_gates):
                if lo == 0:
                    both = nt(-_GATE_ROWS, cw)
                    scan_gates(both[0:_GATE_ROWS])
                    qt = both[_GATE_ROWS:_GATE_ROWS + cw]
                else:
                    qt = nt(lo, lo + cw)
                qbt_ref[0, lo:lo + cw, rs] = qt.astype(BF16)

            def vt_chunk(lo, _b, u=u, nt=nt):
                vbt_ref[0, u, lo:lo + cw, :] = nt(512 + lo, 512 + lo + cw).astype(BF16)

            def gt_chunk(lo, _b, rs=rs, nt=nt):
                gbt_ref[0, lo:lo + cw, rs] = _silu(nt(1024 + lo, 1024 + lo + cw)).astype(BF16)

            heavy = ([(conv_chunk, lo, 0) for lo in range(0, 2 * D_MLSTM, cw)]
                     + [(gate_chunk, lo, 0) for lo in range(0, D_MLSTM, cw)]
                     + [(key_chunk, lo, 0) for lo in range(0, D_MOBA, cw)])
            light = ([(qt_chunk, lo, 0) for lo in range(0, D_MOBA, cw)]
                     + [(vt_chunk, lo, 0) for lo in range(0, D_MOBA, cw)]
                     + [(value_chunk, lo, 0) for lo in range(0, D_MLSTM, cw)]
                     + [(gt_chunk, lo, 0) for lo in range(0, D_MOBA, cw)])
            for idx in range(max(len(heavy), len(light))):
                for group in (light, heavy):
                    if idx < len(group):
                        ptasks.append(group[idx])

        tasks = mtasks
        if with_projection:
            norms()
            tasks = _interleave(ptasks, mtasks)
        for fn, a0, a1 in tasks:
            fn(a0, a1)
        for p in range(npair):
            st_ref[p] = st[p]

    @pl.when(t < _N_ROW_TILES)
    def _():
        step(True)

    @pl.when(t == _N_ROW_TILES)
    def _():
        step(False)


def _proj_mlstm(x, mod, g_norm, w_in, conv_w, conv_b, bif, g_head):
    b, s, d = x.shape
    tm = ROW_TILE
    nb = s // MOBA_BLOCK
    nsub = tm // MOBA_BLOCK
    assert tm % MOBA_BLOCK == 0 and (b * s) // tm == _N_ROW_TILES and s // tm == _TILES_PER_SEQ
    kx = _moba_key_extras(nb)
    kw = N_HEADS_MOBA * 128
    dm = D_MLSTM
    g_bc = jnp.broadcast_to(g_head.astype(F32).reshape(dm, 1), (dm, 128))
    tps = _TILES_PER_SEQ
    last = _N_ROW_TILES - 1
    pt = lambda t: jnp.minimum(t, last)
    mt = lambda t: jnp.maximum(t - 1, 0)
    full2 = lambda t: (0, 0)
    out_shapes = (
        jax.ShapeDtypeStruct((b, D_MOBA, s), BF16),
        jax.ShapeDtypeStruct((b, s, kw), BF16),
        jax.ShapeDtypeStruct((b, nb, D_MOBA), F32),
        jax.ShapeDtypeStruct((b, nb, D_MOBA, MOBA_BLOCK), BF16),
        jax.ShapeDtypeStruct((b, D_MOBA, s), BF16),
        jax.ShapeDtypeStruct((b, dm, s), BF16),
    )
    out_specs = (
        pl.BlockSpec((1, D_MOBA, tm), lambda t: (pt(t) // tps, 0, pt(t) % tps)),
        pl.BlockSpec((1, tm, kw), lambda t: (pt(t) // tps, pt(t) % tps, 0)),
        pl.BlockSpec((1, nb, D_MOBA), lambda t: (pt(t) // tps, 0, 0)),
        pl.BlockSpec((1, nsub, D_MOBA, MOBA_BLOCK), lambda t: (pt(t) // tps, pt(t) % tps, 0, 0)),
        pl.BlockSpec((1, D_MOBA, tm), lambda t: (pt(t) // tps, 0, pt(t) % tps)),
        pl.BlockSpec((1, dm, tm), lambda t: (mt(t) // tps, 0, mt(t) % tps)),
    )
    in_specs = [
        pl.BlockSpec((1, tm, d), lambda t: (pt(t) // tps, pt(t) % tps, 0)),
        pl.BlockSpec(mod.shape, full2),
        pl.BlockSpec((1, d), full2),
        pl.BlockSpec(w_in.shape, full2, pipeline_mode=pl.Buffered(1)),
        pl.BlockSpec(conv_w.shape, full2),
        pl.BlockSpec((1, 2 * dm), full2),
        pl.BlockSpec((_GATE_ROWS, 1), full2),
        pl.BlockSpec((nsub, MOBA_BLOCK, kw), lambda t: (pt(t) % tps, 0, 0)),
        pl.BlockSpec((dm, 128), full2),
    ]
    operand_scratch = [pltpu.VMEM((tm, dm), BF16), pltpu.VMEM((tm, dm), BF16),
                       pltpu.VMEM((dm, tm), BF16), pltpu.VMEM((dm, tm), BF16),
                       pltpu.VMEM((24, tm), F32)]
    return pl.pallas_call(
        _proj_mlstm_kernel,
        grid=(_N_ROW_TILES + 1,),
        in_specs=in_specs,
        out_specs=out_specs,
        out_shape=out_shapes,
        scratch_shapes=[pltpu.VMEM((tm + 8, 2 * dm), F32),
                        pltpu.VMEM((d, 512 * len(_NN_GROUPS)), BF16),
                        pltpu.VMEM((_GATE_ROWS + 512 * len(_NT_GROUPS), d), BF16)]
        + operand_scratch + operand_scratch
        + [pltpu.VMEM((N_HEADS_MLSTM // 2, HEAD_DIM_MLSTM + 16, 2 * HEAD_DIM_MLSTM), F32),
           pltpu.VMEM((8, 128), F32)],
        compiler_params=pltpu.CompilerParams(dimension_semantics=("arbitrary",),
                                             vmem_limit_bytes=VMEM_LIMIT),
        name="proj_mlstm",
    )(x, mod, g_norm.reshape(1, d), w_in, conv_w, conv_b.reshape(1, -1), bif, kx, g_bc)


def _moba_steps(nb):
    jt, it = [], []
    for j in range(nb):
        for i in [j] + list(range(j)):
            jt.append(j)
            it.append(i)
    return jt, it


def _moba_key_extras(nb):
    blk = MOBA_BLOCK
    lane = np.arange(N_HEADS_MOBA * 128)
    head = lane // 128
    w = (lane % 128 - np.where(head % 2 == 0, HEAD_DIM_MOBA, 0))[None, None, :]
    slope = (2.0 ** -(head + 1.0))[None, None, :]
    i = np.arange(nb)[:, None, None]
    c = np.arange(blk)[None, :, None]
    tab = np.where(w == i, 1.0, 0.0)
    rest = LOG2E * slope * (i * blk + c)
    for term in range(N_POS_TERMS):
        piece = rest.astype(np.float32).astype(BF16).astype(np.float64)
        tab = tab + np.where(w == nb + term, piece, 0.0)
        rest = rest - piece
    return jnp.asarray(tab.astype(np.float32).astype(BF16))


def _moba_kernel(jt_ref, it_ref, qt_ref, k_ref, vt_ref, kmean_ref, gt_ref, x_ref, mod_ref, oat_ref, w_ref,
                 gf_ref, o_ref,
                 qaug_ref, m_ref, acc_ref, cmax_ref, wb_ref, obt_ref, *s_refs):
    @pl.when(jnp.logical_and(pl.program_id(0) == 0, pl.program_id(1) == 0))
    def _():
        for r in range(0, w_ref.shape[0], 256):
            wb_ref[r:r + 256, :] = w_ref[r:r + 256, :].astype(BF16)

    t = pl.program_id(1)
    j = jt_ref[t]
    i = it_ref[t]
    blk = MOBA_BLOCK
    dh = HEAD_DIM_MOBA
    nb = kmean_ref.shape[1]
    nbat = qt_ref.shape[0]
    first = i == j
    last = jnp.logical_or(i == j - 1, j == 0)

    def key_tile(bb, h):
        return k_ref[bb, :, 128 * h:128 * (h + 1)]

    def rows(h):
        return slice(dh * h, dh * (h + 1))

    ones_tail = jnp.ones((16, blk), BF16)

    def values_t(bb, h):
        return jnp.concatenate([vt_ref[bb, 0, rows(h), :], ones_tail], axis=0)

    @pl.when(first)
    def _():
        blk_i = lax.broadcasted_iota(jnp.int32, (nb, blk), 0)
        lane_m = lax.broadcasted_iota(jnp.int32, (nb, 2 * dh), 1)
        key_c = lax.broadcasted_iota(jnp.int32, (blk, blk), 0)
        qry_r = lax.broadcasted_iota(jnp.int32, (blk, blk), 1)
        causal = key_c <= qry_r
        ones_rows = jnp.where(blk_i < N_POS_TERMS, 1.0, 0.0)
        gates = {}
        for bb in range(nbat):
            for h in range(N_HEADS_MOBA):
                pr, hh = divmod(h, 2)
                qt_pair = qt_ref[bb, 128 * pr:128 * (pr + 1), :]
                in_head = (lane_m < dh) if hh == 0 else (lane_m >= dh)
                km = jnp.where(in_head, kmean_ref[bb, :, 128 * pr:128 * (pr + 1)], 0.0).astype(BF16)
                gates[bb, h] = jnp.dot(km, qt_pair, preferred_element_type=F32)
        for bb in range(nbat):
            for h in range(N_HEADS_MOBA):
                pr, hh = divmod(h, 2)
                gate = gates[bb, h]
                rank = jnp.zeros((nb, blk), jnp.int32)
                for i2 in range(nb):
                    g2 = gate[i2:i2 + 1, :]
                    beats = (g2 > gate) | ((g2 == gate) & (i2 < blk_i))
                    rank = rank + jnp.where(beats & (i2 < j), 1, 0)
                keep = ((blk_i < j) & (rank < MOBA_TOPK)) | (blk_i == j)
                sel_bias = jnp.where(keep, 0.0, MASK_BIAS)
                extra = jnp.concatenate([sel_bias, ones_rows, jnp.zeros((dh - 16, blk), F32)], axis=0)
                extra = extra.astype(BF16)
                qt_h = qt_ref[bb, dh * h:dh * (h + 1), :]
                qaug_ref[bb, h] = jnp.concatenate([qt_h, extra] if hh == 0 else [extra, qt_h], axis=0)

        for bb in range(nbat):
            for h in range(N_HEADS_MOBA):
                s = jnp.dot(key_tile(bb, h), qaug_ref[bb, h], preferred_element_type=F32)
                s = jnp.where(causal, s, MASK_BIAS)
                s_refs[bb][h] = s
                m_ref[bb, h:h + 1, :] = jnp.max(s, axis=0, keepdims=True)

        for bb in range(nbat):
            for h in range(N_HEADS_MOBA):
                p0 = jnp.exp2(s_refs[bb][h] - m_ref[bb, h:h + 1, :])
                acc_ref[bb, h] = jnp.dot(values_t(bb, h), p0.astype(BF16), preferred_element_type=F32)

    @pl.when(jnp.logical_not(first))
    def _():
        def score_phase(bb, h):
            s = jnp.dot(key_tile(bb, h), qaug_ref[bb, h], preferred_element_type=F32)
            s_refs[bb][h] = s
            cmax_ref[bb, h:h + 1, :] = jnp.max(s, axis=0, keepdims=True)

        def update_phase(bb, h):
            m_run = m_ref[bb, h:h + 1, :]
            m_new = jnp.maximum(m_run, cmax_ref[bb, h:h + 1, :])
            alpha = jnp.exp2(m_run - m_new)
            pr_ = jnp.exp2(s_refs[bb][h] - m_new)
            m_ref[bb, h:h + 1, :] = m_new
            acc_ref[bb, h] = alpha * acc_ref[bb, h] + jnp.dot(values_t(bb, h), pr_.astype(BF16),
                                                              preferred_element_type=F32)

        units = [(bb, h) for bb in range(nbat) for h in range(N_HEADS_MOBA)]
        for step in range(len(units) + MOBA_LAG):
            if step < len(units):
                score_phase(*units[step])
            if step >= MOBA_LAG:
                update_phase(*units[step - MOBA_LAG])

    @pl.when(last)
    def _():
        for bb in range(nbat):
            for h in range(N_HEADS_MOBA):
                acc = acc_ref[bb, h]
                out = acc[0:dh] / acc[dh:dh + 1] * gt_ref[bb, rows(h), :].astype(F32)
                obt_ref[bb, rows(h), :] = out.astype(BF16)
        for bb in range(nbat):
            y = (_dot_tn(oat_ref[bb], wb_ref[0:D_MLSTM, :])
                 + _dot_tn(obt_ref[bb], wb_ref[D_MLSTM:D_MLSTM + D_MOBA, :]))
            gate = mod_ref[pl.ds(pl.program_id(0) * nbat + bb, 1), 2 * D_MODEL:3 * D_MODEL]
            r = x_ref[bb] + gate * y
            o_ref[bb] = r * lax.rsqrt(jnp.mean(r * r, axis=-1, keepdims=True) + EPS) * gf_ref[...]


def _moba_out(qbt, kaug, vbt, kmean, gbt, x, mod, out_at, w_out, g_final):
    b, dmb, s = qbt.shape
    d = x.shape[2]
    blk = MOBA_BLOCK
    nb = s // blk
    kw = kaug.shape[2]
    nbat = MOBA_BATCH
    assert nb == 8, "selection extras assume 8 key blocks"
    jt, it = _moba_steps(nb)
    qmap = lambda bi, t, jt_r, it_r: (bi, 0, jt_r[t])
    rmap = lambda bi, t, jt_r, it_r: (bi, jt_r[t], 0)
    full2 = lambda bi, t, jt_r, it_r: (0, 0)
    grid_spec = pltpu.PrefetchScalarGridSpec(
        num_scalar_prefetch=2,
        grid=(b // nbat, len(jt)),
        in_specs=[pl.BlockSpec((nbat, dmb, blk), qmap),
                  pl.BlockSpec((nbat, blk, kw), lambda bi, t, jt_r, it_r: (bi, it_r[t], 0)),
                  pl.BlockSpec((nbat, 1, dmb, blk), lambda bi, t, jt_r, it_r: (bi, it_r[t], 0, 0)),
                  pl.BlockSpec((nbat, nb, dmb), lambda bi, t, jt_r, it_r: (bi, 0, 0)),
                  pl.BlockSpec((nbat, dmb, blk), qmap),
                  pl.BlockSpec((nbat, blk, d), rmap),
                  pl.BlockSpec(mod.shape, full2),
                  pl.BlockSpec((nbat, D_MLSTM, blk), qmap),
                  pl.BlockSpec(w_out.shape, full2, pipeline_mode=pl.Buffered(1)),
                  pl.BlockSpec((1, d), full2)],
        out_specs=pl.BlockSpec((nbat, blk, d), rmap),
        scratch_shapes=[pltpu.VMEM((nbat, N_HEADS_MOBA, 2 * HEAD_DIM_MOBA, blk), BF16),
                        pltpu.VMEM((nbat, N_HEADS_MOBA, blk), F32),
                        pltpu.VMEM((nbat, N_HEADS_MOBA, HEAD_DIM_MOBA + 16, blk), F32),
                        pltpu.VMEM((nbat, N_HEADS_MOBA, blk), F32),
                        pltpu.VMEM(w_out.shape, BF16),
                        pltpu.VMEM((nbat, dmb, blk), BF16)]
        + [pltpu.VMEM((N_HEADS_MOBA, blk, blk), F32) for _ in range(nbat)],
    )
    return pl.pallas_call(
        _moba_kernel,
        grid_spec=grid_spec,
        out_shape=jax.ShapeDtypeStruct((b, s, d), x.dtype),
        compiler_params=pltpu.CompilerParams(dimension_semantics=("arbitrary", "arbitrary"),
                                             vmem_limit_bytes=VMEM_LIMIT),
        name="moba_out",
    )(jnp.asarray(jt, jnp.int32), jnp.asarray(it, jnp.int32), qbt, kaug, vbt, kmean, gbt,
      x, mod, out_at, w_out, g_final.reshape(1, d))


def _layer(x, c, w_ada, b_ada, g_norm, w_in, conv_w, conv_b, b_igate, b_fgate, g_mlstm_head, w_out,
           g_final):
    mod = _ada(c, w_ada, b_ada)

    z4 = jnp.zeros((4,), F32)
    bif = jnp.concatenate([b_igate.astype(F32), z4, b_fgate.astype(F32), z4]).reshape(_GATE_ROWS, 1)

    qbt, kaug, kmean, vbt, gbt, out_at = _proj_mlstm(
        x, mod, g_norm, jnp.transpose(w_in), conv_w, conv_b, bif, g_mlstm_head)
    return _moba_out(qbt, kaug, vbt, kmean, gbt, x, mod, out_at, w_out, g_final)


def kernel(x, c, w_ada, b_ada, g_norm, w_in, conv_w, conv_b, b_igate, b_fgate, g_mlstm_head, w_out,
           g_final):
    assert w_ada.shape[0] == 1, "single-layer trunk"
    return _layer(x, c, w_ada[0], b_ada[0], g_norm[0], w_in[0], conv_w[0], conv_b[0], b_igate[0],
                  b_fgate[0], g_mlstm_head[0], w_out[0], g_final)
```

```python
import jax
import jax.numpy as jnp
import numpy as np
from jax import lax
from jax.experimental import pallas as pl
from jax.experimental.pallas import tpu as pltpu

F32 = jnp.float32
BF16 = jnp.bfloat16

D_MODEL = 1024
D_MLSTM = 512
N_HEADS_MLSTM = 4
HEAD_DIM_MLSTM = 128
D_MOBA = 512
N_HEADS_MOBA = 8
HEAD_DIM_MOBA = 64
CONV_K = 4
MOBA_BLOCK = 256
MOBA_TOPK = 3
EPS = 1e-6

ROW_TILE = 512
PROJ_CHUNK = 256
MLSTM_CHUNK = 128
MOBA_BATCH = 4
MOBA_LAG = 5
MASK_BIAS = -1e30
LOG2E = 1.4426950408889634
N_POS_TERMS = 3
VMEM_LIMIT = 56 * 1024 * 1024
_TILES_PER_SEQ = 2048 // ROW_TILE
_N_ROW_TILES = 8 * _TILES_PER_SEQ

_OFF = {}
_o = 0
for _name, _size in (("qm", 512), ("km", 512), ("vm", 512), ("om", 512), ("im", 4), ("fm", 4),
                     ("zm", 512), ("qb", 512), ("kb", 512), ("vb", 512), ("zb", 512)):
    _OFF[_name] = (_o, _o + _size)
    _o += _size
_NN_GROUPS = ("qm", "km", "kb")
_NT_GROUPS = ("qb", "vb", "zb", "vm", "om", "zm")
_GATE_ROWS = 16


def _silu(v):
    return v * jax.nn.sigmoid(v)


def _dot_nt(a, b):
    return lax.dot_general(a, b, (((1,), (1,)), ((), ())), preferred_element_type=F32)


def _dot_tn(a, b):
    return lax.dot_general(a, b, (((0,), (0,)), ((), ())), preferred_element_type=F32)


def _ada_kernel(c_ref, w_ref, b_ref, o_ref):
    a = _silu(c_ref[...])
    w = w_ref[...]
    a_hi = a.astype(BF16)
    a_lo = (a - a_hi.astype(F32)).astype(BF16)
    w_hi = w.astype(BF16)
    w_lo = (w - w_hi.astype(F32)).astype(BF16)
    acc = jnp.dot(jnp.concatenate([a_hi, a_lo], axis=0), w_hi, preferred_element_type=F32)
    n = a.shape[0]
    o_ref[...] = (acc[0:n] + acc[n:2 * n] + jnp.dot(a_hi, w_lo, preferred_element_type=F32)) + b_ref[...]


def _ada(c, w_ada, b_ada):
    b, d = c.shape
    n = w_ada.shape[1]
    tn = 512
    return pl.pallas_call(
        _ada_kernel,
        grid=(n // tn,),
        in_specs=[pl.BlockSpec((b, d), lambda i: (0, 0)),
                  pl.BlockSpec((d, tn), lambda i: (0, i)),
                  pl.BlockSpec((1, tn), lambda i: (0, i))],
        out_specs=pl.BlockSpec((b, tn), lambda i: (0, i)),
        out_shape=jax.ShapeDtypeStruct((b, n), F32),
        compiler_params=pltpu.CompilerParams(dimension_semantics=("arbitrary",),
                                             vmem_limit_bytes=VMEM_LIMIT),
        name="ada",
    )(c, w_ada, b_ada.reshape(1, n))


def _chunk_scan(v, combine, fill):
    pos = lax.broadcasted_iota(jnp.int32, v.shape, 1) & (MLSTM_CHUNK - 1)
    d = 1
    while d < MLSTM_CHUNK:
        shifted = pltpu.roll(v, d, axis=1)
        v = combine(v, jnp.where(pos >= d, shifted, fill))
        d *= 2
    return v


def _col_bcast(row, n):
    return jnp.transpose(jnp.broadcast_to(row, (n, n)))


def _interleave(major, minor):
    out, done = [], 0
    for idx, task in enumerate(major):
        out.append(task)
        want = (idx + 1) * len(minor) // len(major)
        out.extend(minor[done:want])
        done = want
    return out


def _proj_mlstm_kernel(x_ref, mod_ref, gn_ref, w_ref, cw_ref, cb_ref, bif_ref, kx_ref, gh_ref,
                       qbt_ref, kaug_ref, kmean_ref, vbt_ref, gbt_ref, oat_ref,
                       ext_ref, wnn_ref, wnt_ref,
                       qp_ref, kp_ref, vp_ref, gp_ref, tp_ref,
                       qm_ref, km_ref, vm_ref, gm_ref, tm_ref, st_ref, m_ref):
    t = pl.program_id(0)
    tm = x_ref.shape[1]
    ts = MOBA_BLOCK
    nsub = tm // ts
    cw = PROJ_CHUNK
    halo = CONV_K - 1
    L = MLSTM_CHUNK
    dh = HEAD_DIM_MLSTM
    npair = N_HEADS_MLSTM // 2
    nchunk = tm // L
    tile_p = jnp.minimum(t, _N_ROW_TILES - 1)
    jp = tile_p % _TILES_PER_SEQ
    shift = mod_ref[pl.ds(tile_p // _TILES_PER_SEQ, 1), 0:D_MODEL]
    scale = mod_ref[pl.ds(tile_p // _TILES_PER_SEQ, 1), D_MODEL:2 * D_MODEL]
    jm = jnp.maximum(t - 1, 0) % _TILES_PER_SEQ

    @pl.when(t == 0)
    def _():
        for g, name in enumerate(_NN_GROUPS):
            lo = _OFF[name][0]
            for c in range(0, 512, 256):
                wnn_ref[:, 512 * g + c:512 * g + c + 256] = jnp.transpose(
                    w_ref[lo + c:lo + c + 256, :]).astype(BF16)
        for g, name in enumerate(_NT_GROUPS):
            lo = _OFF[name][0]
            for c in range(0, 512, 256):
                r0 = _GATE_ROWS + 512 * g + c
                rows = w_ref[lo + c:lo + c + 256, :]
                if name == "qb":
                    rows = rows * (HEAD_DIM_MOBA ** -0.5)
                wnt_ref[r0:r0 + 256, :] = rows.astype(BF16)
        gi = _OFF["im"][0]
        row_w = lax.broadcasted_iota(jnp.int32, (_GATE_ROWS, w_ref.shape[1]), 0)
        wnt_ref[0:_GATE_ROWS, :] = jnp.where(
            row_w < 4, w_ref[gi:gi + _GATE_ROWS, :],
            jnp.where((row_w >= 8) & (row_w < 12), w_ref[gi - 4:gi - 4 + _GATE_ROWS, :], 0.0)).astype(BF16)
        for ref in (qp_ref, kp_ref, vp_ref, gp_ref, tp_ref, st_ref, m_ref):
            ref[...] = jnp.zeros(ref.shape, ref.dtype)

    @pl.when(jp == 0)
    def _():
        ext_ref[0:8, :] = jnp.zeros((8, 2 * D_MLSTM), F32)

    @pl.when(jm == 0)
    def _():
        st_ref[...] = jnp.zeros(st_ref.shape, F32)
        m_ref[...] = jnp.zeros(m_ref.shape, F32)

    def step(with_projection):
        qm_ref[...] = qp_ref[...]
        km_ref[...] = kp_ref[...]
        vm_ref[...] = vp_ref[...]
        gm_ref[...] = gp_ref[...]
        tm_ref[...] = tp_ref[...]

        s_idx = lax.broadcasted_iota(jnp.int32, (L, 2 * L), 0)
        t_idx = lax.broadcasted_iota(jnp.int32, (L, 2 * L), 1) & (L - 1)
        causal = s_idx <= t_idx
        lane2 = lax.broadcasted_iota(jnp.int32, (1, 2 * L), 1)
        ones_tail = jnp.ones((16, L), BF16)
        zeros_blk = jnp.zeros((L, dh), BF16)

        def pair_row(v, p):
            return jnp.concatenate([v[2 * p:2 * p + 1, :], v[2 * p + 1:2 * p + 2, :]], axis=1)

        def pair_scalar(v, p):
            return jnp.where(lane2 < L, v[2 * p:2 * p + 1, :], v[2 * p + 1:2 * p + 2, :])

        def block_diag(a, b):
            return jnp.concatenate([jnp.concatenate([a, zeros_blk], axis=1),
                                    jnp.concatenate([zeros_blk, b], axis=1)], axis=0)

        m_prev = m_ref[:, 0:1]
        rowv = []
        for c in range(nchunk):
            cs_ = slice(c * L, (c + 1) * L)
            cum_f = tm_ref[0:8, cs_]
            g = tm_ref[8:16, cs_]
            g_cmax = tm_ref[16:24, cs_]
            f_tot = cum_f[:, L - 1:L]
            g_max = g_cmax[:, L - 1:L]
            a_max = f_tot + g_max
            a_row = -jnp.maximum(m_prev, g_cmax)
            m_new = jnp.maximum(f_tot + m_prev, a_max)
            rowv.append(dict(
                ts=cs_, g=g, a_row=a_row,
                w=jnp.exp(g - g_max),
                inter=jnp.exp(a_row + m_prev),
                floor=jnp.exp(a_row - cum_f),
                s_old=jnp.exp(f_tot + m_prev - m_new), s_new=jnp.exp(a_max - m_new)))
            m_prev = m_new
        m_ref[...] = jnp.broadcast_to(m_prev, m_ref.shape)

        def q_diag(cs_, p):
            return block_diag(qm_ref[cs_, 2 * p * dh:(2 * p + 1) * dh],
                              qm_ref[cs_, (2 * p + 1) * dh:(2 * p + 2) * dh])

        s_kq, vts, carried, intra = {}, {}, {}, {}
        st = {p: st_ref[p] for p in range(npair)}
        mtasks = []

        def score_task(c, p):
            rv = rowv[c]
            k2 = km_ref[rv["ts"], 2 * p * dh:(2 * p + 2) * dh]
            g_col = jnp.concatenate([_col_bcast(rv["g"][2 * p:2 * p + 1, :], L),
                                     _col_bcast(rv["g"][2 * p + 1:2 * p + 2, :], L)], axis=1)
            decay = jnp.exp(jnp.where(causal, g_col + pair_row(rv["a_row"], p), -jnp.inf))
            s_kq[c, p] = (_dot_nt(k2, q_diag(rv["ts"], p)) * decay).astype(BF16)

        def mix_task(c, p):
            cs_ = rowv[c]["ts"]
            vt2 = jnp.concatenate(
                [jnp.concatenate([vm_ref[(2 * p + e) * dh:(2 * p + e + 1) * dh, cs_], ones_tail], axis=0)
                 for e in range(2)], axis=1)
            vts[c, p] = vt2
            carried[c, p] = _dot_nt(st[p].astype(BF16), q_diag(cs_, p))
            intra[c, p] = jnp.dot(vt2, block_diag(s_kq[c, p][:, 0:L], s_kq[c, p][:, L:2 * L]),
                                  preferred_element_type=F32)

        def finish_task(c, p):
            rv = rowv[c]
            cs_ = rv["ts"]
            mixed = intra[c, p] + pair_row(rv["inter"], p) * carried[c, p]
            den = mixed[dh:dh + 1, :]
            hh = mixed[0:dh] / jnp.maximum(jnp.abs(den), pair_row(rv["floor"], p))
            hh = hh * lax.rsqrt(jnp.mean(hh * hh, axis=0, keepdims=True) + EPS)
            for e in range(2):
                sl = slice((2 * p + e) * dh, (2 * p + e + 1) * dh)
                oat_ref[0, sl, cs_] = (hh[:, e * L:(e + 1) * L] * gh_ref[sl, :]
                                       * gm_ref[sl, cs_].astype(F32)).astype(BF16)

        def state_task(c, p):
            rv = rowv[c]
            cs_ = rv["ts"]
            vw = (vts[c, p].astype(F32) * pair_row(rv["w"], p)).astype(BF16)
            k_bd = block_diag(km_ref[cs_, 2 * p * dh:(2 * p + 1) * dh],
                              km_ref[cs_, (2 * p + 1) * dh:(2 * p + 2) * dh])
            st_loc = jnp.dot(vw, k_bd, preferred_element_type=F32)
            st[p] = pair_scalar(rv["s_old"], p) * st[p] + pair_scalar(rv["s_new"], p) * st_loc

        for c in range(nchunk):
            for p in range(npair):
                mtasks.append((score_task, c, p))
        for c in range(nchunk):
            for fn in (mix_task, finish_task, state_task):
                for p in range(npair):
                    mtasks.append((fn, c, p))

        head_start = 2 * npair
        for fn, a0, a1 in mtasks[:head_start]:
            fn(a0, a1)
        mtasks = mtasks[head_start:]

        lane = lax.broadcasted_iota(jnp.int32, (ts, 128), 1)
        hbs = []

        def norms():
            for u in range(nsub):
                x = x_ref[0, u * ts:(u + 1) * ts, :]
                ms = jnp.mean(x * x, axis=-1, keepdims=True)
                h = x * lax.rsqrt(ms + EPS) * gn_ref[...] * (1.0 + scale) + shift
                hbs.append(h.astype(BF16))

        ptasks = []
        for u in range(nsub):
            rs = slice(u * ts, (u + 1) * ts)

            def scan_gates(ift, rs=rs):
                ift = ift + bif_ref[...]
                log_i = ift[0:8]
                f_pre = ift[8:16]
                log_f = jnp.minimum(f_pre, 0.0) - jnp.log1p(jnp.exp(-jnp.abs(f_pre)))
                cum_f = _chunk_scan(log_f, jnp.add, 0.0)
                g = log_i - cum_f
                tp_ref[0:8, rs] = cum_f
                tp_ref[8:16, rs] = g
                tp_ref[16:24, rs] = _chunk_scan(g, jnp.maximum, -jnp.inf)

            def nn(lo, hi, u=u):
                return jnp.dot(hbs[u], wnn_ref[:, lo:hi], preferred_element_type=F32)

            def nt(lo, hi, u=u):
                return _dot_nt(wnt_ref[_GATE_ROWS + lo:_GATE_ROWS + hi, :], hbs[u])

            def conv_chunk(lo, _b, u=u, rs=rs, nn=nn):
                cs = slice(lo, lo + cw)
                base = 8 + u * ts
                qk = nn(lo, lo + cw)
                ext_ref[base:base + ts, cs] = qk
                conv = cb_ref[:, cs] + qk * cw_ref[CONV_K - 1:CONV_K, cs]
                for tap in range(halo):
                    conv = conv + ext_ref[pl.ds(base - halo + tap, ts), cs] * cw_ref[tap:tap + 1, cs]
                if u == nsub - 1:
                    ext_ref[0:8, cs] = qk[ts - 8:ts, :]
                act = _silu(conv)
                if lo < D_MLSTM:
                    qp_ref[rs, lo:lo + cw] = act.astype(BF16)
                else:
                    kp_ref[rs, lo - D_MLSTM:lo - D_MLSTM + cw] = (act * (HEAD_DIM_MLSTM ** -0.5)).astype(BF16)

            def value_chunk(lo, _b, rs=rs, nt=nt):
                vp_ref[lo:lo + cw, rs] = nt(1536 + lo, 1536 + lo + cw).astype(BF16)

            def gate_chunk(lo, _b, rs=rs, nt=nt):
                gp_ref[lo:lo + cw, rs] = (jax.nn.sigmoid(nt(2048 + lo, 2048 + lo + cw))
                                         * _silu(nt(2560 + lo, 2560 + lo + cw))).astype(BF16)

            def key_chunk(lo, _b, u=u, rs=rs, nn=nn):
                kb = nn(1024 + lo, 1024 + lo + cw)
                kmean_ref[0, pl.ds(jp * nsub + u, 1), lo:lo + cw] = jnp.mean(kb, axis=0, keepdims=True)
                kb2 = (kb * LOG2E).astype(BF16)
                for pr in range(lo // 128, (lo + cw) // 128):
                    for hh in range(2):
                        hd = 2 * pr + hh
                        is_key = (lane < HEAD_DIM_MOBA) if hh == 0 else (lane >= HEAD_DIM_MOBA)
                        grp = slice(128 * hd, 128 * (hd + 1))
                        kaug_ref[0, rs, grp] = jnp.where(is_key, kb2[:, 128 * pr - lo:128 * (pr + 1) - lo],
                                                         kx_ref[u, :, grp])

            def qt_chunk(lo, _b, rs=rs, nt=nt, scan_gates=scan_gates):
                if lo == 0:
                    both = nt(-_GATE_ROWS, cw)
                    scan_gates(both[0:_GATE_ROWS])
                    qt = both[_GATE_ROWS:_GATE_ROWS + cw]
                else:
                    qt = nt(lo, lo + cw)
                qbt_ref[0, lo:lo + cw, rs] = qt.astype(BF16)

            def vt_chunk(lo, _b, u=u, nt=nt):
                vbt_ref[0, u, lo:lo + cw, :] = nt(512 + lo, 512 + lo + cw).astype(BF16)

            def gt_chunk(lo, _b, rs=rs, nt=nt):
                gbt_ref[0, lo:lo + cw, rs] = _silu(nt(1024 + lo, 1024 + lo + cw)).astype(BF16)

            heavy = ([(conv_chunk, lo, 0) for lo in range(0, 2 * D_MLSTM, cw)]
                     + [(gate_chunk, lo, 0) for lo in range(0, D_MLSTM, cw)]
                     + [(key_chunk, lo, 0) for lo in range(0, D_MOBA, cw)])
            light = ([(qt_chunk, lo, 0) for lo in range(0, D_MOBA, cw)]
                     + [(vt_chunk, lo, 0) for lo in range(0, D_MOBA, cw)]
                     + [(value_chunk, lo, 0) for lo in range(0, D_MLSTM, cw)]
                     + [(gt_chunk, lo, 0) for lo in range(0, D_MOBA, cw)])
            for idx in range(max(len(heavy), len(light))):
                for group in (light, heavy):
                    if idx < len(group):
                        ptasks.append(group[idx])

        tasks = mtasks
        if with_projection:
            norms()
            tasks = _interleave(ptasks, mtasks)
        for fn, a0, a1 in tasks:
            fn(a0, a1)
        for p in range(npair):
            st_ref[p] = st[p]

    @pl.when(t < _N_ROW_TILES)
    def _():
        step(True)

    @pl.when(t == _N_ROW_TILES)
    def _():
        step(False)


def _proj_mlstm(x, mod, g_norm, w_in, conv_w, conv_b, bif, g_head):
    b, s, d = x.shape
    tm = ROW_TILE
    nb = s // MOBA_BLOCK
    nsub = tm // MOBA_BLOCK
    assert tm % MOBA_BLOCK == 0 and (b * s) // tm == _N_ROW_TILES and s // tm == _TILES_PER_SEQ
    kx = _moba_key_extras(nb)
    kw = N_HEADS_MOBA * 128
    dm = D_MLSTM
    g_bc = jnp.broadcast_to(g_head.astype(F32).reshape(dm, 1), (dm, 128))
    tps = _TILES_PER_SEQ
    last = _N_ROW_TILES - 1
    pt = lambda t: jnp.minimum(t, last)
    mt = lambda t: jnp.maximum(t - 1, 0)
    full2 = lambda t: (0, 0)
    out_shapes = (
        jax.ShapeDtypeStruct((b, D_MOBA, s), BF16),
        jax.ShapeDtypeStruct((b, s, kw), BF16),
        jax.ShapeDtypeStruct((b, nb, D_MOBA), F32),
        jax.ShapeDtypeStruct((b, nb, D_MOBA, MOBA_BLOCK), BF16),
        jax.ShapeDtypeStruct((b, D_MOBA, s), BF16),
        jax.ShapeDtypeStruct((b, dm, s), BF16),
    )
    out_specs = (
        pl.BlockSpec((1, D_MOBA, tm), lambda t: (pt(t) // tps, 0, pt(t) % tps)),
        pl.BlockSpec((1, tm, kw), lambda t: (pt(t) // tps, pt(t) % tps, 0)),
        pl.BlockSpec((1, nb, D_MOBA), lambda t: (pt(t) // tps, 0, 0)),
        pl.BlockSpec((1, nsub, D_MOBA, MOBA_BLOCK), lambda t: (pt(t) // tps, pt(t) % tps, 0, 0)),
        pl.BlockSpec((1, D_MOBA, tm), lambda t: (pt(t) // tps, 0, pt(t) % tps)),
        pl.BlockSpec((1, dm, tm), lambda t: (mt(t) // tps, 0, mt(t) % tps)),
    )
    in_specs = [
        pl.BlockSpec((1, tm, d), lambda t: (pt(t) // tps, pt(t) % tps, 0)),
        pl.BlockSpec(mod.shape, full2),
        pl.BlockSpec((1, d), full2),
        pl.BlockSpec(w_in.shape, full2, pipeline_mode=pl.Buffered(1)),
        pl.BlockSpec(conv_w.shape, full2),
        pl.BlockSpec((1, 2 * dm), full2),
        pl.BlockSpec((_GATE_ROWS, 1), full2),
        pl.BlockSpec((nsub, MOBA_BLOCK, kw), lambda t: (pt(t) % tps, 0, 0)),
        pl.BlockSpec((dm, 128), full2),
    ]
    operand_scratch = [pltpu.VMEM((tm, dm), BF16), pltpu.VMEM((tm, dm), BF16),
                       pltpu.VMEM((dm, tm), BF16), pltpu.VMEM((dm, tm), BF16),
                       pltpu.VMEM((24, tm), F32)]
    return pl.pallas_call(
        _proj_mlstm_kernel,
        grid=(_N_ROW_TILES + 1,),
        in_specs=in_specs,
        out_specs=out_specs,
        out_shape=out_shapes,
        scratch_shapes=[pltpu.VMEM((tm + 8, 2 * dm), F32),
                        pltpu.VMEM((d, 512 * len(_NN_GROUPS)), BF16),
                        pltpu.VMEM((_GATE_ROWS + 512 * len(_NT_GROUPS), d), BF16)]
        + operand_scratch + operand_scratch
        + [pltpu.VMEM((N_HEADS_MLSTM // 2, HEAD_DIM_MLSTM + 16, 2 * HEAD_DIM_MLSTM), F32),
           pltpu.VMEM((8, 128), F32)],
        compiler_params=pltpu.CompilerParams(dimension_semantics=("arbitrary",),
                                             vmem_limit_bytes=VMEM_LIMIT),
        name="proj_mlstm",
    )(x, mod, g_norm.reshape(1, d), w_in, conv_w, conv_b.reshape(1, -1), bif, kx, g_bc)


def _moba_steps(nb):
    jt, it = [], []
    for j in range(nb):
        for i in [j] + list(range(j)):
            jt.append(j)
            it.append(i)
    return jt, it


def _moba_key_extras(nb):
    blk = MOBA_BLOCK
    lane = np.arange(N_HEADS_MOBA * 128)
    head = lane // 128
    w = (lane % 128 - np.where(head % 2 == 0, HEAD_DIM_MOBA, 0))[None, None, :]
    slope = (2.0 ** -(head + 1.0))[None, None, :]
    i = np.arange(nb)[:, None, None]
    c = np.arange(blk)[None, :, None]
    tab = np.where(w == i, 1.0, 0.0)
    rest = LOG2E * slope * (i * blk + c)
    for term in range(N_POS_TERMS):
        piece = rest.astype(np.float32).astype(BF16).astype(np.float64)
        tab = tab + np.where(w == nb + term, piece, 0.0)
        rest = rest - piece
    return jnp.asarray(tab.astype(np.float32).astype(BF16))


def _moba_kernel(jt_ref, it_ref, qt_ref, k_ref, vt_ref, kmean_ref, gt_ref, x_ref, mod_ref, oat_ref, w_ref,
                 gf_ref, o_ref,
                 qaug_ref, m_ref, acc_ref, cmax_ref, wb_ref, obt_ref, vprev_ref, *s_refs):
    @pl.when(jnp.logical_and(pl.program_id(0) == 0, pl.program_id(1) == 0))
    def _():
        for r in range(0, w_ref.shape[0], 256):
            wb_ref[r:r + 256, :] = w_ref[r:r + 256, :].astype(BF16)

    t = pl.program_id(1)
    j = jt_ref[t]
    i = it_ref[t]
    blk = MOBA_BLOCK
    dh = HEAD_DIM_MOBA
    nb = kmean_ref.shape[1]
    nbat = qt_ref.shape[0]
    first = i == j
    last = jnp.logical_or(i == j - 1, j == 0)

    def key_tile(bb, h):
        return k_ref[bb, :, 128 * h:128 * (h + 1)]

    def rows(h):
        return slice(dh * h, dh * (h + 1))

    ones_tail = jnp.ones((16, blk), BF16)

    def values_t(bb, h):
        return jnp.concatenate([vt_ref[bb, 0, rows(h), :], ones_tail], axis=0)

    late = nbat - 1
    par = pl.program_id(1) & 1
    units = [(bb, h) for bb in range(nbat) for h in range(N_HEADS_MOBA)]
    assert MOBA_LAG <= N_HEADS_MOBA

    def late_values_prev(_bb, h):
        return jnp.concatenate([vprev_ref[par, rows(h), :], ones_tail], axis=0)

    def late_values_t(_bb, h):
        return jnp.concatenate([vprev_ref[1 - par, rows(h), :], ones_tail], axis=0)

    def update_phase(bb, h, values):
        m_run = m_ref[bb, h:h + 1, :]
        m_new = jnp.maximum(m_run, cmax_ref[bb, h:h + 1, :])
        alpha = jnp.exp2(m_run - m_new)
        pr_ = jnp.exp2(s_refs[bb][h] - m_new)
        m_ref[bb, h:h + 1, :] = m_new
        acc_ref[bb, h] = alpha * acc_ref[bb, h] + jnp.dot(values(bb, h), pr_.astype(BF16),
                                                          preferred_element_type=F32)

    @pl.when(first)
    def _():
        blk_i = lax.broadcasted_iota(jnp.int32, (nb, blk), 0)
        lane_m = lax.broadcasted_iota(jnp.int32, (nb, 2 * dh), 1)
        key_c = lax.broadcasted_iota(jnp.int32, (blk, blk), 0)
        qry_r = lax.broadcasted_iota(jnp.int32, (blk, blk), 1)
        causal = key_c <= qry_r
        ones_rows = jnp.where(blk_i < N_POS_TERMS, 1.0, 0.0)
        gates = {}
        for bb in range(nbat):
            for h in range(N_HEADS_MOBA):
                pr, hh = divmod(h, 2)
                qt_pair = qt_ref[bb, 128 * pr:128 * (pr + 1), :]
                in_head = (lane_m < dh) if hh == 0 else (lane_m >= dh)
                km = jnp.where(in_head, kmean_ref[bb, :, 128 * pr:128 * (pr + 1)], 0.0).astype(BF16)
                gates[bb, h] = jnp.dot(km, qt_pair, preferred_element_type=F32)
        for bb in range(nbat):
            for h in range(N_HEADS_MOBA):
                pr, hh = divmod(h, 2)
                gate = gates[bb, h]
                rank = jnp.zeros((nb, blk), jnp.int32)
                for i2 in range(nb):
                    g2 = gate[i2:i2 + 1, :]
                    beats = (g2 > gate) | ((g2 == gate) & (i2 < blk_i))
                    rank = rank + jnp.where(beats & (i2 < j), 1, 0)
                keep = ((blk_i < j) & (rank < MOBA_TOPK)) | (blk_i == j)
                sel_bias = jnp.where(keep, 0.0, MASK_BIAS)
                extra = jnp.concatenate([sel_bias, ones_rows, jnp.zeros((dh - 16, blk), F32)], axis=0)
                extra = extra.astype(BF16)
                qt_h = qt_ref[bb, dh * h:dh * (h + 1), :]
                qaug_ref[bb, h] = jnp.concatenate([qt_h, extra] if hh == 0 else [extra, qt_h], axis=0)

        for bb in range(nbat):
            for h in range(N_HEADS_MOBA):
                s = jnp.dot(key_tile(bb, h), qaug_ref[bb, h], preferred_element_type=F32)
                s = jnp.where(causal, s, MASK_BIAS)
                s_refs[bb][h] = s
                m_ref[bb, h:h + 1, :] = jnp.max(s, axis=0, keepdims=True)

        for bb, h in units[:-MOBA_LAG]:
            p0 = jnp.exp2(s_refs[bb][h] - m_ref[bb, h:h + 1, :])
            acc_ref[bb, h] = jnp.dot(values_t(bb, h), p0.astype(BF16), preferred_element_type=F32)
        for bb, h in units[-MOBA_LAG:]:
            cmax_ref[bb, h:h + 1, :] = m_ref[bb, h:h + 1, :]
            acc_ref[bb, h] = jnp.zeros(acc_ref.shape[2:], F32)
        vprev_ref[1 - par] = vt_ref[late, 0]

    @pl.when(jnp.logical_not(first))
    def _():
        def score_phase(bb, h):
            s = jnp.dot(key_tile(bb, h), qaug_ref[bb, h], preferred_element_type=F32)
            s_refs[bb][h] = s
            cmax_ref[bb, h:h + 1, :] = jnp.max(s, axis=0, keepdims=True)

        vprev_ref[1 - par] = vt_ref[late, 0]
        for slot in range(len(units)):
            score_phase(*units[slot])
            if slot < MOBA_LAG:
                update_phase(*units[slot - MOBA_LAG], late_values_prev)
            else:
                update_phase(*units[slot - MOBA_LAG], values_t)

    @pl.when(last)
    def _():
        for bb, h in units[-MOBA_LAG:]:
            update_phase(bb, h, late_values_t)
        for bb in range(nbat):
            for h in range(N_HEADS_MOBA):
                acc = acc_ref[bb, h]
                out = acc[0:dh] / acc[dh:dh + 1] * gt_ref[bb, rows(h), :].astype(F32)
                obt_ref[bb, rows(h), :] = out.astype(BF16)
        for bb in range(nbat):
            y = (_dot_tn(oat_ref[bb], wb_ref[0:D_MLSTM, :])
                 + _dot_tn(obt_ref[bb], wb_ref[D_MLSTM:D_MLSTM + D_MOBA, :]))
            gate = mod_ref[pl.ds(pl.program_id(0) * nbat + bb, 1), 2 * D_MODEL:3 * D_MODEL]
            r = x_ref[bb] + gate * y
            o_ref[bb] = r * lax.rsqrt(jnp.mean(r * r, axis=-1, keepdims=True) + EPS) * gf_ref[...]


def _moba_out(qbt, kaug, vbt, kmean, gbt, x, mod, out_at, w_out, g_final):
    b, dmb, s = qbt.shape
    d = x.shape[2]
    blk = MOBA_BLOCK
    nb = s // blk
    kw = kaug.shape[2]
    nbat = MOBA_BATCH
    assert nb == 8, "selection extras assume 8 key blocks"
    jt, it = _moba_steps(nb)
    qmap = lambda bi, t, jt_r, it_r: (bi, 0, jt_r[t])
    rmap = lambda bi, t, jt_r, it_r: (bi, jt_r[t], 0)
    full2 = lambda bi, t, jt_r, it_r: (0, 0)
    grid_spec = pltpu.PrefetchScalarGridSpec(
        num_scalar_prefetch=2,
        grid=(b // nbat, len(jt)),
        in_specs=[pl.BlockSpec((nbat, dmb, blk), qmap),
                  pl.BlockSpec((nbat, blk, kw), lambda bi, t, jt_r, it_r: (bi, it_r[t], 0)),
                  pl.BlockSpec((nbat, 1, dmb, blk), lambda bi, t, jt_r, it_r: (bi, it_r[t], 0, 0)),
                  pl.BlockSpec((nbat, nb, dmb), lambda bi, t, jt_r, it_r: (bi, 0, 0)),
                  pl.BlockSpec((nbat, dmb, blk), qmap),
                  pl.BlockSpec((nbat, blk, d), rmap),
                  pl.BlockSpec(mod.shape, full2),
                  pl.BlockSpec((nbat, D_MLSTM, blk), qmap),
                  pl.BlockSpec(w_out.shape, full2, pipeline_mode=pl.Buffered(1)),
                  pl.BlockSpec((1, d), full2)],
        out_specs=pl.BlockSpec((nbat, blk, d), rmap),
        scratch_shapes=[pltpu.VMEM((nbat, N_HEADS_MOBA, 2 * HEAD_DIM_MOBA, blk), BF16),
                        pltpu.VMEM((nbat, N_HEADS_MOBA, blk), F32),
                        pltpu.VMEM((nbat, N_HEADS_MOBA, HEAD_DIM_MOBA + 16, blk), F32),
                        pltpu.VMEM((nbat, N_HEADS_MOBA, blk), F32),
                        pltpu.VMEM(w_out.shape, BF16),
                        pltpu.VMEM((nbat, dmb, blk), BF16),
                        pltpu.VMEM((2, dmb, blk), BF16)]
        + [pltpu.VMEM((N_HEADS_MOBA, blk, blk), F32) for _ in range(nbat)],
    )
    return pl.pallas_call(
        _moba_kernel,
        grid_spec=grid_spec,
        out_shape=jax.ShapeDtypeStruct((b, s, d), x.dtype),
        compiler_params=pltpu.CompilerParams(dimension_semantics=("arbitrary", "arbitrary"),
                                             vmem_limit_bytes=VMEM_LIMIT),
        name="moba_out",
    )(jnp.asarray(jt, jnp.int32), jnp.asarray(it, jnp.int32), qbt, kaug, vbt, kmean, gbt,
      x, mod, out_at, w_out, g_final.reshape(1, d))


def _layer(x, c, w_ada, b_ada, g_norm, w_in, conv_w, conv_b, b_igate, b_fgate, g_mlstm_head, w_out,
           g_final):
    mod = _ada(c, w_ada, b_ada)

    z4 = jnp.zeros((4,), F32)
    bif = jnp.concatenate([b_igate.astype(F32), z4, b_fgate.astype(F32), z4]).reshape(_GATE_ROWS, 1)

    qbt, kaug, kmean, vbt, gbt, out_at = _proj_mlstm(
        x, mod, g_norm, jnp.transpose(w_in), conv_w, conv_b, bif, g_mlstm_head)
    return _moba_out(qbt, kaug, vbt, kmean, gbt, x, mod, out_at, w_out, g_final)


def kernel(x, c, w_ada, b_ada, g_norm, w_in, conv_w, conv_b, b_igate, b_fgate, g_mlstm_head, w_out,
           g_final):
    assert w_ada.shape[0] == 1, "single-layer trunk"
    return _layer(x, c, w_ada[0], b_ada[0], g_norm[0], w_in[0], conv_w[0], conv_b[0], b_igate[0],
                  b_fgate[0], g_mlstm_head[0], w_out[0], g_final)
```

```python
import jax
import jax.numpy as jnp
import numpy as np
from jax import lax
from jax.experimental import pallas as pl
from jax.experimental.pallas import tpu as pltpu

F32 = jnp.float32
BF16 = jnp.bfloat16

D_MODEL = 1024
D_MLSTM = 512
N_HEADS_MLSTM = 4
HEAD_DIM_MLSTM = 128
D_MOBA = 512
N_HEADS_MOBA = 8
HEAD_DIM_MOBA = 64
CONV_K = 4
MOBA_BLOCK = 256
MOBA_TOPK = 3
EPS = 1e-6

ROW_TILE = 512
PROJ_CHUNK = 256
MLSTM_CHUNK = 128
MOBA_BATCH = 4
MOBA_LAG = 5
MASK_BIAS = -1e30
LOG2E = 1.4426950408889634
N_POS_TERMS = 3
VMEM_LIMIT = 56 * 1024 * 1024
_TILES_PER_SEQ = 2048 // ROW_TILE
_N_ROW_TILES = 8 * _TILES_PER_SEQ

_OFF = {}
_o = 0
for _name, _size in (("qm", 512), ("km", 512), ("vm", 512), ("om", 512), ("im", 4), ("fm", 4),
                     ("zm", 512), ("qb", 512), ("kb", 512), ("vb", 512), ("zb", 512)):
    _OFF[_name] = (_o, _o + _size)
    _o += _size
_NN_GROUPS = ("qm", "km", "kb")
_NT_GROUPS = ("qb", "vb", "zb", "vm", "om", "zm")
_GATE_ROWS = 16


def _silu(v):
    return v * jax.nn.sigmoid(v)


def _dot_nt(a, b):
    return lax.dot_general(a, b, (((1,), (1,)), ((), ())), preferred_element_type=F32)


def _dot_tn(a, b):
    return lax.dot_general(a, b, (((0,), (0,)), ((), ())), preferred_element_type=F32)


def _ada_kernel(c_ref, w_ref, b_ref, o_ref):
    a = _silu(c_ref[...])
    w = w_ref[...]
    a_hi = a.astype(BF16)
    a_lo = (a - a_hi.astype(F32)).astype(BF16)
    w_hi = w.astype(BF16)
    w_lo = (w - w_hi.astype(F32)).astype(BF16)
    acc = jnp.dot(jnp.concatenate([a_hi, a_lo], axis=0), w_hi, preferred_element_type=F32)
    n = a.shape[0]
    o_ref[...] = (acc[0:n] + acc[n:2 * n] + jnp.dot(a_hi, w_lo, preferred_element_type=F32)) + b_ref[...]


def _ada(c, w_ada, b_ada):
    b, d = c.shape
    n = w_ada.shape[1]
    tn = 512
    return pl.pallas_call(
        _ada_kernel,
        grid=(n // tn,),
        in_specs=[pl.BlockSpec((b, d), lambda i: (0, 0)),
                  pl.BlockSpec((d, tn), lambda i: (0, i)),
                  pl.BlockSpec((1, tn), lambda i: (0, i))],
        out_specs=pl.BlockSpec((b, tn), lambda i: (0, i)),
        out_shape=jax.ShapeDtypeStruct((b, n), F32),
        compiler_params=pltpu.CompilerParams(dimension_semantics=("arbitrary",),
                                             vmem_limit_bytes=VMEM_LIMIT),
        name="ada",
    )(c, w_ada, b_ada.reshape(1, n))


def _chunk_scan(v, combine, fill):
    pos = lax.broadcasted_iota(jnp.int32, v.shape, 1) & (MLSTM_CHUNK - 1)
    d = 1
    while d < MLSTM_CHUNK:
        shifted = pltpu.roll(v, d, axis=1)
        v = combine(v, jnp.where(pos >= d, shifted, fill))
        d *= 2
    return v


def _col_bcast(row, n):
    return jnp.transpose(jnp.broadcast_to(row, (n, n)))


def _interleave(major, minor):
    out, done = [], 0
    for idx, task in enumerate(major):
        out.append(task)
        want = (idx + 1) * len(minor) // len(major)
        out.extend(minor[done:want])
        done = want
    return out


def _proj_mlstm_kernel(x_ref, mod_ref, gn_ref, w_ref, cw_ref, cb_ref, bif_ref, kx_ref, gh_ref,
                       qbt_ref, kaug_ref, kmean_ref, vbt_ref, gbt_ref, oat_ref,
                       ext_ref, wnn_ref, wnt_ref,
                       qp_ref, kp_ref, vp_ref, gp_ref, tp_ref,
                       qm_ref, km_ref, vm_ref, gm_ref, tm_ref, st_ref, m_ref):
    t = pl.program_id(0)
    tm = x_ref.shape[1]
    ts = MOBA_BLOCK
    nsub = tm // ts
    cw = PROJ_CHUNK
    halo = CONV_K - 1
    L = MLSTM_CHUNK
    dh = HEAD_DIM_MLSTM
    npair = N_HEADS_MLSTM // 2
    nchunk = tm // L
    tile_p = jnp.minimum(t, _N_ROW_TILES - 1)
    jp = tile_p % _TILES_PER_SEQ
    shift = mod_ref[pl.ds(tile_p // _TILES_PER_SEQ, 1), 0:D_MODEL]
    scale = mod_ref[pl.ds(tile_p // _TILES_PER_SEQ, 1), D_MODEL:2 * D_MODEL]
    jm = jnp.maximum(t - 1, 0) % _TILES_PER_SEQ

    @pl.when(t == 0)
    def _():
        for g, name in enumerate(_NN_GROUPS):
            lo = _OFF[name][0]
            for c in range(0, 512, 256):
                wnn_ref[:, 512 * g + c:512 * g + c + 256] = jnp.transpose(
                    w_ref[lo + c:lo + c + 256, :]).astype(BF16)
        for g, name in enumerate(_NT_GROUPS):
            lo = _OFF[name][0]
            for c in range(0, 512, 256):
                r0 = _GATE_ROWS + 512 * g + c
                rows = w_ref[lo + c:lo + c + 256, :]
                if name == "qb":
                    rows = rows * (HEAD_DIM_MOBA ** -0.5)
                wnt_ref[r0:r0 + 256, :] = rows.astype(BF16)
        gi = _OFF["im"][0]
        row_w = lax.broadcasted_iota(jnp.int32, (_GATE_ROWS, w_ref.shape[1]), 0)
        wnt_ref[0:_GATE_ROWS, :] = jnp.where(
            row_w < 4, w_ref[gi:gi + _GATE_ROWS, :],
            jnp.where((row_w >= 8) & (row_w < 12), w_ref[gi - 4:gi - 4 + _GATE_ROWS, :], 0.0)).astype(BF16)
        for ref in (qp_ref, kp_ref, vp_ref, gp_ref, tp_ref, st_ref, m_ref):
            ref[...] = jnp.zeros(ref.shape, ref.dtype)

    @pl.when(jp == 0)
    def _():
        ext_ref[0:8, :] = jnp.zeros((8, 2 * D_MLSTM), F32)

    @pl.when(jm == 0)
    def _():
        st_ref[...] = jnp.zeros(st_ref.shape, F32)
        m_ref[...] = jnp.zeros(m_ref.shape, F32)

    def step(with_projection):
        qm_ref[...] = qp_ref[...]
        km_ref[...] = kp_ref[...]
        vm_ref[...] = vp_ref[...]
        gm_ref[...] = gp_ref[...]
        tm_ref[...] = tp_ref[...]

        s_idx = lax.broadcasted_iota(jnp.int32, (L, 2 * L), 0)
        t_idx = lax.broadcasted_iota(jnp.int32, (L, 2 * L), 1) & (L - 1)
        causal = s_idx <= t_idx
        lane2 = lax.broadcasted_iota(jnp.int32, (1, 2 * L), 1)
        ones_tail = jnp.ones((16, L), BF16)
        zeros_blk = jnp.zeros((L, dh), BF16)

        def pair_row(v, p):
            return jnp.concatenate([v[2 * p:2 * p + 1, :], v[2 * p + 1:2 * p + 2, :]], axis=1)

        def pair_scalar(v, p):
            return jnp.where(lane2 < L, v[2 * p:2 * p + 1, :], v[2 * p + 1:2 * p + 2, :])

        def block_diag(a, b):
            return jnp.concatenate([jnp.concatenate([a, zeros_blk], axis=1),
                                    jnp.concatenate([zeros_blk, b], axis=1)], axis=0)

        m_prev = m_ref[:, 0:1]
        rowv = []
        for c in range(nchunk):
            cs_ = slice(c * L, (c + 1) * L)
            cum_f = tm_ref[0:8, cs_]
            g = tm_ref[8:16, cs_]
            g_cmax = tm_ref[16:24, cs_]
            f_tot = cum_f[:, L - 1:L]
            g_max = g_cmax[:, L - 1:L]
            a_max = f_tot + g_max
            a_row = -jnp.maximum(m_prev, g_cmax)
            m_new = jnp.maximum(f_tot + m_prev, a_max)
            rowv.append(dict(
                ts=cs_, g=g, a_row=a_row,
                w=jnp.exp(g - g_max),
                inter=jnp.exp(a_row + m_prev),
                floor=jnp.exp(a_row - cum_f),
                s_old=jnp.exp(f_tot + m_prev - m_new), s_new=jnp.exp(a_max - m_new)))
            m_prev = m_new
        m_ref[...] = jnp.broadcast_to(m_prev, m_ref.shape)

        def q_diag(cs_, p):
            return block_diag(qm_ref[cs_, 2 * p * dh:(2 * p + 1) * dh],
                              qm_ref[cs_, (2 * p + 1) * dh:(2 * p + 2) * dh])

        s_kq, vts, carried, intra = {}, {}, {}, {}
        st = {p: st_ref[p] for p in range(npair)}
        mtasks = []

        def score_task(c, p):
            rv = rowv[c]
            k2 = km_ref[rv["ts"], 2 * p * dh:(2 * p + 2) * dh]
            g_col = jnp.concatenate([_col_bcast(rv["g"][2 * p:2 * p + 1, :], L),
                                     _col_bcast(rv["g"][2 * p + 1:2 * p + 2, :], L)], axis=1)
            decay = jnp.exp(jnp.where(causal, g_col + pair_row(rv["a_row"], p), -jnp.inf))
            s_kq[c, p] = (_dot_nt(k2, q_diag(rv["ts"], p)) * decay).astype(BF16)

        def mix_task(c, p):
            cs_ = rowv[c]["ts"]
            vt2 = jnp.concatenate(
                [jnp.concatenate([vm_ref[(2 * p + e) * dh:(2 * p + e + 1) * dh, cs_], ones_tail], axis=0)
                 for e in range(2)], axis=1)
            vts[c, p] = vt2
            carried[c, p] = _dot_nt(st[p].astype(BF16), q_diag(cs_, p))
            intra[c, p] = jnp.dot(vt2, block_diag(s_kq[c, p][:, 0:L], s_kq[c, p][:, L:2 * L]),
                                  preferred_element_type=F32)

        def finish_task(c, p):
            rv = rowv[c]
            cs_ = rv["ts"]
            mixed = intra[c, p] + pair_row(rv["inter"], p) * carried[c, p]
            den = mixed[dh:dh + 1, :]
            hh = mixed[0:dh] / jnp.maximum(jnp.abs(den), pair_row(rv["floor"], p))
            hh = hh * lax.rsqrt(jnp.mean(hh * hh, axis=0, keepdims=True) + EPS)
            for e in range(2):
                sl = slice((2 * p + e) * dh, (2 * p + e + 1) * dh)
                oat_ref[0, sl, cs_] = (hh[:, e * L:(e + 1) * L] * gh_ref[sl, :]
                                       * gm_ref[sl, cs_].astype(F32)).astype(BF16)

        def state_task(c, p):
            rv = rowv[c]
            cs_ = rv["ts"]
            vw = (vts[c, p].astype(F32) * pair_row(rv["w"], p)).astype(BF16)
            k_bd = block_diag(km_ref[cs_, 2 * p * dh:(2 * p + 1) * dh],
                              km_ref[cs_, (2 * p + 1) * dh:(2 * p + 2) * dh])
            st_loc = jnp.dot(vw, k_bd, preferred_element_type=F32)
            st[p] = pair_scalar(rv["s_old"], p) * st[p] + pair_scalar(rv["s_new"], p) * st_loc

        for c in range(nchunk):
            for p in range(npair):
                mtasks.append((score_task, c, p))
        for c in range(nchunk):
            for fn in (mix_task, finish_task, state_task):
                for p in range(npair):
                    mtasks.append((fn, c, p))

        lane = lax.broadcasted_iota(jnp.int32, (ts, 128), 1)
        hbs = []

        def norms():
            for u in range(nsub):
                x = x_ref[0, u * ts:(u + 1) * ts, :]
                ms = jnp.mean(x * x, axis=-1, keepdims=True)
                h = x * lax.rsqrt(ms + EPS) * gn_ref[...] * (1.0 + scale) + shift
                hbs.append(h.astype(BF16))

        ptasks = []
        for u in range(nsub):
            rs = slice(u * ts, (u + 1) * ts)

            def scan_gates(ift, rs=rs):
                ift = ift + bif_ref[...]
                log_i = ift[0:8]
                f_pre = ift[8:16]
                log_f = jnp.minimum(f_pre, 0.0) - jnp.log1p(jnp.exp(-jnp.abs(f_pre)))
                cum_f = _chunk_scan(log_f, jnp.add, 0.0)
                g = log_i - cum_f
                tp_ref[0:8, rs] = cum_f
                tp_ref[8:16, rs] = g
                tp_ref[16:24, rs] = _chunk_scan(g, jnp.maximum, -jnp.inf)

            def nn(lo, hi, u=u):
                return jnp.dot(hbs[u], wnn_ref[:, lo:hi], preferred_element_type=F32)

            def nt(lo, hi, u=u):
                return _dot_nt(wnt_ref[_GATE_ROWS + lo:_GATE_ROWS + hi, :], hbs[u])

            def conv_chunk(lo, _b, u=u, rs=rs, nn=nn):
                cs = slice(lo, lo + cw)
                base = 8 + u * ts
                qk = nn(lo, lo + cw)
                ext_ref[base:base + ts, cs] = qk
                conv = cb_ref[:, cs] + qk * cw_ref[CONV_K - 1:CONV_K, cs]
                for tap in range(halo):
                    conv = conv + ext_ref[pl.ds(base - halo + tap, ts), cs] * cw_ref[tap:tap + 1, cs]
                if u == nsub - 1:
                    ext_ref[0:8, cs] = qk[ts - 8:ts, :]
                act = _silu(conv)
                if lo < D_MLSTM:
                    qp_ref[rs, lo:lo + cw] = act.astype(BF16)
                else:
                    kp_ref[rs, lo - D_MLSTM:lo - D_MLSTM + cw] = (act * (HEAD_DIM_MLSTM ** -0.5)).astype(BF16)

            def value_chunk(lo, _b, rs=rs, nt=nt):
                vp_ref[lo:lo + cw, rs] = nt(1536 + lo, 1536 + lo + cw).astype(BF16)

            def gate_chunk(lo, _b, rs=rs, nt=nt):
                gp_ref[lo:lo + cw, rs] = (jax.nn.sigmoid(nt(2048 + lo, 2048 + lo + cw))
                                         * _silu(nt(2560 + lo, 2560 + lo + cw))).astype(BF16)

            def key_chunk(lo, _b, u=u, rs=rs, nn=nn):
                kb = nn(1024 + lo, 1024 + lo + cw)
                kmean_ref[0, pl.ds(jp * nsub + u, 1), lo:lo + cw] = jnp.mean(kb, axis=0, keepdims=True)
                kb2 = (kb * LOG2E).astype(BF16)
                for pr in range(lo // 128, (lo + cw) // 128):
                    for hh in range(2):
                        hd = 2 * pr + hh
                        is_key = (lane < HEAD_DIM_MOBA) if hh == 0 else (lane >= HEAD_DIM_MOBA)
                        grp = slice(128 * hd, 128 * (hd + 1))
                        kaug_ref[0, rs, grp] = jnp.where(is_key, kb2[:, 128 * pr - lo:128 * (pr + 1) - lo],
                                                         kx_ref[u, :, grp])

            def qt_chunk(lo, _b, rs=rs, nt=nt, scan_gates=scan_gates):
                if lo == 0:
                    both = nt(-_GATE_ROWS, cw)
                    scan_gates(both[0:_GATE_ROWS])
                    qt = both[_GATE_ROWS:_GATE_ROWS + cw]
                else:
                    qt = nt(lo, lo + cw)
                qbt_ref[0, lo:lo + cw, rs] = qt.astype(BF16)

            def vt_chunk(lo, _b, u=u, nt=nt):
                vbt_ref[0, u, lo:lo + cw, :] = nt(512 + lo, 512 + lo + cw).astype(BF16)

            def gt_chunk(lo, _b, rs=rs, nt=nt):
                gbt_ref[0, lo:lo + cw, rs] = _silu(nt(1024 + lo, 1024 + lo + cw)).astype(BF16)

            heavy = ([(conv_chunk, lo, 0) for lo in range(0, 2 * D_MLSTM, cw)]
                     + [(gate_chunk, lo, 0) for lo in range(0, D_MLSTM, cw)]
                     + [(key_chunk, lo, 0) for lo in range(0, D_MOBA, cw)])
            light = ([(qt_chunk, lo, 0) for lo in range(0, D_MOBA, cw)]
                     + [(vt_chunk, lo, 0) for lo in range(0, D_MOBA, cw)]
                     + [(value_chunk, lo, 0) for lo in range(0, D_MLSTM, cw)]
                     + [(gt_chunk, lo, 0) for lo in range(0, D_MOBA, cw)])
            for idx in range(max(len(heavy), len(light))):
                for group in (light, heavy):
                    if idx < len(group):
                        ptasks.append(group[idx])

        tasks = mtasks
        if with_projection:
            norms()
            tasks = _interleave(ptasks, mtasks)
        for fn, a0, a1 in tasks:
            fn(a0, a1)
        for p in range(npair):
            st_ref[p] = st[p]

    @pl.when(t < _N_ROW_TILES)
    def _():
        step(True)

    @pl.when(t == _N_ROW_TILES)
    def _():
        step(False)


def _proj_mlstm(x, mod, g_norm, w_in, conv_w, conv_b, bif, g_head):
    b, s, d = x.shape
    tm = ROW_TILE
    nb = s // MOBA_BLOCK
    nsub = tm // MOBA_BLOCK
    assert tm % MOBA_BLOCK == 0 and (b * s) // tm == _N_ROW_TILES and s // tm == _TILES_PER_SEQ
    kx = _moba_key_extras(nb)
    kw = N_HEADS_MOBA * 128
    dm = D_MLSTM
    g_bc = jnp.broadcast_to(g_head.astype(F32).reshape(dm, 1), (dm, 128))
    tps = _TILES_PER_SEQ
    last = _N_ROW_TILES - 1
    pt = lambda t: jnp.minimum(t, last)
    mt = lambda t: jnp.maximum(t - 1, 0)
    full2 = lambda t: (0, 0)
    out_shapes = (
        jax.ShapeDtypeStruct((b, D_MOBA, s), BF16),
        jax.ShapeDtypeStruct((b, s, kw), BF16),
        jax.ShapeDtypeStruct((b, nb, D_MOBA), F32),
        jax.ShapeDtypeStruct((b, nb, D_MOBA, MOBA_BLOCK), BF16),
        jax.ShapeDtypeStruct((b, D_MOBA, s), BF16),
        jax.ShapeDtypeStruct((b, dm, s), BF16),
    )
    out_specs = (
        pl.BlockSpec((1, D_MOBA, tm), lambda t: (pt(t) // tps, 0, pt(t) % tps)),
        pl.BlockSpec((1, tm, kw), lambda t: (pt(t) // tps, pt(t) % tps, 0)),
        pl.BlockSpec((1, nb, D_MOBA), lambda t: (pt(t) // tps, 0, 0)),
        pl.BlockSpec((1, nsub, D_MOBA, MOBA_BLOCK), lambda t: (pt(t) // tps, pt(t) % tps, 0, 0)),
        pl.BlockSpec((1, D_MOBA, tm), lambda t: (pt(t) // tps, 0, pt(t) % tps)),
        pl.BlockSpec((1, dm, tm), lambda t: (mt(t) // tps, 0, mt(t) % tps)),
    )
    in_specs = [
        pl.BlockSpec((1, tm, d), lambda t: (pt(t) // tps, pt(t) % tps, 0)),
        pl.BlockSpec(mod.shape, full2),
        pl.BlockSpec((1, d), full2),
        pl.BlockSpec(w_in.shape, full2, pipeline_mode=pl.Buffered(1)),
        pl.BlockSpec(conv_w.shape, full2),
        pl.BlockSpec((1, 2 * dm), full2),
        pl.BlockSpec((_GATE_ROWS, 1), full2),
        pl.BlockSpec((nsub, MOBA_BLOCK, kw), lambda t: (pt(t) % tps, 0, 0)),
        pl.BlockSpec((dm, 128), full2),
    ]
    operand_scratch = [pltpu.VMEM((tm, dm), BF16), pltpu.VMEM((tm, dm), BF16),
                       pltpu.VMEM((dm, tm), BF16), pltpu.VMEM((dm, tm), BF16),
                       pltpu.VMEM((24, tm), F32)]
    return pl.pallas_call(
        _proj_mlstm_kernel,
        grid=(_N_ROW_TILES + 1,),
        in_specs=in_specs,
        out_specs=out_specs,
        out_shape=out_shapes,
        scratch_shapes=[pltpu.VMEM((tm + 8, 2 * dm), F32),
                        pltpu.VMEM((d, 512 * len(_NN_GROUPS)), BF16),
                        pltpu.VMEM((_GATE_ROWS + 512 * len(_NT_GROUPS), d), BF16)]
        + operand_scratch + operand_scratch
        + [pltpu.VMEM((N_HEADS_MLSTM // 2, HEAD_DIM_MLSTM + 16, 2 * HEAD_DIM_MLSTM), F32),
           pltpu.VMEM((8, 128), F32)],
        compiler_params=pltpu.CompilerParams(dimension_semantics=("arbitrary",),
                                             vmem_limit_bytes=VMEM_LIMIT),
        name="proj_mlstm",
    )(x, mod, g_norm.reshape(1, d), w_in, conv_w, conv_b.reshape(1, -1), bif, kx, g_bc)


def _moba_steps(nb):
    jt, it = [], []
    for j in range(nb):
        for i in [j] + list(range(j)):
            jt.append(j)
            it.append(i)
    return jt, it


def _moba_key_extras(nb):
    blk = MOBA_BLOCK
    lane = np.arange(N_HEADS_MOBA * 128)
    head = lane // 128
    w = (lane % 128 - np.where(head % 2 == 0, HEAD_DIM_MOBA, 0))[None, None, :]
    slope = (2.0 ** -(head + 1.0))[None, None, :]
    i = np.arange(nb)[:, None, None]
    c = np.arange(blk)[None, :, None]
    tab = np.where(w == i, 1.0, 0.0)
    rest = LOG2E * slope * (i * blk + c)
    for term in range(N_POS_TERMS):
        piece = rest.astype(np.float32).astype(BF16).astype(np.float64)
        tab = tab + np.where(w == nb + term, piece, 0.0)
        rest = rest - piece
    return jnp.asarray(tab.astype(np.float32).astype(BF16))


def _moba_kernel(jt_ref, it_ref, qt_ref, k_ref, vt_ref, kmean_ref, gt_ref, x_ref, mod_ref, oat_ref, w_ref,
                 gf_ref, o_ref,
                 qaug_ref, m_ref, acc_ref, cmax_ref, wb_ref, obt_ref, vprev_ref, *s_refs):
    @pl.when(jnp.logical_and(pl.program_id(0) == 0, pl.program_id(1) == 0))
    def _():
        for r in range(0, w_ref.shape[0], 256):
            wb_ref[r:r + 256, :] = w_ref[r:r + 256, :].astype(BF16)

    t = pl.program_id(1)
    j = jt_ref[t]
    i = it_ref[t]
    blk = MOBA_BLOCK
    dh = HEAD_DIM_MOBA
    nb = kmean_ref.shape[1]
    nbat = qt_ref.shape[0]
    first = i == j
    last = jnp.logical_or(i == j - 1, j == 0)

    def key_tile(bb, h):
        return k_ref[bb, :, 128 * h:128 * (h + 1)]

    def rows(h):
        return slice(dh * h, dh * (h + 1))

    ones_tail = jnp.ones((16, blk), BF16)

    def values_t(bb, h):
        return jnp.concatenate([vt_ref[bb, 0, rows(h), :], ones_tail], axis=0)

    late = nbat - 1
    par = pl.program_id(1) & 1
    units = [(bb, h) for bb in range(nbat) for h in range(N_HEADS_MOBA)]
    assert MOBA_LAG <= N_HEADS_MOBA

    def late_values_prev(_bb, h):
        return jnp.concatenate([vprev_ref[par, rows(h), :], ones_tail], axis=0)

    def late_values_t(_bb, h):
        return jnp.concatenate([vprev_ref[1 - par, rows(h), :], ones_tail], axis=0)

    def update_phase(bb, h, values):
        m_run = m_ref[bb, h:h + 1, :]
        m_new = jnp.maximum(m_run, cmax_ref[bb, h:h + 1, :])
        alpha = jnp.exp2(m_run - m_new)
        pr_ = jnp.exp2(s_refs[bb][h] - m_new)
        m_ref[bb, h:h + 1, :] = m_new
        acc_ref[bb, h] = alpha * acc_ref[bb, h] + jnp.dot(values(bb, h), pr_.astype(BF16),
                                                          preferred_element_type=F32)

    @pl.when(first)
    def _():
        blk_i = lax.broadcasted_iota(jnp.int32, (nb, blk), 0)
        lane_m = lax.broadcasted_iota(jnp.int32, (nb, 2 * dh), 1)
        key_c = lax.broadcasted_iota(jnp.int32, (blk, blk), 0)
        qry_r = lax.broadcasted_iota(jnp.int32, (blk, blk), 1)
        causal = key_c <= qry_r
        ones_rows = jnp.where(blk_i < N_POS_TERMS, 1.0, 0.0)
        gates = {}
        for bb in range(nbat):
            for h in range(N_HEADS_MOBA):
                pr, hh = divmod(h, 2)
                qt_pair = qt_ref[bb, 128 * pr:128 * (pr + 1), :]
                in_head = (lane_m < dh) if hh == 0 else (lane_m >= dh)
                km = jnp.where(in_head, kmean_ref[bb, :, 128 * pr:128 * (pr + 1)], 0.0).astype(BF16)
                gates[bb, h] = jnp.dot(km, qt_pair, preferred_element_type=F32)
        for bb in range(nbat):
            for h in range(N_HEADS_MOBA):
                pr, hh = divmod(h, 2)
                gate = gates[bb, h]
                rank = jnp.zeros((nb, blk), jnp.int32)
                for i2 in range(nb):
                    g2 = gate[i2:i2 + 1, :]
                    beats = (g2 > gate) | ((g2 == gate) & (i2 < blk_i))
                    rank = rank + jnp.where(beats & (i2 < j), 1, 0)
                keep = ((blk_i < j) & (rank < MOBA_TOPK)) | (blk_i == j)
                sel_bias = jnp.where(keep, 0.0, MASK_BIAS)
                extra = jnp.concatenate([sel_bias, ones_rows, jnp.zeros((dh - 16, blk), F32)], axis=0)
                extra = extra.astype(BF16)
                qt_h = qt_ref[bb, dh * h:dh * (h + 1), :]
                qaug_ref[bb, h] = jnp.concatenate([qt_h, extra] if hh == 0 else [extra, qt_h], axis=0)

        for bb in range(nbat):
            for h in range(N_HEADS_MOBA):
                s = jnp.dot(key_tile(bb, h), qaug_ref[bb, h], preferred_element_type=F32)
                s = jnp.where(causal, s, MASK_BIAS)
                s_refs[bb][h] = s
                m_ref[bb, h:h + 1, :] = jnp.max(s, axis=0, keepdims=True)

        for bb, h in units[:-MOBA_LAG]:
            p0 = jnp.exp2(s_refs[bb][h] - m_ref[bb, h:h + 1, :])
            acc_ref[bb, h] = jnp.dot(values_t(bb, h), p0.astype(BF16), preferred_element_type=F32)
        for bb, h in units[-MOBA_LAG:]:
            cmax_ref[bb, h:h + 1, :] = m_ref[bb, h:h + 1, :]
            acc_ref[bb, h] = jnp.zeros(acc_ref.shape[2:], F32)
        vprev_ref[1 - par] = vt_ref[late, 0]

    @pl.when(jnp.logical_not(first))
    def _():
        def score_phase(bb, h):
            s = jnp.dot(key_tile(bb, h), qaug_ref[bb, h], preferred_element_type=F32)
            s_refs[bb][h] = s
            cmax_ref[bb, h:h + 1, :] = jnp.max(s, axis=0, keepdims=True)

        vprev_ref[1 - par] = vt_ref[late, 0]
        for slot in range(len(units)):
            score_phase(*units[slot])
            if slot < MOBA_LAG:
                update_phase(*units[slot - MOBA_LAG], late_values_prev)
            else:
                update_phase(*units[slot - MOBA_LAG], values_t)

    @pl.when(last)
    def _():
        for bb, h in units[-MOBA_LAG:]:
            update_phase(bb, h, late_values_t)
        for bb in range(nbat):
            for h in range(N_HEADS_MOBA):
                acc = acc_ref[bb, h]
                out = acc[0:dh] / acc[dh:dh + 1] * gt_ref[bb, rows(h), :].astype(F32)
                obt_ref[bb, rows(h), :] = out.astype(BF16)
        for bb in range(nbat):
            y = (_dot_tn(oat_ref[bb], wb_ref[0:D_MLSTM, :])
                 + _dot_tn(obt_ref[bb], wb_ref[D_MLSTM:D_MLSTM + D_MOBA, :]))
            gate = mod_ref[pl.ds(pl.program_id(0) * nbat + bb, 1), 2 * D_MODEL:3 * D_MODEL]
            r = x_ref[bb] + gate * y
            o_ref[bb] = r * lax.rsqrt(jnp.mean(r * r, axis=-1, keepdims=True) + EPS) * gf_ref[...]


def _moba_out(qbt, kaug, vbt, kmean, gbt, x, mod, out_at, w_out, g_final):
    b, dmb, s = qbt.shape
    d = x.shape[2]
    blk = MOBA_BLOCK
    nb = s // blk
    kw = kaug.shape[2]
    nbat = MOBA_BATCH
    assert nb == 8, "selection extras assume 8 key blocks"
    jt, it = _moba_steps(nb)
    qmap = lambda bi, t, jt_r, it_r: (bi, 0, jt_r[t])
    rmap = lambda bi, t, jt_r, it_r: (bi, jt_r[t], 0)
    full2 = lambda bi, t, jt_r, it_r: (0, 0)
    grid_spec = pltpu.PrefetchScalarGridSpec(
        num_scalar_prefetch=2,
        grid=(b // nbat, len(jt)),
        in_specs=[pl.BlockSpec((nbat, dmb, blk), qmap),
                  pl.BlockSpec((nbat, blk, kw), lambda bi, t, jt_r, it_r: (bi, it_r[t], 0)),
                  pl.BlockSpec((nbat, 1, dmb, blk), lambda bi, t, jt_r, it_r: (bi, it_r[t], 0, 0)),
                  pl.BlockSpec((nbat, nb, dmb), lambda bi, t, jt_r, it_r: (bi, 0, 0)),
                  pl.BlockSpec((nbat, dmb, blk), qmap),
                  pl.BlockSpec((nbat, blk, d), rmap),
                  pl.BlockSpec(mod.shape, full2),
                  pl.BlockSpec((nbat, D_MLSTM, blk), qmap),
                  pl.BlockSpec(w_out.shape, full2, pipeline_mode=pl.Buffered(1)),
                  pl.BlockSpec((1, d), full2)],
        out_specs=pl.BlockSpec((nbat, blk, d), rmap),
        scratch_shapes=[pltpu.VMEM((nbat, N_HEADS_MOBA, 2 * HEAD_DIM_MOBA, blk), BF16),
                        pltpu.VMEM((nbat, N_HEADS_MOBA, blk), F32),
                        pltpu.VMEM((nbat, N_HEADS_MOBA, HEAD_DIM_MOBA + 16, blk), F32),
                        pltpu.VMEM((nbat, N_HEADS_MOBA, blk), F32),
                        pltpu.VMEM(w_out.shape, BF16),
                        pltpu.VMEM((nbat, dmb, blk), BF16),
                        pltpu.VMEM((2, dmb, blk), BF16)]
        + [pltpu.VMEM((N_HEADS_MOBA, blk, blk), F32) for _ in range(nbat)],
    )
    return pl.pallas_call(
        _moba_kernel,
        grid_spec=grid_spec,
        out_shape=jax.ShapeDtypeStruct((b, s, d), x.dtype),
        compiler_params=pltpu.CompilerParams(dimension_semantics=("arbitrary", "arbitrary"),
                                             vmem_limit_bytes=VMEM_LIMIT),
        name="moba_out",
    )(jnp.asarray(jt, jnp.int32), jnp.asarray(it, jnp.int32), qbt, kaug, vbt, kmean, gbt,
      x, mod, out_at, w_out, g_final.reshape(1, d))


def _layer(x, c, w_ada, b_ada, g_norm, w_in, conv_w, conv_b, b_igate, b_fgate, g_mlstm_head, w_out,
           g_final):
    mod = _ada(c, w_ada, b_ada)

    z4 = jnp.zeros((4,), F32)
    bif = jnp.concatenate([b_igate.astype(F32), z4, b_fgate.astype(F32), z4]).reshape(_GATE_ROWS, 1)

    qbt, kaug, kmean, vbt, gbt, out_at = _proj_mlstm(
        x, mod, g_norm, jnp.transpose(w_in), conv_w, conv_b, bif, g_mlstm_head)
    return _moba_out(qbt, kaug, vbt, kmean, gbt, x, mod, out_at, w_out, g_final)


def kernel(x, c, w_ada, b_ada, g_norm, w_in, conv_w, conv_b, b_igate, b_fgate, g_mlstm_head, w_out,
           g_final):
    assert w_ada.shape[0] == 1, "single-layer trunk"
    return _layer(x, c, w_ada[0], b_ada[0], g_norm[0], w_in[0], conv_w[0], conv_b[0], b_igate[0],
                  b_fgate[0], g_mlstm_head[0], w_out[0], g_final)
```

```python
import jax
import jax.numpy as jnp
import numpy as np
from jax import lax
from jax.experimental import pallas as pl
from jax.experimental.pallas import tpu as pltpu

F32 = jnp.float32
BF16 = jnp.bfloat16

D_MODEL = 1024
D_MLSTM = 512
N_HEADS_MLSTM = 4
HEAD_DIM_MLSTM = 128
D_MOBA = 512
N_HEADS_MOBA = 8
HEAD_DIM_MOBA = 64
CONV_K = 4
MOBA_BLOCK = 256
MOBA_TOPK = 3
EPS = 1e-6

ROW_TILE = 512
PROJ_CHUNK = 256
MLSTM_CHUNK = 128
MOBA_BATCH = 4
MOBA_LAG = 5
MASK_BIAS = -1e30
LOG2E = 1.4426950408889634
N_POS_TERMS = 3
VMEM_LIMIT = 56 * 1024 * 1024
_TILES_PER_SEQ = 2048 // ROW_TILE
_N_ROW_TILES = 8 * _TILES_PER_SEQ

_OFF = {}
_o = 0
for _name, _size in (("qm", 512), ("km", 512), ("vm", 512), ("om", 512), ("im", 4), ("fm", 4),
                     ("zm", 512), ("qb", 512), ("kb", 512), ("vb", 512), ("zb", 512)):
    _OFF[_name] = (_o, _o + _size)
    _o += _size
_NN_GROUPS = ("qm", "km", "kb")
_NT_GROUPS = ("qb", "vb", "zb", "vm", "om", "zm")
_GATE_ROWS = 16


def _silu(v):
    return v * jax.nn.sigmoid(v)


def _dot_nt(a, b):
    return lax.dot_general(a, b, (((1,), (1,)), ((), ())), preferred_element_type=F32)


def _dot_tn(a, b):
    return lax.dot_general(a, b, (((0,), (0,)), ((), ())), preferred_element_type=F32)


def _ada_kernel(c_ref, w_ref, b_ref, o_ref):
    a = _silu(c_ref[...])
    w = w_ref[...]
    a_hi = a.astype(BF16)
    a_lo = (a - a_hi.astype(F32)).astype(BF16)
    w_hi = w.astype(BF16)
    w_lo = (w - w_hi.astype(F32)).astype(BF16)
    acc = jnp.dot(jnp.concatenate([a_hi, a_lo], axis=0), w_hi, preferred_element_type=F32)
    n = a.shape[0]
    o_ref[...] = (acc[0:n] + acc[n:2 * n] + jnp.dot(a_hi, w_lo, preferred_element_type=F32)) + b_ref[...]


def _ada(c, w_ada, b_ada):
    b, d = c.shape
    n = w_ada.shape[1]
    tn = 512
    return pl.pallas_call(
        _ada_kernel,
        grid=(n // tn,),
        in_specs=[pl.BlockSpec((b, d), lambda i: (0, 0)),
                  pl.BlockSpec((d, tn), lambda i: (0, i)),
                  pl.BlockSpec((1, tn), lambda i: (0, i))],
        out_specs=pl.BlockSpec((b, tn), lambda i: (0, i)),
        out_shape=jax.ShapeDtypeStruct((b, n), F32),
        compiler_params=pltpu.CompilerParams(dimension_semantics=("arbitrary",),
                                             vmem_limit_bytes=VMEM_LIMIT),
        name="ada",
    )(c, w_ada, b_ada.reshape(1, n))


def _chunk_scan(v, combine, fill):
    pos = lax.broadcasted_iota(jnp.int32, v.shape, 1) & (MLSTM_CHUNK - 1)
    d = 1
    while d < MLSTM_CHUNK:
        shifted = pltpu.roll(v, d, axis=1)
        v = combine(v, jnp.where(pos >= d, shifted, fill))
        d *= 2
    return v


def _col_bcast(row, n):
    return jnp.transpose(jnp.broadcast_to(row, (n, n)))


def _interleave(major, minor):
    out, done = [], 0
    for idx, task in enumerate(major):
        out.append(task)
        want = (idx + 1) * len(minor) // len(major)
        out.extend(minor[done:want])
        done = want
    return out


def _proj_mlstm_kernel(x_ref, mod_ref, gn_ref, w_ref, cw_ref, cb_ref, bif_ref, kx_ref, gh_ref,
                       qbt_ref, kaug_ref, kmean_ref, vbt_ref, gbt_ref, oat_ref,
                       ext_ref, wnn_ref, wnt_ref,
                       qp_ref, kp_ref, vp_ref, gp_ref, tp_ref,
                       qm_ref, km_ref, vm_ref, gm_ref, tm_ref, st_ref, m_ref):
    t = pl.program_id(0)
    tm = x_ref.shape[1]
    ts = MOBA_BLOCK
    nsub = tm // ts
    cw = PROJ_CHUNK
    halo = CONV_K - 1
    L = MLSTM_CHUNK
    dh = HEAD_DIM_MLSTM
    npair = N_HEADS_MLSTM // 2
    nchunk = tm // L
    tile_p = jnp.minimum(t, _N_ROW_TILES - 1)
    jp = tile_p % _TILES_PER_SEQ
    shift = mod_ref[pl.ds(tile_p // _TILES_PER_SEQ, 1), 0:D_MODEL]
    scale = mod_ref[pl.ds(tile_p // _TILES_PER_SEQ, 1), D_MODEL:2 * D_MODEL]
    jm = jnp.maximum(t - 1, 0) % _TILES_PER_SEQ

    @pl.when(t == 0)
    def _():
        for g, name in enumerate(_NN_GROUPS):
            lo = _OFF[name][0]
            for c in range(0, 512, 256):
                wnn_ref[:, 512 * g + c:512 * g + c + 256] = jnp.transpose(
                    w_ref[lo + c:lo + c + 256, :]).astype(BF16)
        for g, name in enumerate(_NT_GROUPS):
            lo = _OFF[name][0]
            for c in range(0, 512, 256):
                r0 = _GATE_ROWS + 512 * g + c
                rows = w_ref[lo + c:lo + c + 256, :]
                if name == "qb":
                    rows = rows * (HEAD_DIM_MOBA ** -0.5)
                wnt_ref[r0:r0 + 256, :] = rows.astype(BF16)
        gi = _OFF["im"][0]
        row_w = lax.broadcasted_iota(jnp.int32, (_GATE_ROWS, w_ref.shape[1]), 0)
        wnt_ref[0:_GATE_ROWS, :] = jnp.where(
            row_w < 4, w_ref[gi:gi + _GATE_ROWS, :],
            jnp.where((row_w >= 8) & (row_w < 12), w_ref[gi - 4:gi - 4 + _GATE_ROWS, :], 0.0)).astype(BF16)
        for ref in (qp_ref, kp_ref, vp_ref, gp_ref, tp_ref, st_ref, m_ref):
            ref[...] = jnp.zeros(ref.shape, ref.dtype)

    @pl.when(jp == 0)
    def _():
        ext_ref[0:8, :] = jnp.zeros((8, 2 * D_MLSTM), F32)

    @pl.when(jm == 0)
    def _():
        st_ref[...] = jnp.zeros(st_ref.shape, F32)
        m_ref[...] = jnp.zeros(m_ref.shape, F32)

    def step(with_projection):
        qm_ref[...] = qp_ref[...]
        km_ref[...] = kp_ref[...]
        vm_ref[...] = vp_ref[...]
        gm_ref[...] = gp_ref[...]
        tm_ref[...] = tp_ref[...]

        s_idx = lax.broadcasted_iota(jnp.int32, (L, 2 * L), 0)
        t_idx = lax.broadcasted_iota(jnp.int32, (L, 2 * L), 1) & (L - 1)
        causal = s_idx <= t_idx
        lane2 = lax.broadcasted_iota(jnp.int32, (1, 2 * L), 1)
        ones_tail = jnp.ones((16, L), BF16)
        zeros_blk = jnp.zeros((L, dh), BF16)

        def pair_row(v, p):
            return jnp.concatenate([v[2 * p:2 * p + 1, :], v[2 * p + 1:2 * p + 2, :]], axis=1)

        def pair_scalar(v, p):
            return jnp.where(lane2 < L, v[2 * p:2 * p + 1, :], v[2 * p + 1:2 * p + 2, :])

        def block_diag(a, b):
            return jnp.concatenate([jnp.concatenate([a, zeros_blk], axis=1),
                                    jnp.concatenate([zeros_blk, b], axis=1)], axis=0)

        m_prev = m_ref[:, 0:1]
        rowv = []
        for c in range(nchunk):
            cs_ = slice(c * L, (c + 1) * L)
            cum_f = tm_ref[0:8, cs_]
            g = tm_ref[8:16, cs_]
            g_cmax = tm_ref[16:24, cs_]
            f_tot = cum_f[:, L - 1:L]
            g_max = g_cmax[:, L - 1:L]
            a_max = f_tot + g_max
            a_row = -jnp.maximum(m_prev, g_cmax)
            m_new = jnp.maximum(f_tot + m_prev, a_max)
            rowv.append(dict(
                ts=cs_, g=g, a_row=a_row,
                w=jnp.exp(g - g_max),
                inter=jnp.exp(a_row + m_prev),
                floor=jnp.exp(a_row - cum_f),
                s_old=jnp.exp(f_tot + m_prev - m_new), s_new=jnp.exp(a_max - m_new)))
            m_prev = m_new
        m_ref[...] = jnp.broadcast_to(m_prev, m_ref.shape)

        def q_diag(cs_, p):
            return block_diag(qm_ref[cs_, 2 * p * dh:(2 * p + 1) * dh],
                              qm_ref[cs_, (2 * p + 1) * dh:(2 * p + 2) * dh])

        s_kq, vts, carried, intra = {}, {}, {}, {}
        st = {p: st_ref[p] for p in range(npair)}
        mtasks = []

        def score_task(c, p):
            rv = rowv[c]
            k2 = km_ref[rv["ts"], 2 * p * dh:(2 * p + 2) * dh]
            g_col = jnp.concatenate([_col_bcast(rv["g"][2 * p:2 * p + 1, :], L),
                                     _col_bcast(rv["g"][2 * p + 1:2 * p + 2, :], L)], axis=1)
            decay = jnp.exp(jnp.where(causal, g_col + pair_row(rv["a_row"], p), -jnp.inf))
            s_kq[c, p] = (_dot_nt(k2, q_diag(rv["ts"], p)) * decay).astype(BF16)

        def mix_task(c, p):
            cs_ = rowv[c]["ts"]
            vt2 = jnp.concatenate(
                [jnp.concatenate([vm_ref[(2 * p + e) * dh:(2 * p + e + 1) * dh, cs_], ones_tail], axis=0)
                 for e in range(2)], axis=1)
            vts[c, p] = vt2
            carried[c, p] = _dot_nt(st[p].astype(BF16), q_diag(cs_, p))
            intra[c, p] = jnp.dot(vt2, block_diag(s_kq[c, p][:, 0:L], s_kq[c, p][:, L:2 * L]),
                                  preferred_element_type=F32)

        def finish_task(c, p):
            rv = rowv[c]
            cs_ = rv["ts"]
            mixed = intra[c, p] + pair_row(rv["inter"], p) * carried[c, p]
            den = mixed[dh:dh + 1, :]
            hh = mixed[0:dh] / jnp.maximum(jnp.abs(den), pair_row(rv["floor"], p))
            hh = hh * lax.rsqrt(jnp.mean(hh * hh, axis=0, keepdims=True) + EPS)
            for e in range(2):
                sl = slice((2 * p + e) * dh, (2 * p + e + 1) * dh)
                oat_ref[0, sl, cs_] = (hh[:, e * L:(e + 1) * L] * gh_ref[sl, :]
                                       * gm_ref[sl, cs_].astype(F32)).astype(BF16)

        def state_task(c, p):
            rv = rowv[c]
            cs_ = rv["ts"]
            vw = (vts[c, p].astype(F32) * pair_row(rv["w"], p)).astype(BF16)
            k_bd = block_diag(km_ref[cs_, 2 * p * dh:(2 * p + 1) * dh],
                              km_ref[cs_, (2 * p + 1) * dh:(2 * p + 2) * dh])
            st_loc = jnp.dot(vw, k_bd, preferred_element_type=F32)
            st[p] = pair_scalar(rv["s_old"], p) * st[p] + pair_scalar(rv["s_new"], p) * st_loc

        for c in range(nchunk):
            for p in range(npair):
                mtasks.append((score_task, c, p))
            for fn in (mix_task, finish_task, state_task):
                for p in range(npair):
                    mtasks.append((fn, c, p))

        lane = lax.broadcasted_iota(jnp.int32, (ts, 128), 1)
        hbs = []

        def norms():
            for u in range(nsub):
                x = x_ref[0, u * ts:(u + 1) * ts, :]
                ms = jnp.mean(x * x, axis=-1, keepdims=True)
                h = x * lax.rsqrt(ms + EPS) * gn_ref[...] * (1.0 + scale) + shift
                hbs.append(h.astype(BF16))

        ptasks = []
        for u in range(nsub):
            rs = slice(u * ts, (u + 1) * ts)

            def scan_gates(ift, rs=rs):
                ift = ift + bif_ref[...]
                log_i = ift[0:8]
                f_pre = ift[8:16]
                log_f = jnp.minimum(f_pre, 0.0) - jnp.log1p(jnp.exp(-jnp.abs(f_pre)))
                cum_f = _chunk_scan(log_f, jnp.add, 0.0)
                g = log_i - cum_f
                tp_ref[0:8, rs] = cum_f
                tp_ref[8:16, rs] = g
                tp_ref[16:24, rs] = _chunk_scan(g, jnp.maximum, -jnp.inf)

            def nn(lo, hi, u=u):
                return jnp.dot(hbs[u], wnn_ref[:, lo:hi], preferred_element_type=F32)

            def nt(lo, hi, u=u):
                return _dot_nt(wnt_ref[_GATE_ROWS + lo:_GATE_ROWS + hi, :], hbs[u])

            def conv_chunk(lo, _b, u=u, rs=rs, nn=nn):
                cs = slice(lo, lo + cw)
                base = 8 + u * ts
                qk = nn(lo, lo + cw)
                ext_ref[base:base + ts, cs] = qk
                conv = cb_ref[:, cs] + qk * cw_ref[CONV_K - 1:CONV_K, cs]
                for tap in range(halo):
                    conv = conv + ext_ref[pl.ds(base - halo + tap, ts), cs] * cw_ref[tap:tap + 1, cs]
                if u == nsub - 1:
                    ext_ref[0:8, cs] = qk[ts - 8:ts, :]
                act = _silu(conv)
                if lo < D_MLSTM:
                    qp_ref[rs, lo:lo + cw] = act.astype(BF16)
                else:
                    kp_ref[rs, lo - D_MLSTM:lo - D_MLSTM + cw] = (act * (HEAD_DIM_MLSTM ** -0.5)).astype(BF16)

            def value_chunk(lo, _b, rs=rs, nt=nt):
                vp_ref[lo:lo + cw, rs] = nt(1536 + lo, 1536 + lo + cw).astype(BF16)

            def gate_chunk(lo, _b, rs=rs, nt=nt):
                gp_ref[lo:lo + cw, rs] = (jax.nn.sigmoid(nt(2048 + lo, 2048 + lo + cw))
                                         * _silu(nt(2560 + lo, 2560 + lo + cw))).astype(BF16)

            def key_chunk(lo, _b, u=u, rs=rs, nn=nn):
                kb = nn(1024 + lo, 1024 + lo + cw)
                kmean_ref[0, pl.ds(jp * nsub + u, 1), lo:lo + cw] = jnp.mean(kb, axis=0, keepdims=True)
                kb2 = (kb * LOG2E).astype(BF16)
                for pr in range(lo // 128, (lo + cw) // 128):
                    for hh in range(2):
                        hd = 2 * pr + hh
                        is_key = (lane < HEAD_DIM_MOBA) if hh == 0 else (lane >= HEAD_DIM_MOBA)
                        grp = slice(128 * hd, 128 * (hd + 1))
                        kaug_ref[0, rs, grp] = jnp.where(is_key, kb2[:, 128 * pr - lo:128 * (pr + 1) - lo],
                                                         kx_ref[u, :, grp])

            def qt_chunk(lo, _b, rs=rs, nt=nt, scan_gates=scan_gates):
                if lo == 0:
                    both = nt(-_GATE_ROWS, cw)
                    scan_gates(both[0:_GATE_ROWS])
                    qt = both[_GATE_ROWS:_GATE_ROWS + cw]
                else:
                    qt = nt(lo, lo + cw)
                qbt_ref[0, lo:lo + cw, rs] = qt.astype(BF16)

            def vt_chunk(lo, _b, u=u, nt=nt):
                vbt_ref[0, u, lo:lo + cw, :] = nt(512 + lo, 512 + lo + cw).astype(BF16)

            def gt_chunk(lo, _b, rs=rs, nt=nt):
                gbt_ref[0, lo:lo + cw, rs] = _silu(nt(1024 + lo, 1024 + lo + cw)).astype(BF16)

            heavy = ([(conv_chunk, lo, 0) for lo in range(0, 2 * D_MLSTM, cw)]
                     + [(gate_chunk, lo, 0) for lo in range(0, D_MLSTM, cw)]
                     + [(key_chunk, lo, 0) for lo in range(0, D_MOBA, cw)])
            light = ([(qt_chunk, lo, 0) for lo in range(0, D_MOBA, cw)]
                     + [(vt_chunk, lo, 0) for lo in range(0, D_MOBA, cw)]
                     + [(value_chunk, lo, 0) for lo in range(0, D_MLSTM, cw)]
                     + [(gt_chunk, lo, 0) for lo in range(0, D_MOBA, cw)])
            for idx in range(max(len(heavy), len(light))):
                for group in (light, heavy):
                    if idx < len(group):
                        ptasks.append(group[idx])

        tasks = mtasks
        if with_projection:
            norms()
            tasks = _interleave(ptasks, mtasks)
        for fn, a0, a1 in tasks:
            fn(a0, a1)
        for p in range(npair):
            st_ref[p] = st[p]

    @pl.when(t < _N_ROW_TILES)
    def _():
        step(True)

    @pl.when(t == _N_ROW_TILES)
    def _():
        step(False)


def _proj_mlstm(x, mod, g_norm, w_in, conv_w, conv_b, bif, g_head):
    b, s, d = x.shape
    tm = ROW_TILE
    nb = s // MOBA_BLOCK
    nsub = tm // MOBA_BLOCK
    assert tm % MOBA_BLOCK == 0 and (b * s) // tm == _N_ROW_TILES and s // tm == _TILES_PER_SEQ
    kx = _moba_key_extras(nb)
    kw = N_HEADS_MOBA * 128
    dm = D_MLSTM
    g_bc = jnp.broadcast_to(g_head.astype(F32).reshape(dm, 1), (dm, 128))
    tps = _TILES_PER_SEQ
    last = _N_ROW_TILES - 1
    pt = lambda t: jnp.minimum(t, last)
    mt = lambda t: jnp.maximum(t - 1, 0)
    full2 = lambda t: (0, 0)
    out_shapes = (
        jax.ShapeDtypeStruct((b, D_MOBA, s), BF16),
        jax.ShapeDtypeStruct((b, s, kw), BF16),
        jax.ShapeDtypeStruct((b, nb, D_MOBA), F32),
        jax.ShapeDtypeStruct((b, nb, D_MOBA, MOBA_BLOCK), BF16),
        jax.ShapeDtypeStruct((b, D_MOBA, s), BF16),
        jax.ShapeDtypeStruct((b, dm, s), BF16),
    )
    out_specs = (
        pl.BlockSpec((1, D_MOBA, tm), lambda t: (pt(t) // tps, 0, pt(t) % tps)),
        pl.BlockSpec((1, tm, kw), lambda t: (pt(t) // tps, pt(t) % tps, 0)),
        pl.BlockSpec((1, nb, D_MOBA), lambda t: (pt(t) // tps, 0, 0)),
        pl.BlockSpec((1, nsub, D_MOBA, MOBA_BLOCK), lambda t: (pt(t) // tps, pt(t) % tps, 0, 0)),
        pl.BlockSpec((1, D_MOBA, tm), lambda t: (pt(t) // tps, 0, pt(t) % tps)),
        pl.BlockSpec((1, dm, tm), lambda t: (mt(t) // tps, 0, mt(t) % tps)),
    )
    in_specs = [
        pl.BlockSpec((1, tm, d), lambda t: (pt(t) // tps, pt(t) % tps, 0)),
        pl.BlockSpec(mod.shape, full2),
        pl.BlockSpec((1, d), full2),
        pl.BlockSpec(w_in.shape, full2, pipeline_mode=pl.Buffered(1)),
        pl.BlockSpec(conv_w.shape, full2),
        pl.BlockSpec((1, 2 * dm), full2),
        pl.BlockSpec((_GATE_ROWS, 1), full2),
        pl.BlockSpec((nsub, MOBA_BLOCK, kw), lambda t: (pt(t) % tps, 0, 0)),
        pl.BlockSpec((dm, 128), full2),
    ]
    operand_scratch = [pltpu.VMEM((tm, dm), BF16), pltpu.VMEM((tm, dm), BF16),
                       pltpu.VMEM((dm, tm), BF16), pltpu.VMEM((dm, tm), BF16),
                       pltpu.VMEM((24, tm), F32)]
    return pl.pallas_call(
        _proj_mlstm_kernel,
        grid=(_N_ROW_TILES + 1,),
        in_specs=in_specs,
        out_specs=out_specs,
        out_shape=out_shapes,
        scratch_shapes=[pltpu.VMEM((tm + 8, 2 * dm), F32),
                        pltpu.VMEM((d, 512 * len(_NN_GROUPS)), BF16),
                        pltpu.VMEM((_GATE_ROWS + 512 * len(_NT_GROUPS), d), BF16)]
        + operand_scratch + operand_scratch
        + [pltpu.VMEM((N_HEADS_MLSTM // 2, HEAD_DIM_MLSTM + 16, 2 * HEAD_DIM_MLSTM), F32),
           pltpu.VMEM((8, 128), F32)],
        compiler_params=pltpu.CompilerParams(dimension_semantics=("arbitrary",),
                                             vmem_limit_bytes=VMEM_LIMIT),
        name="proj_mlstm",
    )(x, mod, g_norm.reshape(1, d), w_in, conv_w, conv_b.reshape(1, -1), bif, kx, g_bc)


def _moba_steps(nb):
    jt, it = [], []
    for j in range(nb):
        for i in [j] + list(range(j)):
            jt.append(j)
            it.append(i)
    return jt, it


def _moba_key_extras(nb):
    blk = MOBA_BLOCK
    lane = np.arange(N_HEADS_MOBA * 128)
    head = lane // 128
    w = (lane % 128 - np.where(head % 2 == 0, HEAD_DIM_MOBA, 0))[None, None, :]
    slope = (2.0 ** -(head + 1.0))[None, None, :]
    i = np.arange(nb)[:, None, None]
    c = np.arange(blk)[None, :, None]
    tab = np.where(w == i, 1.0, 0.0)
    rest = LOG2E * slope * (i * blk + c)
    for term in range(N_POS_TERMS):
        piece = rest.astype(np.float32).astype(BF16).astype(np.float64)
        tab = tab + np.where(w == nb + term, piece, 0.0)
        rest = rest - piece
    return jnp.asarray(tab.astype(np.float32).astype(BF16))


def _moba_kernel(jt_ref, it_ref, qt_ref, k_ref, vt_ref, kmean_ref, gt_ref, x_ref, mod_ref, oat_ref, w_ref,
                 gf_ref, o_ref,
                 qaug_ref, m_ref, acc_ref, cmax_ref, wb_ref, obt_ref, vprev_ref, *s_refs):
    @pl.when(jnp.logical_and(pl.program_id(0) == 0, pl.program_id(1) == 0))
    def _():
        for r in range(0, w_ref.shape[0], 256):
            wb_ref[r:r + 256, :] = w_ref[r:r + 256, :].astype(BF16)

    t = pl.program_id(1)
    j = jt_ref[t]
    i = it_ref[t]
    blk = MOBA_BLOCK
    dh = HEAD_DIM_MOBA
    nb = kmean_ref.shape[1]
    nbat = qt_ref.shape[0]
    first = i == j
    last = jnp.logical_or(i == j - 1, j == 0)

    def key_tile(bb, h):
        return k_ref[bb, :, 128 * h:128 * (h + 1)]

    def rows(h):
        return slice(dh * h, dh * (h + 1))

    ones_tail = jnp.ones((16, blk), BF16)

    def values_t(bb, h):
        return jnp.concatenate([vt_ref[bb, 0, rows(h), :], ones_tail], axis=0)

    late = nbat - 1
    par = pl.program_id(1) & 1
    units = [(bb, h) for bb in range(nbat) for h in range(N_HEADS_MOBA)]
    assert MOBA_LAG <= N_HEADS_MOBA

    def late_values_prev(_bb, h):
        return jnp.concatenate([vprev_ref[par, rows(h), :], ones_tail], axis=0)

    def late_values_t(_bb, h):
        return jnp.concatenate([vprev_ref[1 - par, rows(h), :], ones_tail], axis=0)

    def update_phase(bb, h, values):
        m_run = m_ref[bb, h:h + 1, :]
        m_new = jnp.maximum(m_run, cmax_ref[bb, h:h + 1, :])
        alpha = jnp.exp2(m_run - m_new)
        pr_ = jnp.exp2(s_refs[bb][h] - m_new)
        m_ref[bb, h:h + 1, :] = m_new
        acc_ref[bb, h] = alpha * acc_ref[bb, h] + jnp.dot(values(bb, h), pr_.astype(BF16),
                                                          preferred_element_type=F32)

    @pl.when(first)
    def _():
        blk_i = lax.broadcasted_iota(jnp.int32, (nb, blk), 0)
        lane_m = lax.broadcasted_iota(jnp.int32, (nb, 2 * dh), 1)
        key_c = lax.broadcasted_iota(jnp.int32, (blk, blk), 0)
        qry_r = lax.broadcasted_iota(jnp.int32, (blk, blk), 1)
        causal = key_c <= qry_r
        ones_rows = jnp.where(blk_i < N_POS_TERMS, 1.0, 0.0)
        gates = {}
        for bb in range(nbat):
            for h in range(N_HEADS_MOBA):
                pr, hh = divmod(h, 2)
                qt_pair = qt_ref[bb, 128 * pr:128 * (pr + 1), :]
                in_head = (lane_m < dh) if hh == 0 else (lane_m >= dh)
                km = jnp.where(in_head, kmean_ref[bb, :, 128 * pr:128 * (pr + 1)], 0.0).astype(BF16)
                gates[bb, h] = jnp.dot(km, qt_pair, preferred_element_type=F32)
        for bb in range(nbat):
            for h in range(N_HEADS_MOBA):
                pr, hh = divmod(h, 2)
                gate = gates[bb, h]
                rank = jnp.zeros((nb, blk), jnp.int32)
                for i2 in range(nb):
                    g2 = gate[i2:i2 + 1, :]
                    beats = (g2 > gate) | ((g2 == gate) & (i2 < blk_i))
                    rank = rank + jnp.where(beats & (i2 < j), 1, 0)
                keep = ((blk_i < j) & (rank < MOBA_TOPK)) | (blk_i == j)
                sel_bias = jnp.where(keep, 0.0, MASK_BIAS)
                extra = jnp.concatenate([sel_bias, ones_rows, jnp.zeros((dh - 16, blk), F32)], axis=0)
                extra = extra.astype(BF16)
                qt_h = qt_ref[bb, dh * h:dh * (h + 1), :]
                qaug_ref[bb, h] = jnp.concatenate([qt_h, extra] if hh == 0 else [extra, qt_h], axis=0)

        for bb in range(nbat):
            for h in range(N_HEADS_MOBA):
                s = jnp.dot(key_tile(bb, h), qaug_ref[bb, h], preferred_element_type=F32)
                s = jnp.where(causal, s, MASK_BIAS)
                s_refs[bb][h] = s
                m_ref[bb, h:h + 1, :] = jnp.max(s, axis=0, keepdims=True)

        for bb, h in units[:-MOBA_LAG]:
            p0 = jnp.exp2(s_refs[bb][h] - m_ref[bb, h:h + 1, :])
            acc_ref[bb, h] = jnp.dot(values_t(bb, h), p0.astype(BF16), preferred_element_type=F32)
        for bb, h in units[-MOBA_LAG:]:
            cmax_ref[bb, h:h + 1, :] = m_ref[bb, h:h + 1, :]
            acc_ref[bb, h] = jnp.zeros(acc_ref.shape[2:], F32)
        vprev_ref[1 - par] = vt_ref[late, 0]

    @pl.when(jnp.logical_not(first))
    def _():
        def score_phase(bb, h):
            s = jnp.dot(key_tile(bb, h), qaug_ref[bb, h], preferred_element_type=F32)
            s_refs[bb][h] = s
            cmax_ref[bb, h:h + 1, :] = jnp.max(s, axis=0, keepdims=True)

        vprev_ref[1 - par] = vt_ref[late, 0]
        for slot in range(len(units)):
            score_phase(*units[slot])
            if slot < MOBA_LAG:
                update_phase(*units[slot - MOBA_LAG], late_values_prev)
            else:
                update_phase(*units[slot - MOBA_LAG], values_t)

    @pl.when(last)
    def _():
        for bb, h in units[-MOBA_LAG:]:
            update_phase(bb, h, late_values_t)
        for bb in range(nbat):
            for h in range(N_HEADS_MOBA):
                acc = acc_ref[bb, h]
                out = acc[0:dh] / acc[dh:dh + 1] * gt_ref[bb, rows(h), :].astype(F32)
                obt_ref[bb, rows(h), :] = out.astype(BF16)
        for bb in range(nbat):
            y = (_dot_tn(oat_ref[bb], wb_ref[0:D_MLSTM, :])
                 + _dot_tn(obt_ref[bb], wb_ref[D_MLSTM:D_MLSTM + D_MOBA, :]))
            gate = mod_ref[pl.ds(pl.program_id(0) * nbat + bb, 1), 2 * D_MODEL:3 * D_MODEL]
            r = x_ref[bb] + gate * y
            o_ref[bb] = r * lax.rsqrt(jnp.mean(r * r, axis=-1, keepdims=True) + EPS) * gf_ref[...]


def _moba_out(qbt, kaug, vbt, kmean, gbt, x, mod, out_at, w_out, g_final):
    b, dmb, s = qbt.shape
    d = x.shape[2]
    blk = MOBA_BLOCK
    nb = s // blk
    kw = kaug.shape[2]
    nbat = MOBA_BATCH
    assert nb == 8, "selection extras assume 8 key blocks"
    jt, it = _moba_steps(nb)
    qmap = lambda bi, t, jt_r, it_r: (bi, 0, jt_r[t])
    rmap = lambda bi, t, jt_r, it_r: (bi, jt_r[t], 0)
    full2 = lambda bi, t, jt_r, it_r: (0, 0)
    grid_spec = pltpu.PrefetchScalarGridSpec(
        num_scalar_prefetch=2,
        grid=(b // nbat, len(jt)),
        in_specs=[pl.BlockSpec((nbat, dmb, blk), qmap),
                  pl.BlockSpec((nbat, blk, kw), lambda bi, t, jt_r, it_r: (bi, it_r[t], 0)),
                  pl.BlockSpec((nbat, 1, dmb, blk), lambda bi, t, jt_r, it_r: (bi, it_r[t], 0, 0)),
                  pl.BlockSpec((nbat, nb, dmb), lambda bi, t, jt_r, it_r: (bi, 0, 0)),
                  pl.BlockSpec((nbat, dmb, blk), qmap),
                  pl.BlockSpec((nbat, blk, d), rmap),
                  pl.BlockSpec(mod.shape, full2),
                  pl.BlockSpec((nbat, D_MLSTM, blk), qmap),
                  pl.BlockSpec(w_out.shape, full2, pipeline_mode=pl.Buffered(1)),
                  pl.BlockSpec((1, d), full2)],
        out_specs=pl.BlockSpec((nbat, blk, d), rmap),
        scratch_shapes=[pltpu.VMEM((nbat, N_HEADS_MOBA, 2 * HEAD_DIM_MOBA, blk), BF16),
                        pltpu.VMEM((nbat, N_HEADS_MOBA, blk), F32),
                        pltpu.VMEM((nbat, N_HEADS_MOBA, HEAD_DIM_MOBA + 16, blk), F32),
                        pltpu.VMEM((nbat, N_HEADS_MOBA, blk), F32),
                        pltpu.VMEM(w_out.shape, BF16),
                        pltpu.VMEM((nbat, dmb, blk), BF16),
                        pltpu.VMEM((2, dmb, blk), BF16)]
        + [pltpu.VMEM((N_HEADS_MOBA, blk, blk), F32) for _ in range(nbat)],
    )
    return pl.pallas_call(
        _moba_kernel,
        grid_spec=grid_spec,
        out_shape=jax.ShapeDtypeStruct((b, s, d), x.dtype),
        compiler_params=pltpu.CompilerParams(dimension_semantics=("arbitrary", "arbitrary"),
                                             vmem_limit_bytes=VMEM_LIMIT),
        name="moba_out",
    )(jnp.asarray(jt, jnp.int32), jnp.asarray(it, jnp.int32), qbt, kaug, vbt, kmean, gbt,
      x, mod, out_at, w_out, g_final.reshape(1, d))


def _layer(x, c, w_ada, b_ada, g_norm, w_in, conv_w, conv_b, b_igate, b_fgate, g_mlstm_head, w_out,
           g_final):
    mod = _ada(c, w_ada, b_ada)

    z4 = jnp.zeros((4,), F32)
    bif = jnp.concatenate([b_igate.astype(F32), z4, b_fgate.astype(F32), z4]).reshape(_GATE_ROWS, 1)

    qbt, kaug, kmean, vbt, gbt, out_at = _proj_mlstm(
        x, mod, g_norm, jnp.transpose(w_in), conv_w, conv_b, bif, g_mlstm_head)
    return _moba_out(qbt, kaug, vbt, kmean, gbt, x, mod, out_at, w_out, g_final)


def kernel(x, c, w_ada, b_ada, g_norm, w_in, conv_w, conv_b, b_igate, b_fgate, g_mlstm_head, w_out,
           g_final):
    assert w_ada.shape[0] == 1, "single-layer trunk"
    return _layer(x, c, w_ada[0], b_ada[0], g_norm[0], w_in[0], conv_w[0], conv_b[0], b_igate[0],
                  b_fgate[0], g_mlstm_head[0], w_out[0], g_final)
```

```python
import jax
import jax.numpy as jnp
import numpy as np
from jax import lax
from jax.experimental import pallas as pl
from jax.experimental.pallas import tpu as pltpu

F32 = jnp.float32
BF16 = jnp.bfloat16

D_MODEL = 1024
D_MLSTM = 512
N_HEADS_MLSTM = 4
HEAD_DIM_MLSTM = 128
D_MOBA = 512
N_HEADS_MOBA = 8
HEAD_DIM_MOBA = 64
CONV_K = 4
MOBA_BLOCK = 256
MOBA_TOPK = 3
EPS = 1e-6

ROW_TILE = 512
PROJ_CHUNK = 256
MLSTM_CHUNK = 128
MOBA_BATCH = 4
KV_SLOTS = 3
MOBA_LAG = 5
MASK_BIAS = -1e30
LOG2E = 1.4426950408889634
N_POS_TERMS = 3
VMEM_LIMIT = 56 * 1024 * 1024
_TILES_PER_SEQ = 2048 // ROW_TILE
_N_ROW_TILES = 8 * _TILES_PER_SEQ

_OFF = {}
_o = 0
for _name, _size in (("qm", 512), ("km", 512), ("vm", 512), ("om", 512), ("im", 4), ("fm", 4),
                     ("zm", 512), ("qb", 512), ("kb", 512), ("vb", 512), ("zb", 512)):
    _OFF[_name] = (_o, _o + _size)
    _o += _size
_NN_GROUPS = ("qm", "km", "kb")
_NT_GROUPS = ("qb", "vb", "zb", "vm", "om", "zm")
_GATE_ROWS = 16


def _silu(v):
    return v * jax.nn.sigmoid(v)


def _dot_nt(a, b):
    return lax.dot_general(a, b, (((1,), (1,)), ((), ())), preferred_element_type=F32)


def _dot_tn(a, b):
    return lax.dot_general(a, b, (((0,), (0,)), ((), ())), preferred_element_type=F32)


def _ada_kernel(c_ref, w_ref, b_ref, o_ref):
    a = _silu(c_ref[...])
    w = w_ref[...]
    a_hi = a.astype(BF16)
    a_lo = (a - a_hi.astype(F32)).astype(BF16)
    w_hi = w.astype(BF16)
    w_lo = (w - w_hi.astype(F32)).astype(BF16)
    acc = jnp.dot(jnp.concatenate([a_hi, a_lo], axis=0), w_hi, preferred_element_type=F32)
    n = a.shape[0]
    o_ref[...] = (acc[0:n] + acc[n:2 * n] + jnp.dot(a_hi, w_lo, preferred_element_type=F32)) + b_ref[...]


def _ada(c, w_ada, b_ada):
    b, d = c.shape
    n = w_ada.shape[1]
    tn = 512
    return pl.pallas_call(
        _ada_kernel,
        grid=(n // tn,),
        in_specs=[pl.BlockSpec((b, d), lambda i: (0, 0)),
                  pl.BlockSpec((d, tn), lambda i: (0, i)),
                  pl.BlockSpec((1, tn), lambda i: (0, i))],
        out_specs=pl.BlockSpec((b, tn), lambda i: (0, i)),
        out_shape=jax.ShapeDtypeStruct((b, n), F32),
        compiler_params=pltpu.CompilerParams(dimension_semantics=("arbitrary",),
                                             vmem_limit_bytes=VMEM_LIMIT),
        name="ada",
    )(c, w_ada, b_ada.reshape(1, n))


def _chunk_scan(v, combine, fill):
    pos = lax.broadcasted_iota(jnp.int32, v.shape, 1) & (MLSTM_CHUNK - 1)
    d = 1
    while d < MLSTM_CHUNK:
        shifted = pltpu.roll(v, d, axis=1)
        v = combine(v, jnp.where(pos >= d, shifted, fill))
        d *= 2
    return v


def _col_bcast(row, n):
    return jnp.transpose(jnp.broadcast_to(row, (n, n)))


def _interleave(major, minor):
    out, done = [], 0
    for idx, task in enumerate(major):
        out.append(task)
        want = (idx + 1) * len(minor) // len(major)
        out.extend(minor[done:want])
        done = want
    return out


def _proj_mlstm_kernel(x_ref, mod_ref, gn_ref, w_ref, cw_ref, cb_ref, bif_ref, kx_ref, gh_ref,
                       qbt_ref, kaug_ref, kmean_ref, vbt_ref, gbt_ref, oat_ref,
                       ext_ref, wnn_ref, wnt_ref,
                       qp_ref, kp_ref, vp_ref, gp_ref, tp_ref,
                       qm_ref, km_ref, vm_ref, gm_ref, tm_ref, st_ref, m_ref):
    t = pl.program_id(0)
    tm = x_ref.shape[1]
    ts = MOBA_BLOCK
    nsub = tm // ts
    cw = PROJ_CHUNK
    halo = CONV_K - 1
    L = MLSTM_CHUNK
    dh = HEAD_DIM_MLSTM
    npair = N_HEADS_MLSTM // 2
    nchunk = tm // L
    tile_p = jnp.minimum(t, _N_ROW_TILES - 1)
    jp = tile_p % _TILES_PER_SEQ
    shift = mod_ref[pl.ds(tile_p // _TILES_PER_SEQ, 1), 0:D_MODEL]
    scale = mod_ref[pl.ds(tile_p // _TILES_PER_SEQ, 1), D_MODEL:2 * D_MODEL]
    jm = jnp.maximum(t - 1, 0) % _TILES_PER_SEQ

    @pl.when(t == 0)
    def _():
        for g, name in enumerate(_NN_GROUPS):
            lo = _OFF[name][0]
            for c in range(0, 512, 256):
                wnn_ref[:, 512 * g + c:512 * g + c + 256] = jnp.transpose(
                    w_ref[lo + c:lo + c + 256, :]).astype(BF16)
        for g, name in enumerate(_NT_GROUPS):
            lo = _OFF[name][0]
            for c in range(0, 512, 256):
                r0 = _GATE_ROWS + 512 * g + c
                rows = w_ref[lo + c:lo + c + 256, :]
                if name == "qb":
                    rows = rows * (HEAD_DIM_MOBA ** -0.5)
                wnt_ref[r0:r0 + 256, :] = rows.astype(BF16)
        gi = _OFF["im"][0]
        row_w = lax.broadcasted_iota(jnp.int32, (_GATE_ROWS, w_ref.shape[1]), 0)
        wnt_ref[0:_GATE_ROWS, :] = jnp.where(
            row_w < 4, w_ref[gi:gi + _GATE_ROWS, :],
            jnp.where((row_w >= 8) & (row_w < 12), w_ref[gi - 4:gi - 4 + _GATE_ROWS, :], 0.0)).astype(BF16)
        for ref in (qp_ref, kp_ref, vp_ref, gp_ref, tp_ref, st_ref, m_ref):
            ref[...] = jnp.zeros(ref.shape, ref.dtype)

    @pl.when(jp == 0)
    def _():
        ext_ref[0:8, :] = jnp.zeros((8, 2 * D_MLSTM), F32)

    @pl.when(jm == 0)
    def _():
        st_ref[...] = jnp.zeros(st_ref.shape, F32)
        m_ref[...] = jnp.zeros(m_ref.shape, F32)

    def step(with_projection):
        qm_ref[...] = qp_ref[...]
        km_ref[...] = kp_ref[...]
        vm_ref[...] = vp_ref[...]
        gm_ref[...] = gp_ref[...]
        tm_ref[...] = tp_ref[...]

        s_idx = lax.broadcasted_iota(jnp.int32, (L, 2 * L), 0)
        t_idx = lax.broadcasted_iota(jnp.int32, (L, 2 * L), 1) & (L - 1)
        causal = s_idx <= t_idx
        lane2 = lax.broadcasted_iota(jnp.int32, (1, 2 * L), 1)
        ones_tail = jnp.ones((16, L), BF16)
        zeros_blk = jnp.zeros((L, dh), BF16)

        def pair_row(v, p):
            return jnp.concatenate([v[2 * p:2 * p + 1, :], v[2 * p + 1:2 * p + 2, :]], axis=1)

        def pair_scalar(v, p):
            return jnp.where(lane2 < L, v[2 * p:2 * p + 1, :], v[2 * p + 1:2 * p + 2, :])

        def block_diag(a, b):
            return jnp.concatenate([jnp.concatenate([a, zeros_blk], axis=1),
                                    jnp.concatenate([zeros_blk, b], axis=1)], axis=0)

        m_prev = m_ref[:, 0:1]
        rowv = []
        for c in range(nchunk):
            cs_ = slice(c * L, (c + 1) * L)
            cum_f = tm_ref[0:8, cs_]
            g = tm_ref[8:16, cs_]
            g_cmax = tm_ref[16:24, cs_]
            f_tot = cum_f[:, L - 1:L]
            g_max = g_cmax[:, L - 1:L]
            a_max = f_tot + g_max
            a_row = -jnp.maximum(m_prev, g_cmax)
            m_new = jnp.maximum(f_tot + m_prev, a_max)
            rowv.append(dict(
                ts=cs_, g=g, a_row=a_row,
                w=jnp.exp(g - g_max),
                inter=jnp.exp(a_row + m_prev),
                floor=jnp.exp(a_row - cum_f),
                s_old=jnp.exp(f_tot + m_prev - m_new), s_new=jnp.exp(a_max - m_new)))
            m_prev = m_new
        m_ref[...] = jnp.broadcast_to(m_prev, m_ref.shape)

        def q_diag(cs_, p):
            return block_diag(qm_ref[cs_, 2 * p * dh:(2 * p + 1) * dh],
                              qm_ref[cs_, (2 * p + 1) * dh:(2 * p + 2) * dh])

        s_kq, vts, carried, intra = {}, {}, {}, {}
        st = {p: st_ref[p] for p in range(npair)}
        mtasks = []

        def score_task(c, p):
            rv = rowv[c]
            k2 = km_ref[rv["ts"], 2 * p * dh:(2 * p + 2) * dh]
            g_col = jnp.concatenate([_col_bcast(rv["g"][2 * p:2 * p + 1, :], L),
                                     _col_bcast(rv["g"][2 * p + 1:2 * p + 2, :], L)], axis=1)
            decay = jnp.exp(jnp.where(causal, g_col + pair_row(rv["a_row"], p), -jnp.inf))
            s_kq[c, p] = (_dot_nt(k2, q_diag(rv["ts"], p)) * decay).astype(BF16)

        def mix_task(c, p):
            cs_ = rowv[c]["ts"]
            vt2 = jnp.concatenate(
                [jnp.concatenate([vm_ref[(2 * p + e) * dh:(2 * p + e + 1) * dh, cs_], ones_tail], axis=0)
                 for e in range(2)], axis=1)
            vts[c, p] = vt2
            carried[c, p] = _dot_nt(st[p].astype(BF16), q_diag(cs_, p))
            intra[c, p] = jnp.dot(vt2, block_diag(s_kq[c, p][:, 0:L], s_kq[c, p][:, L:2 * L]),
                                  preferred_element_type=F32)

        def finish_task(c, p):
            rv = rowv[c]
            cs_ = rv["ts"]
            mixed = intra[c, p] + pair_row(rv["inter"], p) * carried[c, p]
            den = mixed[dh:dh + 1, :]
            hh = mixed[0:dh] / jnp.maximum(jnp.abs(den), pair_row(rv["floor"], p))
            hh = hh * lax.rsqrt(jnp.mean(hh * hh, axis=0, keepdims=True) + EPS)
            for e in range(2):
                sl = slice((2 * p + e) * dh, (2 * p + e + 1) * dh)
                oat_ref[0, sl, cs_] = (hh[:, e * L:(e + 1) * L] * gh_ref[sl, :]
                                       * gm_ref[sl, cs_].astype(F32)).astype(BF16)

        def state_task(c, p):
            rv = rowv[c]
            cs_ = rv["ts"]
            vw = (vts[c, p].astype(F32) * pair_row(rv["w"], p)).astype(BF16)
            k_bd = block_diag(km_ref[cs_, 2 * p * dh:(2 * p + 1) * dh],
                              km_ref[cs_, (2 * p + 1) * dh:(2 * p + 2) * dh])
            st_loc = jnp.dot(vw, k_bd, preferred_element_type=F32)
            st[p] = pair_scalar(rv["s_old"], p) * st[p] + pair_scalar(rv["s_new"], p) * st_loc

        for c in range(nchunk):
            for p in range(npair):
                mtasks.append((score_task, c, p))
        for c in range(nchunk):
            for fn in (mix_task, finish_task, state_task):
                for p in range(npair):
                    mtasks.append((fn, c, p))

        lane = lax.broadcasted_iota(jnp.int32, (ts, 128), 1)
        hbs = []

        def norms():
            for u in range(nsub):
                x = x_ref[0, u * ts:(u + 1) * ts, :]
                ms = jnp.mean(x * x, axis=-1, keepdims=True)
                h = x * lax.rsqrt(ms + EPS) * gn_ref[...] * (1.0 + scale) + shift
                hbs.append(h.astype(BF16))

        ptasks = []
        for u in range(nsub):
            rs = slice(u * ts, (u + 1) * ts)

            def scan_gates(ift, rs=rs):
                ift = ift + bif_ref[...]
                log_i = ift[0:8]
                f_pre = ift[8:16]
                log_f = jnp.minimum(f_pre, 0.0) - jnp.log1p(jnp.exp(-jnp.abs(f_pre)))
                cum_f = _chunk_scan(log_f, jnp.add, 0.0)
                g = log_i - cum_f
                tp_ref[0:8, rs] = cum_f
                tp_ref[8:16, rs] = g
                tp_ref[16:24, rs] = _chunk_scan(g, jnp.maximum, -jnp.inf)

            def nn(lo, hi, u=u):
                return jnp.dot(hbs[u], wnn_ref[:, lo:hi], preferred_element_type=F32)

            def nt(lo, hi, u=u):
                return _dot_nt(wnt_ref[_GATE_ROWS + lo:_GATE_ROWS + hi, :], hbs[u])

            def conv_chunk(lo, _b, u=u, rs=rs, nn=nn):
                cs = slice(lo, lo + cw)
                base = 8 + u * ts
                qk = nn(lo, lo + cw)
                ext_ref[base:base + ts, cs] = qk
                conv = cb_ref[:, cs] + qk * cw_ref[CONV_K - 1:CONV_K, cs]
                for tap in range(halo):
                    conv = conv + ext_ref[pl.ds(base - halo + tap, ts), cs] * cw_ref[tap:tap + 1, cs]
                if u == nsub - 1:
                    ext_ref[0:8, cs] = qk[ts - 8:ts, :]
                act = _silu(conv)
                if lo < D_MLSTM:
                    qp_ref[rs, lo:lo + cw] = act.astype(BF16)
                else:
                    kp_ref[rs, lo - D_MLSTM:lo - D_MLSTM + cw] = (act * (HEAD_DIM_MLSTM ** -0.5)).astype(BF16)

            def value_chunk(lo, _b, rs=rs, nt=nt):
                vp_ref[lo:lo + cw, rs] = nt(1536 + lo, 1536 + lo + cw).astype(BF16)

            def gate_chunk(lo, _b, rs=rs, nt=nt):
                gp_ref[lo:lo + cw, rs] = (jax.nn.sigmoid(nt(2048 + lo, 2048 + lo + cw))
                                         * _silu(nt(2560 + lo, 2560 + lo + cw))).astype(BF16)

            def key_chunk(lo, _b, u=u, rs=rs, nn=nn):
                kb = nn(1024 + lo, 1024 + lo + cw)
                kmean_ref[0, pl.ds(jp * nsub + u, 1), lo:lo + cw] = jnp.mean(kb, axis=0, keepdims=True)
                kb2 = (kb * LOG2E).astype(BF16)
                for pr in range(lo // 128, (lo + cw) // 128):
                    for hh in range(2):
                        hd = 2 * pr + hh
                        is_key = (lane < HEAD_DIM_MOBA) if hh == 0 else (lane >= HEAD_DIM_MOBA)
                        grp = slice(128 * hd, 128 * (hd + 1))
                        kaug_ref[0, rs, grp] = jnp.where(is_key, kb2[:, 128 * pr - lo:128 * (pr + 1) - lo],
                                                         kx_ref[u, :, grp])

            def qt_chunk(lo, _b, rs=rs, nt=nt, scan_gates=scan_gates):
                if lo == 0:
                    both = nt(-_GATE_ROWS, cw)
                    scan_gates(both[0:_GATE_ROWS])
                    qt = both[_GATE_ROWS:_GATE_ROWS + cw]
                else:
                    qt = nt(lo, lo + cw)
                qbt_ref[0, lo:lo + cw, rs] = qt.astype(BF16)

            def vt_chunk(lo, _b, u=u, nt=nt):
                vbt_ref[0, u, lo:lo + cw, :] = nt(512 + lo, 512 + lo + cw).astype(BF16)

            def gt_chunk(lo, _b, rs=rs, nt=nt):
                gbt_ref[0, lo:lo + cw, rs] = _silu(nt(1024 + lo, 1024 + lo + cw)).astype(BF16)

            heavy = ([(conv_chunk, lo, 0) for lo in range(0, 2 * D_MLSTM, cw)]
                     + [(gate_chunk, lo, 0) for lo in range(0, D_MLSTM, cw)]
                     + [(key_chunk, lo, 0) for lo in range(0, D_MOBA, cw)])
            light = ([(qt_chunk, lo, 0) for lo in range(0, D_MOBA, cw)]
                     + [(vt_chunk, lo, 0) for lo in range(0, D_MOBA, cw)]
                     + [(value_chunk, lo, 0) for lo in range(0, D_MLSTM, cw)]
                     + [(gt_chunk, lo, 0) for lo in range(0, D_MOBA, cw)])
            for idx in range(max(len(heavy), len(light))):
                for group in (light, heavy):
                    if idx < len(group):
                        ptasks.append(group[idx])

        tasks = mtasks
        if with_projection:
            norms()
            tasks = _interleave(ptasks, mtasks)
        for fn, a0, a1 in tasks:
            fn(a0, a1)
        for p in range(npair):
            st_ref[p] = st[p]

    @pl.when(t < _N_ROW_TILES)
    def _():
        step(True)

    @pl.when(t == _N_ROW_TILES)
    def _():
        step(False)


def _proj_mlstm(x, mod, g_norm, w_in, conv_w, conv_b, bif, g_head):
    b, s, d = x.shape
    tm = ROW_TILE
    nb = s // MOBA_BLOCK
    nsub = tm // MOBA_BLOCK
    assert tm % MOBA_BLOCK == 0 and (b * s) // tm == _N_ROW_TILES and s // tm == _TILES_PER_SEQ
    kx = _moba_key_extras(nb)
    kw = N_HEADS_MOBA * 128
    dm = D_MLSTM
    g_bc = jnp.broadcast_to(g_head.astype(F32).reshape(dm, 1), (dm, 128))
    tps = _TILES_PER_SEQ
    last = _N_ROW_TILES - 1
    pt = lambda t: jnp.minimum(t, last)
    mt = lambda t: jnp.maximum(t - 1, 0)
    full2 = lambda t: (0, 0)
    out_shapes = (
        jax.ShapeDtypeStruct((b, D_MOBA, s), BF16),
        jax.ShapeDtypeStruct((b, s, kw), BF16),
        jax.ShapeDtypeStruct((b, nb, D_MOBA), F32),
        jax.ShapeDtypeStruct((b, nb, D_MOBA, MOBA_BLOCK), BF16),
        jax.ShapeDtypeStruct((b, D_MOBA, s), BF16),
        jax.ShapeDtypeStruct((b, dm, s), BF16),
    )
    out_specs = (
        pl.BlockSpec((1, D_MOBA, tm), lambda t: (pt(t) // tps, 0, pt(t) % tps)),
        pl.BlockSpec((1, tm, kw), lambda t: (pt(t) // tps, pt(t) % tps, 0)),
        pl.BlockSpec((1, nb, D_MOBA), lambda t: (pt(t) // tps, 0, 0)),
        pl.BlockSpec((1, nsub, D_MOBA, MOBA_BLOCK), lambda t: (pt(t) // tps, pt(t) % tps, 0, 0)),
        pl.BlockSpec((1, D_MOBA, tm), lambda t: (pt(t) // tps, 0, pt(t) % tps)),
        pl.BlockSpec((1, dm, tm), lambda t: (mt(t) // tps, 0, mt(t) % tps)),
    )
    in_specs = [
        pl.BlockSpec((1, tm, d), lambda t: (pt(t) // tps, pt(t) % tps, 0)),
        pl.BlockSpec(mod.shape, full2),
        pl.BlockSpec((1, d), full2),
        pl.BlockSpec(w_in.shape, full2, pipeline_mode=pl.Buffered(1)),
        pl.BlockSpec(conv_w.shape, full2),
        pl.BlockSpec((1, 2 * dm), full2),
        pl.BlockSpec((_GATE_ROWS, 1), full2),
        pl.BlockSpec((nsub, MOBA_BLOCK, kw), lambda t: (pt(t) % tps, 0, 0)),
        pl.BlockSpec((dm, 128), full2),
    ]
    operand_scratch = [pltpu.VMEM((tm, dm), BF16), pltpu.VMEM((tm, dm), BF16),
                       pltpu.VMEM((dm, tm), BF16), pltpu.VMEM((dm, tm), BF16),
                       pltpu.VMEM((24, tm), F32)]
    return pl.pallas_call(
        _proj_mlstm_kernel,
        grid=(_N_ROW_TILES + 1,),
        in_specs=in_specs,
        out_specs=out_specs,
        out_shape=out_shapes,
        scratch_shapes=[pltpu.VMEM((tm + 8, 2 * dm), F32),
                        pltpu.VMEM((d, 512 * len(_NN_GROUPS)), BF16),
                        pltpu.VMEM((_GATE_ROWS + 512 * len(_NT_GROUPS), d), BF16)]
        + operand_scratch + operand_scratch
        + [pltpu.VMEM((N_HEADS_MLSTM // 2, HEAD_DIM_MLSTM + 16, 2 * HEAD_DIM_MLSTM), F32),
           pltpu.VMEM((8, 128), F32)],
        compiler_params=pltpu.CompilerParams(dimension_semantics=("arbitrary",),
                                             vmem_limit_bytes=VMEM_LIMIT),
        name="proj_mlstm",
    )(x, mod, g_norm.reshape(1, d), w_in, conv_w, conv_b.reshape(1, -1), bif, kx, g_bc)


def _moba_steps(nb):
    jt, it = [], []
    for j in range(nb):
        for i in [j] + list(range(j)):
            jt.append(j)
            it.append(i)
    return jt, it


def _moba_key_extras(nb):
    blk = MOBA_BLOCK
    lane = np.arange(N_HEADS_MOBA * 128)
    head = lane // 128
    w = (lane % 128 - np.where(head % 2 == 0, HEAD_DIM_MOBA, 0))[None, None, :]
    slope = (2.0 ** -(head + 1.0))[None, None, :]
    i = np.arange(nb)[:, None, None]
    c = np.arange(blk)[None, :, None]
    tab = np.where(w == i, 1.0, 0.0)
    rest = LOG2E * slope * (i * blk + c)
    for term in range(N_POS_TERMS):
        piece = rest.astype(np.float32).astype(BF16).astype(np.float64)
        tab = tab + np.where(w == nb + term, piece, 0.0)
        rest = rest - piece
    return jnp.asarray(tab.astype(np.float32).astype(BF16))


def _moba_kernel(jt_ref, it_ref, qt_ref, k_ref, vt_ref, kmean_ref, gt_ref, x_ref, mod_ref, oat_ref, w_ref,
                 gf_ref, o_ref,
                 qaug_ref, m_ref, acc_ref, cmax_ref, wb_ref, obt_ref, vprev_ref, kbuf_ref, vbuf_ref, kv_sem, *s_refs):
    @pl.when(jnp.logical_and(pl.program_id(0) == 0, pl.program_id(1) == 0))
    def _():
        for r in range(0, w_ref.shape[0], 256):
            wb_ref[r:r + 256, :] = w_ref[r:r + 256, :].astype(BF16)

    t = pl.program_id(1)
    j = jt_ref[t]
    i = it_ref[t]
    blk = MOBA_BLOCK
    dh = HEAD_DIM_MOBA
    nb = kmean_ref.shape[1]
    nbat = qt_ref.shape[0]
    first = i == j
    last = jnp.logical_or(i == j - 1, j == 0)

    n_t = pl.num_programs(1)
    flat = pl.program_id(0) * n_t + t
    n_flat = pl.num_programs(0) * n_t
    kv_slot = flat % KV_SLOTS

    def kv_copies(fp):
        g_ = fp // n_t
        i_ = it_ref[fp % n_t]
        sl_ = fp % KV_SLOTS
        rows_ = pl.ds(g_ * nbat, nbat)
        return (pltpu.make_async_copy(k_ref.at[rows_, pl.ds(i_ * blk, blk), :], kbuf_ref.at[sl_], kv_sem.at[0, sl_]),
                pltpu.make_async_copy(vt_ref.at[rows_, i_], vbuf_ref.at[sl_], kv_sem.at[1, sl_]))

    @pl.when(flat == 0)
    def _():
        for fp in range(KV_SLOTS - 1):
            for cp in kv_copies(fp):
                cp.start()

    @pl.when(flat + KV_SLOTS - 1 < n_flat)
    def _():
        for cp in kv_copies(flat + KV_SLOTS - 1):
            cp.start()

    for cp in kv_copies(flat):
        cp.wait()

    def key_tile(bb, h):
        return kbuf_ref[kv_slot, bb, :, 128 * h:128 * (h + 1)]

    def rows(h):
        return slice(dh * h, dh * (h + 1))

    ones_tail = jnp.ones((16, blk), BF16)

    def values_t(bb, h):
        return jnp.concatenate([vbuf_ref[kv_slot, bb, rows(h), :], ones_tail], axis=0)

    late = nbat - 1
    par = pl.program_id(1) & 1
    units = [(bb, h) for bb in range(nbat) for h in range(N_HEADS_MOBA)]
    assert MOBA_LAG <= N_HEADS_MOBA

    def late_values_prev(_bb, h):
        return jnp.concatenate([vprev_ref[par, rows(h), :], ones_tail], axis=0)

    def late_values_t(_bb, h):
        return jnp.concatenate([vprev_ref[1 - par, rows(h), :], ones_tail], axis=0)

    def update_phase(bb, h, values):
        m_run = m_ref[bb, h:h + 1, :]
        m_new = jnp.maximum(m_run, cmax_ref[bb, h:h + 1, :])
        alpha = jnp.exp2(m_run - m_new)
        pr_ = jnp.exp2(s_refs[bb][h] - m_new)
        m_ref[bb, h:h + 1, :] = m_new
        acc_ref[bb, h] = alpha * acc_ref[bb, h] + jnp.dot(values(bb, h), pr_.astype(BF16),
                                                          preferred_element_type=F32)

    @pl.when(first)
    def _():
        blk_i = lax.broadcasted_iota(jnp.int32, (nb, blk), 0)
        lane_m = lax.broadcasted_iota(jnp.int32, (nb, 2 * dh), 1)
        key_c = lax.broadcasted_iota(jnp.int32, (blk, blk), 0)
        qry_r = lax.broadcasted_iota(jnp.int32, (blk, blk), 1)
        causal = key_c <= qry_r
        ones_rows = jnp.where(blk_i < N_POS_TERMS, 1.0, 0.0)
        gates = {}
        for bb in range(nbat):
            for h in range(N_HEADS_MOBA):
                pr, hh = divmod(h, 2)
                qt_pair = qt_ref[bb, 128 * pr:128 * (pr + 1), :]
                in_head = (lane_m < dh) if hh == 0 else (lane_m >= dh)
                km = jnp.where(in_head, kmean_ref[bb, :, 128 * pr:128 * (pr + 1)], 0.0).astype(BF16)
                gates[bb, h] = jnp.dot(km, qt_pair, preferred_element_type=F32)
        for bb in range(nbat):
            for h in range(N_HEADS_MOBA):
                pr, hh = divmod(h, 2)
                gate = gates[bb, h]
                rank = jnp.zeros((nb, blk), jnp.int32)
                for i2 in range(nb):
                    g2 = gate[i2:i2 + 1, :]
                    beats = (g2 > gate) | ((g2 == gate) & (i2 < blk_i))
                    rank = rank + jnp.where(beats & (i2 < j), 1, 0)
                keep = ((blk_i < j) & (rank < MOBA_TOPK)) | (blk_i == j)
                sel_bias = jnp.where(keep, 0.0, MASK_BIAS)
                extra = jnp.concatenate([sel_bias, ones_rows, jnp.zeros((dh - 16, blk), F32)], axis=0)
                extra = extra.astype(BF16)
                qt_h = qt_ref[bb, dh * h:dh * (h + 1), :]
                qaug_ref[bb, h] = jnp.concatenate([qt_h, extra] if hh == 0 else [extra, qt_h], axis=0)

        for bb in range(nbat):
            for h in range(N_HEADS_MOBA):
                s = jnp.dot(key_tile(bb, h), qaug_ref[bb, h], preferred_element_type=F32)
                s = jnp.where(causal, s, MASK_BIAS)
                s_refs[bb][h] = s
                m_ref[bb, h:h + 1, :] = jnp.max(s, axis=0, keepdims=True)

        for bb, h in units[:-MOBA_LAG]:
            p0 = jnp.exp2(s_refs[bb][h] - m_ref[bb, h:h + 1, :])
            acc_ref[bb, h] = jnp.dot(values_t(bb, h), p0.astype(BF16), preferred_element_type=F32)
        for bb, h in units[-MOBA_LAG:]:
            cmax_ref[bb, h:h + 1, :] = m_ref[bb, h:h + 1, :]
            acc_ref[bb, h] = jnp.zeros(acc_ref.shape[2:], F32)
        vprev_ref[1 - par] = vbuf_ref[kv_slot, late]

    @pl.when(jnp.logical_not(first))
    def _():
        def score_phase(bb, h):
            s = jnp.dot(key_tile(bb, h), qaug_ref[bb, h], preferred_element_type=F32)
            s_refs[bb][h] = s
            cmax_ref[bb, h:h + 1, :] = jnp.max(s, axis=0, keepdims=True)

        vprev_ref[1 - par] = vbuf_ref[kv_slot, late]
        for slot in range(len(units)):
            score_phase(*units[slot])
            if slot < MOBA_LAG:
                update_phase(*units[slot - MOBA_LAG], late_values_prev)
            else:
                update_phase(*units[slot - MOBA_LAG], values_t)

    @pl.when(last)
    def _():
        for bb, h in units[-MOBA_LAG:]:
            update_phase(bb, h, late_values_t)
        for bb in range(nbat):
            for h in range(N_HEADS_MOBA):
                acc = acc_ref[bb, h]
                out = acc[0:dh] / acc[dh:dh + 1] * gt_ref[bb, rows(h), :].astype(F32)
                obt_ref[bb, rows(h), :] = out.astype(BF16)
        for bb in range(nbat):
            y = (_dot_tn(oat_ref[bb], wb_ref[0:D_MLSTM, :])
                 + _dot_tn(obt_ref[bb], wb_ref[D_MLSTM:D_MLSTM + D_MOBA, :]))
            gate = mod_ref[pl.ds(pl.program_id(0) * nbat + bb, 1), 2 * D_MODEL:3 * D_MODEL]
            r = x_ref[bb] + gate * y
            o_ref[bb] = r * lax.rsqrt(jnp.mean(r * r, axis=-1, keepdims=True) + EPS) * gf_ref[...]


def _moba_out(qbt, kaug, vbt, kmean, gbt, x, mod, out_at, w_out, g_final):
    b, dmb, s = qbt.shape
    d = x.shape[2]
    blk = MOBA_BLOCK
    nb = s // blk
    kw = kaug.shape[2]
    nbat = MOBA_BATCH
    assert nb == 8, "selection extras assume 8 key blocks"
    jt, it = _moba_steps(nb)
    qmap = lambda bi, t, jt_r, it_r: (bi, 0, jt_r[t])
    rmap = lambda bi, t, jt_r, it_r: (bi, jt_r[t], 0)
    full2 = lambda bi, t, jt_r, it_r: (0, 0)
    grid_spec = pltpu.PrefetchScalarGridSpec(
        num_scalar_prefetch=2,
        grid=(b // nbat, len(jt)),
        in_specs=[pl.BlockSpec((nbat, dmb, blk), qmap),
                  pl.BlockSpec(memory_space=pl.ANY),
                  pl.BlockSpec(memory_space=pl.ANY),
                  pl.BlockSpec((nbat, nb, dmb), lambda bi, t, jt_r, it_r: (bi, 0, 0)),
                  pl.BlockSpec((nbat, dmb, blk), qmap),
                  pl.BlockSpec((nbat, blk, d), rmap),
                  pl.BlockSpec(mod.shape, full2),
                  pl.BlockSpec((nbat, D_MLSTM, blk), qmap),
                  pl.BlockSpec(w_out.shape, full2, pipeline_mode=pl.Buffered(1)),
                  pl.BlockSpec((1, d), full2)],
        out_specs=pl.BlockSpec((nbat, blk, d), rmap),
        scratch_shapes=[pltpu.VMEM((nbat, N_HEADS_MOBA, 2 * HEAD_DIM_MOBA, blk), BF16),
                        pltpu.VMEM((nbat, N_HEADS_MOBA, blk), F32),
                        pltpu.VMEM((nbat, N_HEADS_MOBA, HEAD_DIM_MOBA + 16, blk), F32),
                        pltpu.VMEM((nbat, N_HEADS_MOBA, blk), F32),
                        pltpu.VMEM(w_out.shape, BF16),
                        pltpu.VMEM((nbat, dmb, blk), BF16),
                        pltpu.VMEM((2, dmb, blk), BF16),
                        pltpu.VMEM((KV_SLOTS, nbat, blk, kw), BF16),
                        pltpu.VMEM((KV_SLOTS, nbat, dmb, blk), BF16),
                        pltpu.SemaphoreType.DMA((2, KV_SLOTS))]
        + [pltpu.VMEM((N_HEADS_MOBA, blk, blk), F32) for _ in range(nbat)],
    )
    return pl.pallas_call(
        _moba_kernel,
        grid_spec=grid_spec,
        out_shape=jax.ShapeDtypeStruct((b, s, d), x.dtype),
        compiler_params=pltpu.CompilerParams(dimension_semantics=("arbitrary", "arbitrary"),
                                             vmem_limit_bytes=VMEM_LIMIT),
        name="moba_out",
    )(jnp.asarray(jt, jnp.int32), jnp.asarray(it, jnp.int32), qbt, kaug, vbt, kmean, gbt,
      x, mod, out_at, w_out, g_final.reshape(1, d))


def _layer(x, c, w_ada, b_ada, g_norm, w_in, conv_w, conv_b, b_igate, b_fgate, g_mlstm_head, w_out,
           g_final):
    mod = _ada(c, w_ada, b_ada)

    z4 = jnp.zeros((4,), F32)
    bif = jnp.concatenate([b_igate.astype(F32), z4, b_fgate.astype(F32), z4]).reshape(_GATE_ROWS, 1)

    qbt, kaug, kmean, vbt, gbt, out_at = _proj_mlstm(
        x, mod, g_norm, jnp.transpose(w_in), conv_w, conv_b, bif, g_mlstm_head)
    return _moba_out(qbt, kaug, vbt, kmean, gbt, x, mod, out_at, w_out, g_final)


def kernel(x, c, w_ada, b_ada, g_norm, w_in, conv_w, conv_b, b_igate, b_fgate, g_mlstm_head, w_out,
           g_final):
    assert w_ada.shape[0] == 1, "single-layer trunk"
    return _layer(x, c, w_ada[0], b_ada[0], g_norm[0], w_in[0], conv_w[0], conv_b[0], b_igate[0],
                  b_fgate[0], g_mlstm_head[0], w_out[0], g_final)
```
